```python
import math
import numpy as np
import jax
import jax.numpy as jnp
from jax import lax

D_MODEL = 1024
BATCH = 2
SEQ = 8192
DEPTH = 2

ATTN_GROUPS = ((128, 1), (512, 4), (2048, 16))
ATTN_HEADS_PER_GROUP = 4
ATTN_HEAD_DIM = 128
ATTN_HEADS = ATTN_HEADS_PER_GROUP * 3
ATTN_WIDTH = ATTN_HEADS * ATTN_HEAD_DIM
ATTN_OUT_WIDTH = ATTN_HEADS_PER_GROUP * ATTN_HEAD_DIM
NUM_BUCKETS = 32
MAX_DISTANCE = 1024
MASK_VALUE = -1e30

SSD_D_INNER = 2 * D_MODEL
SSD_HEAD_DIM = 64
SSD_HEADS = SSD_D_INNER // SSD_HEAD_DIM
SSD_GROUPS = 4
SSD_STATE = 128
SSD_CONV = 7
SSD_CHUNK = 256
SSD_CONV_CH = SSD_D_INNER + 2 * SSD_GROUPS * SSD_STATE

FOURIER_GROUPS = 6
FOURIER_GROUP_DIM = 256
FOURIER_WIDTH = FOURIER_GROUPS * FOURIER_GROUP_DIM

N_BRANCHES = 3
IN_SPLITS = (ATTN_WIDTH, ATTN_WIDTH, ATTN_WIDTH, SSD_D_INNER, SSD_CONV_CH, 2 * SSD_HEADS, FOURIER_WIDTH, N_BRANCHES * D_MODEL)
IN_WIDTH = 3 * ATTN_WIDTH + SSD_D_INNER + SSD_CONV_CH + 2 * SSD_HEADS + FOURIER_WIDTH + N_BRANCHES * D_MODEL

MEM_TOKENS = 256
MEM_HEADS = 4
MEM_HEAD_DIM = D_MODEL // MEM_HEADS

D_FF = ((8 * D_MODEL + 3 * 256 - 1) // (3 * 256)) * 256
NORM_EPS = 1e-6

kernel_name = "hybrid_dilated_ssd_fourier_encoder"


def rmsnorm(x, g):
    xf = x.astype(jnp.float32)
    y = xf * lax.rsqrt(jnp.mean(xf * xf, axis=-1, keepdims=True) + NORM_EPS)
    return (y * g.astype(jnp.float32)).astype(x.dtype)


def t5_bucket(rel):
    half_b = NUM_BUCKETS // 2
    exact = half_b // 2
    dist = np.abs(rel)
    log_ratio = np.log(np.maximum(dist, 1) / exact) / np.log(MAX_DISTANCE / exact)
    far = np.minimum(exact + (log_ratio * (half_b - exact)).astype(np.int32), half_b - 1)
    return np.where(rel > 0, half_b, 0) + np.where(dist < exact, dist, far)


def dilated_window_attention(q, k, v, bias_tab, dilation, half):
    bsz, seq, heads, hd = q.shape
    sub = seq // dilation
    nblk = -(-sub // half)
    padded = nblk * half

    def to_sub(t):
        return t.reshape(bsz, sub, dilation, heads, hd).transpose(0, 2, 1, 3, 4)

    def pad_seq(t, lo, hi):
        return jnp.pad(t, ((0, 0), (0, 0), (lo, hi), (0, 0), (0, 0)))

    qb = pad_seq(to_sub(q), 0, padded - sub).reshape(bsz, dilation, nblk, half, heads, hd)

    def kv_blocks(t):
        t = pad_seq(to_sub(t), half, padded - sub + half).reshape(bsz, dilation, nblk + 2, half, heads, hd)
        return jnp.concatenate([t[:, :, :-2], t[:, :, 1:-1], t[:, :, 2:]], axis=3)

    kb, vb = kv_blocks(k), kv_blocks(v)
    rel = np.arange(3 * half)[None, :] - half - np.arange(half)[:, None]
    kpos = np.arange(nblk)[:, None] * half + np.arange(3 * half)[None, :] - half
    mask = (np.abs(rel) <= half)[None] & ((kpos >= 0) & (kpos < sub))[:, None, :]
    bias = jnp.transpose(bias_tab[t5_bucket(rel * dilation)], (2, 0, 1)).astype(jnp.float32)

    logits = jnp.einsum('bdnqhe,bdnkhe->bdnhqk', qb, kb).astype(jnp.float32) * (hd ** -0.5) + bias
    logits = jnp.where(mask[:, None], logits, MASK_VALUE)
    lse = jax.nn.logsumexp(logits, axis=-1)
    probs = jnp.exp(logits - lse[..., None]).astype(v.dtype)
    out = jnp.einsum('bdnhqk,bdnkhe->bdnqhe', probs, vb)
    out = out.reshape(bsz, dilation, padded, heads, hd)[:, :, :sub]
    out = out.transpose(0, 2, 1, 3, 4).reshape(bsz, seq, heads, hd)
    lse = lse.transpose(0, 1, 2, 4, 3).reshape(bsz, dilation, padded, heads)[:, :, :sub]
    lse = lse.transpose(0, 2, 1, 3).reshape(bsz, seq, heads)
    return out, lse


def depthwise_conv_centred(u, w, b):
    k, c = w.shape
    y = lax.conv_general_dilated(u, w[:, None, :], window_strides=(1,), padding=[(k // 2, k // 2)],
                                 dimension_numbers=('NWC', 'WIO', 'NWC'), feature_group_count=c)
    return y + b


def ssd_chunked(xs, dt, a, b_ssm, c_ssm):
    bsz, seq, heads, hp = xs.shape
    groups, nst = b_ssm.shape[2], b_ssm.shape[3]
    hpg = heads // groups
    t = SSD_CHUNK
    nc = -(-seq // t)
    pad = nc * t - seq

    def padseq(u):
        return jnp.pad(u, [(0, 0), (0, pad)] + [(0, 0)] * (u.ndim - 2))

    f32 = jnp.float32
    xd = padseq(xs.astype(f32) * dt[..., None]).reshape(bsz, nc, t, groups, hpg, hp)
    la = jnp.moveaxis(padseq(dt * a).reshape(bsz, nc, t, groups, hpg), 2, -1)
    a_cs = jnp.cumsum(la, axis=-1)
    bc = padseq(b_ssm.astype(f32)).reshape(bsz, nc, t, groups, nst)
    cc = padseq(c_ssm.astype(f32)).reshape(bsz, nc, t, groups, nst)

    tri = np.tril(np.ones((t, t), dtype=bool))
    lmat = jnp.exp(jnp.where(tri, a_cs[..., :, None] - a_cs[..., None, :], -jnp.inf))
    cb = jnp.einsum('bclgn,bcsgn->bcgls', cc, bc)
    y_diag = jnp.einsum('bcgls,bcgels,bcsgep->bclgep', cb, lmat, xd)

    decay_to_end = jnp.exp(a_cs[..., -1:] - a_cs)
    states = jnp.einsum('bcsgn,bcges,bcsgep->bcgepn', bc, decay_to_end, xd)
    chunk_decay = jnp.exp(a_cs[..., -1])

    def step(h, inp):
        s, da = inp
        return h * da[..., None, None] + s, h

    h0 = jnp.zeros((bsz, groups, hpg, hp, nst), f32)
    _, prev = lax.scan(step, h0, (jnp.moveaxis(states, 1, 0), jnp.moveaxis(chunk_decay, 1, 0)))
    prev = jnp.moveaxis(prev, 0, 1)
    y_off = jnp.einsum('bclgn,bcgepn,bcgel->bclgep', cc, prev, jnp.exp(a_cs))
    y = (y_diag + y_off).reshape(bsz, nc * t, heads, hp)[:, :seq]
    return y.astype(xs.dtype)


def hybrid_mixer(h, rel_bias, w_in, gate_bias, attn_q_g, attn_k_g, conv_w, conv_b, dt_bias, a_log,
                 d_skip, ssd_norm_g, w_b_attn, w_b_ssd, w_b_fourier, w_out):
    bsz, seq, _ = h.shape
    f32 = jnp.float32
    proj = h @ w_in
    q, k, v, z, xbc, dt_raw, f_in, gate_logit = jnp.split(proj, np.cumsum(IN_SPLITS)[:-1].tolist(), axis=-1)

    hs = (bsz, seq, ATTN_HEADS, ATTN_HEAD_DIM)
    q = rmsnorm(q.reshape(hs), attn_q_g)
    k = rmsnorm(k.reshape(hs), attn_k_g)
    v = v.reshape(hs)
    outs, lses = [], []
    for gi, (win, dil) in enumerate(ATTN_GROUPS):
        sl = slice(gi * ATTN_HEADS_PER_GROUP, (gi + 1) * ATTN_HEADS_PER_GROUP)
        o, s = dilated_window_attention(q[:, :, sl], k[:, :, sl], v[:, :, sl], rel_bias[:, sl], dil, win // (2 * dil))
        outs.append(o)
        lses.append(s)
    alpha = jax.nn.softmax(jnp.stack(lses), axis=0)[..., None]
    o = jnp.sum(alpha * jnp.stack(outs).astype(f32), axis=0).astype(h.dtype)
    y_attn = o.reshape(bsz, seq, ATTN_OUT_WIDTH) @ w_b_attn

    xbc = jax.nn.silu(depthwise_conv_centred(xbc, conv_w, conv_b))
    xs, b_ssm, c_ssm = jnp.split(xbc, [SSD_D_INNER, SSD_D_INNER + SSD_GROUPS * SSD_STATE], axis=-1)
    xs = xs.reshape(bsz, seq, SSD_HEADS, SSD_HEAD_DIM)
    b_ssm = b_ssm.reshape(bsz, seq, SSD_GROUPS, SSD_STATE)
    c_ssm = c_ssm.reshape(bsz, seq, SSD_GROUPS, SSD_STATE)
    dt = jax.nn.softplus(dt_raw.reshape(bsz, seq, 2, SSD_HEADS).astype(f32) + dt_bias.astype(f32))
    a = -jnp.exp(a_log.astype(f32))
    flip = lambda u: jnp.flip(u, axis=1)
    y_fwd = ssd_chunked(xs, dt[:, :, 0], a[0], b_ssm, c_ssm)
    y_bwd = flip(ssd_chunked(flip(xs), flip(dt[:, :, 1]), a[1], flip(b_ssm), flip(c_ssm)))
    y = (y_fwd + y_bwd + xs * d_skip[:, None]).reshape(bsz, seq, SSD_D_INNER) * jax.nn.silu(z)
    y = rmsnorm(y.reshape(bsz, seq, SSD_GROUPS, SSD_D_INNER // SSD_GROUPS),
                ssd_norm_g.reshape(SSD_GROUPS, SSD_D_INNER // SSD_GROUPS)).reshape(bsz, seq, SSD_D_INNER)
    y_ssd = y @ w_b_ssd

    f = f_in.reshape(bsz, seq, FOURIER_GROUPS, FOURIER_GROUP_DIM).astype(f32)
    f = jnp.fft.fft2(f, axes=(1, 3), norm="ortho").real.astype(h.dtype).reshape(bsz, seq, FOURIER_WIDTH)
    y_fourier = f @ w_b_fourier

    gates = jax.nn.sigmoid((gate_logit + gate_bias).astype(f32)).astype(h.dtype)
    gates = gates.reshape(bsz, seq, N_BRANCHES, D_MODEL)
    merged = gates[:, :, 0] * y_attn + gates[:, :, 1] * y_ssd + gates[:, :, 2] * y_fourier
    return merged @ w_out


def memory_cross_attention(h, m, w_xq, w_xk, w_xv, q_g, k_g, w_xo):
    bsz, seq, _ = h.shape
    q = rmsnorm((h @ w_xq).reshape(bsz, seq, MEM_HEADS, MEM_HEAD_DIM), q_g)
    k = rmsnorm((m @ w_xk).reshape(bsz, -1, MEM_HEADS, MEM_HEAD_DIM), k_g)
    v = (m @ w_xv).reshape(bsz, -1, MEM_HEADS, MEM_HEAD_DIM)
    logits = jnp.einsum('bshe,bmhe->bhsm', q, k).astype(jnp.float32) * (MEM_HEAD_DIM ** -0.5)
    probs = jax.nn.softmax(logits, axis=-1).astype(v.dtype)
    o = jnp.einsum('bhsm,bmhe->bshe', probs, v).reshape(bsz, seq, D_MODEL)
    return o @ w_xo


def swiglu(h, w_gate, w_up, w_down):
    return (jax.nn.silu(h @ w_gate) * (h @ w_up)) @ w_down


def setup_inputs(seed: int = 0) -> dict:
    key = jax.random.key(seed)
    ks = iter(jax.random.split(key, 40))
    L = DEPTH

    def nrm(shape, scale):
        return jax.random.normal(next(ks), shape, jnp.float32) * scale

    def gain(shape):
        return 1.0 + nrm(shape, 0.02)

    dt0 = jnp.exp(jax.random.uniform(next(ks), (L, 2, SSD_HEADS), jnp.float32,
                                     minval=math.log(1e-3), maxval=math.log(1e-1)))
    dt_bias = dt0 + jnp.log(-jnp.expm1(-dt0))
    a_log = jnp.log(jax.random.uniform(next(ks), (L, 2, SSD_HEADS), jnp.float32, minval=1.0, maxval=16.0))
    return {
        "x": nrm((BATCH, SEQ, D_MODEL), 1.0),
        "mem": nrm((BATCH, MEM_TOKENS, D_MODEL), 1.0),
        "rel_bias": nrm((NUM_BUCKETS, ATTN_HEADS), 0.5),
        "mix_norm_g": gain((L, D_MODEL)),
        "w_in": nrm((L, D_MODEL, IN_WIDTH), D_MODEL ** -0.5),
        "gate_bias": nrm((L, N_BRANCHES * D_MODEL), 0.01),
        "attn_q_norm_g": gain((L, ATTN_HEAD_DIM)),
        "attn_k_norm_g": gain((L, ATTN_HEAD_DIM)),
        "conv_w": nrm((L, SSD_CONV, SSD_CONV_CH), SSD_CONV ** -0.5),
        "conv_b": nrm((L, SSD_CONV_CH), 0.01),
        "dt_bias": dt_bias,
        "a_log": a_log,
        "d_skip": gain((L, SSD_HEADS)),
        "ssd_norm_g": gain((L, SSD_D_INNER)),
        "w_branch_attn": nrm((L, ATTN_OUT_WIDTH, D_MODEL), ATTN_OUT_WIDTH ** -0.5),
        "w_branch_ssd": nrm((L, SSD_D_INNER, D_MODEL), SSD_D_INNER ** -0.5),
        "w_branch_fourier": nrm((L, FOURIER_WIDTH, D_MODEL), FOURIER_WIDTH ** -0.5),
        "w_mix_out": nrm((L, D_MODEL, D_MODEL), D_MODEL ** -0.5),
        "xattn_norm_g": gain((L, D_MODEL)),
        "mem_norm_g": gain((L, D_MODEL)),
        "w_xq": nrm((L, D_MODEL, D_MODEL), D_MODEL ** -0.5),
        "w_xk": nrm((L, D_MODEL, D_MODEL), D_MODEL ** -0.5),
        "w_xv": nrm((L, D_MODEL, D_MODEL), D_MODEL ** -0.5),
        "xattn_q_norm_g": gain((L, MEM_HEAD_DIM)),
        "xattn_k_norm_g": gain((L, MEM_HEAD_DIM)),
        "w_xo": nrm((L, D_MODEL, D_MODEL), D_MODEL ** -0.5),
        "ffn_norm_g": gain((L, D_MODEL)),
        "w_ffn_gate": nrm((L, D_MODEL, D_FF), D_MODEL ** -0.5),
        "w_ffn_up": nrm((L, D_MODEL, D_FF), D_MODEL ** -0.5),
        "w_ffn_down": nrm((L, D_FF, D_MODEL), D_FF ** -0.5),
    }


def reference(x, mem, rel_bias, mix_norm_g, w_in, gate_bias, attn_q_norm_g, attn_k_norm_g, conv_w, conv_b,
              dt_bias, a_log, d_skip, ssd_norm_g, w_branch_attn, w_branch_ssd, w_branch_fourier, w_mix_out,
              xattn_norm_g, mem_norm_g, w_xq, w_xk, w_xv, xattn_q_norm_g, xattn_k_norm_g, w_xo,
              ffn_norm_g, w_ffn_gate, w_ffn_up, w_ffn_down):
    for l in range(DEPTH):
        h = rmsnorm(x, mix_norm_g[l])
        x = x + hybrid_mixer(h, rel_bias, w_in[l], gate_bias[l], attn_q_norm_g[l], attn_k_norm_g[l],
                             conv_w[l], conv_b[l], dt_bias[l], a_log[l], d_skip[l], ssd_norm_g[l],
                             w_branch_attn[l], w_branch_ssd[l], w_branch_fourier[l], w_mix_out[l])
        h = rmsnorm(x, xattn_norm_g[l])
        m = rmsnorm(mem, mem_norm_g[l])
        x = x + memory_cross_attention(h, m, w_xq[l], w_xk[l], w_xv[l], xattn_q_norm_g[l],
                                       xattn_k_norm_g[l], w_xo[l])
        h = rmsnorm(x, ffn_norm_g[l])
        x = x + swiglu(h, w_ffn_gate[l], w_ffn_up[l], w_ffn_down[l])
    return x
```

```python
import functools
import math

import numpy as np
import jax
import jax.numpy as jnp
from jax import lax
from jax.experimental import pallas as pl
from jax.experimental.pallas import tpu as pltpu

F32 = jnp.float32
BF16 = jnp.bfloat16

NORM_EPS = 1e-6
MASK_VALUE = -1e30

ATTN_GROUPS = ((128, 1), (512, 4), (2048, 16))
ATTN_HPG = 4
ATTN_HD = 128
ATTN_HEADS = ATTN_HPG * len(ATTN_GROUPS)
ATTN_WIDTH = ATTN_HEADS * ATTN_HD
ATTN_OUT = ATTN_HPG * ATTN_HD
ATTN_HALF = 64
ATTN_TQ = 2 * ATTN_HALF
ATTN_TK = ATTN_TQ + 2 * ATTN_HALF
NUM_BUCKETS = 32
MAX_DISTANCE = 1024

SSD_HEADS = 32
SSD_HD = 64
SSD_GROUPS = 4
SSD_HPG = SSD_HEADS // SSD_GROUPS
SSD_STATE = 128
SSD_INNER = SSD_HEADS * SSD_HD
SSD_GW = SSD_INNER // SSD_GROUPS
SSD_CONV = 7
SSD_CHUNK = 256
SSD_CONV_CH = SSD_INNER + 2 * SSD_GROUPS * SSD_STATE
LANES = 128

F_GROUPS = 6
F_GD = 256
F_WIDTH = F_GROUPS * F_GD
F_N2 = 128

MEM_HEADS = 4

VMEM_LIMIT = 56 * 1024 * 1024


def _cparams(sem):
    return pltpu.CompilerParams(dimension_semantics=sem, vmem_limit_bytes=VMEM_LIMIT)


def _silu(x):
    return x * jax.nn.sigmoid(x)


def _rms(x, g):
    ms = jnp.mean(x * x, axis=-1, keepdims=True)
    return x * lax.rsqrt(ms + NORM_EPS) * g


def _norm_mm_kernel(*refs, epilogue, head_dim):
    if epilogue == "headnorm":
        x_ref, g_ref, w_ref, hg_ref, o_ref, h_ref = refs
    elif epilogue == "chandft":
        x_ref, g_ref, w_ref, dft_ref, o_ref, h_ref = refs
    else:
        x_ref, g_ref, w_ref, o_ref, h_ref = refs

    @pl.when(pl.program_id(1) == 0)
    def _():
        h_ref[...] = _rms(x_ref[...], g_ref[...]).astype(BF16)

    acc = jnp.dot(h_ref[...], w_ref[...], preferred_element_type=F32)
    tn = acc.shape[1]
    if epilogue == "headnorm":
        for hh in range(tn // head_dim):
            sl = slice(hh * head_dim, (hh + 1) * head_dim)
            o_ref[:, sl] = _rms(acc[:, sl], hg_ref[...]).astype(o_ref.dtype)
    elif epilogue == "chandft":
        a16 = acc.astype(BF16)
        for gg in range(tn // F_GD):
            sl = slice(gg * F_GD, (gg + 1) * F_GD)
            r = jnp.dot(a16[:, sl], dft_ref[...], preferred_element_type=F32)
            o_ref[0, :, sl] = r[:, :F_GD].astype(o_ref.dtype)
            o_ref[1, :, sl] = r[:, F_GD:].astype(o_ref.dtype)
    else:
        o_ref[...] = acc.astype(o_ref.dtype)


def _norm_matmul(x, g, w, *, out_dtype, tn, tm=1024, epilogue="plain", extra=None, head_dim=None):
    m, k = x.shape
    n = w.shape[1]
    tm = min(tm, m)
    assert m % tm == 0 and n % tn == 0
    in_specs = [
        pl.BlockSpec((tm, k), lambda i, j: (i, 0)),
        pl.BlockSpec((1, k), lambda i, j: (0, 0)),
        pl.BlockSpec((k, tn), lambda i, j: (0, j)),
    ]
    args = [x, g.reshape(1, k), w]
    if epilogue == "headnorm":
        in_specs.append(pl.BlockSpec((1, head_dim), lambda i, j: (0, 0)))
        args.append(extra.reshape(1, head_dim))
    elif epilogue == "chandft":
        in_specs.append(pl.BlockSpec(extra.shape, lambda i, j: (0, 0)))
        args.append(extra)
    if epilogue == "chandft":
        out_shape = jax.ShapeDtypeStruct((2, m, n), out_dtype)
        out_spec = pl.BlockSpec((2, tm, tn), lambda i, j: (0, i, j))
    else:
        out_shape = jax.ShapeDtypeStruct((m, n), out_dtype)
        out_spec = pl.BlockSpec((tm, tn), lambda i, j: (i, j))
    return pl.pallas_call(
        functools.partial(_norm_mm_kernel, epilogue=epilogue, head_dim=head_dim),
        grid=(m // tm, n // tn),
        in_specs=in_specs,
        out_specs=out_spec,
        out_shape=out_shape,
        scratch_shapes=[pltpu.VMEM((tm, k), BF16)],
        compiler_params=_cparams(("parallel", "arbitrary")),
        name="norm_matmul_" + epilogue,
    )(*args)


def _t5_bucket_np(rel):
    half_b = NUM_BUCKETS // 2
    exact = half_b // 2
    dist = np.abs(rel)
    log_ratio = np.log(np.maximum(dist, 1) / exact) / np.log(MAX_DISTANCE / exact)
    far = np.minimum(exact + (log_ratio * (half_b - exact)).astype(np.int32), half_b - 1)
    return np.where(rel > 0, half_b, 0) + np.where(dist < exact, dist, far)


def _attn_bias(rel_bias, gi, dil):
    i = np.arange(ATTN_TQ)[:, None]
    j = np.arange(ATTN_TK)[None, :]
    rel = j - ATTN_HALF - i
    idx = _t5_bucket_np(rel * dil)
    b = rel_bias[idx][:, :, gi * ATTN_HPG:(gi + 1) * ATTN_HPG].astype(F32)
    b = jnp.transpose(b, (2, 0, 1))
    return jnp.where(jnp.asarray(np.abs(rel) <= ATTN_HALF)[None], b, MASK_VALUE)


def _attn_kernel(*refs, sub, has_prev, last):
    q_ref, k_ref, v_ref, bias_ref = refs[:4]
    refs = refs[4:]
    if has_prev:
        po_ref, pl_ref = refs[:2]
        refs = refs[2:]
    if last:
        o_ref, kpad, vpad = refs
    else:
        o_ref, lse_ref, kpad, vpad = refs

    zeros = jnp.zeros((ATTN_HALF, ATTN_HD), BF16)
    kpad[0:ATTN_HALF, :] = zeros
    vpad[0:ATTN_HALF, :] = zeros
    kpad[ATTN_HALF + sub:, :] = zeros
    vpad[ATTN_HALF + sub:, :] = zeros
    kpad[ATTN_HALF:ATTN_HALF + sub, :] = k_ref[...]
    vpad[ATTN_HALF:ATTN_HALF + sub, :] = v_ref[...]

    scale = ATTN_HD ** -0.5
    bias = bias_ref[...]
    col = lax.broadcasted_iota(jnp.int32, (ATTN_TQ, ATTN_TK), 1)

    def body(t, carry):
        s0 = pl.multiple_of(t * ATTN_TQ, ATTN_TQ)
        q = q_ref[pl.ds(s0, ATTN_TQ), :]
        kw = kpad[pl.ds(s0, ATTN_TK), :]
        vw = vpad[pl.ds(s0, ATTN_TK), :]
        s = lax.dot_general(q, kw, (((1,), (1,)), ((), ())), preferred_element_type=F32) * scale + bias
        kpos = col + (s0 - ATTN_HALF)
        s = jnp.where((kpos >= 0) & (kpos < sub), s, MASK_VALUE)
        m = jnp.max(s, axis=-1, keepdims=True)
        p = jnp.exp(s - m)
        l = jnp.sum(p, axis=-1, keepdims=True)
        o = jnp.dot(p.astype(BF16), vw, preferred_element_type=F32) / l
        lse = jnp.broadcast_to(m + jnp.log(l), (ATTN_TQ, ATTN_HD))
        if has_prev:
            lp = pl_ref[pl.ds(s0, ATTN_TQ), :]
            op = po_ref[pl.ds(s0, ATTN_TQ), :]
            mx = jnp.maximum(lp, lse)
            wa = jnp.exp(lp - mx)
            wb = jnp.exp(lse - mx)
            den = wa + wb
            o = (op * wa + o * wb) / den
            lse = mx + jnp.log(den)
        o_ref[pl.ds(s0, ATTN_TQ), :] = o.astype(o_ref.dtype)
        if not last:
            lse_ref[pl.ds(s0, ATTN_TQ), :] = lse
        return carry

    lax.fori_loop(0, sub // ATTN_TQ, body, 0)


def _attn_group(qn, kn, v, bias, gi, dil, bsz, seq, prev, last):
    sub = seq // dil
    assert sub % ATTN_TQ == 0
    view = lambda t: t.reshape(bsz, sub, dil * t.shape[-1])
    qkv_spec = pl.BlockSpec((None, sub, ATTN_HD), lambda b, r, h: (b, 0, r * ATTN_HEADS + gi * ATTN_HPG + h))
    out_spec = pl.BlockSpec((None, sub, ATTN_HD), lambda b, r, h: (b, 0, r * ATTN_HPG + h))
    in_specs = [qkv_spec, qkv_spec, qkv_spec, pl.BlockSpec((None, ATTN_TQ, ATTN_TK), lambda b, r, h: (h, 0, 0))]
    args = [view(qn), view(kn), view(v), bias]
    if prev is not None:
        in_specs += [out_spec, out_spec]
        args += [view(prev[0]), view(prev[1])]
    o_shape = jax.ShapeDtypeStruct((bsz, sub, dil * ATTN_OUT), BF16 if last else F32)
    if last:
        out_shape, out_specs = o_shape, out_spec
    else:
        out_shape = (o_shape, jax.ShapeDtypeStruct((bsz, sub, dil * ATTN_OUT), F32))
        out_specs = (out_spec, out_spec)
    res = pl.pallas_call(
        functools.partial(_attn_kernel, sub=sub, has_prev=prev is not None, last=last),
        grid=(bsz, dil, ATTN_HPG),
        in_specs=in_specs,
        out_specs=out_specs,
        out_shape=out_shape,
        scratch_shapes=[pltpu.VMEM((sub + 2 * ATTN_HALF, ATTN_HD), BF16)] * 2,
        compiler_params=_cparams(("parallel", "parallel", "parallel")),
        name=f"dilated_attn_g{gi}",
    )(*args)
    if last:
        return res.reshape(bsz * seq, ATTN_OUT)
    return res[0].reshape(bsz * seq, ATTN_OUT), res[1].reshape(bsz * seq, ATTN_OUT)


CONV_HALO = 8
CONV_RC = 64


def _conv_kernel(prev_ref, cur_ref, next_ref, w_ref, b_ref, o_ref, xp_ref, *, tm, nblk):
    i = pl.program_id(1)
    pad = SSD_CONV // 2
    xp_ref[0:CONV_HALO, :] = jnp.where(i > 0, prev_ref[...], 0.0)
    xp_ref[CONV_HALO:CONV_HALO + tm, :] = cur_ref[...]
    xp_ref[CONV_HALO + tm:, :] = jnp.where(i < nblk - 1, next_ref[...], 0.0)
    for c in range(tm // CONV_RC):
        r0 = c * CONV_RC
        acc = jnp.broadcast_to(b_ref[...], (CONV_RC, b_ref.shape[1]))
        for k in range(SSD_CONV):
            off = CONV_HALO + r0 + k - pad
            acc = acc + w_ref[k:k + 1, :] * xp_ref[off:off + CONV_RC, :]
        o_ref[r0:r0 + CONV_RC, :] = _silu(acc).astype(o_ref.dtype)


def _conv_silu(xbc, w, b, *, tm=512, tc=1024):
    bsz, seq, ch = xbc.shape
    nblk = seq // tm
    hb = tm // CONV_HALO
    return pl.pallas_call(
        functools.partial(_conv_kernel, tm=tm, nblk=nblk),
        grid=(bsz, nblk, ch // tc),
        in_specs=[
            pl.BlockSpec((None, CONV_HALO, tc), lambda bb, i, j: (bb, jnp.maximum(i * hb - 1, 0), j)),
            pl.BlockSpec((None, tm, tc), lambda bb, i, j: (bb, i, j)),
            pl.BlockSpec((None, CONV_HALO, tc), lambda bb, i, j: (bb, jnp.minimum((i + 1) * hb, seq // CONV_HALO - 1), j)),
            pl.BlockSpec((SSD_CONV, tc), lambda bb, i, j: (0, j)),
            pl.BlockSpec((1, tc), lambda bb, i, j: (0, j)),
        ],
        out_specs=pl.BlockSpec((None, tm, tc), lambda bb, i, j: (bb, i, j)),
        out_shape=jax.ShapeDtypeStruct((bsz, seq, ch), BF16),
        scratch_shapes=[pltpu.VMEM((tm + 2 * CONV_HALO, tc), F32)],
        compiler_params=_cparams(("parallel", "parallel", "parallel")),
        name="conv_silu",
    )(xbc, xbc, xbc, w, b.reshape(1, ch))


def _softplus(x):
    return jnp.maximum(x, 0.0) + jnp.log1p(jnp.exp(-jnp.abs(x)))


def _ssd_decay_terms(dtr_ref, dtb_ref, alog_ref):
    t = dtr_ref.shape[0]
    dt = _softplus(dtr_ref[...] + dtb_ref[...])
    la = dt * (-jnp.exp(alog_ref[...]))
    row = lax.broadcasted_iota(jnp.int32, (t, t), 0)
    colm = lax.broadcasted_iota(jnp.int32, (t, t), 1)
    tri = (colm <= row).astype(BF16)
    hi = la.astype(BF16)
    r1 = la - hi.astype(F32)
    mid = r1.astype(BF16)
    lo = (r1 - mid.astype(F32)).astype(BF16)
    acs = (jnp.dot(tri, hi, preferred_element_type=F32) + jnp.dot(tri, mid, preferred_element_type=F32)
           + jnp.dot(tri, lo, preferred_element_type=F32))
    return dt, la, acs


def _ssd_bwd_state_kernel(xs_ref, b_ref, dtr_ref, dtb_ref, alog_ref, gin_ref, g_sc):
    c = pl.program_id(1)
    g = pl.program_id(2)

    @pl.when(c == 0)
    def _():
        g_sc[g] = jnp.zeros(g_sc.shape[1:], F32)

    t = xs_ref.shape[0]
    dt, la, acs = _ssd_decay_terms(dtr_ref, dtb_ref, alog_ref)
    exb = acs - la
    wts = jnp.exp(exb) * dt
    tot = jnp.exp(acs[t - 1:t, :])
    bm = b_ref[...]
    for e in range(SSD_HPG):
        lane = SSD_HPG + e
        gin_ref[e] = g_sc[g, e].astype(gin_ref.dtype)
        xw = (xs_ref[:, e * SSD_HD:(e + 1) * SSD_HD].astype(F32) * wts[:, lane:lane + 1]).astype(BF16)
        st = jnp.dot(xw.T, bm, preferred_element_type=F32)
        g_sc[g, e] = g_sc[g, e] * tot[:, lane:lane + 1] + st


def _ssd_main_kernel(xs_ref, b_ref, c_ref, dtr_ref, z_ref, gin_ref, dtb_ref, alog_ref, dsk_ref, ng_ref,
                     y_ref, h_sc, y_sc):
    c = pl.program_id(1)
    g = pl.program_id(2)

    @pl.when(c == 0)
    def _():
        h_sc[g] = jnp.zeros(h_sc.shape[1:], F32)

    t = xs_ref.shape[0]
    dt, la, acs = _ssd_decay_terms(dtr_ref, dtb_ref, alog_ref)
    exb = acs - la
    acs_t = acs.T
    exb_t = exb.T
    dt_t = dt.T
    tot = acs[t - 1:t, :]
    e_in = jnp.exp(acs)
    e_out = jnp.exp(tot - exb)
    w_state = jnp.exp(tot - acs) * dt
    e_tot = jnp.exp(tot)

    bm = b_ref[...]
    cm = c_ref[...]
    cb = lax.dot_general(cm, bm, (((1,), (1,)), ((), ())), preferred_element_type=F32)
    li = lax.broadcasted_iota(jnp.int32, (t, t), 0)
    si = lax.broadcasted_iota(jnp.int32, (t, t), 1)
    lower = li >= si
    upper = si >= li

    for e in range(SSD_HPG):
        lf, lb = e, SSD_HPG + e
        sl = slice(e * SSD_HD, (e + 1) * SSD_HD)
        xe = xs_ref[:, sl]
        wf = jnp.where(lower, jnp.exp(acs[:, lf:lf + 1] - acs_t[lf:lf + 1, :]), 0.0) * dt_t[lf:lf + 1, :]
        wb = jnp.where(upper, jnp.exp(exb_t[lb:lb + 1, :] - exb[:, lb:lb + 1]), 0.0) * dt_t[lb:lb + 1, :]
        mm = (cb * (wf + wb)).astype(BF16)
        y = jnp.dot(mm, xe, preferred_element_type=F32)
        hprev = h_sc[g, e]
        yf = lax.dot_general(cm, hprev.astype(BF16), (((1,), (1,)), ((), ())), preferred_element_type=F32)
        yb = lax.dot_general(cm, gin_ref[e], (((1,), (1,)), ((), ())), preferred_element_type=F32)
        y = y + yf * e_in[:, lf:lf + 1] + yb * e_out[:, lb:lb + 1]
        xf = xe.astype(F32)
        y_sc[:, sl] = y + xf * dsk_ref[:, sl]
        xw = (xf * w_state[:, lf:lf + 1]).astype(BF16)
        st = jnp.dot(xw.T, bm, preferred_element_type=F32)
        h_sc[g, e] = hprev * e_tot[:, lf:lf + 1] + st

    yy = y_sc[...] * _silu(z_ref[...].astype(F32))
    y_ref[...] = _rms(yy, ng_ref[...]).astype(y_ref.dtype)


def _ssd(conv_out, dt_raw, z, dtb, alog, dskip, norm_g, bsz, seq):
    t = SSD_CHUNK
    nc = seq // t
    assert seq % t == 0
    b_off = SSD_INNER // SSD_STATE
    c_off = b_off + SSD_GROUPS
    const_spec = pl.BlockSpec((1, LANES), lambda b, c, g: (0, g))
    rev = lambda c: nc - 1 - c
    gin = pl.pallas_call(
        _ssd_bwd_state_kernel,
        grid=(bsz, nc, SSD_GROUPS),
        in_specs=[
            pl.BlockSpec((None, t, SSD_GW), lambda b, c, g: (b, rev(c), g)),
            pl.BlockSpec((None, t, SSD_STATE), lambda b, c, g: (b, rev(c), b_off + g)),
            pl.BlockSpec((None, t, LANES), lambda b, c, g: (b, rev(c), g)),
            const_spec, const_spec,
        ],
        out_specs=pl.BlockSpec((None, None, SSD_HPG, SSD_HD, SSD_STATE), lambda b, c, g: (b, rev(c), g, 0, 0)),
        out_shape=jax.ShapeDtypeStruct((bsz, nc, SSD_HEADS, SSD_HD, SSD_STATE), BF16),
        scratch_shapes=[pltpu.VMEM((SSD_GROUPS, SSD_HPG, SSD_HD, SSD_STATE), F32)],
        compiler_params=_cparams(("parallel", "arbitrary", "arbitrary")),
        name="ssd_bwd_states",
    )(conv_out, conv_out, dt_raw, dtb, alog)

    gw_spec = pl.BlockSpec((1, SSD_GW), lambda b, c, g: (0, g))
    return pl.pallas_call(
        _ssd_main_kernel,
        grid=(bsz, nc, SSD_GROUPS),
        in_specs=[
            pl.BlockSpec((None, t, SSD_GW), lambda b, c, g: (b, c, g)),
            pl.BlockSpec((None, t, SSD_STATE), lambda b, c, g: (b, c, b_off + g)),
            pl.BlockSpec((None, t, SSD_STATE), lambda b, c, g: (b, c, c_off + g)),
            pl.BlockSpec((None, t, LANES), lambda b, c, g: (b, c, g)),
            pl.BlockSpec((None, t, SSD_GW), lambda b, c, g: (b, c, g)),
            pl.BlockSpec((None, None, SSD_HPG, SSD_HD, SSD_STATE), lambda b, c, g: (b, c, g, 0, 0)),
            const_spec, const_spec, gw_spec, gw_spec,
        ],
        out_specs=pl.BlockSpec((None, t, SSD_GW), lambda b, c, g: (b, c, g)),
        out_shape=jax.ShapeDtypeStruct((bsz, seq, SSD_INNER), BF16),
        scratch_shapes=[pltpu.VMEM((SSD_GROUPS, SSD_HPG, SSD_HD, SSD_STATE), F32),
                        pltpu.VMEM((t, SSD_GW), F32)],
        compiler_params=_cparams(("parallel", "arbitrary", "arbitrary")),
        name="ssd_main",
    )(conv_out, conv_out, conv_out, dt_raw, z, gin, dtb, alog, dskip, norm_g)


def _dft_tables(seq):
    n1, n2 = seq // F_N2, F_N2
    c = np.arange(F_GD)
    ang = 2 * np.pi * np.outer(c, c) / F_GD
    chan = np.concatenate([np.cos(ang), -np.sin(ang)], axis=1) / math.sqrt(F_GD)
    k1 = np.arange(n1)
    a1 = 2 * np.pi * np.outer(k1, k1) / n1
    stage_a = np.block([[np.cos(a1), np.sin(a1)], [-np.sin(a1), np.cos(a1)]])
    s2 = np.arange(n2)
    at = 2 * np.pi * np.outer(s2, k1) / seq
    tw_c, tw_s = np.cos(at)[..., None], np.sin(at)[..., None]
    a2 = 2 * np.pi * np.outer(s2, s2) / n2
    stage_b = np.concatenate([np.cos(a2), np.sin(a2)], axis=1)
    return (jnp.asarray(chan, BF16), jnp.asarray(stage_a, BF16), jnp.asarray(tw_c, F32),
            jnp.asarray(tw_s, F32), jnp.asarray(stage_b, BF16))


def _fft_a_kernel(z_ref, ma_ref, twc_ref, tws_ref, o_ref, *, n1, sb):
    cols = z_ref.shape[-1]
    zz = z_ref[...].reshape(2 * n1, cols)
    y = jnp.dot(ma_ref[...], zz, preferred_element_type=F32)
    yr, yi = y[:n1], y[n1:]
    w = cols // sb
    for j in range(sb):
        sl = slice(j * w, (j + 1) * w)
        tc, ts = twc_ref[j], tws_ref[j]
        o_ref[0, :, sl] = (yr[:, sl] * tc + yi[:, sl] * ts).astype(o_ref.dtype)
        o_ref[1, :, sl] = (yi[:, sl] * tc - yr[:, sl] * ts).astype(o_ref.dtype)


def _fft_b_kernel(y_ref, mb_ref, o_ref, *, n2, scale):
    ct = y_ref.shape[-1]
    yy = y_ref[...].reshape(2 * n2, ct)
    o_ref[...] = (jnp.dot(mb_ref[...], yy, preferred_element_type=F32) * scale).astype(o_ref.dtype)


def _fourier_seq(z2, tabs, bsz, seq, *, sb=2, ct=768):
    _, ma, twc, tws, mb = tabs
    n1, n2 = seq // F_N2, F_N2
    w = z2.shape[-1]
    za = z2.reshape(2, bsz, n1, n2 * w)
    ya = pl.pallas_call(
        functools.partial(_fft_a_kernel, n1=n1, sb=sb),
        grid=(bsz, n2 // sb),
        in_specs=[
            pl.BlockSpec((2, None, n1, sb * w), lambda b, j: (0, b, 0, j)),
            pl.BlockSpec((2 * n1, 2 * n1), lambda b, j: (0, 0)),
            pl.BlockSpec((sb, n1, 1), lambda b, j: (j, 0, 0)),
            pl.BlockSpec((sb, n1, 1), lambda b, j: (j, 0, 0)),
        ],
        out_specs=pl.BlockSpec((2, None, n1, sb * w), lambda b, j: (0, b, 0, j)),
        out_shape=jax.ShapeDtypeStruct((2, bsz, n1, n2 * w), BF16),
        compiler_params=_cparams(("parallel", "parallel")),
        name="fourier_stage_a",
    )(za, ma, twc, tws)
    yb = ya.reshape(2, bsz, n1, n2, w)
    nct = w // ct
    out = pl.pallas_call(
        functools.partial(_fft_b_kernel, n2=n2, scale=1.0 / math.sqrt(seq)),
        grid=(bsz, n1, nct),
        in_specs=[
            pl.BlockSpec((2, None, None, n2, ct), lambda b, k, j: (0, b, k, 0, j)),
            pl.BlockSpec((n2, 2 * n2), lambda b, k, j: (0, 0)),
        ],
        out_specs=pl.BlockSpec((None, n2, ct), lambda b, k, j: (b, 0, k * nct + j)),
        out_shape=jax.ShapeDtypeStruct((bsz, n2, n1 * w), BF16),
        compiler_params=_cparams(("parallel", "parallel", "parallel")),
        name="fourier_stage_b",
    )(yb, mb)
    return out.reshape(bsz * seq, w)


def _merge_kernel(x_ref, o_ref, y_ref, f_ref, gl_ref, gb_ref, wa_ref, ws_ref, wf_ref, wo_ref, out_ref):
    d = x_ref.shape[1]
    ya = jnp.dot(o_ref[...], wa_ref[...], preferred_element_type=F32)
    ys = jnp.dot(y_ref[...], ws_ref[...], preferred_element_type=F32)
    yf = jnp.dot(f_ref[...], wf_ref[...], preferred_element_type=F32)
    gates = jax.nn.sigmoid(gl_ref[...] + gb_ref[...])
    merged = gates[:, :d] * ya + gates[:, d:2 * d] * ys + gates[:, 2 * d:] * yf
    out_ref[...] = x_ref[...] + jnp.dot(merged.astype(BF16), wo_ref[...], preferred_element_type=F32)


def _merge(x, o, y, f, gl, gb, wa, ws, wf, wo, *, tm=256):
    m, d = x.shape
    row = lambda width: pl.BlockSpec((tm, width), lambda i: (i, 0))
    full = lambda a: pl.BlockSpec(a.shape, lambda i: (0, 0))
    gb = gb.reshape(1, -1)
    return pl.pallas_call(
        _merge_kernel,
        grid=(m // tm,),
        in_specs=[row(d), row(o.shape[1]), row(y.shape[1]), row(f.shape[1]), row(gl.shape[1]), full(gb),
                  full(wa), full(ws), full(wf), full(wo)],
        out_specs=row(d),
        out_shape=jax.ShapeDtypeStruct((m, d), F32),
        compiler_params=_cparams(("parallel",)),
        name="branch_merge",
    )(x, o, y, f, gl, gb, wa, ws, wf, wo)


def _xattn_kernel(x_ref, g_ref, wq_ref, qg_ref, k_ref, v_ref, wo_ref, out_ref, o_sc):
    x = x_ref[...]
    d = x.shape[1]
    hd = d // MEM_HEADS
    h = _rms(x, g_ref[...]).astype(BF16)
    q = jnp.dot(h, wq_ref[...], preferred_element_type=F32)
    scale = hd ** -0.5
    for hh in range(MEM_HEADS):
        sl = slice(hh * hd, (hh + 1) * hd)
        qn = _rms(q[:, sl], qg_ref[...]).astype(BF16)
        s = lax.dot_general(qn, k_ref[:, sl], (((1,), (1,)), ((), ())), preferred_element_type=F32) * scale
        m = jnp.max(s, axis=-1, keepdims=True)
        p = jnp.exp(s - m)
        l = jnp.sum(p, axis=-1, keepdims=True)
        o_sc[:, sl] = jnp.dot(p.astype(BF16), v_ref[:, sl], preferred_element_type=F32) / l
    out_ref[...] = x + jnp.dot(o_sc[...].astype(BF16), wo_ref[...], preferred_element_type=F32)


def _xattn(x, g, wq, qg, k, v, wo, bsz, seq, *, tm=512):
    d = x.shape[-1]
    mt = k.shape[1]
    full = lambda a: pl.BlockSpec(a.shape, lambda b, i: (0, 0))
    g = g.reshape(1, d)
    qg = qg.reshape(1, -1)
    return pl.pallas_call(
        _xattn_kernel,
        grid=(bsz, seq // tm),
        in_specs=[
            pl.BlockSpec((None, tm, d), lambda b, i: (b, i, 0)),
            full(g), full(wq), full(qg),
            pl.BlockSpec((None, mt, d), lambda b, i: (b, 0, 0)),
            pl.BlockSpec((None, mt, d), lambda b, i: (b, 0, 0)),
            full(wo),
        ],
        out_specs=pl.BlockSpec((None, tm, d), lambda b, i: (b, i, 0)),
        out_shape=jax.ShapeDtypeStruct((bsz, seq, d), F32),
        scratch_shapes=[pltpu.VMEM((tm, d), F32)],
        compiler_params=_cparams(("parallel", "parallel")),
        name="mem_xattn",
    )(x.reshape(bsz, seq, d), g, wq, qg, k, v, wo).reshape(bsz * seq, d)


def _ffn_kernel(x_ref, g_ref, wg_ref, wu_ref, wd_ref, out_ref, *, tf):
    x = x_ref[...]
    h = _rms(x, g_ref[...]).astype(BF16)
    acc = x
    for c in range(wg_ref.shape[1] // tf):
        sl = slice(c * tf, (c + 1) * tf)
        gt = jnp.dot(h, wg_ref[:, sl], preferred_element_type=F32)
        up = jnp.dot(h, wu_ref[:, sl], preferred_element_type=F32)
        a = (_silu(gt) * up).astype(BF16)
        acc = acc + jnp.dot(a, wd_ref[sl, :], preferred_element_type=F32)
    out_ref[...] = acc


def _ffn(x, g, wg, wu, wd, *, tm=256):
    m, d = x.shape
    dff = wg.shape[1]
    tf = dff // 2 if (dff // 2) % LANES == 0 else dff
    full = lambda a: pl.BlockSpec(a.shape, lambda i: (0, 0))
    g = g.reshape(1, d)
    return pl.pallas_call(
        functools.partial(_ffn_kernel, tf=tf),
        grid=(m // tm,),
        in_specs=[pl.BlockSpec((tm, d), lambda i: (i, 0)), full(g), full(wg), full(wu), full(wd)],
        out_specs=pl.BlockSpec((tm, d), lambda i: (i, 0)),
        out_shape=jax.ShapeDtypeStruct((m, d), F32),
        compiler_params=_cparams(("parallel",)),
        name="swiglu_ffn",
    )(x, g, wg, wu, wd)


def _group_lanes(v2h):
    v = v2h.reshape(2, SSD_GROUPS, SSD_HPG).transpose(1, 0, 2).reshape(SSD_GROUPS, 2 * SSD_HPG)
    v = jnp.pad(v, ((0, 0), (0, LANES - 2 * SSD_HPG)))
    return v.reshape(1, SSD_GROUPS * LANES).astype(F32)


def _group_lane_cols(w_dt):
    d = w_dt.shape[0]
    w = w_dt.reshape(d, 2, SSD_GROUPS, SSD_HPG).transpose(0, 2, 1, 3).reshape(d, SSD_GROUPS, 2 * SSD_HPG)
    w = jnp.pad(w, ((0, 0), (0, 0), (0, LANES - 2 * SSD_HPG)))
    return w.reshape(d, SSD_GROUPS * LANES)


def kernel(x, mem, rel_bias, mix_norm_g, w_in, gate_bias, attn_q_norm_g, attn_k_norm_g, conv_w, conv_b, dt_bias, a_log, d_skip, ssd_norm_g, w_branch_attn, w_branch_ssd, w_branch_fourier, w_mix_out, xattn_norm_g, mem_norm_g, w_xq, w_xk, w_xv, xattn_q_norm_g, xattn_k_norm_g, w_xo, ffn_norm_g, w_ffn_gate, w_ffn_up, w_ffn_down):
    bsz, seq, d = x.shape
    depth = w_in.shape[0]
    m = bsz * seq
    xf = x.reshape(m, d)
    memf = mem.reshape(bsz * mem.shape[1], d)
    tabs = _dft_tables(seq)
    biases = [_attn_bias(rel_bias, gi, dil) for gi, (_, dil) in enumerate(ATTN_GROUPS)]
    offs = np.cumsum([0, ATTN_WIDTH, ATTN_WIDTH, ATTN_WIDTH, SSD_INNER, SSD_CONV_CH, 2 * SSD_HEADS, F_WIDTH, 3 * d])
    bf = lambda a: a.astype(BF16)

    for l in range(depth):
        wl = w_in[l]
        seg = lambda i: wl[:, offs[i]:offs[i + 1]]
        g_mix = mix_norm_g[l]
        qn = _norm_matmul(xf, g_mix, bf(seg(0)), out_dtype=BF16, tn=512, epilogue="headnorm",
                          extra=attn_q_norm_g[l], head_dim=ATTN_HD)
        kn = _norm_matmul(xf, g_mix, bf(seg(1)), out_dtype=BF16, tn=512, epilogue="headnorm",
                          extra=attn_k_norm_g[l], head_dim=ATTN_HD)
        v = _norm_matmul(xf, g_mix, bf(seg(2)), out_dtype=BF16, tn=512)
        z = _norm_matmul(xf, g_mix, bf(seg(3)), out_dtype=BF16, tn=512)
        xbc = _norm_matmul(xf, g_mix, bf(seg(4)), out_dtype=F32, tn=512)
        dt_raw = _norm_matmul(xf, g_mix, bf(_group_lane_cols(seg(5))), out_dtype=F32, tn=512)
        z2 = _norm_matmul(xf, g_mix, bf(seg(6)), out_dtype=BF16, tn=512, epilogue="chandft", extra=tabs[0])
        gl = _norm_matmul(xf, g_mix, bf(seg(7)), out_dtype=F32, tn=512)

        prev = None
        for gi, (_, dil) in enumerate(ATTN_GROUPS):
            prev = _attn_group(qn, kn, v, biases[gi], gi, dil, bsz, seq, prev, gi == len(ATTN_GROUPS) - 1)
        o_attn = prev

        conv_out = _conv_silu(xbc.reshape(bsz, seq, SSD_CONV_CH), conv_w[l], conv_b[l])
        y_ssd = _ssd(conv_out, dt_raw.reshape(bsz, seq, -1), z.reshape(bsz, seq, SSD_INNER),
                     _group_lanes(dt_bias[l]), _group_lanes(a_log[l]),
                     jnp.repeat(d_skip[l], SSD_HD).reshape(1, SSD_INNER), ssd_norm_g[l].reshape(1, SSD_INNER),
                     bsz, seq).reshape(m, SSD_INNER)

        f_re = _fourier_seq(z2, tabs, bsz, seq)

        xf = _merge(xf, o_attn, y_ssd, f_re, gl, gate_bias[l], bf(w_branch_attn[l]), bf(w_branch_ssd[l]),
                    bf(w_branch_fourier[l]), bf(w_mix_out[l]))

        hd_m = d // MEM_HEADS
        km = _norm_matmul(memf, mem_norm_g[l], bf(w_xk[l]), out_dtype=BF16, tn=512, epilogue="headnorm",
                          extra=xattn_k_norm_g[l], head_dim=hd_m)
        vm = _norm_matmul(memf, mem_norm_g[l], bf(w_xv[l]), out_dtype=BF16, tn=512)
        xf = _xattn(xf, xattn_norm_g[l], bf(w_xq[l]), xattn_q_norm_g[l], km.reshape(bsz, -1, d),
                    vm.reshape(bsz, -1, d), bf(w_xo[l]), bsz, seq)

        xf = _ffn(xf, ffn_norm_g[l], bf(w_ffn_gate[l]), bf(w_ffn_up[l]), bf(w_ffn_down[l]))

    return xf.reshape(bsz, seq, d)
```

```python
import functools
import math

import numpy as np
import jax
import jax.numpy as jnp
from jax import lax
from jax.experimental import pallas as pl
from jax.experimental.pallas import tpu as pltpu

F32 = jnp.float32
BF16 = jnp.bfloat16

NORM_EPS = 1e-6
MASK_VALUE = -1e30

ATTN_GROUPS = ((128, 1), (512, 4), (2048, 16))
ATTN_HPG = 4
ATTN_HD = 128
ATTN_HEADS = ATTN_HPG * len(ATTN_GROUPS)
ATTN_WIDTH = ATTN_HEADS * ATTN_HD
ATTN_OUT = ATTN_HPG * ATTN_HD
ATTN_HALF = 64
ATTN_TQ = 2 * ATTN_HALF
ATTN_TK = ATTN_TQ + 2 * ATTN_HALF
NUM_BUCKETS = 32
MAX_DISTANCE = 1024

SSD_HEADS = 32
SSD_HD = 64
SSD_GROUPS = 4
SSD_HPG = SSD_HEADS // SSD_GROUPS
SSD_STATE = 128
SSD_INNER = SSD_HEADS * SSD_HD
SSD_GW = SSD_INNER // SSD_GROUPS
SSD_CONV = 7
SSD_CHUNK = 256
SSD_CONV_CH = SSD_INNER + 2 * SSD_GROUPS * SSD_STATE
LANES = 128
BF16_ROWS = 16

F_GROUPS = 6
F_GD = 256
F_WIDTH = F_GROUPS * F_GD
F_N2 = 128

MEM_HEADS = 4

VMEM_LIMIT = 56 * 1024 * 1024


def _cparams(sem):
    return pltpu.CompilerParams(dimension_semantics=sem, vmem_limit_bytes=VMEM_LIMIT)


def _silu(x):
    return x * jax.nn.sigmoid(x)


def _rms(x, g):
    ms = jnp.mean(x * x, axis=-1, keepdims=True)
    return x * lax.rsqrt(ms + NORM_EPS) * g


def _get_rows(ref3, start=0, size=None, stride=1):
    idx = slice(None) if size is None else pl.ds(start, size, stride=stride)
    return jnp.concatenate([ref3[c, idx, :] for c in range(ref3.shape[0])], axis=1)


def _put_rows(ref3, val, start=0, stride=1):
    size = val.shape[0]
    idx = slice(None) if (stride == 1 and size == ref3.shape[1]) else pl.ds(start, size, stride=stride)
    for c in range(ref3.shape[0]):
        ref3[c, idx, :] = val[:, c * LANES:(c + 1) * LANES]


def _rmsnorm_kernel(x_ref, g_ref, o_ref):
    o_ref[...] = _rms(x_ref[...], g_ref[...]).astype(o_ref.dtype)


def _rmsnorm(x, g, *, tm=1024):
    m, k = x.shape
    tm = min(tm, m)
    return pl.pallas_call(
        _rmsnorm_kernel,
        grid=(m // tm,),
        in_specs=[pl.BlockSpec((tm, k), lambda i: (i, 0)), pl.BlockSpec((1, k), lambda i: (0, 0))],
        out_specs=pl.BlockSpec((tm, k), lambda i: (i, 0)),
        out_shape=jax.ShapeDtypeStruct((m, k), BF16),
        compiler_params=_cparams(("parallel",)),
        name="rmsnorm",
    )(x, g.reshape(1, k))


def _mm_kernel(*refs, epilogue, head_dim):
    h_ref, w_ref = refs[:2]
    acc = jnp.dot(h_ref[...], w_ref[...], preferred_element_type=F32)
    tn = acc.shape[1]
    if epilogue == "headnorm":
        hg_ref, o_ref = refs[2:]
        for hh in range(tn // head_dim):
            sl = slice(hh * head_dim, (hh + 1) * head_dim)
            o_ref[:, sl] = _rms(acc[:, sl], hg_ref[...]).astype(o_ref.dtype)
    elif epilogue == "sigmoid_bias":
        b_ref, o_ref = refs[2:]
        o_ref[...] = jax.nn.sigmoid(acc + b_ref[...]).astype(o_ref.dtype)
    elif epilogue == "chandft":
        dft_ref, o_ref = refs[2:]
        a16 = acc.astype(BF16)
        for gg in range(tn // F_GD):
            sl = slice(gg * F_GD, (gg + 1) * F_GD)
            r = jnp.dot(a16[:, sl], dft_ref[...], preferred_element_type=F32)
            o_ref[0, :, sl] = r[:, :F_GD].astype(o_ref.dtype)
            o_ref[1, :, sl] = r[:, F_GD:].astype(o_ref.dtype)
    else:
        o_ref = refs[2]
        o_ref[...] = acc.astype(o_ref.dtype)


def _matmul(h, w, *, out_dtype, tn, tm=1024, epilogue="plain", extra=None, head_dim=None):
    m, k = h.shape
    n = w.shape[1]
    tm = min(tm, m)
    assert m % tm == 0 and n % tn == 0
    in_specs = [pl.BlockSpec((tm, k), lambda i, j: (i, 0)), pl.BlockSpec((k, tn), lambda i, j: (0, j))]
    args = [h, w]
    if epilogue == "headnorm":
        in_specs.append(pl.BlockSpec((1, head_dim), lambda i, j: (0, 0)))
        args.append(extra.reshape(1, head_dim))
    elif epilogue == "sigmoid_bias":
        in_specs.append(pl.BlockSpec((1, tn), lambda i, j: (0, j)))
        args.append(extra.reshape(1, n))
    elif epilogue == "chandft":
        in_specs.append(pl.BlockSpec(extra.shape, lambda i, j: (0, 0)))
        args.append(extra)
    if epilogue == "chandft":
        out_shape = jax.ShapeDtypeStruct((2, m, n), out_dtype)
        out_spec = pl.BlockSpec((2, tm, tn), lambda i, j: (0, i, j))
    else:
        out_shape = jax.ShapeDtypeStruct((m, n), out_dtype)
        out_spec = pl.BlockSpec((tm, tn), lambda i, j: (i, j))
    return pl.pallas_call(
        functools.partial(_mm_kernel, epilogue=epilogue, head_dim=head_dim),
        grid=(m // tm, n // tn),
        in_specs=in_specs,
        out_specs=out_spec,
        out_shape=out_shape,
        compiler_params=_cparams(("parallel", "parallel")),
        name="matmul_" + epilogue,
    )(*args)


def _qkv_kernel(h_ref, w_ref, hg_ref, o_ref, *scratch, dil):
    j = pl.program_id(1)
    acc = jnp.dot(h_ref[...], w_ref[...], preferred_element_type=F32)
    tm, width = acc.shape
    rows = tm // dil
    nh = width // ATTN_HD
    if dil > 1:
        scr = scratch[0]
        for hh in range(nh):
            scr[hh] = acc[:, hh * ATTN_HD:(hh + 1) * ATTN_HD]
    for r in range(dil):
        for hh in range(nh):
            if dil == 1:
                ph = acc[:, hh * ATTN_HD:(hh + 1) * ATTN_HD]
            else:
                ph = scr[hh, pl.ds(r, rows, stride=dil), :]
            val = jnp.where(j < 2, _rms(ph, hg_ref[...]), ph)
            c0 = r * width + hh * ATTN_HD
            o_ref[:, c0:c0 + ATTN_HD] = val.astype(o_ref.dtype)


def _qkv_group(h, w3, gains, dil, bsz, seq, *, tm=1024):
    m, k = h.shape
    sub = seq // dil
    per_b = seq // tm
    rows = tm // dil
    assert seq % tm == 0 and rows % BF16_ROWS == 0
    return pl.pallas_call(
        functools.partial(_qkv_kernel, dil=dil),
        grid=(m // tm, 3),
        in_specs=[
            pl.BlockSpec((tm, k), lambda i, j: (i, 0)),
            pl.BlockSpec((k, ATTN_OUT), lambda i, j: (0, j)),
            pl.BlockSpec((None, 1, ATTN_HD), lambda i, j: (j, 0, 0)),
        ],
        out_specs=pl.BlockSpec((None, None, rows, dil * ATTN_OUT), lambda i, j: (j, i // per_b, i % per_b, 0)),
        out_shape=jax.ShapeDtypeStruct((3, bsz, sub, dil * ATTN_OUT), BF16),
        scratch_shapes=[pltpu.VMEM((ATTN_HPG, tm, ATTN_HD), F32)] if dil > 1 else [],
        compiler_params=_cparams(("parallel", "parallel")),
        name=f"qkv_proj_d{dil}",
    )(h, w3, gains)


def _t5_bucket_np(rel):
    half_b = NUM_BUCKETS // 2
    exact = half_b // 2
    dist = np.abs(rel)
    log_ratio = np.log(np.maximum(dist, 1) / exact) / np.log(MAX_DISTANCE / exact)
    far = np.minimum(exact + (log_ratio * (half_b - exact)).astype(np.int32), half_b - 1)
    return np.where(rel > 0, half_b, 0) + np.where(dist < exact, dist, far)


def _attn_bias(rel_bias, gi, dil):
    i = np.arange(ATTN_TQ)[:, None]
    j = np.arange(ATTN_TK)[None, :]
    rel = j - ATTN_HALF - i
    idx = _t5_bucket_np(rel * dil)
    onehot = jnp.asarray(np.eye(NUM_BUCKETS, dtype=np.float32)[idx])
    tab = rel_bias[:, gi * ATTN_HPG:(gi + 1) * ATTN_HPG].astype(F32)
    b = jnp.einsum("qkn,nh->hqk", onehot, tab, precision=lax.Precision.HIGHEST)
    return jnp.where(jnp.asarray(np.abs(rel) <= ATTN_HALF)[None], b, MASK_VALUE)


def _attn_kernel(q_ref, kp_ref, kc_ref, kn_ref, vp_ref, vc_ref, vn_ref, bias_ref, o_ref, lse_ref, kpad, vpad,
                 *, tile, sub):
    i = pl.program_id(2)
    hf = ATTN_HALF
    kpad[0:hf, :] = kp_ref[...]
    kpad[hf:hf + tile, :] = kc_ref[...]
    kpad[hf + tile:, :] = kn_ref[...]
    vpad[0:hf, :] = vp_ref[...]
    vpad[hf:hf + tile, :] = vc_ref[...]
    vpad[hf + tile:, :] = vn_ref[...]

    scale = ATTN_HD ** -0.5
    col = lax.broadcasted_iota(jnp.int32, (ATTN_TQ, ATTN_TK), 1)
    lane = lax.broadcasted_iota(jnp.int32, (ATTN_TQ, LANES), 1)

    def body(t, carry):
        s0 = pl.multiple_of(t * ATTN_TQ, ATTN_TQ)
        kpos = col + (i * tile + s0 - hf)
        valid = (kpos >= 0) & (kpos < sub)
        lse_tile = jnp.zeros((ATTN_TQ, LANES), F32)
        for hh in range(ATTN_HPG):
            sl = slice(hh * ATTN_HD, (hh + 1) * ATTN_HD)
            q = q_ref[pl.ds(s0, ATTN_TQ), sl]
            kw = kpad[pl.ds(s0, ATTN_TK), sl]
            vw = vpad[pl.ds(s0, ATTN_TK), sl]
            s = lax.dot_general(q, kw, (((1,), (1,)), ((), ())), preferred_element_type=F32) * scale + bias_ref[hh]
            s = jnp.where(valid, s, MASK_VALUE)
            m = jnp.max(s, axis=-1, keepdims=True)
            p = jnp.exp(s - m)
            l = jnp.sum(p, axis=-1, keepdims=True)
            o = jnp.dot(p.astype(BF16), vw, preferred_element_type=F32) / l
            o_ref[pl.ds(s0, ATTN_TQ), sl] = o.astype(o_ref.dtype)
            lse_tile = jnp.where(lane == hh, m + jnp.log(l), lse_tile)
        lse_ref[pl.ds(s0, ATTN_TQ), :] = lse_tile
        return carry

    lax.fori_loop(0, tile // ATTN_TQ, body, 0)


def _attn_group(qkv, bias, dil, bsz, seq, *, tile=1024):
    sub = seq // dil
    tile = min(tile, sub)
    nblk = sub // tile
    hb = tile // ATTN_HALF
    last_h = sub // ATTN_HALF - 1
    assert sub % tile == 0 and tile % ATTN_TQ == 0
    cur = lambda which: pl.BlockSpec((None, None, tile, ATTN_OUT), lambda b, r, i: (which, b, i, r))
    prev = lambda which: pl.BlockSpec((None, None, ATTN_HALF, ATTN_OUT),
                                      lambda b, r, i: (which, b, jnp.maximum(i * hb - 1, 0), r))
    nxt = lambda which: pl.BlockSpec((None, None, ATTN_HALF, ATTN_OUT),
                                     lambda b, r, i: (which, b, jnp.minimum((i + 1) * hb, last_h), r))
    return pl.pallas_call(
        functools.partial(_attn_kernel, tile=tile, sub=sub),
        grid=(bsz, dil, nblk),
        in_specs=[cur(0), prev(1), cur(1), nxt(1), prev(2), cur(2), nxt(2),
                  pl.BlockSpec((ATTN_HPG, ATTN_TQ, ATTN_TK), lambda b, r, i: (0, 0, 0))],
        out_specs=(pl.BlockSpec((None, tile, ATTN_OUT), lambda b, r, i: (b, i, r)),
                   pl.BlockSpec((None, tile, LANES), lambda b, r, i: (b, i, r))),
        out_shape=(jax.ShapeDtypeStruct((bsz, sub, dil * ATTN_OUT), BF16),
                   jax.ShapeDtypeStruct((bsz, sub, dil * LANES), F32)),
        scratch_shapes=[pltpu.VMEM((tile + 2 * ATTN_HALF, ATTN_OUT), BF16)] * 2,
        compiler_params=_cparams(("parallel", "parallel", "parallel")),
        name=f"dilated_attn_d{dil}",
    )(qkv, qkv, qkv, qkv, qkv, qkv, qkv, bias)


CONV_HALO = 8
CONV_RC = 64


def _conv_kernel(prev_ref, cur_ref, next_ref, w_ref, b_ref, o_ref, xp_ref, *, tm, nblk):
    i = pl.program_id(1)
    pad = SSD_CONV // 2
    xp_ref[0:CONV_HALO, :] = jnp.where(i > 0, prev_ref[...], 0.0)
    xp_ref[CONV_HALO:CONV_HALO + tm, :] = cur_ref[...]
    xp_ref[CONV_HALO + tm:, :] = jnp.where(i < nblk - 1, next_ref[...], 0.0)
    for c in range(tm // CONV_RC):
        r0 = c * CONV_RC
        acc = jnp.broadcast_to(b_ref[...], (CONV_RC, b_ref.shape[1]))
        for k in range(SSD_CONV):
            off = CONV_HALO + r0 + k - pad
            acc = acc + w_ref[k:k + 1, :] * xp_ref[off:off + CONV_RC, :]
        o_ref[r0:r0 + CONV_RC, :] = _silu(acc).astype(o_ref.dtype)


def _conv_silu(xbc, w, b, *, tm=512, tc=1024):
    bsz, seq, ch = xbc.shape
    nblk = seq // tm
    hb = tm // CONV_HALO
    return pl.pallas_call(
        functools.partial(_conv_kernel, tm=tm, nblk=nblk),
        grid=(bsz, nblk, ch // tc),
        in_specs=[
            pl.BlockSpec((None, CONV_HALO, tc), lambda bb, i, j: (bb, jnp.maximum(i * hb - 1, 0), j)),
            pl.BlockSpec((None, tm, tc), lambda bb, i, j: (bb, i, j)),
            pl.BlockSpec((None, CONV_HALO, tc), lambda bb, i, j: (bb, jnp.minimum((i + 1) * hb, seq // CONV_HALO - 1), j)),
            pl.BlockSpec((SSD_CONV, tc), lambda bb, i, j: (0, j)),
            pl.BlockSpec((1, tc), lambda bb, i, j: (0, j)),
        ],
        out_specs=pl.BlockSpec((None, tm, tc), lambda bb, i, j: (bb, i, j)),
        out_shape=jax.ShapeDtypeStruct((bsz, seq, ch), BF16),
        scratch_shapes=[pltpu.VMEM((tm + 2 * CONV_HALO, tc), F32)],
        compiler_params=_cparams(("parallel", "parallel", "parallel")),
        name="conv_silu",
    )(xbc, xbc, xbc, w, b.reshape(1, ch))


def _softplus(x):
    return jnp.maximum(x, 0.0) + jnp.log1p(jnp.exp(-jnp.abs(x)))


def _ssd_decay_terms(dtr_ref, dtb_ref, alog_ref):
    t = dtr_ref.shape[0]
    dt = _softplus(dtr_ref[...] + dtb_ref[...])
    la = dt * (-jnp.exp(alog_ref[...]))
    row = lax.broadcasted_iota(jnp.int32, (t, t), 0)
    colm = lax.broadcasted_iota(jnp.int32, (t, t), 1)
    tri = (colm <= row).astype(BF16)
    hi = la.astype(BF16)
    r1 = la - hi.astype(F32)
    mid = r1.astype(BF16)
    lo = (r1 - mid.astype(F32)).astype(BF16)
    acs = (jnp.dot(tri, hi, preferred_element_type=F32) + jnp.dot(tri, mid, preferred_element_type=F32)
           + jnp.dot(tri, lo, preferred_element_type=F32))
    return dt, la, acs


def _ssd_bwd_state_kernel(xs_ref, b_ref, dtr_ref, dtb_ref, alog_ref, gin_ref, g_sc):
    c = pl.program_id(1)
    g = pl.program_id(2)

    @pl.when(c == 0)
    def _():
        g_sc[g] = jnp.zeros(g_sc.shape[1:], F32)

    t = xs_ref.shape[0]
    dt, la, acs = _ssd_decay_terms(dtr_ref, dtb_ref, alog_ref)
    exb = acs - la
    wts = jnp.exp(exb) * dt
    tot = jnp.exp(acs[t - 1:t, :])
    bm = b_ref[...]
    for e in range(SSD_HPG):
        lane = SSD_HPG + e
        gin_ref[e] = g_sc[g, e].astype(gin_ref.dtype)
        xw = (xs_ref[:, e * SSD_HD:(e + 1) * SSD_HD].astype(F32) * wts[:, lane:lane + 1]).astype(BF16)
        st = jnp.dot(xw.T, bm, preferred_element_type=F32)
        g_sc[g, e] = g_sc[g, e] * tot[:, lane:lane + 1] + st


def _ssd_main_kernel(xs_ref, b_ref, c_ref, dtr_ref, z_ref, gin_ref, dtb_ref, alog_ref, dsk_ref, ng_ref,
                     y_ref, h_sc, y_sc):
    c = pl.program_id(1)
    g = pl.program_id(2)

    @pl.when(c == 0)
    def _():
        h_sc[g] = jnp.zeros(h_sc.shape[1:], F32)

    t = xs_ref.shape[0]
    dt, la, acs = _ssd_decay_terms(dtr_ref, dtb_ref, alog_ref)
    exb = acs - la
    acs_t = acs.T
    exb_t = exb.T
    dt_t = dt.T
    tot = acs[t - 1:t, :]
    e_in = jnp.exp(acs)
    e_out = jnp.exp(tot - exb)
    w_state = jnp.exp(tot - acs) * dt
    e_tot = jnp.exp(tot)

    bm = b_ref[...]
    cm = c_ref[...]
    cb = lax.dot_general(cm, bm, (((1,), (1,)), ((), ())), preferred_element_type=F32)
    li = lax.broadcasted_iota(jnp.int32, (t, t), 0)
    si = lax.broadcasted_iota(jnp.int32, (t, t), 1)
    lower = li >= si
    upper = si >= li

    for e in range(SSD_HPG):
        lf, lb = e, SSD_HPG + e
        sl = slice(e * SSD_HD, (e + 1) * SSD_HD)
        xe = xs_ref[:, sl]
        wf = jnp.where(lower, jnp.exp(acs[:, lf:lf + 1] - acs_t[lf:lf + 1, :]), 0.0) * dt_t[lf:lf + 1, :]
        wb = jnp.where(upper, jnp.exp(exb_t[lb:lb + 1, :] - exb[:, lb:lb + 1]), 0.0) * dt_t[lb:lb + 1, :]
        mm = (cb * (wf + wb)).astype(BF16)
        y = jnp.dot(mm, xe, preferred_element_type=F32)
        hprev = h_sc[g, e]
        yf = lax.dot_general(cm, hprev.astype(BF16), (((1,), (1,)), ((), ())), preferred_element_type=F32)
        yb = lax.dot_general(cm, gin_ref[e], (((1,), (1,)), ((), ())), preferred_element_type=F32)
        y = y + yf * e_in[:, lf:lf + 1] + yb * e_out[:, lb:lb + 1]
        xf = xe.astype(F32)
        y_sc[:, sl] = y + xf * dsk_ref[:, sl]
        xw = (xf * w_state[:, lf:lf + 1]).astype(BF16)
        st = jnp.dot(xw.T, bm, preferred_element_type=F32)
        h_sc[g, e] = hprev * e_tot[:, lf:lf + 1] + st

    yy = y_sc[...] * _silu(z_ref[...].astype(F32))
    y_ref[...] = _rms(yy, ng_ref[...]).astype(y_ref.dtype)


def _ssd(conv_out, dt_raw, z, dtb, alog, dskip, norm_g, bsz, seq):
    t = SSD_CHUNK
    nc = seq // t
    assert seq % t == 0
    b_off = SSD_INNER // SSD_STATE
    c_off = b_off + SSD_GROUPS
    const_spec = pl.BlockSpec((1, LANES), lambda b, c, g: (0, g))
    rev = lambda c: nc - 1 - c
    gin = pl.pallas_call(
        _ssd_bwd_state_kernel,
        grid=(bsz, nc, SSD_GROUPS),
        in_specs=[
            pl.BlockSpec((None, t, SSD_GW), lambda b, c, g: (b, rev(c), g)),
            pl.BlockSpec((None, t, SSD_STATE), lambda b, c, g: (b, rev(c), b_off + g)),
            pl.BlockSpec((None, t, LANES), lambda b, c, g: (b, rev(c), g)),
            const_spec, const_spec,
        ],
        out_specs=pl.BlockSpec((None, None, SSD_HPG, SSD_HD, SSD_STATE), lambda b, c, g: (b, rev(c), g, 0, 0)),
        out_shape=jax.ShapeDtypeStruct((bsz, nc, SSD_HEADS, SSD_HD, SSD_STATE), BF16),
        scratch_shapes=[pltpu.VMEM((SSD_GROUPS, SSD_HPG, SSD_HD, SSD_STATE), F32)],
        compiler_params=_cparams(("parallel", "arbitrary", "arbitrary")),
        name="ssd_bwd_states",
    )(conv_out, conv_out, dt_raw, dtb, alog)

    gw_spec = pl.BlockSpec((1, SSD_GW), lambda b, c, g: (0, g))
    return pl.pallas_call(
        _ssd_main_kernel,
        grid=(bsz, nc, SSD_GROUPS),
        in_specs=[
            pl.BlockSpec((None, t, SSD_GW), lambda b, c, g: (b, c, g)),
            pl.BlockSpec((None, t, SSD_STATE), lambda b, c, g: (b, c, b_off + g)),
            pl.BlockSpec((None, t, SSD_STATE), lambda b, c, g: (b, c, c_off + g)),
            pl.BlockSpec((None, t, LANES), lambda b, c, g: (b, c, g)),
            pl.BlockSpec((None, t, SSD_GW), lambda b, c, g: (b, c, g)),
            pl.BlockSpec((None, None, SSD_HPG, SSD_HD, SSD_STATE), lambda b, c, g: (b, c, g, 0, 0)),
            const_spec, const_spec, gw_spec, gw_spec,
        ],
        out_specs=pl.BlockSpec((None, t, SSD_GW), lambda b, c, g: (b, c, g)),
        out_shape=jax.ShapeDtypeStruct((bsz, seq, SSD_INNER), BF16),
        scratch_shapes=[pltpu.VMEM((SSD_GROUPS, SSD_HPG, SSD_HD, SSD_STATE), F32),
                        pltpu.VMEM((t, SSD_GW), F32)],
        compiler_params=_cparams(("parallel", "arbitrary", "arbitrary")),
        name="ssd_main",
    )(conv_out, conv_out, conv_out, dt_raw, z, gin, dtb, alog, dskip, norm_g)


def _dft_tables(seq):
    n1, n2 = seq // F_N2, F_N2
    c = np.arange(F_GD)
    ang = 2 * np.pi * np.outer(c, c) / F_GD
    chan = np.concatenate([np.cos(ang), -np.sin(ang)], axis=1) / math.sqrt(F_GD)
    k1 = np.arange(n1)
    a1 = 2 * np.pi * np.outer(k1, k1) / n1
    stage_a = np.block([[np.cos(a1), np.sin(a1)], [-np.sin(a1), np.cos(a1)]])
    s2 = np.arange(n2)
    at = 2 * np.pi * np.outer(s2, k1) / seq
    tw_c, tw_s = np.cos(at)[..., None], np.sin(at)[..., None]
    a2 = 2 * np.pi * np.outer(s2, s2) / n2
    stage_b = np.concatenate([np.cos(a2), np.sin(a2)], axis=1)
    return (jnp.asarray(chan, BF16), jnp.asarray(stage_a, BF16), jnp.asarray(tw_c, F32),
            jnp.asarray(tw_s, F32), jnp.asarray(stage_b, BF16))


def _fft_a_kernel(z_ref, ma_ref, twc_ref, tws_ref, o_ref, zs, ys, *, n1, rb):
    ct = z_ref.shape[-1]
    _put_rows(zs, z_ref[...].astype(F32).reshape(2 * n1 * rb, ct))
    for a in range(rb):
        zz = _get_rows(zs, a, 2 * n1, rb).astype(BF16)
        y = jnp.dot(ma_ref[...], zz, preferred_element_type=F32)
        yr, yi = y[:n1], y[n1:]
        tc, ts = twc_ref[a], tws_ref[a]
        _put_rows(ys, yr * tc + yi * ts, a, rb)
        _put_rows(ys, yi * tc - yr * ts, n1 * rb + a, rb)
    o_ref[...] = _get_rows(ys).reshape(2, n1, rb, ct).astype(o_ref.dtype)


def _fft_b_kernel(y_ref, mb_ref, o_ref, os_ref, *, n2, kb, scale):
    ct = y_ref.shape[-1]
    for kk in range(kb):
        yy = y_ref[:, kk].reshape(2 * n2, ct)
        _put_rows(os_ref, jnp.dot(mb_ref[...], yy, preferred_element_type=F32) * scale, kk, kb)
    o_ref[...] = _get_rows(os_ref).reshape(n2, kb, ct).astype(o_ref.dtype)


def _fourier_seq(z2, tabs, bsz, seq, *, rb=BF16_ROWS, kb=BF16_ROWS, ct=768):
    _, ma, twc, tws, mb = tabs
    n1, n2 = seq // F_N2, F_N2
    w = z2.shape[-1]
    nct = w // ct
    za = z2.reshape(2, bsz, n1, n2, w)
    ab_spec = pl.BlockSpec((2, None, n1, rb, ct), lambda b, a, j: (0, b, 0, a, j))
    ya = pl.pallas_call(
        functools.partial(_fft_a_kernel, n1=n1, rb=rb),
        grid=(bsz, n2 // rb, nct),
        in_specs=[
            ab_spec,
            pl.BlockSpec((2 * n1, 2 * n1), lambda b, a, j: (0, 0)),
            pl.BlockSpec((rb, n1, 1), lambda b, a, j: (a, 0, 0)),
            pl.BlockSpec((rb, n1, 1), lambda b, a, j: (a, 0, 0)),
        ],
        out_specs=ab_spec,
        out_shape=jax.ShapeDtypeStruct((2, bsz, n1, n2, w), BF16),
        scratch_shapes=[pltpu.VMEM((ct // LANES, 2 * n1 * rb, LANES), F32)] * 2,
        compiler_params=_cparams(("parallel", "parallel", "parallel")),
        name="fourier_stage_a",
    )(za, ma, twc, tws)
    out = pl.pallas_call(
        functools.partial(_fft_b_kernel, n2=n2, kb=kb, scale=1.0 / math.sqrt(seq)),
        grid=(bsz, n1 // kb, nct),
        in_specs=[
            pl.BlockSpec((2, None, kb, n2, ct), lambda b, k, j: (0, b, k, 0, j)),
            pl.BlockSpec((n2, 2 * n2), lambda b, k, j: (0, 0)),
        ],
        out_specs=pl.BlockSpec((None, n2, kb, ct), lambda b, k, j: (b, 0, k, j)),
        out_shape=jax.ShapeDtypeStruct((bsz, n2, n1, w), BF16),
        scratch_shapes=[pltpu.VMEM((ct // LANES, n2 * kb, LANES), F32)],
        compiler_params=_cparams(("parallel", "parallel", "parallel")),
        name="fourier_stage_b",
    )(ya, mb)
    return out.reshape(bsz, seq, w)


def _merge_kernel(*refs, dils):
    ng = len(dils)
    x_ref = refs[0]
    o_refs = refs[1:1 + ng]
    l_refs = refs[1 + ng:1 + 2 * ng]
    y_ref, f_ref, gt_ref, wa_ref, ws_ref, wf_ref, wo_ref, out_ref, o_sc, l_sc = refs[1 + 2 * ng:]
    tm, d = x_ref.shape

    for gi, dil in enumerate(dils):
        rows = tm // dil
        idx = lambda r: slice(None) if dil == 1 else pl.ds(r, rows, stride=dil)
        for r in range(dil):
            l_sc[gi, idx(r), :] = l_refs[gi][:, r * LANES:(r + 1) * LANES]
            for hh in range(ATTN_HPG):
                c0 = r * ATTN_OUT + hh * ATTN_HD
                o_sc[gi * ATTN_HPG + hh, idx(r), :] = o_refs[gi][:, c0:c0 + ATTN_HD].astype(F32)

    ls = [l_sc[gi] for gi in range(ng)]
    mx = functools.reduce(jnp.maximum, ls)
    ws = [jnp.exp(v - mx) for v in ls]
    inv = 1.0 / functools.reduce(lambda a, b: a + b, ws)
    heads = []
    for hh in range(ATTN_HPG):
        acc = None
        for gi in range(ng):
            alpha = (ws[gi] * inv)[:, hh:hh + 1]
            term = alpha * o_sc[gi * ATTN_HPG + hh]
            acc = term if acc is None else acc + term
        heads.append(acc.astype(BF16))
    o_attn = jnp.concatenate(heads, axis=1)

    ya = jnp.dot(o_attn, wa_ref[...], preferred_element_type=F32)
    ys = jnp.dot(y_ref[...], ws_ref[...], preferred_element_type=F32)
    yf = jnp.dot(f_ref[...], wf_ref[...], preferred_element_type=F32)
    gates = gt_ref[...].astype(F32)
    merged = gates[:, :d] * ya + gates[:, d:2 * d] * ys + gates[:, 2 * d:] * yf
    out_ref[...] = x_ref[...] + jnp.dot(merged.astype(BF16), wo_ref[...], preferred_element_type=F32)


def _merge(x, os_, ls_, dils, y, f, gates, wa, ws, wf, wo, bsz, seq, *, tm=256):
    d = x.shape[-1]
    assert all((tm // dil) % BF16_ROWS == 0 for dil in dils)
    tok = lambda width: pl.BlockSpec((None, tm, width), lambda b, i: (b, i, 0))
    full = lambda a: pl.BlockSpec(a.shape, lambda b, i: (0, 0))
    ng = len(dils)
    in_specs = [tok(d)]
    in_specs += [pl.BlockSpec((None, tm // dil, dil * ATTN_OUT), lambda b, i: (b, i, 0)) for dil in dils]
    in_specs += [pl.BlockSpec((None, tm // dil, dil * LANES), lambda b, i: (b, i, 0)) for dil in dils]
    in_specs += [tok(y.shape[-1]), tok(f.shape[-1]), tok(gates.shape[-1]), full(wa), full(ws), full(wf), full(wo)]
    return pl.pallas_call(
        functools.partial(_merge_kernel, dils=tuple(dils)),
        grid=(bsz, seq // tm),
        in_specs=in_specs,
        out_specs=tok(d),
        out_shape=jax.ShapeDtypeStruct((bsz, seq, d), F32),
        scratch_shapes=[pltpu.VMEM((ng * ATTN_HPG, tm, ATTN_HD), F32), pltpu.VMEM((ng, tm, LANES), F32)],
        compiler_params=_cparams(("parallel", "parallel")),
        name="branch_merge",
    )(x, *os_, *ls_, y, f, gates, wa, ws, wf, wo)


def _xattn_kernel(x_ref, g_ref, wq_ref, qg_ref, k_ref, v_ref, wo_ref, out_ref, o_sc):
    x = x_ref[...]
    d = x.shape[1]
    hd = d // MEM_HEADS
    h = _rms(x, g_ref[...]).astype(BF16)
    q = jnp.dot(h, wq_ref[...], preferred_element_type=F32)
    scale = hd ** -0.5
    for hh in range(MEM_HEADS):
        sl = slice(hh * hd, (hh + 1) * hd)
        qn = _rms(q[:, sl], qg_ref[...]).astype(BF16)
        s = lax.dot_general(qn, k_ref[:, sl], (((1,), (1,)), ((), ())), preferred_element_type=F32) * scale
        m = jnp.max(s, axis=-1, keepdims=True)
        p = jnp.exp(s - m)
        l = jnp.sum(p, axis=-1, keepdims=True)
        o_sc[:, sl] = jnp.dot(p.astype(BF16), v_ref[:, sl], preferred_element_type=F32) / l
    out_ref[...] = x + jnp.dot(o_sc[...].astype(BF16), wo_ref[...], preferred_element_type=F32)


def _xattn(x, g, wq, qg, k, v, wo, *, tm=512):
    bsz, seq, d = x.shape
    mt = k.shape[1]
    full = lambda a: pl.BlockSpec(a.shape, lambda b, i: (0, 0))
    g = g.reshape(1, d)
    qg = qg.reshape(1, -1)
    return pl.pallas_call(
        _xattn_kernel,
        grid=(bsz, seq // tm),
        in_specs=[
            pl.BlockSpec((None, tm, d), lambda b, i: (b, i, 0)),
            full(g), full(wq), full(qg),
            pl.BlockSpec((None, mt, d), lambda b, i: (b, 0, 0)),
            pl.BlockSpec((None, mt, d), lambda b, i: (b, 0, 0)),
            full(wo),
        ],
        out_specs=pl.BlockSpec((None, tm, d), lambda b, i: (b, i, 0)),
        out_shape=jax.ShapeDtypeStruct((bsz, seq, d), F32),
        scratch_shapes=[pltpu.VMEM((tm, d), F32)],
        compiler_params=_cparams(("parallel", "parallel")),
        name="mem_xattn",
    )(x, g, wq, qg, k, v, wo)


def _ffn_kernel(x_ref, g_ref, wg_ref, wu_ref, wd_ref, out_ref, *, tf):
    x = x_ref[...]
    h = _rms(x, g_ref[...]).astype(BF16)
    acc = x
    for c in range(wg_ref.shape[1] // tf):
        sl = slice(c * tf, (c + 1) * tf)
        gt = jnp.dot(h, wg_ref[:, sl], preferred_element_type=F32)
        up = jnp.dot(h, wu_ref[:, sl], preferred_element_type=F32)
        a = (_silu(gt) * up).astype(BF16)
        acc = acc + jnp.dot(a, wd_ref[sl, :], preferred_element_type=F32)
    out_ref[...] = acc


def _ffn(x, g, wg, wu, wd, *, tm=256):
    m, d = x.shape
    dff = wg.shape[1]
    tf = dff // 2 if (dff // 2) % LANES == 0 else dff
    full = lambda a: pl.BlockSpec(a.shape, lambda i: (0, 0))
    g = g.reshape(1, d)
    return pl.pallas_call(
        functools.partial(_ffn_kernel, tf=tf),
        grid=(m // tm,),
        in_specs=[pl.BlockSpec((tm, d), lambda i: (i, 0)), full(g), full(wg), full(wu), full(wd)],
        out_specs=pl.BlockSpec((tm, d), lambda i: (i, 0)),
        out_shape=jax.ShapeDtypeStruct((m, d), F32),
        compiler_params=_cparams(("parallel",)),
        name="swiglu_ffn",
    )(x, g, wg, wu, wd)


def _group_lanes(v2h):
    v = v2h.reshape(2, SSD_GROUPS, SSD_HPG).transpose(1, 0, 2).reshape(SSD_GROUPS, 2 * SSD_HPG)
    v = jnp.pad(v, ((0, 0), (0, LANES - 2 * SSD_HPG)))
    return v.reshape(1, SSD_GROUPS * LANES).astype(F32)


def _group_lane_cols(w_dt):
    d = w_dt.shape[0]
    w = w_dt.reshape(d, 2, SSD_GROUPS, SSD_HPG).transpose(0, 2, 1, 3).reshape(d, SSD_GROUPS, 2 * SSD_HPG)
    w = jnp.pad(w, ((0, 0), (0, 0), (0, LANES - 2 * SSD_HPG)))
    return w.reshape(d, SSD_GROUPS * LANES)


def kernel(x, mem, rel_bias, mix_norm_g, w_in, gate_bias, attn_q_norm_g, attn_k_norm_g, conv_w, conv_b, dt_bias, a_log, d_skip, ssd_norm_g, w_branch_attn, w_branch_ssd, w_branch_fourier, w_mix_out, xattn_norm_g, mem_norm_g, w_xq, w_xk, w_xv, xattn_q_norm_g, xattn_k_norm_g, w_xo, ffn_norm_g, w_ffn_gate, w_ffn_up, w_ffn_down):
    bsz, seq, d = x.shape
    depth = w_in.shape[0]
    m = bsz * seq
    xf = x.reshape(m, d)
    memf = mem.reshape(bsz * mem.shape[1], d)
    tabs = _dft_tables(seq)
    dils = [dil for _, dil in ATTN_GROUPS]
    biases = [_attn_bias(rel_bias, gi, dil) for gi, dil in enumerate(dils)]
    offs = np.cumsum([0, ATTN_WIDTH, ATTN_WIDTH, ATTN_WIDTH, SSD_INNER, SSD_CONV_CH, 2 * SSD_HEADS, F_WIDTH, 3 * d])
    bf = lambda a: a.astype(BF16)

    for l in range(depth):
        wl = w_in[l]
        seg = lambda i: wl[:, offs[i]:offs[i + 1]]
        h = _rmsnorm(xf, mix_norm_g[l])
        gains = jnp.stack([attn_q_norm_g[l], attn_k_norm_g[l], jnp.ones_like(attn_q_norm_g[l])]).reshape(3, 1, ATTN_HD)
        os_, ls_ = [], []
        for gi, dil in enumerate(dils):
            cols = slice(gi * ATTN_OUT, (gi + 1) * ATTN_OUT)
            w3 = bf(jnp.concatenate([seg(0)[:, cols], seg(1)[:, cols], seg(2)[:, cols]], axis=1))
            qkv = _qkv_group(h, w3, gains, dil, bsz, seq)
            o_g, lse_g = _attn_group(qkv, biases[gi], dil, bsz, seq)
            os_.append(o_g)
            ls_.append(lse_g)

        z = _matmul(h, bf(seg(3)), out_dtype=BF16, tn=1024)
        xbc = _matmul(h, bf(seg(4)), out_dtype=F32, tn=1024)
        dt_raw = _matmul(h, bf(_group_lane_cols(seg(5))), out_dtype=F32, tn=512)
        z2 = _matmul(h, bf(seg(6)), out_dtype=BF16, tn=768, epilogue="chandft", extra=tabs[0])
        gates = _matmul(h, bf(seg(7)), out_dtype=BF16, tn=1024, epilogue="sigmoid_bias", extra=gate_bias[l])

        conv_out = _conv_silu(xbc.reshape(bsz, seq, SSD_CONV_CH), conv_w[l], conv_b[l])
        y_ssd = _ssd(conv_out, dt_raw.reshape(bsz, seq, -1), z.reshape(bsz, seq, SSD_INNER),
                     _group_lanes(dt_bias[l]), _group_lanes(a_log[l]),
                     jnp.repeat(d_skip[l], SSD_HD).reshape(1, SSD_INNER), ssd_norm_g[l].reshape(1, SSD_INNER),
                     bsz, seq)

        f_re = _fourier_seq(z2, tabs, bsz, seq)

        x3 = _merge(xf.reshape(bsz, seq, d), os_, ls_, dils, y_ssd, f_re, gates.reshape(bsz, seq, 3 * d),
                    bf(w_branch_attn[l]), bf(w_branch_ssd[l]), bf(w_branch_fourier[l]), bf(w_mix_out[l]), bsz, seq)

        hd_m = d // MEM_HEADS
        hm = _rmsnorm(memf, mem_norm_g[l])
        km = _matmul(hm, bf(w_xk[l]), out_dtype=BF16, tn=512, epilogue="headnorm",
                     extra=xattn_k_norm_g[l], head_dim=hd_m)
        vm = _matmul(hm, bf(w_xv[l]), out_dtype=BF16, tn=512)
        x3 = _xattn(x3, xattn_norm_g[l], bf(w_xq[l]), xattn_q_norm_g[l], km.reshape(bsz, -1, d),
                    vm.reshape(bsz, -1, d), bf(w_xo[l]))

        xf = _ffn(x3.reshape(m, d), ffn_norm_g[l], bf(w_ffn_gate[l]), bf(w_ffn_up[l]), bf(w_ffn_down[l]))

    return xf.reshape(bsz, seq, d)
```

```python
import functools
import math

import numpy as np
import jax
import jax.numpy as jnp
from jax import lax
from jax.experimental import pallas as pl
from jax.experimental.pallas import tpu as pltpu

F32 = jnp.float32
BF16 = jnp.bfloat16

NORM_EPS = 1e-6
MASK_VALUE = -1e30

ATTN_GROUPS = ((128, 1), (512, 4), (2048, 16))
ATTN_HPG = 4
ATTN_HD = 128
ATTN_HEADS = ATTN_HPG * len(ATTN_GROUPS)
ATTN_WIDTH = ATTN_HEADS * ATTN_HD
ATTN_OUT = ATTN_HPG * ATTN_HD
ATTN_HALF = 64
ATTN_TQ = 2 * ATTN_HALF
ATTN_TK = ATTN_TQ + 2 * ATTN_HALF
NUM_BUCKETS = 32
MAX_DISTANCE = 1024

SSD_HEADS = 32
SSD_HD = 64
SSD_GROUPS = 4
SSD_HPG = SSD_HEADS // SSD_GROUPS
SSD_STATE = 128
SSD_INNER = SSD_HEADS * SSD_HD
SSD_GW = SSD_INNER // SSD_GROUPS
SSD_CONV = 7
SSD_CHUNK = 128
SSD_CONV_CH = SSD_INNER + 2 * SSD_GROUPS * SSD_STATE
LANES = 128
BF16_ROWS = 16

F_GROUPS = 6
F_GD = 256
F_WIDTH = F_GROUPS * F_GD
F_N2 = 128

MEM_HEADS = 4

VMEM_LIMIT = 56 * 1024 * 1024


def _cparams(sem):
    return pltpu.CompilerParams(dimension_semantics=sem, vmem_limit_bytes=VMEM_LIMIT)


def _silu(x):
    return x * jax.nn.sigmoid(x)


def _rms(x, g):
    ms = jnp.mean(x * x, axis=-1, keepdims=True)
    return x * lax.rsqrt(ms + NORM_EPS) * g


def _get_rows(ref3, start=0, size=None, stride=1):
    idx = slice(None) if size is None else pl.ds(start, size, stride=stride)
    return jnp.concatenate([ref3[c, idx, :] for c in range(ref3.shape[0])], axis=1)


def _put_rows(ref3, val, start=0, stride=1):
    size = val.shape[0]
    idx = slice(None) if (stride == 1 and size == ref3.shape[1]) else pl.ds(start, size, stride=stride)
    for c in range(ref3.shape[0]):
        ref3[c, idx, :] = val[:, c * LANES:(c + 1) * LANES]


def _rmsnorm_kernel(x_ref, g_ref, o_ref):
    o_ref[...] = _rms(x_ref[...], g_ref[...]).astype(o_ref.dtype)


def _rmsnorm(x, g, *, tm=1024):
    m, k = x.shape
    tm = min(tm, m)
    return pl.pallas_call(
        _rmsnorm_kernel,
        grid=(m // tm,),
        in_specs=[pl.BlockSpec((tm, k), lambda i: (i, 0)), pl.BlockSpec((1, k), lambda i: (0, 0))],
        out_specs=pl.BlockSpec((tm, k), lambda i: (i, 0)),
        out_shape=jax.ShapeDtypeStruct((m, k), BF16),
        compiler_params=_cparams(("parallel",)),
        name="rmsnorm",
    )(x, g.reshape(1, k))


def _mm_kernel(*refs, epilogue, head_dim):
    h_ref, w_ref = refs[:2]
    acc = jnp.dot(h_ref[...], w_ref[...], preferred_element_type=F32)
    tn = acc.shape[1]
    if epilogue == "headnorm":
        hg_ref, o_ref = refs[2:]
        for hh in range(tn // head_dim):
            sl = slice(hh * head_dim, (hh + 1) * head_dim)
            o_ref[:, sl] = _rms(acc[:, sl], hg_ref[...]).astype(o_ref.dtype)
    elif epilogue == "sigmoid_bias":
        b_ref, o_ref = refs[2:]
        o_ref[...] = jax.nn.sigmoid(acc + b_ref[...]).astype(o_ref.dtype)
    elif epilogue == "chandft":
        dft_ref, o_ref = refs[2:]
        a16 = acc.astype(BF16)
        for gg in range(tn // F_GD):
            sl = slice(gg * F_GD, (gg + 1) * F_GD)
            r = jnp.dot(a16[:, sl], dft_ref[...], preferred_element_type=F32)
            o_ref[0, :, sl] = r[:, :F_GD].astype(o_ref.dtype)
            o_ref[1, :, sl] = r[:, F_GD:].astype(o_ref.dtype)
    else:
        o_ref = refs[2]
        o_ref[...] = acc.astype(o_ref.dtype)


def _matmul(h, w, *, out_dtype, tn, tm=1024, epilogue="plain", extra=None, head_dim=None):
    m, k = h.shape
    n = w.shape[1]
    tm = min(tm, m)
    assert m % tm == 0 and n % tn == 0
    in_specs = [pl.BlockSpec((tm, k), lambda i, j: (i, 0)), pl.BlockSpec((k, tn), lambda i, j: (0, j))]
    args = [h, w]
    if epilogue == "headnorm":
        in_specs.append(pl.BlockSpec((1, head_dim), lambda i, j: (0, 0)))
        args.append(extra.reshape(1, head_dim))
    elif epilogue == "sigmoid_bias":
        in_specs.append(pl.BlockSpec((1, tn), lambda i, j: (0, j)))
        args.append(extra.reshape(1, n))
    elif epilogue == "chandft":
        in_specs.append(pl.BlockSpec(extra.shape, lambda i, j: (0, 0)))
        args.append(extra)
    if epilogue == "chandft":
        out_shape = jax.ShapeDtypeStruct((2, m, n), out_dtype)
        out_spec = pl.BlockSpec((2, tm, tn), lambda i, j: (0, i, j))
    else:
        out_shape = jax.ShapeDtypeStruct((m, n), out_dtype)
        out_spec = pl.BlockSpec((tm, tn), lambda i, j: (i, j))
    return pl.pallas_call(
        functools.partial(_mm_kernel, epilogue=epilogue, head_dim=head_dim),
        grid=(m // tm, n // tn),
        in_specs=in_specs,
        out_specs=out_spec,
        out_shape=out_shape,
        compiler_params=_cparams(("parallel", "parallel")),
        name="matmul_" + epilogue,
    )(*args)


def _qkv_kernel(h_ref, w_ref, hg_ref, o_ref, *scratch, dil):
    j = pl.program_id(1)
    acc = jnp.dot(h_ref[...], w_ref[...], preferred_element_type=F32)
    tm, width = acc.shape
    rows = tm // dil
    nh = width // ATTN_HD
    if dil > 1:
        scr = scratch[0]
        for hh in range(nh):
            scr[hh] = acc[:, hh * ATTN_HD:(hh + 1) * ATTN_HD]
    for r in range(dil):
        for hh in range(nh):
            if dil == 1:
                ph = acc[:, hh * ATTN_HD:(hh + 1) * ATTN_HD]
            else:
                ph = scr[hh, pl.ds(r, rows, stride=dil), :]
            val = jnp.where(j < 2, _rms(ph, hg_ref[...]), ph)
            c0 = r * width + hh * ATTN_HD
            o_ref[:, c0:c0 + ATTN_HD] = val.astype(o_ref.dtype)


def _qkv_group(h, w3, gains, dil, bsz, seq, *, tm=1024):
    m, k = h.shape
    sub = seq // dil
    per_b = seq // tm
    rows = tm // dil
    assert seq % tm == 0 and rows % BF16_ROWS == 0
    return pl.pallas_call(
        functools.partial(_qkv_kernel, dil=dil),
        grid=(m // tm, 3),
        in_specs=[
            pl.BlockSpec((tm, k), lambda i, j: (i, 0)),
            pl.BlockSpec((k, ATTN_OUT), lambda i, j: (0, j)),
            pl.BlockSpec((None, 1, ATTN_HD), lambda i, j: (j, 0, 0)),
        ],
        out_specs=pl.BlockSpec((None, None, rows, dil * ATTN_OUT), lambda i, j: (j, i // per_b, i % per_b, 0)),
        out_shape=jax.ShapeDtypeStruct((3, bsz, sub, dil * ATTN_OUT), BF16),
        scratch_shapes=[pltpu.VMEM((ATTN_HPG, tm, ATTN_HD), F32)] if dil > 1 else [],
        compiler_params=_cparams(("parallel", "parallel")),
        name=f"qkv_proj_d{dil}",
    )(h, w3, gains)


def _t5_bucket_np(rel):
    half_b = NUM_BUCKETS // 2
    exact = half_b // 2
    dist = np.abs(rel)
    log_ratio = np.log(np.maximum(dist, 1) / exact) / np.log(MAX_DISTANCE / exact)
    far = np.minimum(exact + (log_ratio * (half_b - exact)).astype(np.int32), half_b - 1)
    return np.where(rel > 0, half_b, 0) + np.where(dist < exact, dist, far)


def _attn_bias(rel_bias, gi, dil):
    i = np.arange(ATTN_TQ)[:, None]
    j = np.arange(ATTN_TK)[None, :]
    rel = j - ATTN_HALF - i
    idx = _t5_bucket_np(rel * dil)
    onehot = jnp.asarray(np.eye(NUM_BUCKETS, dtype=np.float32)[idx])
    tab = rel_bias[:, gi * ATTN_HPG:(gi + 1) * ATTN_HPG].astype(F32)
    b = jnp.einsum("qkn,nh->hqk", onehot, tab, precision=lax.Precision.HIGHEST)
    return jnp.where(jnp.asarray(np.abs(rel) <= ATTN_HALF)[None], b, MASK_VALUE)


def _attn_kernel(q_ref, kp_ref, kc_ref, kn_ref, vp_ref, vc_ref, vn_ref, bias_ref, o_ref, lse_ref, kpad, vpad,
                 *, tile, sub):
    i = pl.program_id(2)
    hf = ATTN_HALF
    kpad[0:hf, :] = kp_ref[...]
    kpad[hf:hf + tile, :] = kc_ref[...]
    kpad[hf + tile:, :] = kn_ref[...]
    vpad[0:hf, :] = vp_ref[...]
    vpad[hf:hf + tile, :] = vc_ref[...]
    vpad[hf + tile:, :] = vn_ref[...]

    scale = ATTN_HD ** -0.5
    col = lax.broadcasted_iota(jnp.int32, (ATTN_TQ, ATTN_TK), 1)
    lane = lax.broadcasted_iota(jnp.int32, (ATTN_TQ, LANES), 1)

    def body(t, carry):
        s0 = pl.multiple_of(t * ATTN_TQ, ATTN_TQ)
        kpos = col + (i * tile + s0 - hf)
        valid = (kpos >= 0) & (kpos < sub)
        lse_tile = jnp.zeros((ATTN_TQ, LANES), F32)
        for hh in range(ATTN_HPG):
            sl = slice(hh * ATTN_HD, (hh + 1) * ATTN_HD)
            q = q_ref[pl.ds(s0, ATTN_TQ), sl]
            kw = kpad[pl.ds(s0, ATTN_TK), sl]
            vw = vpad[pl.ds(s0, ATTN_TK), sl]
            s = lax.dot_general(q, kw, (((1,), (1,)), ((), ())), preferred_element_type=F32) * scale + bias_ref[hh]
            s = jnp.where(valid, s, MASK_VALUE)
            m = jnp.max(s, axis=-1, keepdims=True)
            p = jnp.exp(s - m)
            l = jnp.sum(p, axis=-1, keepdims=True)
            o = jnp.dot(p.astype(BF16), vw, preferred_element_type=F32) / l
            o_ref[pl.ds(s0, ATTN_TQ), sl] = o.astype(o_ref.dtype)
            lse_tile = jnp.where(lane == hh, m + jnp.log(l), lse_tile)
        lse_ref[pl.ds(s0, ATTN_TQ), :] = lse_tile
        return carry

    lax.fori_loop(0, tile // ATTN_TQ, body, 0, unroll=4)


def _attn_group(qkv, bias, dil, bsz, seq, *, tile=1024):
    sub = seq // dil
    tile = min(tile, sub)
    nblk = sub // tile
    hb = tile // ATTN_HALF
    last_h = sub // ATTN_HALF - 1
    assert sub % tile == 0 and tile % ATTN_TQ == 0
    cur = lambda which: pl.BlockSpec((None, None, tile, ATTN_OUT), lambda b, r, i: (which, b, i, r))
    prev = lambda which: pl.BlockSpec((None, None, ATTN_HALF, ATTN_OUT),
                                      lambda b, r, i: (which, b, jnp.maximum(i * hb - 1, 0), r))
    nxt = lambda which: pl.BlockSpec((None, None, ATTN_HALF, ATTN_OUT),
                                     lambda b, r, i: (which, b, jnp.minimum((i + 1) * hb, last_h), r))
    return pl.pallas_call(
        functools.partial(_attn_kernel, tile=tile, sub=sub),
        grid=(bsz, dil, nblk),
        in_specs=[cur(0), prev(1), cur(1), nxt(1), prev(2), cur(2), nxt(2),
                  pl.BlockSpec((ATTN_HPG, ATTN_TQ, ATTN_TK), lambda b, r, i: (0, 0, 0))],
        out_specs=(pl.BlockSpec((None, tile, ATTN_OUT), lambda b, r, i: (b, i, r)),
                   pl.BlockSpec((None, tile, LANES), lambda b, r, i: (b, i, r))),
        out_shape=(jax.ShapeDtypeStruct((bsz, sub, dil * ATTN_OUT), BF16),
                   jax.ShapeDtypeStruct((bsz, sub, dil * LANES), F32)),
        scratch_shapes=[pltpu.VMEM((tile + 2 * ATTN_HALF, ATTN_OUT), BF16)] * 2,
        compiler_params=_cparams(("parallel", "parallel", "parallel")),
        name=f"dilated_attn_d{dil}",
    )(qkv, qkv, qkv, qkv, qkv, qkv, qkv, bias)


CONV_HALO = 8
CONV_RC = 64


def _conv_kernel(prev_ref, cur_ref, next_ref, w_ref, b_ref, o_ref, xp_ref, *, tm, nblk):
    i = pl.program_id(1)
    pad = SSD_CONV // 2
    xp_ref[0:CONV_HALO, :] = jnp.where(i > 0, prev_ref[...], 0.0)
    xp_ref[CONV_HALO:CONV_HALO + tm, :] = cur_ref[...]
    xp_ref[CONV_HALO + tm:, :] = jnp.where(i < nblk - 1, next_ref[...], 0.0)
    for c in range(tm // CONV_RC):
        r0 = c * CONV_RC
        acc = jnp.broadcast_to(b_ref[...], (CONV_RC, b_ref.shape[1]))
        for k in range(SSD_CONV):
            off = CONV_HALO + r0 + k - pad
            acc = acc + w_ref[k:k + 1, :] * xp_ref[off:off + CONV_RC, :]
        o_ref[r0:r0 + CONV_RC, :] = _silu(acc).astype(o_ref.dtype)


def _conv_silu(xbc, w, b, *, tm=512, tc=1024):
    bsz, seq, ch = xbc.shape
    nblk = seq // tm
    hb = tm // CONV_HALO
    return pl.pallas_call(
        functools.partial(_conv_kernel, tm=tm, nblk=nblk),
        grid=(bsz, nblk, ch // tc),
        in_specs=[
            pl.BlockSpec((None, CONV_HALO, tc), lambda bb, i, j: (bb, jnp.maximum(i * hb - 1, 0), j)),
            pl.BlockSpec((None, tm, tc), lambda bb, i, j: (bb, i, j)),
            pl.BlockSpec((None, CONV_HALO, tc), lambda bb, i, j: (bb, jnp.minimum((i + 1) * hb, seq // CONV_HALO - 1), j)),
            pl.BlockSpec((SSD_CONV, tc), lambda bb, i, j: (0, j)),
            pl.BlockSpec((1, tc), lambda bb, i, j: (0, j)),
        ],
        out_specs=pl.BlockSpec((None, tm, tc), lambda bb, i, j: (bb, i, j)),
        out_shape=jax.ShapeDtypeStruct((bsz, seq, ch), BF16),
        scratch_shapes=[pltpu.VMEM((tm + 2 * CONV_HALO, tc), F32)],
        compiler_params=_cparams(("parallel", "parallel", "parallel")),
        name="conv_silu",
    )(xbc, xbc, xbc, w, b.reshape(1, ch))


def _softplus(x):
    return jnp.maximum(x, 0.0) + jnp.log1p(jnp.exp(-jnp.abs(x)))


def _ssd_prep_kernel(dtr_ref, dtb_ref, alog_ref, ccol_ref, rt_ref, esc_ref, wst_ref, etot_ref, *, t):
    half = LANES // 2
    log2e = math.log2(math.e)
    lane = lax.broadcasted_iota(jnp.int32, (t, LANES), 1)
    fwd = lane < SSD_HEADS
    tri = (lax.broadcasted_iota(jnp.int32, (t, t), 1) <= lax.broadcasted_iota(jnp.int32, (t, t), 0)).astype(BF16)
    neg_a = jnp.exp(alog_ref[...])
    for ci in range(dtr_ref.shape[0] // t):
        rs = slice(ci * t, (ci + 1) * t)
        dt = _softplus(dtr_ref[rs, :] + dtb_ref[...])
        la = jnp.where(lane < 2 * SSD_HEADS, -(dt * neg_a), 0.0)
        hi = la.astype(BF16)
        r1 = la - hi.astype(F32)
        mid = r1.astype(BF16)
        lo = (r1 - mid.astype(F32)).astype(BF16)
        packed = (hi.astype(F32) + pltpu.roll(mid.astype(F32), half, 1)).astype(BF16)
        res = jnp.dot(tri, jnp.concatenate([packed, lo], axis=1), preferred_element_type=F32)
        a0 = res[:, :LANES]
        acs = a0 + pltpu.roll(a0, half, 1) + res[:, LANES:]
        exb = acs - la
        ldt = jnp.log(dt)
        tot = acs[t - 1:t, :]
        ccol_ref[rs, :] = jnp.where(fwd, acs, exb) * log2e
        rt_ref[ci] = (jnp.where(fwd, acs - ldt, exb + ldt) * log2e).T
        esc_ref[rs, :] = jnp.exp(jnp.where(fwd, acs, tot - exb))
        wst_ref[rs, :] = jnp.exp(jnp.where(fwd, tot - acs, exb)) * dt
        etot_ref[ci] = jnp.broadcast_to(jnp.exp(tot), etot_ref.shape[1:])


def _split2(v):
    hi = v.astype(BF16)
    return jnp.concatenate([hi, (v - hi.astype(F32)).astype(BF16)], axis=1)


def _head_expand_tables():
    j = np.arange(2 * LANES)[None, :, None] % LANES
    g = np.arange(SSD_GROUPS)[:, None, None]
    c = np.arange(SSD_GW)[None, None, :]
    fwd = (j == g * SSD_HPG + c // SSD_HD)
    bwd = (j == SSD_HEADS + g * SSD_HPG + c // SSD_HD)
    return jnp.asarray(fwd, BF16), jnp.asarray(bwd, BF16)


def _ssd_bwd_state_kernel(xs_ref, b_ref, wst_ref, etot_ref, selb_ref, gin_ref, g_sc):
    @pl.when(pl.program_id(1) == 0)
    def _():
        g_sc[...] = jnp.zeros(g_sc.shape, F32)

    wst2 = _split2(wst_ref[...])
    etot2 = _split2(etot_ref[...])
    gin_ref[...] = g_sc[...].astype(gin_ref.dtype)
    for g in range(SSD_GROUPS):
        gs = slice(g * SSD_GW, (g + 1) * SSD_GW)
        bm_t = b_ref[:, g * SSD_STATE:(g + 1) * SSD_STATE].T
        wts = jnp.dot(wst2, selb_ref[g], preferred_element_type=F32)
        dec = jnp.dot(etot2, selb_ref[g], preferred_element_type=F32)[0:1]
        xw = (xs_ref[:, gs].astype(F32) * wts).astype(BF16)
        st = jnp.dot(bm_t, xw, preferred_element_type=F32)
        g_sc[:, gs] = g_sc[:, gs] * dec + st


def _ssd_main_kernel(xs_ref, b_ref, c_ref, ccol_ref, rt_ref, esc_ref, wst_ref, etot_ref, z_ref, gin_ref,
                     dsk_ref, ng_ref, self_ref, selb_ref, y_ref, h_sc, y_sc):
    @pl.when(pl.program_id(1) == 0)
    def _():
        h_sc[...] = jnp.zeros(h_sc.shape, F32)

    t = xs_ref.shape[0]
    ccol = ccol_ref[...]
    r_t = rt_ref[...]
    esc2 = _split2(esc_ref[...])
    wst2 = _split2(wst_ref[...])
    etot2 = _split2(etot_ref[...])

    li = lax.broadcasted_iota(jnp.int32, (t, t), 0)
    si = lax.broadcasted_iota(jnp.int32, (t, t), 1)
    mask_f = jnp.where(li >= si, 0.0, MASK_VALUE)
    mask_b = jnp.where(si >= li, 0.0, MASK_VALUE)
    lo_half = lax.broadcasted_iota(jnp.int32, (t, LANES), 1) < SSD_HD

    for g in range(SSD_GROUPS):
        gs = slice(g * SSD_GW, (g + 1) * SSD_GW)
        bm = b_ref[:, g * SSD_STATE:(g + 1) * SSD_STATE]
        cm = c_ref[:, g * SSD_STATE:(g + 1) * SSD_STATE]
        cb = lax.dot_general(cm, bm, (((1,), (1,)), ((), ())), preferred_element_type=F32)
        hf = h_sc[:, gs]
        yf_all = jnp.dot(cm, hf.astype(BF16), preferred_element_type=F32)
        yb_all = jnp.dot(cm, gin_ref[:, gs], preferred_element_type=F32)
        e_in = jnp.dot(esc2, self_ref[g], preferred_element_type=F32)
        e_out = jnp.dot(esc2, selb_ref[g], preferred_element_type=F32)
        w_state = jnp.dot(wst2, self_ref[g], preferred_element_type=F32)
        dec = jnp.dot(etot2, self_ref[g], preferred_element_type=F32)[0:1]
        xw = []
        for k in range(SSD_HPG // 2):
            ea = g * SSD_HPG + 2 * k
            sl = slice(g * SSD_GW + k * LANES, g * SSD_GW + (k + 1) * LANES)
            ks = slice(k * LANES, (k + 1) * LANES)
            xp = xs_ref[:, sl]
            mms = []
            for e in (ea, ea + 1):
                eb = SSD_HEADS + e
                w = (jnp.exp2(ccol[:, e:e + 1] - r_t[e:e + 1, :] + mask_f)
                     + jnp.exp2(r_t[eb:eb + 1, :] - ccol[:, eb:eb + 1] + mask_b))
                mms.append((cb * w).astype(BF16))
            zero = jnp.zeros_like(xp)
            xx = jnp.concatenate([jnp.where(lo_half, xp, zero), jnp.where(lo_half, zero, xp)], axis=0)
            ypair = jnp.dot(jnp.concatenate(mms, axis=1), xx, preferred_element_type=F32)
            xf = xp.astype(F32)
            y_sc[:, ks] = (ypair + yf_all[:, ks] * e_in[:, ks] + yb_all[:, ks] * e_out[:, ks]
                           + xf * dsk_ref[:, sl])
            xw.append((xf * w_state[:, ks]).astype(BF16))
        st = jnp.dot(bm.T, jnp.concatenate(xw, axis=1), preferred_element_type=F32)
        h_sc[:, gs] = hf * dec + st
        yy = y_sc[...] * _silu(z_ref[:, gs].astype(F32))
        y_ref[:, gs] = _rms(yy, ng_ref[:, gs]).astype(y_ref.dtype)


def _ssd(conv_out, dt_raw, z, dtb, alog, dskip, norm_g, bsz, seq):
    t = SSD_CHUNK
    nc = seq // t
    assert seq % t == 0
    gn = SSD_GROUPS * SSD_STATE
    b_blk = SSD_INNER // gn
    c_blk = b_blk + 1
    cpp = min(8, nc)
    assert nc % cpp == 0
    sub8 = 8
    const_spec = pl.BlockSpec((1, LANES), lambda b, i: (0, 0))
    rows_spec = pl.BlockSpec((None, cpp * t, LANES), lambda b, i: (b, i, 0))
    row_arr = jax.ShapeDtypeStruct((bsz, seq, LANES), F32)
    ccol, rt, esc, wst, etot = pl.pallas_call(
        functools.partial(_ssd_prep_kernel, t=t),
        grid=(bsz, nc // cpp),
        in_specs=[rows_spec, const_spec, const_spec],
        out_specs=(rows_spec, pl.BlockSpec((None, cpp, LANES, t), lambda b, i: (b, i, 0, 0)), rows_spec, rows_spec,
                   pl.BlockSpec((None, cpp, sub8, LANES), lambda b, i: (b, i, 0, 0))),
        out_shape=(row_arr, jax.ShapeDtypeStruct((bsz, nc, LANES, t), F32), row_arr, row_arr,
                   jax.ShapeDtypeStruct((bsz, nc, sub8, LANES), F32)),
        compiler_params=_cparams(("parallel", "parallel")),
        name="ssd_decay_terms",
    )(dt_raw, dtb, alog)

    rev = lambda c: nc - 1 - c
    sel_f, sel_b = _head_expand_tables()
    sel_spec = pl.BlockSpec(sel_f.shape, lambda b, c: (0, 0, 0))
    gin = pl.pallas_call(
        _ssd_bwd_state_kernel,
        grid=(bsz, nc),
        in_specs=[
            pl.BlockSpec((None, t, SSD_INNER), lambda b, c: (b, rev(c), 0)),
            pl.BlockSpec((None, t, gn), lambda b, c: (b, rev(c), b_blk)),
            pl.BlockSpec((None, t, LANES), lambda b, c: (b, rev(c), 0)),
            pl.BlockSpec((None, None, sub8, LANES), lambda b, c: (b, rev(c), 0, 0)),
            sel_spec,
        ],
        out_specs=pl.BlockSpec((None, None, SSD_STATE, SSD_INNER), lambda b, c: (b, rev(c), 0, 0)),
        out_shape=jax.ShapeDtypeStruct((bsz, nc, SSD_STATE, SSD_INNER), BF16),
        scratch_shapes=[pltpu.VMEM((SSD_STATE, SSD_INNER), F32)],
        compiler_params=_cparams(("parallel", "arbitrary")),
        name="ssd_bwd_states",
    )(conv_out, conv_out, wst, etot, sel_b)

    wide_spec = pl.BlockSpec((1, SSD_INNER), lambda b, c: (0, 0))
    tok_spec = pl.BlockSpec((None, t, LANES), lambda b, c: (b, c, 0))
    return pl.pallas_call(
        _ssd_main_kernel,
        grid=(bsz, nc),
        in_specs=[
            pl.BlockSpec((None, t, SSD_INNER), lambda b, c: (b, c, 0)),
            pl.BlockSpec((None, t, gn), lambda b, c: (b, c, b_blk)),
            pl.BlockSpec((None, t, gn), lambda b, c: (b, c, c_blk)),
            tok_spec,
            pl.BlockSpec((None, None, LANES, t), lambda b, c: (b, c, 0, 0)),
            tok_spec, tok_spec,
            pl.BlockSpec((None, None, sub8, LANES), lambda b, c: (b, c, 0, 0)),
            pl.BlockSpec((None, t, SSD_INNER), lambda b, c: (b, c, 0)),
            pl.BlockSpec((None, None, SSD_STATE, SSD_INNER), lambda b, c: (b, c, 0, 0)),
            wide_spec, wide_spec, sel_spec, sel_spec,
        ],
        out_specs=pl.BlockSpec((None, t, SSD_INNER), lambda b, c: (b, c, 0)),
        out_shape=jax.ShapeDtypeStruct((bsz, seq, SSD_INNER), BF16),
        scratch_shapes=[pltpu.VMEM((SSD_STATE, SSD_INNER), F32), pltpu.VMEM((t, SSD_GW), F32)],
        compiler_params=_cparams(("parallel", "arbitrary")),
        name="ssd_main",
    )(conv_out, conv_out, conv_out, ccol, rt, esc, wst, etot, z, gin, dskip, norm_g, sel_f, sel_b)


def _dft_tables(seq):
    n1, n2 = seq // F_N2, F_N2
    c = np.arange(F_GD)
    ang = 2 * np.pi * np.outer(c, c) / F_GD
    chan = np.concatenate([np.cos(ang), -np.sin(ang)], axis=1) / math.sqrt(F_GD)
    k1 = np.arange(n1)
    a1 = 2 * np.pi * np.outer(k1, k1) / n1
    stage_a = np.block([[np.cos(a1), np.sin(a1)], [-np.sin(a1), np.cos(a1)]])
    s2 = np.arange(n2)
    at = 2 * np.pi * np.outer(s2, k1) / seq
    tw_c, tw_s = np.cos(at)[..., None], np.sin(at)[..., None]
    a2 = 2 * np.pi * np.outer(s2, s2) / n2
    stage_b = np.concatenate([np.cos(a2), np.sin(a2)], axis=1)
    return (jnp.asarray(chan, BF16), jnp.asarray(stage_a, BF16), jnp.asarray(tw_c, F32),
            jnp.asarray(tw_s, F32), jnp.asarray(stage_b, BF16))


def _fft_a_kernel(z_ref, ma_ref, twc_ref, tws_ref, o_ref, zs, ys, *, n1, rb):
    ct = z_ref.shape[-1]
    _put_rows(zs, z_ref[...].astype(F32).reshape(2 * n1 * rb, ct))
    for a in range(rb):
        zz = _get_rows(zs, a, 2 * n1, rb).astype(BF16)
        y = jnp.dot(ma_ref[...], zz, preferred_element_type=F32)
        yr, yi = y[:n1], y[n1:]
        tc, ts = twc_ref[a], tws_ref[a]
        _put_rows(ys, yr * tc + yi * ts, a, rb)
        _put_rows(ys, yi * tc - yr * ts, n1 * rb + a, rb)
    o_ref[...] = _get_rows(ys).reshape(2, n1, rb, ct).astype(o_ref.dtype)


def _fft_b_kernel(y_ref, mb_ref, o_ref, os_ref, *, n2, kb, scale):
    ct = y_ref.shape[-1]
    for kk in range(kb):
        yy = y_ref[:, kk].reshape(2 * n2, ct)
        _put_rows(os_ref, jnp.dot(mb_ref[...], yy, preferred_element_type=F32) * scale, kk, kb)
    o_ref[...] = _get_rows(os_ref).reshape(n2, kb, ct).astype(o_ref.dtype)


def _fourier_seq(z2, tabs, bsz, seq, *, rb=BF16_ROWS, kb=BF16_ROWS, ct=768):
    _, ma, twc, tws, mb = tabs
    n1, n2 = seq // F_N2, F_N2
    w = z2.shape[-1]
    nct = w // ct
    za = z2.reshape(2, bsz, n1, n2, w)
    ab_spec = pl.BlockSpec((2, None, n1, rb, ct), lambda b, a, j: (0, b, 0, a, j))
    ya = pl.pallas_call(
        functools.partial(_fft_a_kernel, n1=n1, rb=rb),
        grid=(bsz, n2 // rb, nct),
        in_specs=[
            ab_spec,
            pl.BlockSpec((2 * n1, 2 * n1), lambda b, a, j: (0, 0)),
            pl.BlockSpec((rb, n1, 1), lambda b, a, j: (a, 0, 0)),
            pl.BlockSpec((rb, n1, 1), lambda b, a, j: (a, 0, 0)),
        ],
        out_specs=ab_spec,
        out_shape=jax.ShapeDtypeStruct((2, bsz, n1, n2, w), BF16),
        scratch_shapes=[pltpu.VMEM((ct // LANES, 2 * n1 * rb, LANES), F32)] * 2,
        compiler_params=_cparams(("parallel", "parallel", "parallel")),
        name="fourier_stage_a",
    )(za, ma, twc, tws)
    out = pl.pallas_call(
        functools.partial(_fft_b_kernel, n2=n2, kb=kb, scale=1.0 / math.sqrt(seq)),
        grid=(bsz, n1 // kb, nct),
        in_specs=[
            pl.BlockSpec((2, None, kb, n2, ct), lambda b, k, j: (0, b, k, 0, j)),
            pl.BlockSpec((n2, 2 * n2), lambda b, k, j: (0, 0)),
        ],
        out_specs=pl.BlockSpec((None, n2, kb, ct), lambda b, k, j: (b, 0, k, j)),
        out_shape=jax.ShapeDtypeStruct((bsz, n2, n1, w), BF16),
        scratch_shapes=[pltpu.VMEM((ct // LANES, n2 * kb, LANES), F32)],
        compiler_params=_cparams(("parallel", "parallel", "parallel")),
        name="fourier_stage_b",
    )(ya, mb)
    return out.reshape(bsz, seq, w)


def _merge_kernel(*refs, dils):
    ng = len(dils)
    x_ref = refs[0]
    o_refs = refs[1:1 + ng]
    l_refs = refs[1 + ng:1 + 2 * ng]
    y_ref, f_ref, gt_ref, wa_ref, ws_ref, wf_ref, wo_ref, out_ref, o_sc, l_sc = refs[1 + 2 * ng:]
    tm, d = x_ref.shape

    for gi, dil in enumerate(dils):
        rows = tm // dil
        idx = lambda r: slice(None) if dil == 1 else pl.ds(r, rows, stride=dil)
        for r in range(dil):
            l_sc[gi, idx(r), :] = l_refs[gi][:, r * LANES:(r + 1) * LANES]
            for hh in range(ATTN_HPG):
                c0 = r * ATTN_OUT + hh * ATTN_HD
                o_sc[gi * ATTN_HPG + hh, idx(r), :] = o_refs[gi][:, c0:c0 + ATTN_HD].astype(F32)

    ls = [l_sc[gi] for gi in range(ng)]
    mx = functools.reduce(jnp.maximum, ls)
    ws = [jnp.exp(v - mx) for v in ls]
    inv = 1.0 / functools.reduce(lambda a, b: a + b, ws)
    heads = []
    for hh in range(ATTN_HPG):
        acc = None
        for gi in range(ng):
            alpha = (ws[gi] * inv)[:, hh:hh + 1]
            term = alpha * o_sc[gi * ATTN_HPG + hh]
            acc = term if acc is None else acc + term
        heads.append(acc.astype(BF16))
    o_attn = jnp.concatenate(heads, axis=1)

    ya = jnp.dot(o_attn, wa_ref[...], preferred_element_type=F32)
    ys = jnp.dot(y_ref[...], ws_ref[...], preferred_element_type=F32)
    yf = jnp.dot(f_ref[...], wf_ref[...], preferred_element_type=F32)
    gates = gt_ref[...].astype(F32)
    merged = gates[:, :d] * ya + gates[:, d:2 * d] * ys + gates[:, 2 * d:] * yf
    out_ref[...] = x_ref[...] + jnp.dot(merged.astype(BF16), wo_ref[...], preferred_element_type=F32)


def _merge(x, os_, ls_, dils, y, f, gates, wa, ws, wf, wo, bsz, seq, *, tm=256):
    d = x.shape[-1]
    assert all((tm // dil) % BF16_ROWS == 0 for dil in dils)
    tok = lambda width: pl.BlockSpec((None, tm, width), lambda b, i: (b, i, 0))
    full = lambda a: pl.BlockSpec(a.shape, lambda b, i: (0, 0))
    ng = len(dils)
    in_specs = [tok(d)]
    in_specs += [pl.BlockSpec((None, tm // dil, dil * ATTN_OUT), lambda b, i: (b, i, 0)) for dil in dils]
    in_specs += [pl.BlockSpec((None, tm // dil, dil * LANES), lambda b, i: (b, i, 0)) for dil in dils]
    in_specs += [tok(y.shape[-1]), tok(f.shape[-1]), tok(gates.shape[-1]), full(wa), full(ws), full(wf), full(wo)]
    return pl.pallas_call(
        functools.partial(_merge_kernel, dils=tuple(dils)),
        grid=(bsz, seq // tm),
        in_specs=in_specs,
        out_specs=tok(d),
        out_shape=jax.ShapeDtypeStruct((bsz, seq, d), F32),
        scratch_shapes=[pltpu.VMEM((ng * ATTN_HPG, tm, ATTN_HD), F32), pltpu.VMEM((ng, tm, LANES), F32)],
        compiler_params=_cparams(("parallel", "parallel")),
        name="branch_merge",
    )(x, *os_, *ls_, y, f, gates, wa, ws, wf, wo)


def _xattn_kernel(x_ref, g_ref, wq_ref, qg_ref, k_ref, v_ref, wo_ref, out_ref, o_sc):
    x = x_ref[...]
    d = x.shape[1]
    hd = d // MEM_HEADS
    h = _rms(x, g_ref[...]).astype(BF16)
    q = jnp.dot(h, wq_ref[...], preferred_element_type=F32)
    scale = hd ** -0.5
    for hh in range(MEM_HEADS):
        sl = slice(hh * hd, (hh + 1) * hd)
        qn = _rms(q[:, sl], qg_ref[...]).astype(BF16)
        s = lax.dot_general(qn, k_ref[:, sl], (((1,), (1,)), ((), ())), preferred_element_type=F32) * scale
        m = jnp.max(s, axis=-1, keepdims=True)
        p = jnp.exp(s - m)
        l = jnp.sum(p, axis=-1, keepdims=True)
        o_sc[:, sl] = jnp.dot(p.astype(BF16), v_ref[:, sl], preferred_element_type=F32) / l
    out_ref[...] = x + jnp.dot(o_sc[...].astype(BF16), wo_ref[...], preferred_element_type=F32)


def _xattn(x, g, wq, qg, k, v, wo, *, tm=512):
    bsz, seq, d = x.shape
    mt = k.shape[1]
    full = lambda a: pl.BlockSpec(a.shape, lambda b, i: (0, 0))
    g = g.reshape(1, d)
    qg = qg.reshape(1, -1)
    return pl.pallas_call(
        _xattn_kernel,
        grid=(bsz, seq // tm),
        in_specs=[
            pl.BlockSpec((None, tm, d), lambda b, i: (b, i, 0)),
            full(g), full(wq), full(qg),
            pl.BlockSpec((None, mt, d), lambda b, i: (b, 0, 0)),
            pl.BlockSpec((None, mt, d), lambda b, i: (b, 0, 0)),
            full(wo),
        ],
        out_specs=pl.BlockSpec((None, tm, d), lambda b, i: (b, i, 0)),
        out_shape=jax.ShapeDtypeStruct((bsz, seq, d), F32),
        scratch_shapes=[pltpu.VMEM((tm, d), F32)],
        compiler_params=_cparams(("parallel", "parallel")),
        name="mem_xattn",
    )(x, g, wq, qg, k, v, wo)


def _ffn_kernel(x_ref, g_ref, wg_ref, wu_ref, wd_ref, out_ref, *, tf):
    x = x_ref[...]
    h = _rms(x, g_ref[...]).astype(BF16)
    acc = x
    for c in range(wg_ref.shape[1] // tf):
        sl = slice(c * tf, (c + 1) * tf)
        gt = jnp.dot(h, wg_ref[:, sl], preferred_element_type=F32)
        up = jnp.dot(h, wu_ref[:, sl], preferred_element_type=F32)
        a = (_silu(gt) * up).astype(BF16)
        acc = acc + jnp.dot(a, wd_ref[sl, :], preferred_element_type=F32)
    out_ref[...] = acc


def _ffn(x, g, wg, wu, wd, *, tm=256):
    m, d = x.shape
    dff = wg.shape[1]
    tf = dff // 2 if (dff // 2) % LANES == 0 else dff
    full = lambda a: pl.BlockSpec(a.shape, lambda i: (0, 0))
    g = g.reshape(1, d)
    return pl.pallas_call(
        functools.partial(_ffn_kernel, tf=tf),
        grid=(m // tm,),
        in_specs=[pl.BlockSpec((tm, d), lambda i: (i, 0)), full(g), full(wg), full(wu), full(wd)],
        out_specs=pl.BlockSpec((tm, d), lambda i: (i, 0)),
        out_shape=jax.ShapeDtypeStruct((m, d), F32),
        compiler_params=_cparams(("parallel",)),
        name="swiglu_ffn",
    )(x, g, wg, wu, wd)


def _pad_lanes(a):
    a = a.reshape(-1, 2 * SSD_HEADS)
    return jnp.pad(a, ((0, 0), (0, LANES - 2 * SSD_HEADS)))


def kernel(x, mem, rel_bias, mix_norm_g, w_in, gate_bias, attn_q_norm_g, attn_k_norm_g, conv_w, conv_b, dt_bias, a_log, d_skip, ssd_norm_g, w_branch_attn, w_branch_ssd, w_branch_fourier, w_mix_out, xattn_norm_g, mem_norm_g, w_xq, w_xk, w_xv, xattn_q_norm_g, xattn_k_norm_g, w_xo, ffn_norm_g, w_ffn_gate, w_ffn_up, w_ffn_down):
    bsz, seq, d = x.shape
    depth = w_in.shape[0]
    m = bsz * seq
    xf = x.reshape(m, d)
    memf = mem.reshape(bsz * mem.shape[1], d)
    tabs = _dft_tables(seq)
    dils = [dil for _, dil in ATTN_GROUPS]
    biases = [_attn_bias(rel_bias, gi, dil) for gi, dil in enumerate(dils)]
    offs = np.cumsum([0, ATTN_WIDTH, ATTN_WIDTH, ATTN_WIDTH, SSD_INNER, SSD_CONV_CH, 2 * SSD_HEADS, F_WIDTH, 3 * d])
    bf = lambda a: a.astype(BF16)

    for l in range(depth):
        wl = w_in[l]
        seg = lambda i: wl[:, offs[i]:offs[i + 1]]
        h = _rmsnorm(xf, mix_norm_g[l])
        gains = jnp.stack([attn_q_norm_g[l], attn_k_norm_g[l], jnp.ones_like(attn_q_norm_g[l])]).reshape(3, 1, ATTN_HD)
        os_, ls_ = [], []
        for gi, dil in enumerate(dils):
            cols = slice(gi * ATTN_OUT, (gi + 1) * ATTN_OUT)
            w3 = bf(jnp.concatenate([seg(0)[:, cols], seg(1)[:, cols], seg(2)[:, cols]], axis=1))
            qkv = _qkv_group(h, w3, gains, dil, bsz, seq)
            o_g, lse_g = _attn_group(qkv, biases[gi], dil, bsz, seq)
            os_.append(o_g)
            ls_.append(lse_g)

        z = _matmul(h, bf(seg(3)), out_dtype=BF16, tn=1024)
        xbc = _matmul(h, bf(seg(4)), out_dtype=F32, tn=1024)
        dt_raw = _matmul(h, bf(_pad_lanes(seg(5))), out_dtype=F32, tn=LANES)
        z2 = _matmul(h, bf(seg(6)), out_dtype=BF16, tn=768, epilogue="chandft", extra=tabs[0])
        gates = _matmul(h, bf(seg(7)), out_dtype=BF16, tn=1024, epilogue="sigmoid_bias", extra=gate_bias[l])

        conv_out = _conv_silu(xbc.reshape(bsz, seq, SSD_CONV_CH), conv_w[l], conv_b[l])
        y_ssd = _ssd(conv_out, dt_raw.reshape(bsz, seq, -1), z.reshape(bsz, seq, SSD_INNER),
                     _pad_lanes(dt_bias[l]), _pad_lanes(a_log[l]),
                     jnp.repeat(d_skip[l], SSD_HD).reshape(1, SSD_INNER), ssd_norm_g[l].reshape(1, SSD_INNER),
                     bsz, seq)

        f_re = _fourier_seq(z2, tabs, bsz, seq)

        x3 = _merge(xf.reshape(bsz, seq, d), os_, ls_, dils, y_ssd, f_re, gates.reshape(bsz, seq, 3 * d),
                    bf(w_branch_attn[l]), bf(w_branch_ssd[l]), bf(w_branch_fourier[l]), bf(w_mix_out[l]), bsz, seq)

        hd_m = d // MEM_HEADS
        hm = _rmsnorm(memf, mem_norm_g[l])
        km = _matmul(hm, bf(w_xk[l]), out_dtype=BF16, tn=512, epilogue="headnorm",
                     extra=xattn_k_norm_g[l], head_dim=hd_m)
        vm = _matmul(hm, bf(w_xv[l]), out_dtype=BF16, tn=512)
        x3 = _xattn(x3, xattn_norm_g[l], bf(w_xq[l]), xattn_q_norm_g[l], km.reshape(bsz, -1, d),
                    vm.reshape(bsz, -1, d), bf(w_xo[l]))

        xf = _ffn(x3.reshape(m, d), ffn_norm_g[l], bf(w_ffn_gate[l]), bf(w_ffn_up[l]), bf(w_ffn_down[l]))

    return xf.reshape(bsz, seq, d)
```

```python
import functools
import math

import numpy as np
import jax
import jax.numpy as jnp
from jax import lax
from jax.experimental import pallas as pl
from jax.experimental.pallas import tpu as pltpu

F32 = jnp.float32
BF16 = jnp.bfloat16

NORM_EPS = 1e-6
MASK_VALUE = -1e30

ATTN_GROUPS = ((128, 1), (512, 4), (2048, 16))
ATTN_HPG = 4
ATTN_HD = 128
ATTN_HEADS = ATTN_HPG * len(ATTN_GROUPS)
ATTN_WIDTH = ATTN_HEADS * ATTN_HD
ATTN_OUT = ATTN_HPG * ATTN_HD
ATTN_HALF = 64
ATTN_TQ = 2 * ATTN_HALF
ATTN_TK = ATTN_TQ + 2 * ATTN_HALF
NUM_BUCKETS = 32
MAX_DISTANCE = 1024

SSD_HEADS = 32
SSD_HD = 64
SSD_GROUPS = 4
SSD_HPG = SSD_HEADS // SSD_GROUPS
SSD_STATE = 128
SSD_INNER = SSD_HEADS * SSD_HD
SSD_GW = SSD_INNER // SSD_GROUPS
SSD_CONV = 7
SSD_CHUNK = 128
SSD_CONV_CH = SSD_INNER + 2 * SSD_GROUPS * SSD_STATE
LANES = 128
BF16_ROWS = 16

F_GROUPS = 6
F_GD = 256
F_WIDTH = F_GROUPS * F_GD
F_N2 = 128

MEM_HEADS = 4

VMEM_LIMIT = 56 * 1024 * 1024


def _cparams(sem):
    return pltpu.CompilerParams(dimension_semantics=sem, vmem_limit_bytes=VMEM_LIMIT)


def _silu(x):
    return x * jax.nn.sigmoid(x)


def _rms(x, g):
    ms = jnp.mean(x * x, axis=-1, keepdims=True)
    return x * lax.rsqrt(ms + NORM_EPS) * g


def _get_rows(ref3, start=0, size=None, stride=1):
    idx = slice(None) if size is None else pl.ds(start, size, stride=stride)
    return jnp.concatenate([ref3[c, idx, :] for c in range(ref3.shape[0])], axis=1)


def _put_rows(ref3, val, start=0, stride=1):
    size = val.shape[0]
    idx = slice(None) if (stride == 1 and size == ref3.shape[1]) else pl.ds(start, size, stride=stride)
    for c in range(ref3.shape[0]):
        ref3[c, idx, :] = val[:, c * LANES:(c + 1) * LANES]


def _rmsnorm_kernel(x_ref, g_ref, o_ref):
    o_ref[...] = _rms(x_ref[...], g_ref[...]).astype(o_ref.dtype)


def _rmsnorm(x, g, *, tm=1024):
    m, k = x.shape
    tm = min(tm, m)
    return pl.pallas_call(
        _rmsnorm_kernel,
        grid=(m // tm,),
        in_specs=[pl.BlockSpec((tm, k), lambda i: (i, 0)), pl.BlockSpec((1, k), lambda i: (0, 0))],
        out_specs=pl.BlockSpec((tm, k), lambda i: (i, 0)),
        out_shape=jax.ShapeDtypeStruct((m, k), BF16),
        compiler_params=_cparams(("parallel",)),
        name="rmsnorm",
    )(x, g.reshape(1, k))


def _mm_kernel(*refs, epilogue, head_dim):
    h_ref, w_ref = refs[:2]
    acc = jnp.dot(h_ref[...], w_ref[...], preferred_element_type=F32)
    tn = acc.shape[1]
    if epilogue == "headnorm":
        hg_ref, o_ref = refs[2:]
        for hh in range(tn // head_dim):
            sl = slice(hh * head_dim, (hh + 1) * head_dim)
            o_ref[:, sl] = _rms(acc[:, sl], hg_ref[...]).astype(o_ref.dtype)
    elif epilogue == "sigmoid_bias":
        b_ref, o_ref = refs[2:]
        o_ref[...] = jax.nn.sigmoid(acc + b_ref[...]).astype(o_ref.dtype)
    elif epilogue == "chandft":
        dft_ref, o_ref = refs[2:]
        a16 = acc.astype(BF16)
        for gg in range(tn // F_GD):
            sl = slice(gg * F_GD, (gg + 1) * F_GD)
            r = jnp.dot(a16[:, sl], dft_ref[...], preferred_element_type=F32)
            o_ref[0, :, sl] = r[:, :F_GD].astype(o_ref.dtype)
            o_ref[1, :, sl] = r[:, F_GD:].astype(o_ref.dtype)
    else:
        o_ref = refs[2]
        o_ref[...] = acc.astype(o_ref.dtype)


def _matmul(h, w, *, out_dtype, tn, tm=1024, epilogue="plain", extra=None, head_dim=None):
    m, k = h.shape
    n = w.shape[1]
    tm = min(tm, m)
    assert m % tm == 0 and n % tn == 0
    in_specs = [pl.BlockSpec((tm, k), lambda i, j: (i, 0)), pl.BlockSpec((k, tn), lambda i, j: (0, j))]
    args = [h, w]
    if epilogue == "headnorm":
        in_specs.append(pl.BlockSpec((1, head_dim), lambda i, j: (0, 0)))
        args.append(extra.reshape(1, head_dim))
    elif epilogue == "sigmoid_bias":
        in_specs.append(pl.BlockSpec((1, tn), lambda i, j: (0, j)))
        args.append(extra.reshape(1, n))
    elif epilogue == "chandft":
        in_specs.append(pl.BlockSpec(extra.shape, lambda i, j: (0, 0)))
        args.append(extra)
    if epilogue == "chandft":
        out_shape = jax.ShapeDtypeStruct((2, m, n), out_dtype)
        out_spec = pl.BlockSpec((2, tm, tn), lambda i, j: (0, i, j))
    else:
        out_shape = jax.ShapeDtypeStruct((m, n), out_dtype)
        out_spec = pl.BlockSpec((tm, tn), lambda i, j: (i, j))
    return pl.pallas_call(
        functools.partial(_mm_kernel, epilogue=epilogue, head_dim=head_dim),
        grid=(m // tm, n // tn),
        in_specs=in_specs,
        out_specs=out_spec,
        out_shape=out_shape,
        compiler_params=_cparams(("parallel", "parallel")),
        name="matmul_" + epilogue,
    )(*args)


def _qkv_kernel(h_ref, w_ref, hg_ref, o_ref, *scratch, dil):
    j = pl.program_id(1)
    acc = jnp.dot(h_ref[...], w_ref[...], preferred_element_type=F32)
    tm, width = acc.shape
    rows = tm // dil
    nh = width // ATTN_HD
    if dil > 1:
        scr = scratch[0]
        for hh in range(nh):
            scr[hh] = acc[:, hh * ATTN_HD:(hh + 1) * ATTN_HD]
    for r in range(dil):
        for hh in range(nh):
            if dil == 1:
                ph = acc[:, hh * ATTN_HD:(hh + 1) * ATTN_HD]
            else:
                ph = scr[hh, pl.ds(r, rows, stride=dil), :]
            val = jnp.where(j < 2, _rms(ph, hg_ref[...]), ph)
            c0 = r * width + hh * ATTN_HD
            o_ref[:, c0:c0 + ATTN_HD] = val.astype(o_ref.dtype)


def _qkv_group(h, w3, gains, dil, bsz, seq, *, tm=2048):
    m, k = h.shape
    sub = seq // dil
    per_b = seq // tm
    rows = tm // dil
    assert seq % tm == 0 and rows % BF16_ROWS == 0
    return pl.pallas_call(
        functools.partial(_qkv_kernel, dil=dil),
        grid=(m // tm, 3),
        in_specs=[
            pl.BlockSpec((tm, k), lambda i, j: (i, 0)),
            pl.BlockSpec((k, ATTN_OUT), lambda i, j: (0, j)),
            pl.BlockSpec((None, 1, ATTN_HD), lambda i, j: (j, 0, 0)),
        ],
        out_specs=pl.BlockSpec((None, None, rows, dil * ATTN_OUT), lambda i, j: (j, i // per_b, i % per_b, 0)),
        out_shape=jax.ShapeDtypeStruct((3, bsz, sub, dil * ATTN_OUT), BF16),
        scratch_shapes=[pltpu.VMEM((ATTN_HPG, tm, ATTN_HD), F32)] if dil > 1 else [],
        compiler_params=_cparams(("parallel", "parallel")),
        name=f"qkv_proj_d{dil}",
    )(h, w3, gains)


def _t5_bucket_np(rel):
    half_b = NUM_BUCKETS // 2
    exact = half_b // 2
    dist = np.abs(rel)
    log_ratio = np.log(np.maximum(dist, 1) / exact) / np.log(MAX_DISTANCE / exact)
    far = np.minimum(exact + (log_ratio * (half_b - exact)).astype(np.int32), half_b - 1)
    return np.where(rel > 0, half_b, 0) + np.where(dist < exact, dist, far)


def _attn_bias(rel_bias, gi, dil):
    i = np.arange(ATTN_TQ)[:, None]
    j = np.arange(ATTN_TK)[None, :]
    rel = j - ATTN_HALF - i
    idx = _t5_bucket_np(rel * dil)
    onehot = jnp.asarray(np.eye(NUM_BUCKETS, dtype=np.float32)[idx])
    tab = rel_bias[:, gi * ATTN_HPG:(gi + 1) * ATTN_HPG].astype(F32)
    b = jnp.einsum("qkn,nh->hqk", onehot, tab, precision=lax.Precision.HIGHEST)
    return jnp.where(jnp.asarray(np.abs(rel) <= ATTN_HALF)[None], b, MASK_VALUE)


def _attn_kernel(q_ref, kp_ref, kc_ref, kn_ref, vp_ref, vc_ref, vn_ref, bias_ref, o_ref, lse_ref, kpad, vpad,
                 *, tile, sub):
    i = pl.program_id(2)
    hf = ATTN_HALF
    kpad[0:hf, :] = kp_ref[...]
    kpad[hf:hf + tile, :] = kc_ref[...]
    kpad[hf + tile:, :] = kn_ref[...]
    vpad[0:hf, :] = vp_ref[...]
    vpad[hf:hf + tile, :] = vc_ref[...]
    vpad[hf + tile:, :] = vn_ref[...]

    scale = ATTN_HD ** -0.5
    col = lax.broadcasted_iota(jnp.int32, (ATTN_TQ, ATTN_TK), 1)
    lane = lax.broadcasted_iota(jnp.int32, (ATTN_TQ, LANES), 1)

    def body(t, carry):
        s0 = pl.multiple_of(t * ATTN_TQ, ATTN_TQ)
        kpos = col + (i * tile + s0 - hf)
        valid = (kpos >= 0) & (kpos < sub)
        lse_tile = jnp.zeros((ATTN_TQ, LANES), F32)
        for hh in range(ATTN_HPG):
            sl = slice(hh * ATTN_HD, (hh + 1) * ATTN_HD)
            q = q_ref[pl.ds(s0, ATTN_TQ), sl]
            kw = kpad[pl.ds(s0, ATTN_TK), sl]
            vw = vpad[pl.ds(s0, ATTN_TK), sl]
            s = lax.dot_general(q, kw, (((1,), (1,)), ((), ())), preferred_element_type=F32) * scale + bias_ref[hh]
            s = jnp.where(valid, s, MASK_VALUE)
            m = jnp.max(s, axis=-1, keepdims=True)
            p = jnp.exp(s - m)
            l = jnp.sum(p, axis=-1, keepdims=True)
            o = jnp.dot(p.astype(BF16), vw, preferred_element_type=F32) / l
            o_ref[pl.ds(s0, ATTN_TQ), sl] = o.astype(o_ref.dtype)
            lse_tile = jnp.where(lane == hh, m + jnp.log(l), lse_tile)
        lse_ref[pl.ds(s0, ATTN_TQ), :] = lse_tile
        return carry

    lax.fori_loop(0, tile // ATTN_TQ, body, 0, unroll=4)


def _attn_group(qkv, bias, dil, bsz, seq, *, tile=1024):
    sub = seq // dil
    tile = min(tile, sub)
    nblk = sub // tile
    hb = tile // ATTN_HALF
    last_h = sub // ATTN_HALF - 1
    assert sub % tile == 0 and tile % ATTN_TQ == 0
    cur = lambda which: pl.BlockSpec((None, None, tile, ATTN_OUT), lambda b, r, i: (which, b, i, r))
    prev = lambda which: pl.BlockSpec((None, None, ATTN_HALF, ATTN_OUT),
                                      lambda b, r, i: (which, b, jnp.maximum(i * hb - 1, 0), r))
    nxt = lambda which: pl.BlockSpec((None, None, ATTN_HALF, ATTN_OUT),
                                     lambda b, r, i: (which, b, jnp.minimum((i + 1) * hb, last_h), r))
    return pl.pallas_call(
        functools.partial(_attn_kernel, tile=tile, sub=sub),
        grid=(bsz, dil, nblk),
        in_specs=[cur(0), prev(1), cur(1), nxt(1), prev(2), cur(2), nxt(2),
                  pl.BlockSpec((ATTN_HPG, ATTN_TQ, ATTN_TK), lambda b, r, i: (0, 0, 0))],
        out_specs=(pl.BlockSpec((None, tile, ATTN_OUT), lambda b, r, i: (b, i, r)),
                   pl.BlockSpec((None, tile, LANES), lambda b, r, i: (b, i, r))),
        out_shape=(jax.ShapeDtypeStruct((bsz, sub, dil * ATTN_OUT), BF16),
                   jax.ShapeDtypeStruct((bsz, sub, dil * LANES), F32)),
        scratch_shapes=[pltpu.VMEM((tile + 2 * ATTN_HALF, ATTN_OUT), BF16)] * 2,
        compiler_params=_cparams(("parallel", "parallel", "parallel")),
        name=f"dilated_attn_d{dil}",
    )(qkv, qkv, qkv, qkv, qkv, qkv, qkv, bias)


CONV_HALO = BF16_ROWS
CONV_RC = 64
CONV_APRON = 8


SUBLANES = 8


def _shift_rows(x, s):
    r, c = x.shape
    x3 = x.reshape(r // SUBLANES, SUBLANES, c)
    rot = pltpu.roll(x3, (-s) % SUBLANES, 1)
    sub = lax.broadcasted_iota(jnp.int32, x3.shape, 1)
    if s > 0:
        y = jnp.where(sub < SUBLANES - s, rot, jnp.concatenate([rot[1:], rot[:1]], axis=0))
    else:
        y = jnp.where(sub >= -s, rot, jnp.concatenate([rot[-1:], rot[:-1]], axis=0))
    return y.reshape(r, c)


def _proj_conv_kernel(hp_ref, hc_ref, hn_ref, w_ref, cw_ref, cb_ref, o_ref, xp_ref, *, tm, per_b):
    i = pl.program_id(0)
    first = (i % per_b) == 0
    last = (i % per_b) == per_b - 1
    w = w_ref[...]
    halo_p = jnp.dot(hp_ref[...], w, preferred_element_type=F32)
    halo_n = jnp.dot(hn_ref[...], w, preferred_element_type=F32)
    xp_ref[0:CONV_HALO, :] = jnp.where(first, 0.0, halo_p)
    xp_ref[CONV_HALO:CONV_HALO + tm, :] = jnp.dot(hc_ref[...], w, preferred_element_type=F32)
    xp_ref[CONV_HALO + tm:, :] = jnp.where(last, 0.0, halo_n)
    rows = CONV_RC + 2 * CONV_APRON
    pad = SSD_CONV // 2
    for c in range(tm // CONV_RC):
        r0 = CONV_HALO + c * CONV_RC - CONV_APRON
        x = xp_ref[r0:r0 + rows, :]
        y = cw_ref[pad:pad + 1, :] * x + cb_ref[...]
        for k in range(SSD_CONV):
            if k != pad:
                y = y + cw_ref[k:k + 1, :] * _shift_rows(x, k - pad)
        o_ref[c * CONV_RC:(c + 1) * CONV_RC, :] = _silu(y[CONV_APRON:CONV_APRON + CONV_RC]).astype(o_ref.dtype)


def _proj_conv_silu(h, w, cw, cb, seq, *, tm=1024, tn=512):
    m, k = h.shape
    n = w.shape[1]
    per_b = seq // tm
    hb = tm // CONV_HALO
    assert seq % tm == 0 and n % tn == 0 and SSD_CONV == 7
    return pl.pallas_call(
        functools.partial(_proj_conv_kernel, tm=tm, per_b=per_b),
        grid=(m // tm, n // tn),
        in_specs=[
            pl.BlockSpec((CONV_HALO, k), lambda i, j: (jnp.maximum(i * hb - 1, 0), 0)),
            pl.BlockSpec((tm, k), lambda i, j: (i, 0)),
            pl.BlockSpec((CONV_HALO, k), lambda i, j: (jnp.minimum((i + 1) * hb, m // CONV_HALO - 1), 0)),
            pl.BlockSpec((k, tn), lambda i, j: (0, j)),
            pl.BlockSpec((SSD_CONV, tn), lambda i, j: (0, j)),
            pl.BlockSpec((1, tn), lambda i, j: (0, j)),
        ],
        out_specs=pl.BlockSpec((tm, tn), lambda i, j: (i, j)),
        out_shape=jax.ShapeDtypeStruct((m, n), BF16),
        scratch_shapes=[pltpu.VMEM((tm + 2 * CONV_HALO, tn), F32)],
        compiler_params=_cparams(("parallel", "parallel")),
        name="proj_conv_silu",
    )(h, h, h, w, cw, cb.reshape(1, n))


def _softplus(x):
    return jnp.maximum(x, 0.0) + jnp.log1p(jnp.exp(-jnp.abs(x)))


def _ssd_prep_kernel(dtr_ref, dtb_ref, alog_ref, ccol_ref, rt_ref, esc_ref, wst_ref, etot_ref, *, t):
    half = LANES // 2
    log2e = math.log2(math.e)
    lane = lax.broadcasted_iota(jnp.int32, (t, LANES), 1)
    fwd = lane < SSD_HEADS
    tri = (lax.broadcasted_iota(jnp.int32, (t, t), 1) <= lax.broadcasted_iota(jnp.int32, (t, t), 0)).astype(BF16)
    neg_a = jnp.exp(alog_ref[...])
    for ci in range(dtr_ref.shape[0] // t):
        rs = slice(ci * t, (ci + 1) * t)
        dt = _softplus(dtr_ref[rs, :] + dtb_ref[...])
        la = jnp.where(lane < 2 * SSD_HEADS, -(dt * neg_a), 0.0)
        hi = la.astype(BF16)
        r1 = la - hi.astype(F32)
        mid = r1.astype(BF16)
        lo = (r1 - mid.astype(F32)).astype(BF16)
        packed = (hi.astype(F32) + pltpu.roll(mid.astype(F32), half, 1)).astype(BF16)
        res = jnp.dot(tri, jnp.concatenate([packed, lo], axis=1), preferred_element_type=F32)
        a0 = res[:, :LANES]
        acs = a0 + pltpu.roll(a0, half, 1) + res[:, LANES:]
        exb = acs - la
        ldt = jnp.log(dt)
        tot = acs[t - 1:t, :]
        ccol_ref[rs, :] = jnp.where(fwd, acs, exb) * log2e
        rt_ref[ci] = (jnp.where(fwd, acs - ldt, exb + ldt) * log2e).T
        esc_ref[rs, :] = jnp.exp(jnp.where(fwd, acs, tot - exb))
        wst_ref[rs, :] = jnp.exp(jnp.where(fwd, tot - acs, exb)) * dt
        etot_ref[ci] = jnp.broadcast_to(jnp.exp(tot), etot_ref.shape[1:])


def _split2(v):
    hi = v.astype(BF16)
    return jnp.concatenate([hi, (v - hi.astype(F32)).astype(BF16)], axis=1)


def _head_expand_tables():
    j = np.arange(2 * LANES)[None, :, None] % LANES
    g = np.arange(SSD_GROUPS)[:, None, None]
    c = np.arange(SSD_GW)[None, None, :]
    fwd = (j == g * SSD_HPG + c // SSD_HD)
    bwd = (j == SSD_HEADS + g * SSD_HPG + c // SSD_HD)
    return jnp.asarray(fwd, BF16), jnp.asarray(bwd, BF16)


def _ssd_bwd_state_kernel(xs_ref, b_ref, wst_ref, etot_ref, selb_ref, gin_ref, g_sc):
    @pl.when(pl.program_id(1) == 0)
    def _():
        g_sc[...] = jnp.zeros(g_sc.shape, F32)

    wst2 = _split2(wst_ref[...])
    etot2 = _split2(etot_ref[...])
    gin_ref[...] = g_sc[...].astype(gin_ref.dtype)
    for g in range(SSD_GROUPS):
        gs = slice(g * SSD_GW, (g + 1) * SSD_GW)
        bm_t = b_ref[:, g * SSD_STATE:(g + 1) * SSD_STATE].T
        wts = jnp.dot(wst2, selb_ref[g], preferred_element_type=F32)
        dec = jnp.dot(etot2, selb_ref[g], preferred_element_type=F32)[0:1]
        xw = (xs_ref[:, gs].astype(F32) * wts).astype(BF16)
        st = jnp.dot(bm_t, xw, preferred_element_type=F32)
        g_sc[:, gs] = g_sc[:, gs] * dec + st


def _ssd_main_kernel(xs_ref, b_ref, c_ref, ccol_ref, rt_ref, esc_ref, wst_ref, etot_ref, z_ref, gin_ref,
                     dsk_ref, ng_ref, self_ref, selb_ref, y_ref, h_sc, y_sc):
    @pl.when(pl.program_id(1) == 0)
    def _():
        h_sc[...] = jnp.zeros(h_sc.shape, F32)

    t = xs_ref.shape[0]
    ccol = ccol_ref[...]
    r_t = rt_ref[...]
    esc2 = _split2(esc_ref[...])
    wst2 = _split2(wst_ref[...])
    etot2 = _split2(etot_ref[...])

    li = lax.broadcasted_iota(jnp.int32, (t, t), 0)
    si = lax.broadcasted_iota(jnp.int32, (t, t), 1)
    mask_f = jnp.where(li >= si, 0.0, MASK_VALUE)
    mask_b = jnp.where(si >= li, 0.0, MASK_VALUE)
    lo_half = lax.broadcasted_iota(jnp.int32, (t, LANES), 1) < SSD_HD

    for g in range(SSD_GROUPS):
        gs = slice(g * SSD_GW, (g + 1) * SSD_GW)
        bm = b_ref[:, g * SSD_STATE:(g + 1) * SSD_STATE]
        cm = c_ref[:, g * SSD_STATE:(g + 1) * SSD_STATE]
        cb = lax.dot_general(cm, bm, (((1,), (1,)), ((), ())), preferred_element_type=F32)
        hf = h_sc[:, gs]
        yf_all = jnp.dot(cm, hf.astype(BF16), preferred_element_type=F32)
        yb_all = jnp.dot(cm, gin_ref[:, gs], preferred_element_type=F32)
        e_in = jnp.dot(esc2, self_ref[g], preferred_element_type=F32)
        e_out = jnp.dot(esc2, selb_ref[g], preferred_element_type=F32)
        w_state = jnp.dot(wst2, self_ref[g], preferred_element_type=F32)
        dec = jnp.dot(etot2, self_ref[g], preferred_element_type=F32)[0:1]
        xw = []
        for k in range(SSD_HPG // 2):
            ea = g * SSD_HPG + 2 * k
            sl = slice(g * SSD_GW + k * LANES, g * SSD_GW + (k + 1) * LANES)
            ks = slice(k * LANES, (k + 1) * LANES)
            xp = xs_ref[:, sl]
            mms = []
            for e in (ea, ea + 1):
                eb = SSD_HEADS + e
                w = (jnp.exp2(ccol[:, e:e + 1] - r_t[e:e + 1, :] + mask_f)
                     + jnp.exp2(r_t[eb:eb + 1, :] - ccol[:, eb:eb + 1] + mask_b))
                mms.append((cb * w).astype(BF16))
            zero = jnp.zeros_like(xp)
            xx = jnp.concatenate([jnp.where(lo_half, xp, zero), jnp.where(lo_half, zero, xp)], axis=0)
            ypair = jnp.dot(jnp.concatenate(mms, axis=1), xx, preferred_element_type=F32)
            xf = xp.astype(F32)
            y_sc[:, ks] = (ypair + yf_all[:, ks] * e_in[:, ks] + yb_all[:, ks] * e_out[:, ks]
                           + xf * dsk_ref[:, sl])
            xw.append((xf * w_state[:, ks]).astype(BF16))
        st = jnp.dot(bm.T, jnp.concatenate(xw, axis=1), preferred_element_type=F32)
        h_sc[:, gs] = hf * dec + st
        yy = y_sc[...] * _silu(z_ref[:, gs].astype(F32))
        y_ref[:, gs] = _rms(yy, ng_ref[:, gs]).astype(y_ref.dtype)


def _ssd(conv_out, dt_raw, z, dtb, alog, dskip, norm_g, bsz, seq):
    t = SSD_CHUNK
    nc = seq // t
    assert seq % t == 0
    gn = SSD_GROUPS * SSD_STATE
    b_blk = SSD_INNER // gn
    c_blk = b_blk + 1
    cpp = min(8, nc)
    assert nc % cpp == 0
    sub8 = 8
    const_spec = pl.BlockSpec((1, LANES), lambda b, i: (0, 0))
    rows_spec = pl.BlockSpec((None, cpp * t, LANES), lambda b, i: (b, i, 0))
    row_arr = jax.ShapeDtypeStruct((bsz, seq, LANES), F32)
    ccol, rt, esc, wst, etot = pl.pallas_call(
        functools.partial(_ssd_prep_kernel, t=t),
        grid=(bsz, nc // cpp),
        in_specs=[rows_spec, const_spec, const_spec],
        out_specs=(rows_spec, pl.BlockSpec((None, cpp, LANES, t), lambda b, i: (b, i, 0, 0)), rows_spec, rows_spec,
                   pl.BlockSpec((None, cpp, sub8, LANES), lambda b, i: (b, i, 0, 0))),
        out_shape=(row_arr, jax.ShapeDtypeStruct((bsz, nc, LANES, t), F32), row_arr, row_arr,
                   jax.ShapeDtypeStruct((bsz, nc, sub8, LANES), F32)),
        compiler_params=_cparams(("parallel", "parallel")),
        name="ssd_decay_terms",
    )(dt_raw, dtb, alog)

    rev = lambda c: nc - 1 - c
    sel_f, sel_b = _head_expand_tables()
    sel_spec = pl.BlockSpec(sel_f.shape, lambda b, c: (0, 0, 0))
    gin = pl.pallas_call(
        _ssd_bwd_state_kernel,
        grid=(bsz, nc),
        in_specs=[
            pl.BlockSpec((None, t, SSD_INNER), lambda b, c: (b, rev(c), 0)),
            pl.BlockSpec((None, t, gn), lambda b, c: (b, rev(c), b_blk)),
            pl.BlockSpec((None, t, LANES), lambda b, c: (b, rev(c), 0)),
            pl.BlockSpec((None, None, sub8, LANES), lambda b, c: (b, rev(c), 0, 0)),
            sel_spec,
        ],
        out_specs=pl.BlockSpec((None, None, SSD_STATE, SSD_INNER), lambda b, c: (b, rev(c), 0, 0)),
        out_shape=jax.ShapeDtypeStruct((bsz, nc, SSD_STATE, SSD_INNER), BF16),
        scratch_shapes=[pltpu.VMEM((SSD_STATE, SSD_INNER), F32)],
        compiler_params=_cparams(("parallel", "arbitrary")),
        name="ssd_bwd_states",
    )(conv_out, conv_out, wst, etot, sel_b)

    wide_spec = pl.BlockSpec((1, SSD_INNER), lambda b, c: (0, 0))
    tok_spec = pl.BlockSpec((None, t, LANES), lambda b, c: (b, c, 0))
    return pl.pallas_call(
        _ssd_main_kernel,
        grid=(bsz, nc),
        in_specs=[
            pl.BlockSpec((None, t, SSD_INNER), lambda b, c: (b, c, 0)),
            pl.BlockSpec((None, t, gn), lambda b, c: (b, c, b_blk)),
            pl.BlockSpec((None, t, gn), lambda b, c: (b, c, c_blk)),
            tok_spec,
            pl.BlockSpec((None, None, LANES, t), lambda b, c: (b, c, 0, 0)),
            tok_spec, tok_spec,
            pl.BlockSpec((None, None, sub8, LANES), lambda b, c: (b, c, 0, 0)),
            pl.BlockSpec((None, t, SSD_INNER), lambda b, c: (b, c, 0)),
            pl.BlockSpec((None, None, SSD_STATE, SSD_INNER), lambda b, c: (b, c, 0, 0)),
            wide_spec, wide_spec, sel_spec, sel_spec,
        ],
        out_specs=pl.BlockSpec((None, t, SSD_INNER), lambda b, c: (b, c, 0)),
        out_shape=jax.ShapeDtypeStruct((bsz, seq, SSD_INNER), BF16),
        scratch_shapes=[pltpu.VMEM((SSD_STATE, SSD_INNER), F32), pltpu.VMEM((t, SSD_GW), F32)],
        compiler_params=_cparams(("parallel", "arbitrary")),
        name="ssd_main",
    )(conv_out, conv_out, conv_out, ccol, rt, esc, wst, etot, z, gin, dskip, norm_g, sel_f, sel_b)


def _dft_tables(seq):
    n1, n2 = seq // F_N2, F_N2
    c = np.arange(F_GD)
    ang = 2 * np.pi * np.outer(c, c) / F_GD
    chan = np.concatenate([np.cos(ang), -np.sin(ang)], axis=1) / math.sqrt(F_GD)
    k1 = np.arange(n1)
    a1 = 2 * np.pi * np.outer(k1, k1) / n1
    stage_a = np.block([[np.cos(a1), np.sin(a1)], [-np.sin(a1), np.cos(a1)]])
    s2 = np.arange(n2)
    at = 2 * np.pi * np.outer(s2, k1) / seq
    tw_c, tw_s = np.cos(at)[..., None], np.sin(at)[..., None]
    a2 = 2 * np.pi * np.outer(s2, s2) / n2
    stage_b = np.concatenate([np.cos(a2), np.sin(a2)], axis=1)
    return (jnp.asarray(chan, BF16), jnp.asarray(stage_a, BF16), jnp.asarray(tw_c, F32),
            jnp.asarray(tw_s, F32), jnp.asarray(stage_b, BF16))


def _fft_a_kernel(z_ref, ma_ref, twc_ref, tws_ref, o_ref, zs, ys, *, n1, rb):
    ct = z_ref.shape[-1]
    _put_rows(zs, z_ref[...].astype(F32).reshape(2 * n1 * rb, ct))
    for a in range(rb):
        zz = _get_rows(zs, a, 2 * n1, rb).astype(BF16)
        y = jnp.dot(ma_ref[...], zz, preferred_element_type=F32)
        yr, yi = y[:n1], y[n1:]
        tc, ts = twc_ref[a], tws_ref[a]
        _put_rows(ys, yr * tc + yi * ts, a, rb)
        _put_rows(ys, yi * tc - yr * ts, n1 * rb + a, rb)
    o_ref[...] = _get_rows(ys).reshape(2, n1, rb, ct).astype(o_ref.dtype)


def _fft_b_kernel(y_ref, mb_ref, o_ref, os_ref, *, n2, kb, scale):
    ct = y_ref.shape[-1]
    for kk in range(kb):
        yy = y_ref[:, kk].reshape(2 * n2, ct)
        _put_rows(os_ref, jnp.dot(mb_ref[...], yy, preferred_element_type=F32) * scale, kk, kb)
    o_ref[...] = _get_rows(os_ref).reshape(n2, kb, ct).astype(o_ref.dtype)


def _fourier_seq(z2, tabs, bsz, seq, *, rb=BF16_ROWS, kb=BF16_ROWS, ct=768):
    _, ma, twc, tws, mb = tabs
    n1, n2 = seq // F_N2, F_N2
    w = z2.shape[-1]
    nct = w // ct
    za = z2.reshape(2, bsz, n1, n2, w)
    ab_spec = pl.BlockSpec((2, None, n1, rb, ct), lambda b, a, j: (0, b, 0, a, j))
    ya = pl.pallas_call(
        functools.partial(_fft_a_kernel, n1=n1, rb=rb),
        grid=(bsz, n2 // rb, nct),
        in_specs=[
            ab_spec,
            pl.BlockSpec((2 * n1, 2 * n1), lambda b, a, j: (0, 0)),
            pl.BlockSpec((rb, n1, 1), lambda b, a, j: (a, 0, 0)),
            pl.BlockSpec((rb, n1, 1), lambda b, a, j: (a, 0, 0)),
        ],
        out_specs=ab_spec,
        out_shape=jax.ShapeDtypeStruct((2, bsz, n1, n2, w), BF16),
        scratch_shapes=[pltpu.VMEM((ct // LANES, 2 * n1 * rb, LANES), F32)] * 2,
        compiler_params=_cparams(("parallel", "parallel", "parallel")),
        name="fourier_stage_a",
    )(za, ma, twc, tws)
    out = pl.pallas_call(
        functools.partial(_fft_b_kernel, n2=n2, kb=kb, scale=1.0 / math.sqrt(seq)),
        grid=(bsz, n1 // kb, nct),
        in_specs=[
            pl.BlockSpec((2, None, kb, n2, ct), lambda b, k, j: (0, b, k, 0, j)),
            pl.BlockSpec((n2, 2 * n2), lambda b, k, j: (0, 0)),
        ],
        out_specs=pl.BlockSpec((None, n2, kb, ct), lambda b, k, j: (b, 0, k, j)),
        out_shape=jax.ShapeDtypeStruct((bsz, n2, n1, w), BF16),
        scratch_shapes=[pltpu.VMEM((ct // LANES, n2 * kb, LANES), F32)],
        compiler_params=_cparams(("parallel", "parallel", "parallel")),
        name="fourier_stage_b",
    )(ya, mb)
    return out.reshape(bsz, seq, w)


def _merge_kernel(*refs, dils):
    ng = len(dils)
    x_ref = refs[0]
    o_refs = refs[1:1 + ng]
    l_refs = refs[1 + ng:1 + 2 * ng]
    y_ref, f_ref, gt_ref, wa_ref, ws_ref, wf_ref, wo_ref, out_ref, o_sc, l_sc = refs[1 + 2 * ng:]
    tm, d = x_ref.shape

    for gi, dil in enumerate(dils):
        rows = tm // dil
        idx = lambda r: slice(None) if dil == 1 else pl.ds(r, rows, stride=dil)
        for r in range(dil):
            l_sc[gi, idx(r), :] = l_refs[gi][:, r * LANES:(r + 1) * LANES]
            for hh in range(ATTN_HPG):
                c0 = r * ATTN_OUT + hh * ATTN_HD
                o_sc[gi * ATTN_HPG + hh, idx(r), :] = o_refs[gi][:, c0:c0 + ATTN_HD].astype(F32)

    ls = [l_sc[gi] for gi in range(ng)]
    mx = functools.reduce(jnp.maximum, ls)
    ws = [jnp.exp(v - mx) for v in ls]
    inv = 1.0 / functools.reduce(lambda a, b: a + b, ws)
    heads = []
    for hh in range(ATTN_HPG):
        acc = None
        for gi in range(ng):
            alpha = (ws[gi] * inv)[:, hh:hh + 1]
            term = alpha * o_sc[gi * ATTN_HPG + hh]
            acc = term if acc is None else acc + term
        heads.append(acc.astype(BF16))
    o_attn = jnp.concatenate(heads, axis=1)

    ya = jnp.dot(o_attn, wa_ref[...], preferred_element_type=F32)
    ys = jnp.dot(y_ref[...], ws_ref[...], preferred_element_type=F32)
    yf = jnp.dot(f_ref[...], wf_ref[...], preferred_element_type=F32)
    gates = gt_ref[...].astype(F32)
    merged = gates[:, :d] * ya + gates[:, d:2 * d] * ys + gates[:, 2 * d:] * yf
    out_ref[...] = x_ref[...] + jnp.dot(merged.astype(BF16), wo_ref[...], preferred_element_type=F32)


def _merge(x, os_, ls_, dils, y, f, gates, wa, ws, wf, wo, bsz, seq, *, tm=256):
    d = x.shape[-1]
    assert all((tm // dil) % BF16_ROWS == 0 for dil in dils)
    tok = lambda width: pl.BlockSpec((None, tm, width), lambda b, i: (b, i, 0))
    full = lambda a: pl.BlockSpec(a.shape, lambda b, i: (0, 0))
    ng = len(dils)
    in_specs = [tok(d)]
    in_specs += [pl.BlockSpec((None, tm // dil, dil * ATTN_OUT), lambda b, i: (b, i, 0)) for dil in dils]
    in_specs += [pl.BlockSpec((None, tm // dil, dil * LANES), lambda b, i: (b, i, 0)) for dil in dils]
    in_specs += [tok(y.shape[-1]), tok(f.shape[-1]), tok(gates.shape[-1]), full(wa), full(ws), full(wf), full(wo)]
    return pl.pallas_call(
        functools.partial(_merge_kernel, dils=tuple(dils)),
        grid=(bsz, seq // tm),
        in_specs=in_specs,
        out_specs=tok(d),
        out_shape=jax.ShapeDtypeStruct((bsz, seq, d), F32),
        scratch_shapes=[pltpu.VMEM((ng * ATTN_HPG, tm, ATTN_HD), F32), pltpu.VMEM((ng, tm, LANES), F32)],
        compiler_params=_cparams(("parallel", "parallel")),
        name="branch_merge",
    )(x, *os_, *ls_, y, f, gates, wa, ws, wf, wo)


def _xattn_kernel(x_ref, g_ref, wq_ref, qg_ref, k_ref, v_ref, wo_ref, out_ref, o_sc):
    x = x_ref[...]
    d = x.shape[1]
    hd = d // MEM_HEADS
    h = _rms(x, g_ref[...]).astype(BF16)
    q = jnp.dot(h, wq_ref[...], preferred_element_type=F32)
    scale = hd ** -0.5
    for hh in range(MEM_HEADS):
        sl = slice(hh * hd, (hh + 1) * hd)
        qn = _rms(q[:, sl], qg_ref[...]).astype(BF16)
        s = lax.dot_general(qn, k_ref[:, sl], (((1,), (1,)), ((), ())), preferred_element_type=F32) * scale
        m = jnp.max(s, axis=-1, keepdims=True)
        p = jnp.exp(s - m)
        l = jnp.sum(p, axis=-1, keepdims=True)
        o_sc[:, sl] = jnp.dot(p.astype(BF16), v_ref[:, sl], preferred_element_type=F32) / l
    out_ref[...] = x + jnp.dot(o_sc[...].astype(BF16), wo_ref[...], preferred_element_type=F32)


def _xattn(x, g, wq, qg, k, v, wo, *, tm=512):
    bsz, seq, d = x.shape
    mt = k.shape[1]
    full = lambda a: pl.BlockSpec(a.shape, lambda b, i: (0, 0))
    g = g.reshape(1, d)
    qg = qg.reshape(1, -1)
    return pl.pallas_call(
        _xattn_kernel,
        grid=(bsz, seq // tm),
        in_specs=[
            pl.BlockSpec((None, tm, d), lambda b, i: (b, i, 0)),
            full(g), full(wq), full(qg),
            pl.BlockSpec((None, mt, d), lambda b, i: (b, 0, 0)),
            pl.BlockSpec((None, mt, d), lambda b, i: (b, 0, 0)),
            full(wo),
        ],
        out_specs=pl.BlockSpec((None, tm, d), lambda b, i: (b, i, 0)),
        out_shape=jax.ShapeDtypeStruct((bsz, seq, d), F32),
        scratch_shapes=[pltpu.VMEM((tm, d), F32)],
        compiler_params=_cparams(("parallel", "parallel")),
        name="mem_xattn",
    )(x, g, wq, qg, k, v, wo)


def _ffn_kernel(x_ref, g_ref, wg_ref, wu_ref, wd_ref, out_ref, *, tf):
    x = x_ref[...]
    h = _rms(x, g_ref[...]).astype(BF16)
    acc = x
    for c in range(wg_ref.shape[1] // tf):
        sl = slice(c * tf, (c + 1) * tf)
        gt = jnp.dot(h, wg_ref[:, sl], preferred_element_type=F32)
        up = jnp.dot(h, wu_ref[:, sl], preferred_element_type=F32)
        a = (_silu(gt) * up).astype(BF16)
        acc = acc + jnp.dot(a, wd_ref[sl, :], preferred_element_type=F32)
    out_ref[...] = acc


def _ffn(x, g, wg, wu, wd, *, tm=256):
    m, d = x.shape
    dff = wg.shape[1]
    tf = dff // 2 if (dff // 2) % LANES == 0 else dff
    full = lambda a: pl.BlockSpec(a.shape, lambda i: (0, 0))
    g = g.reshape(1, d)
    return pl.pallas_call(
        functools.partial(_ffn_kernel, tf=tf),
        grid=(m // tm,),
        in_specs=[pl.BlockSpec((tm, d), lambda i: (i, 0)), full(g), full(wg), full(wu), full(wd)],
        out_specs=pl.BlockSpec((tm, d), lambda i: (i, 0)),
        out_shape=jax.ShapeDtypeStruct((m, d), F32),
        compiler_params=_cparams(("parallel",)),
        name="swiglu_ffn",
    )(x, g, wg, wu, wd)


def _pad_lanes(a):
    a = a.reshape(-1, 2 * SSD_HEADS)
    return jnp.pad(a, ((0, 0), (0, LANES - 2 * SSD_HEADS)))


def kernel(x, mem, rel_bias, mix_norm_g, w_in, gate_bias, attn_q_norm_g, attn_k_norm_g, conv_w, conv_b, dt_bias, a_log, d_skip, ssd_norm_g, w_branch_attn, w_branch_ssd, w_branch_fourier, w_mix_out, xattn_norm_g, mem_norm_g, w_xq, w_xk, w_xv, xattn_q_norm_g, xattn_k_norm_g, w_xo, ffn_norm_g, w_ffn_gate, w_ffn_up, w_ffn_down):
    bsz, seq, d = x.shape
    depth = w_in.shape[0]
    m = bsz * seq
    xf = x.reshape(m, d)
    memf = mem.reshape(bsz * mem.shape[1], d)
    tabs = _dft_tables(seq)
    dils = [dil for _, dil in ATTN_GROUPS]
    biases = [_attn_bias(rel_bias, gi, dil) for gi, dil in enumerate(dils)]
    offs = np.cumsum([0, ATTN_WIDTH, ATTN_WIDTH, ATTN_WIDTH, SSD_INNER, SSD_CONV_CH, 2 * SSD_HEADS, F_WIDTH, 3 * d])
    bf = lambda a: a.astype(BF16)

    for l in range(depth):
        wl = w_in[l]
        seg = lambda i: wl[:, offs[i]:offs[i + 1]]
        h = _rmsnorm(xf, mix_norm_g[l])
        gains = jnp.stack([attn_q_norm_g[l], attn_k_norm_g[l], jnp.ones_like(attn_q_norm_g[l])]).reshape(3, 1, ATTN_HD)
        os_, ls_ = [], []
        for gi, dil in enumerate(dils):
            cols = slice(gi * ATTN_OUT, (gi + 1) * ATTN_OUT)
            w3 = bf(jnp.concatenate([seg(0)[:, cols], seg(1)[:, cols], seg(2)[:, cols]], axis=1))
            qkv = _qkv_group(h, w3, gains, dil, bsz, seq)
            o_g, lse_g = _attn_group(qkv, biases[gi], dil, bsz, seq)
            os_.append(o_g)
            ls_.append(lse_g)

        z = _matmul(h, bf(seg(3)), out_dtype=BF16, tn=1024, tm=2048)
        dt_raw = _matmul(h, bf(_pad_lanes(seg(5))), out_dtype=F32, tn=LANES, tm=2048)
        z2 = _matmul(h, bf(seg(6)), out_dtype=BF16, tn=768, tm=2048, epilogue="chandft", extra=tabs[0])
        gates = _matmul(h, bf(seg(7)), out_dtype=BF16, tn=1024, tm=2048, epilogue="sigmoid_bias",
                        extra=gate_bias[l])

        conv_out = _proj_conv_silu(h, bf(seg(4)), conv_w[l], conv_b[l], seq).reshape(bsz, seq, SSD_CONV_CH)
        y_ssd = _ssd(conv_out, dt_raw.reshape(bsz, seq, -1), z.reshape(bsz, seq, SSD_INNER),
                     _pad_lanes(dt_bias[l]), _pad_lanes(a_log[l]),
                     jnp.repeat(d_skip[l], SSD_HD).reshape(1, SSD_INNER), ssd_norm_g[l].reshape(1, SSD_INNER),
                     bsz, seq)

        f_re = _fourier_seq(z2, tabs, bsz, seq)

        x3 = _merge(xf.reshape(bsz, seq, d), os_, ls_, dils, y_ssd, f_re, gates.reshape(bsz, seq, 3 * d),
                    bf(w_branch_attn[l]), bf(w_branch_ssd[l]), bf(w_branch_fourier[l]), bf(w_mix_out[l]), bsz, seq)

        hd_m = d // MEM_HEADS
        hm = _rmsnorm(memf, mem_norm_g[l])
        km = _matmul(hm, bf(w_xk[l]), out_dtype=BF16, tn=512, epilogue="headnorm",
                     extra=xattn_k_norm_g[l], head_dim=hd_m)
        vm = _matmul(hm, bf(w_xv[l]), out_dtype=BF16, tn=512)
        x3 = _xattn(x3, xattn_norm_g[l], bf(w_xq[l]), xattn_q_norm_g[l], km.reshape(bsz, -1, d),
                    vm.reshape(bsz, -1, d), bf(w_xo[l]))

        xf = _ffn(x3.reshape(m, d), ffn_norm_g[l], bf(w_ffn_gate[l]), bf(w_ffn_up[l]), bf(w_ffn_down[l]))

    return xf.reshape(bsz, seq, d)
```

```python
import functools
import math

import numpy as np
import jax
import jax.numpy as jnp
from jax import lax
from jax.experimental import pallas as pl
from jax.experimental.pallas import tpu as pltpu

F32 = jnp.float32
BF16 = jnp.bfloat16

NORM_EPS = 1e-6
MASK_VALUE = -1e30

ATTN_GROUPS = ((128, 1), (512, 4), (2048, 16))
ATTN_HPG = 4
ATTN_HD = 128
ATTN_HEADS = ATTN_HPG * len(ATTN_GROUPS)
ATTN_WIDTH = ATTN_HEADS * ATTN_HD
ATTN_OUT = ATTN_HPG * ATTN_HD
ATTN_HALF = 64
ATTN_TQ = 2 * ATTN_HALF
ATTN_TK = ATTN_TQ + 2 * ATTN_HALF
NUM_BUCKETS = 32
MAX_DISTANCE = 1024

SSD_HEADS = 32
SSD_HD = 64
SSD_GROUPS = 4
SSD_HPG = SSD_HEADS // SSD_GROUPS
SSD_STATE = 128
SSD_INNER = SSD_HEADS * SSD_HD
SSD_GW = SSD_INNER // SSD_GROUPS
SSD_CONV = 7
SSD_CHUNK = 128
SSD_CONV_CH = SSD_INNER + 2 * SSD_GROUPS * SSD_STATE
LANES = 128
BF16_ROWS = 16

F_GROUPS = 6
F_GD = 256
F_WIDTH = F_GROUPS * F_GD
F_N2 = 128

MEM_HEADS = 4

VMEM_LIMIT = 56 * 1024 * 1024


def _cparams(sem):
    return pltpu.CompilerParams(dimension_semantics=sem, vmem_limit_bytes=VMEM_LIMIT)


def _silu(x):
    return x * jax.nn.sigmoid(x)


def _rms(x, g):
    ms = jnp.mean(x * x, axis=-1, keepdims=True)
    return x * lax.rsqrt(ms + NORM_EPS) * g


def _get_rows(ref3, start=0, size=None, stride=1):
    idx = slice(None) if size is None else pl.ds(start, size, stride=stride)
    return jnp.concatenate([ref3[c, idx, :] for c in range(ref3.shape[0])], axis=1)


def _put_rows(ref3, val, start=0, stride=1):
    size = val.shape[0]
    idx = slice(None) if (stride == 1 and size == ref3.shape[1]) else pl.ds(start, size, stride=stride)
    for c in range(ref3.shape[0]):
        ref3[c, idx, :] = val[:, c * LANES:(c + 1) * LANES]


def _rmsnorm_kernel(x_ref, g_ref, o_ref):
    o_ref[...] = _rms(x_ref[...], g_ref[...]).astype(o_ref.dtype)


def _rmsnorm(x, g, *, tm=1024):
    m, k = x.shape
    tm = min(tm, m)
    return pl.pallas_call(
        _rmsnorm_kernel,
        grid=(m // tm,),
        in_specs=[pl.BlockSpec((tm, k), lambda i: (i, 0)), pl.BlockSpec((1, k), lambda i: (0, 0))],
        out_specs=pl.BlockSpec((tm, k), lambda i: (i, 0)),
        out_shape=jax.ShapeDtypeStruct((m, k), BF16),
        compiler_params=_cparams(("parallel",)),
        name="rmsnorm",
    )(x, g.reshape(1, k))


def _mm_kernel(*refs, epilogue, head_dim):
    h_ref, w_ref = refs[:2]
    acc = jnp.dot(h_ref[...], w_ref[...], preferred_element_type=F32)
    tn = acc.shape[1]
    if epilogue == "headnorm":
        hg_ref, o_ref = refs[2:]
        for hh in range(tn // head_dim):
            sl = slice(hh * head_dim, (hh + 1) * head_dim)
            o_ref[:, sl] = _rms(acc[:, sl], hg_ref[...]).astype(o_ref.dtype)
    elif epilogue == "sigmoid_bias":
        b_ref, o_ref = refs[2:]
        o_ref[...] = jax.nn.sigmoid(acc + b_ref[...]).astype(o_ref.dtype)
    elif epilogue == "chandft":
        dft_ref, o_ref = refs[2:]
        a16 = acc.astype(BF16)
        for gg in range(tn // F_GD):
            sl = slice(gg * F_GD, (gg + 1) * F_GD)
            r = jnp.dot(a16[:, sl], dft_ref[...], preferred_element_type=F32)
            o_ref[0, :, sl] = r[:, :F_GD].astype(o_ref.dtype)
            o_ref[1, :, sl] = r[:, F_GD:].astype(o_ref.dtype)
    else:
        o_ref = refs[2]
        o_ref[...] = acc.astype(o_ref.dtype)


def _matmul(h, w, *, out_dtype, tn, tm=1024, epilogue="plain", extra=None, head_dim=None, col0=0, n=None):
    m, k = h.shape
    n = w.shape[1] if n is None else n
    tm = min(tm, m)
    assert m % tm == 0 and n % tn == 0 and col0 % tn == 0
    cb0 = col0 // tn
    in_specs = [pl.BlockSpec((tm, k), lambda i, j: (i, 0)), pl.BlockSpec((k, tn), lambda i, j: (0, cb0 + j))]
    args = [h, w]
    if epilogue == "headnorm":
        in_specs.append(pl.BlockSpec((1, head_dim), lambda i, j: (0, 0)))
        args.append(extra.reshape(1, head_dim))
    elif epilogue == "sigmoid_bias":
        in_specs.append(pl.BlockSpec((1, tn), lambda i, j: (0, j)))
        args.append(extra.reshape(1, n))
    elif epilogue == "chandft":
        in_specs.append(pl.BlockSpec(extra.shape, lambda i, j: (0, 0)))
        args.append(extra)
    if epilogue == "chandft":
        out_shape = jax.ShapeDtypeStruct((2, m, n), out_dtype)
        out_spec = pl.BlockSpec((2, tm, tn), lambda i, j: (0, i, j))
    else:
        out_shape = jax.ShapeDtypeStruct((m, n), out_dtype)
        out_spec = pl.BlockSpec((tm, tn), lambda i, j: (i, j))
    return pl.pallas_call(
        functools.partial(_mm_kernel, epilogue=epilogue, head_dim=head_dim),
        grid=(m // tm, n // tn),
        in_specs=in_specs,
        out_specs=out_spec,
        out_shape=out_shape,
        compiler_params=_cparams(("parallel", "parallel")),
        name="matmul_" + epilogue,
    )(*args)


def _qkv_kernel(h_ref, w_ref, hg_ref, o_ref, *scratch, dil):
    j = pl.program_id(1)
    acc = jnp.dot(h_ref[...], w_ref[...], preferred_element_type=F32)
    tm, width = acc.shape
    rows = tm // dil
    nh = width // ATTN_HD
    if dil > 1:
        scr = scratch[0]
        for hh in range(nh):
            scr[hh] = acc[:, hh * ATTN_HD:(hh + 1) * ATTN_HD]
    for r in range(dil):
        for hh in range(nh):
            if dil == 1:
                ph = acc[:, hh * ATTN_HD:(hh + 1) * ATTN_HD]
            else:
                ph = scr[hh, pl.ds(r, rows, stride=dil), :]
            val = jnp.where(j < 2, _rms(ph, hg_ref[...]), ph)
            c0 = r * width + hh * ATTN_HD
            o_ref[:, c0:c0 + ATTN_HD] = val.astype(o_ref.dtype)


def _qkv_group(h, w, gains, gi, dil, bsz, seq, *, tm=2048):
    m, k = h.shape
    sub = seq // dil
    per_b = seq // tm
    rows = tm // dil
    assert seq % tm == 0 and rows % BF16_ROWS == 0
    return pl.pallas_call(
        functools.partial(_qkv_kernel, dil=dil),
        grid=(m // tm, 3),
        in_specs=[
            pl.BlockSpec((tm, k), lambda i, j: (i, 0)),
            pl.BlockSpec((k, ATTN_OUT), lambda i, j: (0, j * len(ATTN_GROUPS) + gi)),
            pl.BlockSpec((None, 1, ATTN_HD), lambda i, j: (j, 0, 0)),
        ],
        out_specs=pl.BlockSpec((None, None, rows, dil * ATTN_OUT), lambda i, j: (j, i // per_b, i % per_b, 0)),
        out_shape=jax.ShapeDtypeStruct((3, bsz, sub, dil * ATTN_OUT), BF16),
        scratch_shapes=[pltpu.VMEM((ATTN_HPG, tm, ATTN_HD), F32)] if dil > 1 else [],
        compiler_params=_cparams(("parallel", "parallel")),
        name=f"qkv_proj_d{dil}",
    )(h, w, gains)


def _t5_bucket_np(rel):
    half_b = NUM_BUCKETS // 2
    exact = half_b // 2
    dist = np.abs(rel)
    log_ratio = np.log(np.maximum(dist, 1) / exact) / np.log(MAX_DISTANCE / exact)
    far = np.minimum(exact + (log_ratio * (half_b - exact)).astype(np.int32), half_b - 1)
    return np.where(rel > 0, half_b, 0) + np.where(dist < exact, dist, far)


def _attn_bias(rel_bias, gi, dil):
    i = np.arange(ATTN_TQ)[:, None]
    j = np.arange(ATTN_TK)[None, :]
    rel = j - ATTN_HALF - i
    idx = _t5_bucket_np(rel * dil)
    onehot = jnp.asarray(np.eye(NUM_BUCKETS, dtype=np.float32)[idx])
    tab = rel_bias[:, gi * ATTN_HPG:(gi + 1) * ATTN_HPG].astype(F32)
    b = jnp.einsum("qkn,nh->hqk", onehot, tab, precision=lax.Precision.HIGHEST)
    return jnp.where(jnp.asarray(np.abs(rel) <= ATTN_HALF)[None], b, MASK_VALUE)


def _attn_kernel(q_ref, kp_ref, kc_ref, kn_ref, vp_ref, vc_ref, vn_ref, bias_ref, o_ref, lse_ref, kpad, vpad,
                 *, tile, sub):
    i = pl.program_id(2)
    hf = ATTN_HALF
    kpad[0:hf, :] = kp_ref[...]
    kpad[hf:hf + tile, :] = kc_ref[...]
    kpad[hf + tile:, :] = kn_ref[...]
    vpad[0:hf, :] = vp_ref[...]
    vpad[hf:hf + tile, :] = vc_ref[...]
    vpad[hf + tile:, :] = vn_ref[...]

    scale = ATTN_HD ** -0.5
    col = lax.broadcasted_iota(jnp.int32, (ATTN_TQ, ATTN_TK), 1)
    lane = lax.broadcasted_iota(jnp.int32, (ATTN_TQ, LANES), 1)

    def body(t, carry):
        s0 = pl.multiple_of(t * ATTN_TQ, ATTN_TQ)
        kpos = col + (i * tile + s0 - hf)
        valid = (kpos >= 0) & (kpos < sub)
        lse_tile = jnp.zeros((ATTN_TQ, LANES), F32)
        for hh in range(ATTN_HPG):
            sl = slice(hh * ATTN_HD, (hh + 1) * ATTN_HD)
            q = q_ref[pl.ds(s0, ATTN_TQ), sl]
            kw = kpad[pl.ds(s0, ATTN_TK), sl]
            vw = vpad[pl.ds(s0, ATTN_TK), sl]
            s = lax.dot_general(q, kw, (((1,), (1,)), ((), ())), preferred_element_type=F32) * scale + bias_ref[hh]
            s = jnp.where(valid, s, MASK_VALUE)
            m = jnp.max(s, axis=-1, keepdims=True)
            p = jnp.exp(s - m)
            l = jnp.sum(p, axis=-1, keepdims=True)
            o = jnp.dot(p.astype(BF16), vw, preferred_element_type=F32) / l
            o_ref[pl.ds(s0, ATTN_TQ), sl] = o.astype(o_ref.dtype)
            lse_tile = jnp.where(lane == hh, m + jnp.log(l), lse_tile)
        lse_ref[pl.ds(s0, ATTN_TQ), :] = lse_tile
        return carry

    lax.fori_loop(0, tile // ATTN_TQ, body, 0, unroll=4)


def _attn_group(qkv, bias, dil, bsz, seq, *, tile=1024):
    sub = seq // dil
    tile = min(tile, sub)
    nblk = sub // tile
    hb = tile // ATTN_HALF
    last_h = sub // ATTN_HALF - 1
    assert sub % tile == 0 and tile % ATTN_TQ == 0
    cur = lambda which: pl.BlockSpec((None, None, tile, ATTN_OUT), lambda b, r, i: (which, b, i, r))
    prev = lambda which: pl.BlockSpec((None, None, ATTN_HALF, ATTN_OUT),
                                      lambda b, r, i: (which, b, jnp.maximum(i * hb - 1, 0), r))
    nxt = lambda which: pl.BlockSpec((None, None, ATTN_HALF, ATTN_OUT),
                                     lambda b, r, i: (which, b, jnp.minimum((i + 1) * hb, last_h), r))
    return pl.pallas_call(
        functools.partial(_attn_kernel, tile=tile, sub=sub),
        grid=(bsz, dil, nblk),
        in_specs=[cur(0), prev(1), cur(1), nxt(1), prev(2), cur(2), nxt(2),
                  pl.BlockSpec((ATTN_HPG, ATTN_TQ, ATTN_TK), lambda b, r, i: (0, 0, 0))],
        out_specs=(pl.BlockSpec((None, tile, ATTN_OUT), lambda b, r, i: (b, i, r)),
                   pl.BlockSpec((None, tile, LANES), lambda b, r, i: (b, i, r))),
        out_shape=(jax.ShapeDtypeStruct((bsz, sub, dil * ATTN_OUT), BF16),
                   jax.ShapeDtypeStruct((bsz, sub, dil * LANES), F32)),
        scratch_shapes=[pltpu.VMEM((tile + 2 * ATTN_HALF, ATTN_OUT), BF16)] * 2,
        compiler_params=_cparams(("parallel", "parallel", "parallel")),
        name=f"dilated_attn_d{dil}",
    )(qkv, qkv, qkv, qkv, qkv, qkv, qkv, bias)


CONV_HALO = BF16_ROWS
CONV_RC = 64
CONV_APRON = 8


SUBLANES = 8


def _shift_rows(x, s):
    r, c = x.shape
    x3 = x.reshape(r // SUBLANES, SUBLANES, c)
    rot = pltpu.roll(x3, (-s) % SUBLANES, 1)
    sub = lax.broadcasted_iota(jnp.int32, x3.shape, 1)
    if s > 0:
        y = jnp.where(sub < SUBLANES - s, rot, jnp.concatenate([rot[1:], rot[:1]], axis=0))
    else:
        y = jnp.where(sub >= -s, rot, jnp.concatenate([rot[-1:], rot[:-1]], axis=0))
    return y.reshape(r, c)


def _proj_conv_kernel(hp_ref, hc_ref, hn_ref, w_ref, cw_ref, cb_ref, o_ref, xp_ref, *, tm, per_b):
    i = pl.program_id(0)
    first = (i % per_b) == 0
    last = (i % per_b) == per_b - 1
    w = w_ref[...]
    halo_p = jnp.dot(hp_ref[...], w, preferred_element_type=F32)
    halo_n = jnp.dot(hn_ref[...], w, preferred_element_type=F32)
    xp_ref[0:CONV_HALO, :] = jnp.where(first, 0.0, halo_p)
    xp_ref[CONV_HALO:CONV_HALO + tm, :] = jnp.dot(hc_ref[...], w, preferred_element_type=F32)
    xp_ref[CONV_HALO + tm:, :] = jnp.where(last, 0.0, halo_n)
    rows = CONV_RC + 2 * CONV_APRON
    pad = SSD_CONV // 2
    for c in range(tm // CONV_RC):
        r0 = CONV_HALO + c * CONV_RC - CONV_APRON
        x = xp_ref[r0:r0 + rows, :]
        y = cw_ref[pad:pad + 1, :] * x + cb_ref[...]
        for k in range(SSD_CONV):
            if k != pad:
                y = y + cw_ref[k:k + 1, :] * _shift_rows(x, k - pad)
        o_ref[c * CONV_RC:(c + 1) * CONV_RC, :] = _silu(y[CONV_APRON:CONV_APRON + CONV_RC]).astype(o_ref.dtype)


def _proj_conv_silu(h, w, cw, cb, seq, *, col0, tm=1024, tn=512):
    m, k = h.shape
    n = cw.shape[1]
    assert col0 % tn == 0
    cb0 = col0 // tn
    per_b = seq // tm
    hb = tm // CONV_HALO
    assert seq % tm == 0 and n % tn == 0 and SSD_CONV == 7
    return pl.pallas_call(
        functools.partial(_proj_conv_kernel, tm=tm, per_b=per_b),
        grid=(m // tm, n // tn),
        in_specs=[
            pl.BlockSpec((CONV_HALO, k), lambda i, j: (jnp.maximum(i * hb - 1, 0), 0)),
            pl.BlockSpec((tm, k), lambda i, j: (i, 0)),
            pl.BlockSpec((CONV_HALO, k), lambda i, j: (jnp.minimum((i + 1) * hb, m // CONV_HALO - 1), 0)),
            pl.BlockSpec((k, tn), lambda i, j: (0, cb0 + j)),
            pl.BlockSpec((SSD_CONV, tn), lambda i, j: (0, j)),
            pl.BlockSpec((1, tn), lambda i, j: (0, j)),
        ],
        out_specs=pl.BlockSpec((tm, tn), lambda i, j: (i, j)),
        out_shape=jax.ShapeDtypeStruct((m, n), BF16),
        scratch_shapes=[pltpu.VMEM((tm + 2 * CONV_HALO, tn), F32)],
        compiler_params=_cparams(("parallel", "parallel")),
        name="proj_conv_silu",
    )(h, h, h, w, cw, cb.reshape(1, n))


def _softplus(x):
    return jnp.maximum(x, 0.0) + jnp.log1p(jnp.exp(-jnp.abs(x)))


def _ssd_prep_kernel(dtr_ref, dtb_ref, alog_ref, ccol_ref, rt_ref, esc_ref, wst_ref, etot_ref, *, t):
    half = LANES // 2
    log2e = math.log2(math.e)
    lane = lax.broadcasted_iota(jnp.int32, (t, LANES), 1)
    fwd = lane < SSD_HEADS
    tri = (lax.broadcasted_iota(jnp.int32, (t, t), 1) <= lax.broadcasted_iota(jnp.int32, (t, t), 0)).astype(BF16)
    neg_a = jnp.exp(alog_ref[...])
    for ci in range(dtr_ref.shape[0] // t):
        rs = slice(ci * t, (ci + 1) * t)
        dt = _softplus(dtr_ref[rs, :] + dtb_ref[...])
        la = jnp.where(lane < 2 * SSD_HEADS, -(dt * neg_a), 0.0)
        hi = la.astype(BF16)
        r1 = la - hi.astype(F32)
        mid = r1.astype(BF16)
        lo = (r1 - mid.astype(F32)).astype(BF16)
        packed = (hi.astype(F32) + pltpu.roll(mid.astype(F32), half, 1)).astype(BF16)
        res = jnp.dot(tri, jnp.concatenate([packed, lo], axis=1), preferred_element_type=F32)
        a0 = res[:, :LANES]
        acs = a0 + pltpu.roll(a0, half, 1) + res[:, LANES:]
        exb = acs - la
        ldt = jnp.log(dt)
        tot = acs[t - 1:t, :]
        ccol_ref[rs, :] = jnp.where(fwd, acs, exb) * log2e
        rt_ref[ci] = (jnp.where(fwd, acs - ldt, exb + ldt) * log2e).T
        esc_ref[rs, :] = jnp.exp(jnp.where(fwd, acs, tot - exb))
        wst_ref[rs, :] = jnp.exp(jnp.where(fwd, tot - acs, exb)) * dt
        etot_ref[ci] = jnp.broadcast_to(jnp.exp(tot), etot_ref.shape[1:])


def _split2(v):
    hi = v.astype(BF16)
    return jnp.concatenate([hi, (v - hi.astype(F32)).astype(BF16)], axis=1)


def _head_expand_tables():
    j = np.arange(2 * LANES)[None, :, None] % LANES
    g = np.arange(SSD_GROUPS)[:, None, None]
    c = np.arange(SSD_GW)[None, None, :]
    fwd = (j == g * SSD_HPG + c // SSD_HD)
    bwd = (j == SSD_HEADS + g * SSD_HPG + c // SSD_HD)
    return jnp.asarray(fwd, BF16), jnp.asarray(bwd, BF16)


def _ssd_bwd_state_kernel(xs_ref, b_ref, wst_ref, etot_ref, selb_ref, gin_ref, g_sc):
    @pl.when(pl.program_id(1) == 0)
    def _():
        g_sc[...] = jnp.zeros(g_sc.shape, F32)

    wst2 = _split2(wst_ref[...])
    etot2 = _split2(etot_ref[...])
    gin_ref[...] = g_sc[...].astype(gin_ref.dtype)
    for g in range(SSD_GROUPS):
        gs = slice(g * SSD_GW, (g + 1) * SSD_GW)
        bm_t = b_ref[:, g * SSD_STATE:(g + 1) * SSD_STATE].T
        wts = jnp.dot(wst2, selb_ref[g], preferred_element_type=F32)
        dec = jnp.dot(etot2, selb_ref[g], preferred_element_type=F32)[0:1]
        xw = (xs_ref[:, gs].astype(F32) * wts).astype(BF16)
        st = jnp.dot(bm_t, xw, preferred_element_type=F32)
        g_sc[:, gs] = g_sc[:, gs] * dec + st


def _ssd_main_kernel(xs_ref, b_ref, c_ref, ccol_ref, rt_ref, esc_ref, wst_ref, etot_ref, gin_ref,
                     dsk_ref, self_ref, selb_ref, y_ref, h_sc):
    @pl.when(pl.program_id(1) == 0)
    def _():
        h_sc[...] = jnp.zeros(h_sc.shape, F32)

    t = xs_ref.shape[0]
    ccol = ccol_ref[...]
    r_t = rt_ref[...]
    esc2 = _split2(esc_ref[...])
    wst2 = _split2(wst_ref[...])
    etot2 = _split2(etot_ref[...])

    li = lax.broadcasted_iota(jnp.int32, (t, t), 0)
    si = lax.broadcasted_iota(jnp.int32, (t, t), 1)
    mask_f = jnp.where(li >= si, 0.0, MASK_VALUE)
    mask_b = jnp.where(si >= li, 0.0, MASK_VALUE)
    lo_half = lax.broadcasted_iota(jnp.int32, (t, LANES), 1) < SSD_HD

    for g in range(SSD_GROUPS):
        gs = slice(g * SSD_GW, (g + 1) * SSD_GW)
        bm = b_ref[:, g * SSD_STATE:(g + 1) * SSD_STATE]
        cm = c_ref[:, g * SSD_STATE:(g + 1) * SSD_STATE]
        cb = lax.dot_general(cm, bm, (((1,), (1,)), ((), ())), preferred_element_type=F32)
        hf = h_sc[:, gs]
        yf_all = jnp.dot(cm, hf.astype(BF16), preferred_element_type=F32)
        yb_all = jnp.dot(cm, gin_ref[:, gs], preferred_element_type=F32)
        e_in = jnp.dot(esc2, self_ref[g], preferred_element_type=F32)
        e_out = jnp.dot(esc2, selb_ref[g], preferred_element_type=F32)
        w_state = jnp.dot(wst2, self_ref[g], preferred_element_type=F32)
        dec = jnp.dot(etot2, self_ref[g], preferred_element_type=F32)[0:1]
        xw = []
        for k in range(SSD_HPG // 2):
            ea = g * SSD_HPG + 2 * k
            sl = slice(g * SSD_GW + k * LANES, g * SSD_GW + (k + 1) * LANES)
            ks = slice(k * LANES, (k + 1) * LANES)
            xp = xs_ref[:, sl]
            mms = []
            for e in (ea, ea + 1):
                eb = SSD_HEADS + e
                w = (jnp.exp2(ccol[:, e:e + 1] - r_t[e:e + 1, :] + mask_f)
                     + jnp.exp2(r_t[eb:eb + 1, :] - ccol[:, eb:eb + 1] + mask_b))
                mms.append((cb * w).astype(BF16))
            zero = jnp.zeros_like(xp)
            xx = jnp.concatenate([jnp.where(lo_half, xp, zero), jnp.where(lo_half, zero, xp)], axis=0)
            ypair = jnp.dot(jnp.concatenate(mms, axis=1), xx, preferred_element_type=F32)
            xf = xp.astype(F32)
            y_ref[:, sl] = (ypair + yf_all[:, ks] * e_in[:, ks] + yb_all[:, ks] * e_out[:, ks]
                            + xf * dsk_ref[:, sl])
            xw.append((xf * w_state[:, ks]).astype(BF16))
        st = jnp.dot(bm.T, jnp.concatenate(xw, axis=1), preferred_element_type=F32)
        h_sc[:, gs] = hf * dec + st


def _ssd(conv_out, dt_raw, dtb, alog, dskip, bsz, seq):
    t = SSD_CHUNK
    nc = seq // t
    assert seq % t == 0
    gn = SSD_GROUPS * SSD_STATE
    b_blk = SSD_INNER // gn
    c_blk = b_blk + 1
    cpp = min(8, nc)
    assert nc % cpp == 0
    sub8 = 8
    const_spec = pl.BlockSpec((1, LANES), lambda b, i: (0, 0))
    rows_spec = pl.BlockSpec((None, cpp * t, LANES), lambda b, i: (b, i, 0))
    row_arr = jax.ShapeDtypeStruct((bsz, seq, LANES), F32)
    ccol, rt, esc, wst, etot = pl.pallas_call(
        functools.partial(_ssd_prep_kernel, t=t),
        grid=(bsz, nc // cpp),
        in_specs=[rows_spec, const_spec, const_spec],
        out_specs=(rows_spec, pl.BlockSpec((None, cpp, LANES, t), lambda b, i: (b, i, 0, 0)), rows_spec, rows_spec,
                   pl.BlockSpec((None, cpp, sub8, LANES), lambda b, i: (b, i, 0, 0))),
        out_shape=(row_arr, jax.ShapeDtypeStruct((bsz, nc, LANES, t), F32), row_arr, row_arr,
                   jax.ShapeDtypeStruct((bsz, nc, sub8, LANES), F32)),
        compiler_params=_cparams(("parallel", "parallel")),
        name="ssd_decay_terms",
    )(dt_raw, dtb, alog)

    rev = lambda c: nc - 1 - c
    sel_f, sel_b = _head_expand_tables()
    sel_spec = pl.BlockSpec(sel_f.shape, lambda b, c: (0, 0, 0))
    gin = pl.pallas_call(
        _ssd_bwd_state_kernel,
        grid=(bsz, nc),
        in_specs=[
            pl.BlockSpec((None, t, SSD_INNER), lambda b, c: (b, rev(c), 0)),
            pl.BlockSpec((None, t, gn), lambda b, c: (b, rev(c), b_blk)),
            pl.BlockSpec((None, t, LANES), lambda b, c: (b, rev(c), 0)),
            pl.BlockSpec((None, None, sub8, LANES), lambda b, c: (b, rev(c), 0, 0)),
            sel_spec,
        ],
        out_specs=pl.BlockSpec((None, None, SSD_STATE, SSD_INNER), lambda b, c: (b, rev(c), 0, 0)),
        out_shape=jax.ShapeDtypeStruct((bsz, nc, SSD_STATE, SSD_INNER), BF16),
        scratch_shapes=[pltpu.VMEM((SSD_STATE, SSD_INNER), F32)],
        compiler_params=_cparams(("parallel", "arbitrary")),
        name="ssd_bwd_states",
    )(conv_out, conv_out, wst, etot, sel_b)

    wide_spec = pl.BlockSpec((1, SSD_INNER), lambda b, c: (0, 0))
    tok_spec = pl.BlockSpec((None, t, LANES), lambda b, c: (b, c, 0))
    return pl.pallas_call(
        _ssd_main_kernel,
        grid=(bsz, nc),
        in_specs=[
            pl.BlockSpec((None, t, SSD_INNER), lambda b, c: (b, c, 0)),
            pl.BlockSpec((None, t, gn), lambda b, c: (b, c, b_blk)),
            pl.BlockSpec((None, t, gn), lambda b, c: (b, c, c_blk)),
            tok_spec,
            pl.BlockSpec((None, None, LANES, t), lambda b, c: (b, c, 0, 0)),
            tok_spec, tok_spec,
            pl.BlockSpec((None, None, sub8, LANES), lambda b, c: (b, c, 0, 0)),
            pl.BlockSpec((None, None, SSD_STATE, SSD_INNER), lambda b, c: (b, c, 0, 0)),
            wide_spec, sel_spec, sel_spec,
        ],
        out_specs=pl.BlockSpec((None, t, SSD_INNER), lambda b, c: (b, c, 0)),
        out_shape=jax.ShapeDtypeStruct((bsz, seq, SSD_INNER), F32),
        scratch_shapes=[pltpu.VMEM((SSD_STATE, SSD_INNER), F32)],
        compiler_params=_cparams(("parallel", "arbitrary")),
        name="ssd_main",
    )(conv_out, conv_out, conv_out, ccol, rt, esc, wst, etot, gin, dskip, sel_f, sel_b)


def _dft_tables(seq):
    n1, n2 = seq // F_N2, F_N2
    c = np.arange(F_GD)
    ang = 2 * np.pi * np.outer(c, c) / F_GD
    chan = np.concatenate([np.cos(ang), -np.sin(ang)], axis=1) / math.sqrt(F_GD)
    k1 = np.arange(n1)
    a1 = 2 * np.pi * np.outer(k1, k1) / n1
    stage_a = np.block([[np.cos(a1), np.sin(a1)], [-np.sin(a1), np.cos(a1)]])
    stage_a = np.kron(stage_a, np.eye(2))
    s2 = np.arange(n2)
    at = 2 * np.pi * np.outer(s2, k1) / seq
    at = at.reshape(n2 // 2, 2, n1).transpose(0, 2, 1).reshape(n2 // 2, 2 * n1)
    tw_c, tw_s = np.cos(at)[..., None], np.sin(at)[..., None]
    a2 = 2 * np.pi * np.outer(s2, s2) / n2
    stage_b = np.concatenate([np.cos(a2), np.sin(a2)], axis=1)
    return (jnp.asarray(chan, BF16), jnp.asarray(stage_a, BF16), jnp.asarray(tw_c, F32),
            jnp.asarray(tw_s, F32), jnp.asarray(stage_b, BF16))


def _fft_a_kernel(z_ref, ma_ref, twc_ref, tws_ref, o_ref, zs, ys, *, n1, rb):
    ct = z_ref.shape[-1]
    half = rb // 2
    _put_rows(zs, pltpu.bitcast(z_ref[...].reshape(2 * n1 * rb, ct), jnp.uint32))
    for j in range(half):
        zz = pltpu.bitcast(_get_rows(zs, j, 2 * n1, half), BF16)
        y = jnp.dot(ma_ref[...], zz, preferred_element_type=F32)
        yr, yi = y[:2 * n1], y[2 * n1:]
        tc, ts = twc_ref[j], tws_ref[j]
        out = jnp.concatenate([yr * tc + yi * ts, yi * tc - yr * ts], axis=0).astype(BF16)
        _put_rows(ys, pltpu.bitcast(out, jnp.uint32), j, half)
    o_ref[...] = pltpu.bitcast(_get_rows(ys), BF16).reshape(2, n1, rb, ct)


def _fft_b_kernel(y_ref, mb_ref, o_ref, os_ref, *, n2, kb, scale):
    ct = y_ref.shape[-1]
    for kk in range(kb):
        yy = y_ref[:, kk].reshape(2 * n2, ct)
        _put_rows(os_ref, jnp.dot(mb_ref[...], yy, preferred_element_type=F32) * scale, kk, kb)
    o_ref[...] = _get_rows(os_ref).reshape(n2, kb, ct).astype(o_ref.dtype)


def _fourier_seq(z2, tabs, bsz, seq, *, rb=BF16_ROWS, kb=BF16_ROWS, ct=768):
    _, ma, twc, tws, mb = tabs
    n1, n2 = seq // F_N2, F_N2
    w = z2.shape[-1]
    nct = w // ct
    za = z2.reshape(2, bsz, n1, n2, w)
    ab_spec = pl.BlockSpec((2, None, n1, rb, ct), lambda b, a, j: (0, b, 0, a, j))
    ya = pl.pallas_call(
        functools.partial(_fft_a_kernel, n1=n1, rb=rb),
        grid=(bsz, n2 // rb, nct),
        in_specs=[
            ab_spec,
            pl.BlockSpec((4 * n1, 4 * n1), lambda b, a, j: (0, 0)),
            pl.BlockSpec((rb // 2, 2 * n1, 1), lambda b, a, j: (a, 0, 0)),
            pl.BlockSpec((rb // 2, 2 * n1, 1), lambda b, a, j: (a, 0, 0)),
        ],
        out_specs=ab_spec,
        out_shape=jax.ShapeDtypeStruct((2, bsz, n1, n2, w), BF16),
        scratch_shapes=[pltpu.VMEM((ct // LANES, n1 * rb, LANES), jnp.uint32)] * 2,
        compiler_params=_cparams(("parallel", "parallel", "parallel")),
        name="fourier_stage_a",
    )(za, ma, twc, tws)
    out = pl.pallas_call(
        functools.partial(_fft_b_kernel, n2=n2, kb=kb, scale=1.0 / math.sqrt(seq)),
        grid=(bsz, n1 // kb, nct),
        in_specs=[
            pl.BlockSpec((2, None, kb, n2, ct), lambda b, k, j: (0, b, k, 0, j)),
            pl.BlockSpec((n2, 2 * n2), lambda b, k, j: (0, 0)),
        ],
        out_specs=pl.BlockSpec((None, n2, kb, ct), lambda b, k, j: (b, 0, k, j)),
        out_shape=jax.ShapeDtypeStruct((bsz, n2, n1, w), BF16),
        scratch_shapes=[pltpu.VMEM((ct // LANES, n2 * kb, LANES), F32)],
        compiler_params=_cparams(("parallel", "parallel", "parallel")),
        name="fourier_stage_b",
    )(ya, mb)
    return out.reshape(bsz, seq, w)


def _merge_kernel(*refs, dils):
    ng = len(dils)
    x_ref = refs[0]
    o_refs = refs[1:1 + ng]
    l_refs = refs[1 + ng:1 + 2 * ng]
    y_ref, z_ref, yg_ref, f_ref, gt_ref, wa_ref, ws_ref, wf_ref, wo_ref, out_ref, o_sc, l_sc = refs[1 + 2 * ng:]
    tm, d = x_ref.shape

    for gi, dil in enumerate(dils):
        rows = tm // dil
        idx = lambda r: slice(None) if dil == 1 else pl.ds(r, rows, stride=dil)
        for r in range(dil):
            l_sc[gi, idx(r), :] = l_refs[gi][:, r * LANES:(r + 1) * LANES]
            for hh in range(ATTN_HPG):
                c0 = r * ATTN_OUT + hh * ATTN_HD
                o_sc[gi * ATTN_HPG + hh, idx(r), :] = o_refs[gi][:, c0:c0 + ATTN_HD].astype(F32)

    ls = [l_sc[gi] for gi in range(ng)]
    mx = functools.reduce(jnp.maximum, ls)
    ws = [jnp.exp(v - mx) for v in ls]
    inv = 1.0 / functools.reduce(lambda a, b: a + b, ws)
    heads = []
    for hh in range(ATTN_HPG):
        acc = None
        for gi in range(ng):
            alpha = (ws[gi] * inv)[:, hh:hh + 1]
            term = alpha * o_sc[gi * ATTN_HPG + hh]
            acc = term if acc is None else acc + term
        heads.append(acc.astype(BF16))
    o_attn = jnp.concatenate(heads, axis=1)

    ya = jnp.dot(o_attn, wa_ref[...], preferred_element_type=F32)
    y_parts = []
    for g in range(SSD_GROUPS):
        gs = slice(g * SSD_GW, (g + 1) * SSD_GW)
        yy = y_ref[:, gs] * _silu(z_ref[:, gs].astype(F32))
        y_parts.append(_rms(yy, yg_ref[:, gs]).astype(BF16))
    ys = jnp.dot(jnp.concatenate(y_parts, axis=1), ws_ref[...], preferred_element_type=F32)
    yf = jnp.dot(f_ref[...], wf_ref[...], preferred_element_type=F32)
    gates = gt_ref[...].astype(F32)
    merged = gates[:, :d] * ya + gates[:, d:2 * d] * ys + gates[:, 2 * d:] * yf
    out_ref[...] = x_ref[...] + jnp.dot(merged.astype(BF16), wo_ref[...], preferred_element_type=F32)


def _merge(x, os_, ls_, dils, y, z, y_gain, f, gates, wa, ws, wf, wo, bsz, seq, *, tm=256):
    d = x.shape[-1]
    assert all((tm // dil) % BF16_ROWS == 0 for dil in dils)
    tok = lambda width: pl.BlockSpec((None, tm, width), lambda b, i: (b, i, 0))
    full = lambda a: pl.BlockSpec(a.shape, lambda b, i: (0, 0))
    ng = len(dils)
    in_specs = [tok(d)]
    in_specs += [pl.BlockSpec((None, tm // dil, dil * ATTN_OUT), lambda b, i: (b, i, 0)) for dil in dils]
    in_specs += [pl.BlockSpec((None, tm // dil, dil * LANES), lambda b, i: (b, i, 0)) for dil in dils]
    in_specs += [tok(y.shape[-1]), tok(z.shape[-1]), full(y_gain), tok(f.shape[-1]), tok(gates.shape[-1]),
                 full(wa), full(ws), full(wf), full(wo)]
    return pl.pallas_call(
        functools.partial(_merge_kernel, dils=tuple(dils)),
        grid=(bsz, seq // tm),
        in_specs=in_specs,
        out_specs=tok(d),
        out_shape=jax.ShapeDtypeStruct((bsz, seq, d), F32),
        scratch_shapes=[pltpu.VMEM((ng * ATTN_HPG, tm, ATTN_HD), F32), pltpu.VMEM((ng, tm, LANES), F32)],
        compiler_params=_cparams(("parallel", "parallel")),
        name="branch_merge",
    )(x, *os_, *ls_, y, z, y_gain, f, gates, wa, ws, wf, wo)


def _xattn_kernel(x_ref, g_ref, wq_ref, qg_ref, k_ref, v_ref, wo_ref, out_ref, o_sc):
    x = x_ref[...]
    d = x.shape[1]
    hd = d // MEM_HEADS
    h = _rms(x, g_ref[...]).astype(BF16)
    q = jnp.dot(h, wq_ref[...], preferred_element_type=F32)
    scale = hd ** -0.5
    for hh in range(MEM_HEADS):
        sl = slice(hh * hd, (hh + 1) * hd)
        qn = _rms(q[:, sl], qg_ref[...]).astype(BF16)
        s = lax.dot_general(qn, k_ref[:, sl], (((1,), (1,)), ((), ())), preferred_element_type=F32) * scale
        m = jnp.max(s, axis=-1, keepdims=True)
        p = jnp.exp(s - m)
        l = jnp.sum(p, axis=-1, keepdims=True)
        o_sc[:, sl] = jnp.dot(p.astype(BF16), v_ref[:, sl], preferred_element_type=F32) / l
    out_ref[...] = x + jnp.dot(o_sc[...].astype(BF16), wo_ref[...], preferred_element_type=F32)


def _xattn(x, g, wq, qg, k, v, wo, *, tm=512):
    bsz, seq, d = x.shape
    mt = k.shape[1]
    full = lambda a: pl.BlockSpec(a.shape, lambda b, i: (0, 0))
    g = g.reshape(1, d)
    qg = qg.reshape(1, -1)
    return pl.pallas_call(
        _xattn_kernel,
        grid=(bsz, seq // tm),
        in_specs=[
            pl.BlockSpec((None, tm, d), lambda b, i: (b, i, 0)),
            full(g), full(wq), full(qg),
            pl.BlockSpec((None, mt, d), lambda b, i: (b, 0, 0)),
            pl.BlockSpec((None, mt, d), lambda b, i: (b, 0, 0)),
            full(wo),
        ],
        out_specs=pl.BlockSpec((None, tm, d), lambda b, i: (b, i, 0)),
        out_shape=jax.ShapeDtypeStruct((bsz, seq, d), F32),
        scratch_shapes=[pltpu.VMEM((tm, d), F32)],
        compiler_params=_cparams(("parallel", "parallel")),
        name="mem_xattn",
    )(x, g, wq, qg, k, v, wo)


def _ffn_kernel(x_ref, g_ref, wg_ref, wu_ref, wd_ref, out_ref, *, tf):
    x = x_ref[...]
    h = _rms(x, g_ref[...]).astype(BF16)
    acc = x
    for c in range(wg_ref.shape[1] // tf):
        sl = slice(c * tf, (c + 1) * tf)
        gt = jnp.dot(h, wg_ref[:, sl], preferred_element_type=F32)
        up = jnp.dot(h, wu_ref[:, sl], preferred_element_type=F32)
        a = (_silu(gt) * up).astype(BF16)
        acc = acc + jnp.dot(a, wd_ref[sl, :], preferred_element_type=F32)
    out_ref[...] = acc


def _ffn(x, g, wg, wu, wd, *, tm=256):
    m, d = x.shape
    dff = wg.shape[1]
    tf = dff // 2 if (dff // 2) % LANES == 0 else dff
    full = lambda a: pl.BlockSpec(a.shape, lambda i: (0, 0))
    g = g.reshape(1, d)
    return pl.pallas_call(
        functools.partial(_ffn_kernel, tf=tf),
        grid=(m // tm,),
        in_specs=[pl.BlockSpec((tm, d), lambda i: (i, 0)), full(g), full(wg), full(wu), full(wd)],
        out_specs=pl.BlockSpec((tm, d), lambda i: (i, 0)),
        out_shape=jax.ShapeDtypeStruct((m, d), F32),
        compiler_params=_cparams(("parallel",)),
        name="swiglu_ffn",
    )(x, g, wg, wu, wd)


def _pad_lanes(a):
    a = a.reshape(-1, 2 * SSD_HEADS)
    return jnp.pad(a, ((0, 0), (0, LANES - 2 * SSD_HEADS)))


def kernel(x, mem, rel_bias, mix_norm_g, w_in, gate_bias, attn_q_norm_g, attn_k_norm_g, conv_w, conv_b, dt_bias, a_log, d_skip, ssd_norm_g, w_branch_attn, w_branch_ssd, w_branch_fourier, w_mix_out, xattn_norm_g, mem_norm_g, w_xq, w_xk, w_xv, xattn_q_norm_g, xattn_k_norm_g, w_xo, ffn_norm_g, w_ffn_gate, w_ffn_up, w_ffn_down):
    bsz, seq, d = x.shape
    depth = w_in.shape[0]
    m = bsz * seq
    xf = x.reshape(m, d)
    memf = mem.reshape(bsz * mem.shape[1], d)
    tabs = _dft_tables(seq)
    dils = [dil for _, dil in ATTN_GROUPS]
    biases = [_attn_bias(rel_bias, gi, dil) for gi, dil in enumerate(dils)]
    offs = np.cumsum([0, ATTN_WIDTH, ATTN_WIDTH, ATTN_WIDTH, SSD_INNER, SSD_CONV_CH, 2 * SSD_HEADS, F_WIDTH, 3 * d])
    bf = lambda a: a.astype(BF16)

    for l in range(depth):
        wl = bf(w_in[l])
        seg = lambda i: wl[:, offs[i]:offs[i + 1]]
        h = _rmsnorm(xf, mix_norm_g[l])
        gains = jnp.stack([attn_q_norm_g[l], attn_k_norm_g[l], jnp.ones_like(attn_q_norm_g[l])]).reshape(3, 1, ATTN_HD)
        os_, ls_ = [], []
        for gi, dil in enumerate(dils):
            qkv = _qkv_group(h, wl, gains, gi, dil, bsz, seq)
            o_g, lse_g = _attn_group(qkv, biases[gi], dil, bsz, seq)
            os_.append(o_g)
            ls_.append(lse_g)

        z = _matmul(h, wl, out_dtype=BF16, tn=512, tm=2048, col0=int(offs[3]), n=SSD_INNER)
        dt_raw = _matmul(h, _pad_lanes(seg(5)), out_dtype=F32, tn=LANES, tm=2048)
        z2 = _matmul(h, seg(6), out_dtype=BF16, tn=768, tm=2048, epilogue="chandft", extra=tabs[0])
        gates = _matmul(h, seg(7), out_dtype=BF16, tn=1024, tm=2048, epilogue="sigmoid_bias", extra=gate_bias[l])

        conv_out = _proj_conv_silu(h, wl, conv_w[l], conv_b[l], seq, col0=int(offs[4])).reshape(bsz, seq, SSD_CONV_CH)
        y_ssd = _ssd(conv_out, dt_raw.reshape(bsz, seq, -1), _pad_lanes(dt_bias[l]), _pad_lanes(a_log[l]),
                     jnp.repeat(d_skip[l], SSD_HD).reshape(1, SSD_INNER), bsz, seq)

        f_re = _fourier_seq(z2, tabs, bsz, seq)

        x3 = _merge(xf.reshape(bsz, seq, d), os_, ls_, dils, y_ssd, z.reshape(bsz, seq, SSD_INNER),
                    ssd_norm_g[l].reshape(1, SSD_INNER), f_re, gates.reshape(bsz, seq, 3 * d),
                    bf(w_branch_attn[l]), bf(w_branch_ssd[l]), bf(w_branch_fourier[l]), bf(w_mix_out[l]), bsz, seq)

        hd_m = d // MEM_HEADS
        hm = _rmsnorm(memf, mem_norm_g[l])
        km = _matmul(hm, bf(w_xk[l]), out_dtype=BF16, tn=512, epilogue="headnorm",
                     extra=xattn_k_norm_g[l], head_dim=hd_m)
        vm = _matmul(hm, bf(w_xv[l]), out_dtype=BF16, tn=512)
        x3 = _xattn(x3, xattn_norm_g[l], bf(w_xq[l]), xattn_q_norm_g[l], km.reshape(bsz, -1, d),
                    vm.reshape(bsz, -1, d), bf(w_xo[l]))

        xf = _ffn(x3.reshape(m, d), ffn_norm_g[l], bf(w_ffn_gate[l]), bf(w_ffn_up[l]), bf(w_ffn_down[l]))

    return xf.reshape(bsz, seq, d)
```

```python
import functools
import math

import numpy as np
import jax
import jax.numpy as jnp
from jax import lax
from jax.experimental import pallas as pl
from jax.experimental.pallas import tpu as pltpu

F32 = jnp.float32
BF16 = jnp.bfloat16

NORM_EPS = 1e-6
MASK_VALUE = -1e30

ATTN_GROUPS = ((128, 1), (512, 4), (2048, 16))
ATTN_HPG = 4
ATTN_HD = 128
ATTN_HEADS = ATTN_HPG * len(ATTN_GROUPS)
ATTN_WIDTH = ATTN_HEADS * ATTN_HD
ATTN_OUT = ATTN_HPG * ATTN_HD
ATTN_HALF = 64
ATTN_TQ = 2 * ATTN_HALF
ATTN_TK = ATTN_TQ + 2 * ATTN_HALF
NUM_BUCKETS = 32
MAX_DISTANCE = 1024

SSD_HEADS = 32
SSD_HD = 64
SSD_GROUPS = 4
SSD_HPG = SSD_HEADS // SSD_GROUPS
SSD_STATE = 128
SSD_INNER = SSD_HEADS * SSD_HD
SSD_GW = SSD_INNER // SSD_GROUPS
SSD_CONV = 7
SSD_CHUNK = 128
SSD_CONV_CH = SSD_INNER + 2 * SSD_GROUPS * SSD_STATE
LANES = 128
BF16_ROWS = 16

F_GROUPS = 6
F_GD = 256
F_WIDTH = F_GROUPS * F_GD
F_N2 = 128

MEM_HEADS = 4

VMEM_LIMIT = 56 * 1024 * 1024


def _cparams(sem):
    return pltpu.CompilerParams(dimension_semantics=sem, vmem_limit_bytes=VMEM_LIMIT)


def _silu(x):
    return x * jax.nn.sigmoid(x)


def _rms(x, g):
    ms = jnp.mean(x * x, axis=-1, keepdims=True)
    return x * lax.rsqrt(ms + NORM_EPS) * g


def _get_rows(ref3, start=0, size=None, stride=1):
    idx = slice(None) if size is None else pl.ds(start, size, stride=stride)
    return jnp.concatenate([ref3[c, idx, :] for c in range(ref3.shape[0])], axis=1)


def _put_rows(ref3, val, start=0, stride=1):
    size = val.shape[0]
    idx = slice(None) if (stride == 1 and size == ref3.shape[1]) else pl.ds(start, size, stride=stride)
    for c in range(ref3.shape[0]):
        ref3[c, idx, :] = val[:, c * LANES:(c + 1) * LANES]


def _rmsnorm_kernel(x_ref, g_ref, o_ref):
    o_ref[...] = _rms(x_ref[...], g_ref[...]).astype(o_ref.dtype)


def _rmsnorm(x, g, *, tm=1024):
    m, k = x.shape
    tm = min(tm, m)
    return pl.pallas_call(
        _rmsnorm_kernel,
        grid=(m // tm,),
        in_specs=[pl.BlockSpec((tm, k), lambda i: (i, 0)), pl.BlockSpec((1, k), lambda i: (0, 0))],
        out_specs=pl.BlockSpec((tm, k), lambda i: (i, 0)),
        out_shape=jax.ShapeDtypeStruct((m, k), BF16),
        compiler_params=_cparams(("parallel",)),
        name="rmsnorm",
    )(x, g.reshape(1, k))


MM_ROW_CHUNK = 512


def _mm_kernel(*refs, epilogue, head_dim):
    h_ref, w_ref = refs[:2]
    o_ref = refs[-1]
    tm = h_ref.shape[0]
    rc = min(MM_ROW_CHUNK, tm)
    for c in range(tm // rc):
        rs = slice(c * rc, (c + 1) * rc)
        acc = jnp.dot(h_ref[rs, :], w_ref[...], preferred_element_type=F32)
        tn = acc.shape[1]
        if epilogue == "headnorm":
            hg_ref = refs[2]
            for hh in range(tn // head_dim):
                sl = slice(hh * head_dim, (hh + 1) * head_dim)
                o_ref[rs, sl] = _rms(acc[:, sl], hg_ref[...]).astype(o_ref.dtype)
        elif epilogue == "sigmoid_bias":
            b_ref = refs[2]
            o_ref[rs, :] = jax.nn.sigmoid(acc + b_ref[...]).astype(o_ref.dtype)
        elif epilogue == "chandft":
            dft_ref = refs[2]
            a16 = acc.astype(BF16)
            for gg in range(tn // F_GD):
                sl = slice(gg * F_GD, (gg + 1) * F_GD)
                r = jnp.dot(a16[:, sl], dft_ref[...], preferred_element_type=F32)
                o_ref[0, rs, sl] = r[:, :F_GD].astype(o_ref.dtype)
                o_ref[1, rs, sl] = r[:, F_GD:].astype(o_ref.dtype)
        else:
            o_ref[rs, :] = acc.astype(o_ref.dtype)


def _matmul(h, w, *, out_dtype, tn, tm=1024, epilogue="plain", extra=None, head_dim=None, col0=0, n=None):
    m, k = h.shape
    n = w.shape[1] if n is None else n
    tm = min(tm, m)
    assert m % tm == 0 and n % tn == 0 and col0 % tn == 0
    cb0 = col0 // tn
    in_specs = [pl.BlockSpec((tm, k), lambda i, j: (i, 0)), pl.BlockSpec((k, tn), lambda i, j: (0, cb0 + j))]
    args = [h, w]
    if epilogue == "headnorm":
        in_specs.append(pl.BlockSpec((1, head_dim), lambda i, j: (0, 0)))
        args.append(extra.reshape(1, head_dim))
    elif epilogue == "sigmoid_bias":
        in_specs.append(pl.BlockSpec((1, tn), lambda i, j: (0, j)))
        args.append(extra.reshape(1, n))
    elif epilogue == "chandft":
        in_specs.append(pl.BlockSpec(extra.shape, lambda i, j: (0, 0)))
        args.append(extra)
    if epilogue == "chandft":
        out_shape = jax.ShapeDtypeStruct((2, m, n), out_dtype)
        out_spec = pl.BlockSpec((2, tm, tn), lambda i, j: (0, i, j))
    else:
        out_shape = jax.ShapeDtypeStruct((m, n), out_dtype)
        out_spec = pl.BlockSpec((tm, tn), lambda i, j: (i, j))
    return pl.pallas_call(
        functools.partial(_mm_kernel, epilogue=epilogue, head_dim=head_dim),
        grid=(m // tm, n // tn),
        in_specs=in_specs,
        out_specs=out_spec,
        out_shape=out_shape,
        compiler_params=_cparams(("parallel", "parallel")),
        name="matmul_" + epilogue,
    )(*args)


def _qkv_kernel(h_ref, w_ref, hg_ref, o_ref, *scratch, dil):
    j = pl.program_id(1)
    tm = h_ref.shape[0]
    width = w_ref.shape[1]
    nh = width // ATTN_HD
    rc = min(MM_ROW_CHUNK, tm)
    rows = rc // dil
    for c in range(tm // rc):
        acc = jnp.dot(h_ref[c * rc:(c + 1) * rc, :], w_ref[...], preferred_element_type=F32)
        if dil > 1:
            scr = scratch[0]
            for hh in range(nh):
                scr[hh, c * rc:(c + 1) * rc, :] = acc[:, hh * ATTN_HD:(hh + 1) * ATTN_HD]
        for r in range(dil):
            for hh in range(nh):
                if dil == 1:
                    ph = acc[:, hh * ATTN_HD:(hh + 1) * ATTN_HD]
                else:
                    ph = scr[hh, pl.ds(c * rc + r, rows, stride=dil), :]
                val = jnp.where(j < 2, _rms(ph, hg_ref[...]), ph)
                c0 = r * width + hh * ATTN_HD
                o_ref[c * rows:(c + 1) * rows, c0:c0 + ATTN_HD] = val.astype(o_ref.dtype)


def _qkv_group(h, w, gains, gi, dil, bsz, seq, *, tm=2048):
    m, k = h.shape
    sub = seq // dil
    per_b = seq // tm
    rows = tm // dil
    assert seq % tm == 0 and rows % BF16_ROWS == 0
    return pl.pallas_call(
        functools.partial(_qkv_kernel, dil=dil),
        grid=(m // tm, 3),
        in_specs=[
            pl.BlockSpec((tm, k), lambda i, j: (i, 0)),
            pl.BlockSpec((k, ATTN_OUT), lambda i, j: (0, j * len(ATTN_GROUPS) + gi)),
            pl.BlockSpec((None, 1, ATTN_HD), lambda i, j: (j, 0, 0)),
        ],
        out_specs=pl.BlockSpec((None, None, rows, dil * ATTN_OUT), lambda i, j: (j, i // per_b, i % per_b, 0)),
        out_shape=jax.ShapeDtypeStruct((3, bsz, sub, dil * ATTN_OUT), BF16),
        scratch_shapes=[pltpu.VMEM((ATTN_HPG, tm, ATTN_HD), F32)] if dil > 1 else [],
        compiler_params=_cparams(("parallel", "parallel")),
        name=f"qkv_proj_d{dil}",
    )(h, w, gains)


def _t5_bucket_np(rel):
    half_b = NUM_BUCKETS // 2
    exact = half_b // 2
    dist = np.abs(rel)
    log_ratio = np.log(np.maximum(dist, 1) / exact) / np.log(MAX_DISTANCE / exact)
    far = np.minimum(exact + (log_ratio * (half_b - exact)).astype(np.int32), half_b - 1)
    return np.where(rel > 0, half_b, 0) + np.where(dist < exact, dist, far)


def _attn_bias(rel_bias, gi, dil):
    i = np.arange(ATTN_TQ)[:, None]
    j = np.arange(ATTN_TK)[None, :]
    rel = j - ATTN_HALF - i
    idx = _t5_bucket_np(rel * dil)
    onehot = jnp.asarray(np.eye(NUM_BUCKETS, dtype=np.float32)[idx])
    tab = rel_bias[:, gi * ATTN_HPG:(gi + 1) * ATTN_HPG].astype(F32)
    b = jnp.einsum("qkn,nh->hqk", onehot, tab, precision=lax.Precision.HIGHEST)
    return jnp.where(jnp.asarray(np.abs(rel) <= ATTN_HALF)[None], b, MASK_VALUE)


def _attn_kernel(q_ref, kp_ref, kc_ref, kn_ref, vp_ref, vc_ref, vn_ref, bias_ref, o_ref, lse_ref, kpad, vpad,
                 *, tile, sub):
    i = pl.program_id(2)
    hf = ATTN_HALF
    kpad[0:hf, :] = kp_ref[...]
    kpad[hf:hf + tile, :] = kc_ref[...]
    kpad[hf + tile:, :] = kn_ref[...]
    vpad[0:hf, :] = vp_ref[...]
    vpad[hf:hf + tile, :] = vc_ref[...]
    vpad[hf + tile:, :] = vn_ref[...]

    scale = ATTN_HD ** -0.5
    col = lax.broadcasted_iota(jnp.int32, (ATTN_TQ, ATTN_TK), 1)
    lane = lax.broadcasted_iota(jnp.int32, (ATTN_TQ, LANES), 1)

    def body(t, carry):
        s0 = pl.multiple_of(t * ATTN_TQ, ATTN_TQ)
        kpos = col + (i * tile + s0 - hf)
        valid = (kpos >= 0) & (kpos < sub)
        lse_tile = jnp.zeros((ATTN_TQ, LANES), F32)
        for hh in range(ATTN_HPG):
            sl = slice(hh * ATTN_HD, (hh + 1) * ATTN_HD)
            q = q_ref[pl.ds(s0, ATTN_TQ), sl]
            kw = kpad[pl.ds(s0, ATTN_TK), sl]
            vw = vpad[pl.ds(s0, ATTN_TK), sl]
            s = lax.dot_general(q, kw, (((1,), (1,)), ((), ())), preferred_element_type=F32) * scale + bias_ref[hh]
            s = jnp.where(valid, s, MASK_VALUE)
            m = jnp.max(s, axis=-1, keepdims=True)
            p = jnp.exp(s - m)
            l = jnp.sum(p, axis=-1, keepdims=True)
            o = jnp.dot(p.astype(BF16), vw, preferred_element_type=F32) / l
            o_ref[pl.ds(s0, ATTN_TQ), sl] = o.astype(o_ref.dtype)
            lse_tile = jnp.where(lane == hh, m + jnp.log(l), lse_tile)
        lse_ref[pl.ds(s0, ATTN_TQ), :] = lse_tile
        return carry

    lax.fori_loop(0, tile // ATTN_TQ, body, 0, unroll=4)


def _attn_group(qkv, bias, dil, bsz, seq, *, tile=1024):
    sub = seq // dil
    tile = min(tile, sub)
    nblk = sub // tile
    hb = tile // ATTN_HALF
    last_h = sub // ATTN_HALF - 1
    assert sub % tile == 0 and tile % ATTN_TQ == 0
    cur = lambda which: pl.BlockSpec((None, None, tile, ATTN_OUT), lambda b, r, i: (which, b, i, r))
    prev = lambda which: pl.BlockSpec((None, None, ATTN_HALF, ATTN_OUT),
                                      lambda b, r, i: (which, b, jnp.maximum(i * hb - 1, 0), r))
    nxt = lambda which: pl.BlockSpec((None, None, ATTN_HALF, ATTN_OUT),
                                     lambda b, r, i: (which, b, jnp.minimum((i + 1) * hb, last_h), r))
    return pl.pallas_call(
        functools.partial(_attn_kernel, tile=tile, sub=sub),
        grid=(bsz, dil, nblk),
        in_specs=[cur(0), prev(1), cur(1), nxt(1), prev(2), cur(2), nxt(2),
                  pl.BlockSpec((ATTN_HPG, ATTN_TQ, ATTN_TK), lambda b, r, i: (0, 0, 0))],
        out_specs=(pl.BlockSpec((None, tile, ATTN_OUT), lambda b, r, i: (b, i, r)),
                   pl.BlockSpec((None, tile, LANES), lambda b, r, i: (b, i, r))),
        out_shape=(jax.ShapeDtypeStruct((bsz, sub, dil * ATTN_OUT), BF16),
                   jax.ShapeDtypeStruct((bsz, sub, dil * LANES), F32)),
        scratch_shapes=[pltpu.VMEM((tile + 2 * ATTN_HALF, ATTN_OUT), BF16)] * 2,
        compiler_params=_cparams(("parallel", "parallel", "parallel")),
        name=f"dilated_attn_d{dil}",
    )(qkv, qkv, qkv, qkv, qkv, qkv, qkv, bias)


CONV_HALO = BF16_ROWS
CONV_RC = 64
CONV_APRON = 8


SUBLANES = 8


def _shift_rows(x, s):
    r, c = x.shape
    x3 = x.reshape(r // SUBLANES, SUBLANES, c)
    rot = pltpu.roll(x3, (-s) % SUBLANES, 1)
    sub = lax.broadcasted_iota(jnp.int32, x3.shape, 1)
    if s > 0:
        y = jnp.where(sub < SUBLANES - s, rot, jnp.concatenate([rot[1:], rot[:1]], axis=0))
    else:
        y = jnp.where(sub >= -s, rot, jnp.concatenate([rot[-1:], rot[:-1]], axis=0))
    return y.reshape(r, c)


def _proj_conv_kernel(hp_ref, hc_ref, hn_ref, w_ref, cw_ref, cb_ref, o_ref, xp_ref, *, tm, per_b):
    i = pl.program_id(0)
    first = (i % per_b) == 0
    last = (i % per_b) == per_b - 1
    w = w_ref[...]
    halo_p = jnp.dot(hp_ref[...], w, preferred_element_type=F32)
    halo_n = jnp.dot(hn_ref[...], w, preferred_element_type=F32)
    xp_ref[0:CONV_HALO, :] = jnp.where(first, 0.0, halo_p)
    xp_ref[CONV_HALO:CONV_HALO + tm, :] = jnp.dot(hc_ref[...], w, preferred_element_type=F32)
    xp_ref[CONV_HALO + tm:, :] = jnp.where(last, 0.0, halo_n)
    rows = CONV_RC + 2 * CONV_APRON
    pad = SSD_CONV // 2
    for c in range(tm // CONV_RC):
        r0 = CONV_HALO + c * CONV_RC - CONV_APRON
        x = xp_ref[r0:r0 + rows, :]
        y = cw_ref[pad:pad + 1, :] * x + cb_ref[...]
        for k in range(SSD_CONV):
            if k != pad:
                y = y + cw_ref[k:k + 1, :] * _shift_rows(x, k - pad)
        o_ref[c * CONV_RC:(c + 1) * CONV_RC, :] = _silu(y[CONV_APRON:CONV_APRON + CONV_RC]).astype(o_ref.dtype)


def _proj_conv_silu(h, w, cw, cb, seq, *, col0, tm=1024, tn=512):
    m, k = h.shape
    n = cw.shape[1]
    assert col0 % tn == 0
    cb0 = col0 // tn
    per_b = seq // tm
    hb = tm // CONV_HALO
    assert seq % tm == 0 and n % tn == 0 and SSD_CONV == 7
    return pl.pallas_call(
        functools.partial(_proj_conv_kernel, tm=tm, per_b=per_b),
        grid=(m // tm, n // tn),
        in_specs=[
            pl.BlockSpec((CONV_HALO, k), lambda i, j: (jnp.maximum(i * hb - 1, 0), 0)),
            pl.BlockSpec((tm, k), lambda i, j: (i, 0)),
            pl.BlockSpec((CONV_HALO, k), lambda i, j: (jnp.minimum((i + 1) * hb, m // CONV_HALO - 1), 0)),
            pl.BlockSpec((k, tn), lambda i, j: (0, cb0 + j)),
            pl.BlockSpec((SSD_CONV, tn), lambda i, j: (0, j)),
            pl.BlockSpec((1, tn), lambda i, j: (0, j)),
        ],
        out_specs=pl.BlockSpec((tm, tn), lambda i, j: (i, j)),
        out_shape=jax.ShapeDtypeStruct((m, n), BF16),
        scratch_shapes=[pltpu.VMEM((tm + 2 * CONV_HALO, tn), F32)],
        compiler_params=_cparams(("parallel", "parallel")),
        name="proj_conv_silu",
    )(h, h, h, w, cw, cb.reshape(1, n))


def _softplus(x):
    return jnp.maximum(x, 0.0) + jnp.log1p(jnp.exp(-jnp.abs(x)))


def _ssd_prep_kernel(dtr_ref, dtb_ref, alog_ref, ccol_ref, rt_ref, esc_ref, wst_ref, etot_ref, *, t):
    half = LANES // 2
    log2e = math.log2(math.e)
    lane = lax.broadcasted_iota(jnp.int32, (t, LANES), 1)
    fwd = lane < SSD_HEADS
    tri = (lax.broadcasted_iota(jnp.int32, (t, t), 1) <= lax.broadcasted_iota(jnp.int32, (t, t), 0)).astype(BF16)
    neg_a = jnp.exp(alog_ref[...])
    for ci in range(dtr_ref.shape[0] // t):
        rs = slice(ci * t, (ci + 1) * t)
        dt = _softplus(dtr_ref[rs, :] + dtb_ref[...])
        la = jnp.where(lane < 2 * SSD_HEADS, -(dt * neg_a), 0.0)
        hi = la.astype(BF16)
        r1 = la - hi.astype(F32)
        mid = r1.astype(BF16)
        lo = (r1 - mid.astype(F32)).astype(BF16)
        packed = (hi.astype(F32) + pltpu.roll(mid.astype(F32), half, 1)).astype(BF16)
        res = jnp.dot(tri, jnp.concatenate([packed, lo], axis=1), preferred_element_type=F32)
        a0 = res[:, :LANES]
        acs = a0 + pltpu.roll(a0, half, 1) + res[:, LANES:]
        exb = acs - la
        ldt = jnp.log(dt)
        tot = acs[t - 1:t, :]
        ccol_ref[rs, :] = jnp.where(fwd, acs, exb) * log2e
        rt_ref[ci] = (jnp.where(fwd, acs - ldt, exb + ldt) * log2e).T
        esc_ref[rs, :] = jnp.exp(jnp.where(fwd, acs, tot - exb))
        wst_ref[rs, :] = jnp.exp(jnp.where(fwd, tot - acs, exb)) * dt
        etot_ref[ci] = jnp.broadcast_to(jnp.exp(tot), etot_ref.shape[1:])


def _split2(v):
    hi = v.astype(BF16)
    return jnp.concatenate([hi, (v - hi.astype(F32)).astype(BF16)], axis=1)


def _head_expand_tables():
    j = np.arange(2 * LANES)[:, None] % LANES
    c = np.arange(SSD_INNER)[None, :]
    fwd = (j == c // SSD_HD)
    bwd = (j == SSD_HEADS + c // SSD_HD)
    return jnp.asarray(fwd, BF16), jnp.asarray(bwd, BF16)


def _ssd_bwd_state_kernel(xs_ref, b_ref, wst_ref, etot_ref, selb_ref, gin_ref, g_sc):
    @pl.when(pl.program_id(0) == 0)
    def _():
        g_sc[...] = jnp.zeros(g_sc.shape, F32)

    for b in range(xs_ref.shape[0]):
        gin_ref[b] = g_sc[b].astype(gin_ref.dtype)
        both = jnp.concatenate([_split2(wst_ref[b]), _split2(etot_ref[b])], axis=0)
        spread = jnp.dot(both, selb_ref[...], preferred_element_type=F32)
        t = wst_ref.shape[1]
        xw = (xs_ref[b].astype(F32) * spread[:t]).astype(BF16)
        dec = spread[t:t + 1]
        for g in range(SSD_GROUPS):
            gs = slice(g * SSD_GW, (g + 1) * SSD_GW)
            bm_t = b_ref[b, :, g * SSD_STATE:(g + 1) * SSD_STATE].T
            st = jnp.dot(bm_t, xw[:, gs], preferred_element_type=F32)
            g_sc[b, :, gs] = g_sc[b, :, gs] * dec[:, gs] + st


def _ssd_main_kernel(xs_ref, b_ref, c_ref, ccol_ref, rt_ref, esc_ref, wst_ref, etot_ref, gin_ref,
                     dsk_ref, self_ref, selb_ref, y_ref, h_sc):
    @pl.when(pl.program_id(1) == 0)
    def _():
        h_sc[...] = jnp.zeros(h_sc.shape, F32)

    t = xs_ref.shape[0]
    ccol = ccol_ref[...]
    r_t = rt_ref[...]
    esc2 = _split2(esc_ref[...])
    wst2 = _split2(wst_ref[...])
    etot2 = _split2(etot_ref[...])

    li = lax.broadcasted_iota(jnp.int32, (t, t), 0)
    si = lax.broadcasted_iota(jnp.int32, (t, t), 1)
    mask_f = jnp.where(li >= si, 0.0, MASK_VALUE)
    mask_b = jnp.where(si >= li, 0.0, MASK_VALUE)
    lo_half = lax.broadcasted_iota(jnp.int32, (t, LANES), 1) < SSD_HD

    for g in range(SSD_GROUPS):
        gs = slice(g * SSD_GW, (g + 1) * SSD_GW)
        bm = b_ref[:, g * SSD_STATE:(g + 1) * SSD_STATE]
        cm = c_ref[:, g * SSD_STATE:(g + 1) * SSD_STATE]
        cb = lax.dot_general(cm, bm, (((1,), (1,)), ((), ())), preferred_element_type=F32)
        hf = h_sc[:, gs]
        yf_all = jnp.dot(cm, hf.astype(BF16), preferred_element_type=F32)
        yb_all = jnp.dot(cm, gin_ref[:, gs], preferred_element_type=F32)
        e_in = jnp.dot(esc2, self_ref[g], preferred_element_type=F32)
        e_out = jnp.dot(esc2, selb_ref[g], preferred_element_type=F32)
        w_state = jnp.dot(wst2, self_ref[g], preferred_element_type=F32)
        dec = jnp.dot(etot2, self_ref[g], preferred_element_type=F32)[0:1]
        xw = []
        for k in range(SSD_HPG // 2):
            ea = g * SSD_HPG + 2 * k
            sl = slice(g * SSD_GW + k * LANES, g * SSD_GW + (k + 1) * LANES)
            ks = slice(k * LANES, (k + 1) * LANES)
            xp = xs_ref[:, sl]
            mms = []
            for e in (ea, ea + 1):
                eb = SSD_HEADS + e
                w = (jnp.exp2(ccol[:, e:e + 1] - r_t[e:e + 1, :] + mask_f)
                     + jnp.exp2(r_t[eb:eb + 1, :] - ccol[:, eb:eb + 1] + mask_b))
                mms.append((cb * w).astype(BF16))
            zero = jnp.zeros_like(xp)
            xx = jnp.concatenate([jnp.where(lo_half, xp, zero), jnp.where(lo_half, zero, xp)], axis=0)
            ypair = jnp.dot(jnp.concatenate(mms, axis=1), xx, preferred_element_type=F32)
            xf = xp.astype(F32)
            y_ref[:, sl] = (ypair + yf_all[:, ks] * e_in[:, ks] + yb_all[:, ks] * e_out[:, ks]
                            + xf * dsk_ref[:, sl])
            xw.append((xf * w_state[:, ks]).astype(BF16))
        st = jnp.dot(bm.T, jnp.concatenate(xw, axis=1), preferred_element_type=F32)
        h_sc[:, gs] = hf * dec + st


def _ssd(conv_out, dt_raw, dtb, alog, dskip, bsz, seq):
    t = SSD_CHUNK
    nc = seq // t
    assert seq % t == 0
    gn = SSD_GROUPS * SSD_STATE
    b_blk = SSD_INNER // gn
    c_blk = b_blk + 1
    cpp = min(8, nc)
    assert nc % cpp == 0
    sub8 = 8
    const_spec = pl.BlockSpec((1, LANES), lambda b, i: (0, 0))
    rows_spec = pl.BlockSpec((None, cpp * t, LANES), lambda b, i: (b, i, 0))
    row_arr = jax.ShapeDtypeStruct((bsz, seq, LANES), F32)
    ccol, rt, esc, wst, etot = pl.pallas_call(
        functools.partial(_ssd_prep_kernel, t=t),
        grid=(bsz, nc // cpp),
        in_specs=[rows_spec, const_spec, const_spec],
        out_specs=(rows_spec, pl.BlockSpec((None, cpp, LANES, t), lambda b, i: (b, i, 0, 0)), rows_spec, rows_spec,
                   pl.BlockSpec((None, cpp, sub8, LANES), lambda b, i: (b, i, 0, 0))),
        out_shape=(row_arr, jax.ShapeDtypeStruct((bsz, nc, LANES, t), F32), row_arr, row_arr,
                   jax.ShapeDtypeStruct((bsz, nc, sub8, LANES), F32)),
        compiler_params=_cparams(("parallel", "parallel")),
        name="ssd_decay_terms",
    )(dt_raw, dtb, alog)

    rev = lambda c: nc - 1 - c
    sel_f, sel_b = _head_expand_tables()
    by_group = lambda a: a.reshape(a.shape[0], SSD_GROUPS, SSD_GW).transpose(1, 0, 2)
    sel_spec = pl.BlockSpec((SSD_GROUPS, sel_f.shape[0], SSD_GW), lambda b, c: (0, 0, 0))
    gin = pl.pallas_call(
        _ssd_bwd_state_kernel,
        grid=(nc,),
        in_specs=[
            pl.BlockSpec((bsz, t, SSD_INNER), lambda c: (0, rev(c), 0)),
            pl.BlockSpec((bsz, t, gn), lambda c: (0, rev(c), b_blk)),
            pl.BlockSpec((bsz, t, LANES), lambda c: (0, rev(c), 0)),
            pl.BlockSpec((bsz, None, sub8, LANES), lambda c: (0, rev(c), 0, 0)),
            pl.BlockSpec(sel_b.shape, lambda c: (0, 0)),
        ],
        out_specs=pl.BlockSpec((bsz, None, SSD_STATE, SSD_INNER), lambda c: (0, rev(c), 0, 0)),
        out_shape=jax.ShapeDtypeStruct((bsz, nc, SSD_STATE, SSD_INNER), BF16),
        scratch_shapes=[pltpu.VMEM((bsz, SSD_STATE, SSD_INNER), F32)],
        compiler_params=_cparams(("arbitrary",)),
        name="ssd_bwd_states",
    )(conv_out, conv_out, wst, etot, sel_b)

    wide_spec = pl.BlockSpec((1, SSD_INNER), lambda b, c: (0, 0))
    tok_spec = pl.BlockSpec((None, t, LANES), lambda b, c: (b, c, 0))
    return pl.pallas_call(
        _ssd_main_kernel,
        grid=(bsz, nc),
        in_specs=[
            pl.BlockSpec((None, t, SSD_INNER), lambda b, c: (b, c, 0)),
            pl.BlockSpec((None, t, gn), lambda b, c: (b, c, b_blk)),
            pl.BlockSpec((None, t, gn), lambda b, c: (b, c, c_blk)),
            tok_spec,
            pl.BlockSpec((None, None, LANES, t), lambda b, c: (b, c, 0, 0)),
            tok_spec, tok_spec,
            pl.BlockSpec((None, None, sub8, LANES), lambda b, c: (b, c, 0, 0)),
            pl.BlockSpec((None, None, SSD_STATE, SSD_INNER), lambda b, c: (b, c, 0, 0)),
            wide_spec, sel_spec, sel_spec,
        ],
        out_specs=pl.BlockSpec((None, t, SSD_INNER), lambda b, c: (b, c, 0)),
        out_shape=jax.ShapeDtypeStruct((bsz, seq, SSD_INNER), F32),
        scratch_shapes=[pltpu.VMEM((SSD_STATE, SSD_INNER), F32)],
        compiler_params=_cparams(("parallel", "arbitrary")),
        name="ssd_main",
    )(conv_out, conv_out, conv_out, ccol, rt, esc, wst, etot, gin, dskip, by_group(sel_f), by_group(sel_b))


def _dft_tables(seq):
    n1, n2 = seq // F_N2, F_N2
    c = np.arange(F_GD)
    ang = 2 * np.pi * np.outer(c, c) / F_GD
    chan = np.concatenate([np.cos(ang), -np.sin(ang)], axis=1) / math.sqrt(F_GD)
    k1 = np.arange(n1)
    a1 = 2 * np.pi * np.outer(k1, k1) / n1
    stage_a = np.block([[np.cos(a1), np.sin(a1)], [-np.sin(a1), np.cos(a1)]])
    stage_a = np.kron(stage_a, np.eye(2))
    s2 = np.arange(n2)
    at = 2 * np.pi * np.outer(s2, k1) / seq
    at = at.reshape(n2 // 2, 2, n1).transpose(0, 2, 1).reshape(n2 // 2, 2 * n1)
    tw_c, tw_s = np.cos(at)[..., None], np.sin(at)[..., None]
    a2 = 2 * np.pi * np.outer(s2, s2) / n2
    stage_b = np.concatenate([np.cos(a2), np.sin(a2)], axis=1)
    return (jnp.asarray(chan, BF16), jnp.asarray(stage_a, BF16), jnp.asarray(tw_c, F32),
            jnp.asarray(tw_s, F32), jnp.asarray(stage_b, BF16))


def _fft_a_kernel(z_ref, ma_ref, twc_ref, tws_ref, o_ref, zs, ys, *, n1, rb):
    ct = z_ref.shape[-1]
    half = rb // 2
    _put_rows(zs, pltpu.bitcast(z_ref[...].reshape(2 * n1 * rb, ct), jnp.uint32))
    for j in range(half):
        zz = pltpu.bitcast(_get_rows(zs, j, 2 * n1, half), BF16)
        y = jnp.dot(ma_ref[...], zz, preferred_element_type=F32)
        yr, yi = y[:2 * n1], y[2 * n1:]
        tc, ts = twc_ref[j], tws_ref[j]
        out = jnp.concatenate([yr * tc + yi * ts, yi * tc - yr * ts], axis=0).astype(BF16)
        _put_rows(ys, pltpu.bitcast(out, jnp.uint32), j, half)
    o_ref[...] = pltpu.bitcast(_get_rows(ys), BF16).reshape(2, n1, rb, ct)


def _fft_b_kernel(y_ref, mb_ref, o_ref, os_ref, *, n2, kb, scale):
    ct = y_ref.shape[-1]
    for kk in range(kb):
        yy = y_ref[:, kk].reshape(2 * n2, ct)
        _put_rows(os_ref, jnp.dot(mb_ref[...], yy, preferred_element_type=F32) * scale, kk, kb)
    o_ref[...] = _get_rows(os_ref).reshape(n2, kb, ct).astype(o_ref.dtype)


def _fourier_seq(z2, tabs, bsz, seq, *, rb=BF16_ROWS, kb=BF16_ROWS, ct=768):
    _, ma, twc, tws, mb = tabs
    n1, n2 = seq // F_N2, F_N2
    w = z2.shape[-1]
    nct = w // ct
    za = z2.reshape(2, bsz, n1, n2, w)
    ab_spec = pl.BlockSpec((2, None, n1, rb, ct), lambda b, a, j: (0, b, 0, a, j))
    ya = pl.pallas_call(
        functools.partial(_fft_a_kernel, n1=n1, rb=rb),
        grid=(bsz, n2 // rb, nct),
        in_specs=[
            ab_spec,
            pl.BlockSpec((4 * n1, 4 * n1), lambda b, a, j: (0, 0)),
            pl.BlockSpec((rb // 2, 2 * n1, 1), lambda b, a, j: (a, 0, 0)),
            pl.BlockSpec((rb // 2, 2 * n1, 1), lambda b, a, j: (a, 0, 0)),
        ],
        out_specs=ab_spec,
        out_shape=jax.ShapeDtypeStruct((2, bsz, n1, n2, w), BF16),
        scratch_shapes=[pltpu.VMEM((ct // LANES, n1 * rb, LANES), jnp.uint32)] * 2,
        compiler_params=_cparams(("parallel", "parallel", "parallel")),
        name="fourier_stage_a",
    )(za, ma, twc, tws)
    out = pl.pallas_call(
        functools.partial(_fft_b_kernel, n2=n2, kb=kb, scale=1.0 / math.sqrt(seq)),
        grid=(bsz, n1 // kb, nct),
        in_specs=[
            pl.BlockSpec((2, None, kb, n2, ct), lambda b, k, j: (0, b, k, 0, j)),
            pl.BlockSpec((n2, 2 * n2), lambda b, k, j: (0, 0)),
        ],
        out_specs=pl.BlockSpec((None, n2, kb, ct), lambda b, k, j: (b, 0, k, j)),
        out_shape=jax.ShapeDtypeStruct((bsz, n2, n1, w), BF16),
        scratch_shapes=[pltpu.VMEM((ct // LANES, n2 * kb, LANES), F32)],
        compiler_params=_cparams(("parallel", "parallel", "parallel")),
        name="fourier_stage_b",
    )(ya, mb)
    return out.reshape(bsz, seq, w)


def _merge_kernel(*refs, dils):
    ng = len(dils)
    x_ref = refs[0]
    o_refs = refs[1:1 + ng]
    l_refs = refs[1 + ng:1 + 2 * ng]
    y_ref, z_ref, yg_ref, f_ref, gt_ref, wa_ref, ws_ref, wf_ref, wo_ref, out_ref, o_sc, l_sc = refs[1 + 2 * ng:]
    tm, d = x_ref.shape

    for gi, dil in enumerate(dils):
        rows = tm // dil
        idx = lambda r: slice(None) if dil == 1 else pl.ds(r, rows, stride=dil)
        for r in range(dil):
            l_sc[gi, idx(r), :] = l_refs[gi][:, r * LANES:(r + 1) * LANES]
            for hh in range(ATTN_HPG):
                c0 = r * ATTN_OUT + hh * ATTN_HD
                o_sc[gi * ATTN_HPG + hh, idx(r), :] = o_refs[gi][:, c0:c0 + ATTN_HD].astype(F32)

    ls = [l_sc[gi] for gi in range(ng)]
    mx = functools.reduce(jnp.maximum, ls)
    ws = [jnp.exp(v - mx) for v in ls]
    inv = 1.0 / functools.reduce(lambda a, b: a + b, ws)
    heads = []
    for hh in range(ATTN_HPG):
        acc = None
        for gi in range(ng):
            alpha = (ws[gi] * inv)[:, hh:hh + 1]
            term = alpha * o_sc[gi * ATTN_HPG + hh]
            acc = term if acc is None else acc + term
        heads.append(acc.astype(BF16))
    o_attn = jnp.concatenate(heads, axis=1)

    ya = jnp.dot(o_attn, wa_ref[...], preferred_element_type=F32)
    y_parts = []
    for g in range(SSD_GROUPS):
        gs = slice(g * SSD_GW, (g + 1) * SSD_GW)
        yy = y_ref[:, gs] * _silu(z_ref[:, gs].astype(F32))
        y_parts.append(_rms(yy, yg_ref[:, gs]).astype(BF16))
    ys = jnp.dot(jnp.concatenate(y_parts, axis=1), ws_ref[...], preferred_element_type=F32)
    yf = jnp.dot(f_ref[...], wf_ref[...], preferred_element_type=F32)
    gates = gt_ref[...].astype(F32)
    merged = gates[:, :d] * ya + gates[:, d:2 * d] * ys + gates[:, 2 * d:] * yf
    out_ref[...] = x_ref[...] + jnp.dot(merged.astype(BF16), wo_ref[...], preferred_element_type=F32)


def _merge(x, os_, ls_, dils, y, z, y_gain, f, gates, wa, ws, wf, wo, bsz, seq, *, tm=512):
    d = x.shape[-1]
    assert all((tm // dil) % BF16_ROWS == 0 for dil in dils)
    tok = lambda width: pl.BlockSpec((None, tm, width), lambda b, i: (b, i, 0))
    full = lambda a: pl.BlockSpec(a.shape, lambda b, i: (0, 0), pipeline_mode=pl.Buffered(1))
    ng = len(dils)
    in_specs = [tok(d)]
    in_specs += [pl.BlockSpec((None, tm // dil, dil * ATTN_OUT), lambda b, i: (b, i, 0)) for dil in dils]
    in_specs += [pl.BlockSpec((None, tm // dil, dil * LANES), lambda b, i: (b, i, 0)) for dil in dils]
    in_specs += [tok(y.shape[-1]), tok(z.shape[-1]), full(y_gain), tok(f.shape[-1]), tok(gates.shape[-1]),
                 full(wa), full(ws), full(wf), full(wo)]
    return pl.pallas_call(
        functools.partial(_merge_kernel, dils=tuple(dils)),
        grid=(bsz, seq // tm),
        in_specs=in_specs,
        out_specs=tok(d),
        out_shape=jax.ShapeDtypeStruct((bsz, seq, d), F32),
        scratch_shapes=[pltpu.VMEM((ng * ATTN_HPG, tm, ATTN_HD), F32), pltpu.VMEM((ng, tm, LANES), F32)],
        compiler_params=_cparams(("parallel", "parallel")),
        name="branch_merge",
    )(x, *os_, *ls_, y, z, y_gain, f, gates, wa, ws, wf, wo)


def _xattn_kernel(x_ref, g_ref, wq_ref, qg_ref, k_ref, v_ref, wo_ref, out_ref, o_sc):
    x = x_ref[...]
    d = x.shape[1]
    hd = d // MEM_HEADS
    h = _rms(x, g_ref[...]).astype(BF16)
    q = jnp.dot(h, wq_ref[...], preferred_element_type=F32)
    scale = hd ** -0.5
    for hh in range(MEM_HEADS):
        sl = slice(hh * hd, (hh + 1) * hd)
        qn = _rms(q[:, sl], qg_ref[...]).astype(BF16)
        s = lax.dot_general(qn, k_ref[:, sl], (((1,), (1,)), ((), ())), preferred_element_type=F32) * scale
        m = jnp.max(s, axis=-1, keepdims=True)
        p = jnp.exp(s - m)
        l = jnp.sum(p, axis=-1, keepdims=True)
        o_sc[:, sl] = jnp.dot(p.astype(BF16), v_ref[:, sl], preferred_element_type=F32) / l
    out_ref[...] = x + jnp.dot(o_sc[...].astype(BF16), wo_ref[...], preferred_element_type=F32)


def _xattn(x, g, wq, qg, k, v, wo, *, tm=512):
    bsz, seq, d = x.shape
    mt = k.shape[1]
    full = lambda a: pl.BlockSpec(a.shape, lambda b, i: (0, 0))
    g = g.reshape(1, d)
    qg = qg.reshape(1, -1)
    return pl.pallas_call(
        _xattn_kernel,
        grid=(bsz, seq // tm),
        in_specs=[
            pl.BlockSpec((None, tm, d), lambda b, i: (b, i, 0)),
            full(g), full(wq), full(qg),
            pl.BlockSpec((None, mt, d), lambda b, i: (b, 0, 0)),
            pl.BlockSpec((None, mt, d), lambda b, i: (b, 0, 0)),
            full(wo),
        ],
        out_specs=pl.BlockSpec((None, tm, d), lambda b, i: (b, i, 0)),
        out_shape=jax.ShapeDtypeStruct((bsz, seq, d), F32),
        scratch_shapes=[pltpu.VMEM((tm, d), F32)],
        compiler_params=_cparams(("parallel", "parallel")),
        name="mem_xattn",
    )(x, g, wq, qg, k, v, wo)


def _ffn_kernel(x_ref, g_ref, wg_ref, wu_ref, wd_ref, out_ref, *, tf):
    x = x_ref[...]
    h = _rms(x, g_ref[...]).astype(BF16)
    acc = x
    for c in range(wg_ref.shape[1] // tf):
        sl = slice(c * tf, (c + 1) * tf)
        gt = jnp.dot(h, wg_ref[:, sl], preferred_element_type=F32)
        up = jnp.dot(h, wu_ref[:, sl], preferred_element_type=F32)
        a = (_silu(gt) * up).astype(BF16)
        acc = acc + jnp.dot(a, wd_ref[sl, :], preferred_element_type=F32)
    out_ref[...] = acc


def _ffn(x, g, wg, wu, wd, *, tm=256):
    m, d = x.shape
    dff = wg.shape[1]
    tf = dff // 2 if (dff // 2) % LANES == 0 else dff
    full = lambda a: pl.BlockSpec(a.shape, lambda i: (0, 0))
    g = g.reshape(1, d)
    return pl.pallas_call(
        functools.partial(_ffn_kernel, tf=tf),
        grid=(m // tm,),
        in_specs=[pl.BlockSpec((tm, d), lambda i: (i, 0)), full(g), full(wg), full(wu), full(wd)],
        out_specs=pl.BlockSpec((tm, d), lambda i: (i, 0)),
        out_shape=jax.ShapeDtypeStruct((m, d), F32),
        compiler_params=_cparams(("parallel",)),
        name="swiglu_ffn",
    )(x, g, wg, wu, wd)


def _pad_lanes(a):
    a = a.reshape(-1, 2 * SSD_HEADS)
    return jnp.pad(a, ((0, 0), (0, LANES - 2 * SSD_HEADS)))


def kernel(x, mem, rel_bias, mix_norm_g, w_in, gate_bias, attn_q_norm_g, attn_k_norm_g, conv_w, conv_b, dt_bias, a_log, d_skip, ssd_norm_g, w_branch_attn, w_branch_ssd, w_branch_fourier, w_mix_out, xattn_norm_g, mem_norm_g, w_xq, w_xk, w_xv, xattn_q_norm_g, xattn_k_norm_g, w_xo, ffn_norm_g, w_ffn_gate, w_ffn_up, w_ffn_down):
    bsz, seq, d = x.shape
    depth = w_in.shape[0]
    m = bsz * seq
    xf = x.reshape(m, d)
    memf = mem.reshape(bsz * mem.shape[1], d)
    tabs = _dft_tables(seq)
    dils = [dil for _, dil in ATTN_GROUPS]
    biases = [_attn_bias(rel_bias, gi, dil) for gi, dil in enumerate(dils)]
    offs = np.cumsum([0, ATTN_WIDTH, ATTN_WIDTH, ATTN_WIDTH, SSD_INNER, SSD_CONV_CH, 2 * SSD_HEADS, F_WIDTH, 3 * d])
    bf = lambda a: a.astype(BF16)

    for l in range(depth):
        wl = bf(w_in[l])
        seg = lambda i: wl[:, offs[i]:offs[i + 1]]
        h = _rmsnorm(xf, mix_norm_g[l])
        gains = jnp.stack([attn_q_norm_g[l], attn_k_norm_g[l], jnp.ones_like(attn_q_norm_g[l])]).reshape(3, 1, ATTN_HD)
        os_, ls_ = [], []
        for gi, dil in enumerate(dils):
            qkv = _qkv_group(h, wl, gains, gi, dil, bsz, seq)
            o_g, lse_g = _attn_group(qkv, biases[gi], dil, bsz, seq)
            os_.append(o_g)
            ls_.append(lse_g)

        z = _matmul(h, wl, out_dtype=BF16, tn=512, tm=2048, col0=int(offs[3]), n=SSD_INNER)
        dt_raw = _matmul(h, _pad_lanes(seg(5)), out_dtype=F32, tn=LANES, tm=2048)
        z2 = _matmul(h, seg(6), out_dtype=BF16, tn=768, tm=2048, epilogue="chandft", extra=tabs[0])
        gates = _matmul(h, seg(7), out_dtype=BF16, tn=1024, tm=2048, epilogue="sigmoid_bias", extra=gate_bias[l])

        conv_out = _proj_conv_silu(h, wl, conv_w[l], conv_b[l], seq, col0=int(offs[4])).reshape(bsz, seq, SSD_CONV_CH)
        y_ssd = _ssd(conv_out, dt_raw.reshape(bsz, seq, -1), _pad_lanes(dt_bias[l]), _pad_lanes(a_log[l]),
                     jnp.repeat(d_skip[l], SSD_HD).reshape(1, SSD_INNER), bsz, seq)

        f_re = _fourier_seq(z2, tabs, bsz, seq)

        x3 = _merge(xf.reshape(bsz, seq, d), os_, ls_, dils, y_ssd, z.reshape(bsz, seq, SSD_INNER),
                    ssd_norm_g[l].reshape(1, SSD_INNER), f_re, gates.reshape(bsz, seq, 3 * d),
                    bf(w_branch_attn[l]), bf(w_branch_ssd[l]), bf(w_branch_fourier[l]), bf(w_mix_out[l]), bsz, seq)

        hd_m = d // MEM_HEADS
        hm = _rmsnorm(memf, mem_norm_g[l])
        km = _matmul(hm, bf(w_xk[l]), out_dtype=BF16, tn=512, epilogue="headnorm",
                     extra=xattn_k_norm_g[l], head_dim=hd_m)
        vm = _matmul(hm, bf(w_xv[l]), out_dtype=BF16, tn=512)
        x3 = _xattn(x3, xattn_norm_g[l], bf(w_xq[l]), xattn_q_norm_g[l], km.reshape(bsz, -1, d),
                    vm.reshape(bsz, -1, d), bf(w_xo[l]))

        xf = _ffn(x3.reshape(m, d), ffn_norm_g[l], bf(w_ffn_gate[l]), bf(w_ffn_up[l]), bf(w_ffn_down[l]))

    return xf.reshape(bsz, seq, d)
```

```python
import functools
import math

import numpy as np
import jax
import jax.numpy as jnp
from jax import lax
from jax.experimental import pallas as pl
from jax.experimental.pallas import tpu as pltpu

F32 = jnp.float32
BF16 = jnp.bfloat16

NORM_EPS = 1e-6
MASK_VALUE = -1e30

ATTN_GROUPS = ((128, 1), (512, 4), (2048, 16))
ATTN_HPG = 4
ATTN_HD = 128
ATTN_HEADS = ATTN_HPG * len(ATTN_GROUPS)
ATTN_WIDTH = ATTN_HEADS * ATTN_HD
ATTN_OUT = ATTN_HPG * ATTN_HD
ATTN_HALF = 64
ATTN_TQ = 2 * ATTN_HALF
ATTN_TK = ATTN_TQ + 2 * ATTN_HALF
NUM_BUCKETS = 32
MAX_DISTANCE = 1024

SSD_HEADS = 32
SSD_HD = 64
SSD_GROUPS = 4
SSD_HPG = SSD_HEADS // SSD_GROUPS
SSD_STATE = 128
SSD_INNER = SSD_HEADS * SSD_HD
SSD_GW = SSD_INNER // SSD_GROUPS
SSD_CONV = 7
SSD_CHUNK = 128
SSD_CONV_CH = SSD_INNER + 2 * SSD_GROUPS * SSD_STATE
LANES = 128
BF16_ROWS = 16

F_GROUPS = 6
F_GD = 256
F_WIDTH = F_GROUPS * F_GD
F_N2 = 128

MEM_HEADS = 4
XATTN_ROW_CHUNK = 256

VMEM_LIMIT = 56 * 1024 * 1024


def _cparams(sem):
    return pltpu.CompilerParams(dimension_semantics=sem, vmem_limit_bytes=VMEM_LIMIT)


def _silu(x):
    return x * jax.nn.sigmoid(x)


def _rms(x, g):
    ms = jnp.mean(x * x, axis=-1, keepdims=True)
    return x * lax.rsqrt(ms + NORM_EPS) * g


def _get_rows(ref3, start=0, size=None, stride=1):
    idx = slice(None) if size is None else pl.ds(start, size, stride=stride)
    return jnp.concatenate([ref3[c, idx, :] for c in range(ref3.shape[0])], axis=1)


def _put_rows(ref3, val, start=0, stride=1):
    size = val.shape[0]
    idx = slice(None) if (stride == 1 and size == ref3.shape[1]) else pl.ds(start, size, stride=stride)
    for c in range(ref3.shape[0]):
        ref3[c, idx, :] = val[:, c * LANES:(c + 1) * LANES]


def _rmsnorm_kernel(x_ref, g_ref, o_ref):
    o_ref[...] = _rms(x_ref[...], g_ref[...]).astype(o_ref.dtype)


def _rmsnorm(x, g, *, tm=1024):
    m, k = x.shape
    tm = min(tm, m)
    return pl.pallas_call(
        _rmsnorm_kernel,
        grid=(m // tm,),
        in_specs=[pl.BlockSpec((tm, k), lambda i: (i, 0)), pl.BlockSpec((1, k), lambda i: (0, 0))],
        out_specs=pl.BlockSpec((tm, k), lambda i: (i, 0)),
        out_shape=jax.ShapeDtypeStruct((m, k), BF16),
        compiler_params=_cparams(("parallel",)),
        name="rmsnorm",
    )(x, g.reshape(1, k))


MM_ROW_CHUNK = 512


def _mm_kernel(*refs, epilogue, head_dim):
    h_ref, w_ref = refs[:2]
    o_ref = refs[-1]
    tm = h_ref.shape[0]
    rc = min(MM_ROW_CHUNK, tm)
    for c in range(tm // rc):
        rs = slice(c * rc, (c + 1) * rc)
        acc = jnp.dot(h_ref[rs, :], w_ref[...], preferred_element_type=F32)
        tn = acc.shape[1]
        if epilogue == "headnorm":
            hg_ref = refs[2]
            for hh in range(tn // head_dim):
                sl = slice(hh * head_dim, (hh + 1) * head_dim)
                o_ref[rs, sl] = _rms(acc[:, sl], hg_ref[...]).astype(o_ref.dtype)
        elif epilogue == "sigmoid_bias":
            b_ref = refs[2]
            o_ref[rs, :] = jax.nn.sigmoid(acc + b_ref[...]).astype(o_ref.dtype)
        elif epilogue == "chandft":
            dft_ref = refs[2]
            a16 = acc.astype(BF16)
            for gg in range(tn // F_GD):
                sl = slice(gg * F_GD, (gg + 1) * F_GD)
                r = jnp.dot(a16[:, sl], dft_ref[...], preferred_element_type=F32)
                o_ref[0, rs, sl] = r[:, :F_GD].astype(o_ref.dtype)
                o_ref[1, rs, sl] = r[:, F_GD:].astype(o_ref.dtype)
        else:
            o_ref[rs, :] = acc.astype(o_ref.dtype)


def _matmul(h, w, *, out_dtype, tn, tm=1024, epilogue="plain", extra=None, head_dim=None, col0=0, n=None):
    m, k = h.shape
    n = w.shape[1] if n is None else n
    tm = min(tm, m)
    assert m % tm == 0 and n % tn == 0 and col0 % tn == 0
    cb0 = col0 // tn
    in_specs = [pl.BlockSpec((tm, k), lambda i, j: (i, 0)), pl.BlockSpec((k, tn), lambda i, j: (0, cb0 + j))]
    args = [h, w]
    if epilogue == "headnorm":
        in_specs.append(pl.BlockSpec((1, head_dim), lambda i, j: (0, 0)))
        args.append(extra.reshape(1, head_dim))
    elif epilogue == "sigmoid_bias":
        in_specs.append(pl.BlockSpec((1, tn), lambda i, j: (0, j)))
        args.append(extra.reshape(1, n))
    elif epilogue == "chandft":
        in_specs.append(pl.BlockSpec(extra.shape, lambda i, j: (0, 0)))
        args.append(extra)
    if epilogue == "chandft":
        out_shape = jax.ShapeDtypeStruct((2, m, n), out_dtype)
        out_spec = pl.BlockSpec((2, tm, tn), lambda i, j: (0, i, j))
    else:
        out_shape = jax.ShapeDtypeStruct((m, n), out_dtype)
        out_spec = pl.BlockSpec((tm, tn), lambda i, j: (i, j))
    return pl.pallas_call(
        functools.partial(_mm_kernel, epilogue=epilogue, head_dim=head_dim),
        grid=(m // tm, n // tn),
        in_specs=in_specs,
        out_specs=out_spec,
        out_shape=out_shape,
        compiler_params=_cparams(("parallel", "parallel")),
        name="matmul_" + epilogue,
    )(*args)


def _qkv_kernel(h_ref, w_ref, hg_ref, o_ref, *scratch, dil):
    j = pl.program_id(1)
    tm = h_ref.shape[0]
    width = w_ref.shape[1]
    nh = width // ATTN_HD
    rc = min(MM_ROW_CHUNK, tm)
    rows = rc // dil
    for c in range(tm // rc):
        acc = jnp.dot(h_ref[c * rc:(c + 1) * rc, :], w_ref[...], preferred_element_type=F32)
        if dil > 1:
            scr = scratch[0]
            for hh in range(nh):
                scr[hh, c * rc:(c + 1) * rc, :] = acc[:, hh * ATTN_HD:(hh + 1) * ATTN_HD]
        for r in range(dil):
            for hh in range(nh):
                if dil == 1:
                    ph = acc[:, hh * ATTN_HD:(hh + 1) * ATTN_HD]
                else:
                    ph = scr[hh, pl.ds(c * rc + r, rows, stride=dil), :]
                val = jnp.where(j < 2, _rms(ph, hg_ref[...]), ph)
                c0 = r * width + hh * ATTN_HD
                o_ref[c * rows:(c + 1) * rows, c0:c0 + ATTN_HD] = val.astype(o_ref.dtype)


def _qkv_group(h, w, gains, gi, dil, bsz, seq, *, tm=2048):
    m, k = h.shape
    sub = seq // dil
    per_b = seq // tm
    rows = tm // dil
    assert seq % tm == 0 and rows % BF16_ROWS == 0
    return pl.pallas_call(
        functools.partial(_qkv_kernel, dil=dil),
        grid=(m // tm, 3),
        in_specs=[
            pl.BlockSpec((tm, k), lambda i, j: (i, 0)),
            pl.BlockSpec((k, ATTN_OUT), lambda i, j: (0, j * len(ATTN_GROUPS) + gi)),
            pl.BlockSpec((None, 1, ATTN_HD), lambda i, j: (j, 0, 0)),
        ],
        out_specs=pl.BlockSpec((None, None, rows, dil * ATTN_OUT), lambda i, j: (j, i // per_b, i % per_b, 0)),
        out_shape=jax.ShapeDtypeStruct((3, bsz, sub, dil * ATTN_OUT), BF16),
        scratch_shapes=[pltpu.VMEM((ATTN_HPG, tm, ATTN_HD), F32)] if dil > 1 else [],
        compiler_params=_cparams(("parallel", "parallel")),
        name=f"qkv_proj_d{dil}",
    )(h, w, gains)


def _t5_bucket_np(rel):
    half_b = NUM_BUCKETS // 2
    exact = half_b // 2
    dist = np.abs(rel)
    log_ratio = np.log(np.maximum(dist, 1) / exact) / np.log(MAX_DISTANCE / exact)
    far = np.minimum(exact + (log_ratio * (half_b - exact)).astype(np.int32), half_b - 1)
    return np.where(rel > 0, half_b, 0) + np.where(dist < exact, dist, far)


def _attn_bias(rel_bias, gi, dil):
    i = np.arange(ATTN_TQ)[:, None]
    j = np.arange(ATTN_TK)[None, :]
    rel = j - ATTN_HALF - i
    idx = _t5_bucket_np(rel * dil)
    onehot = jnp.asarray(np.eye(NUM_BUCKETS, dtype=np.float32)[idx])
    tab = rel_bias[:, gi * ATTN_HPG:(gi + 1) * ATTN_HPG].astype(F32)
    b = jnp.einsum("qkn,nh->hqk", onehot, tab, precision=lax.Precision.HIGHEST)
    return jnp.where(jnp.asarray(np.abs(rel) <= ATTN_HALF)[None], b, MASK_VALUE)


def _attn_kernel(q_ref, kp_ref, kc_ref, kn_ref, vp_ref, vc_ref, vn_ref, bias_ref, o_ref, lse_ref, kpad, vpad,
                 *, tile, sub):
    i = pl.program_id(2)
    hf = ATTN_HALF
    kpad[0:hf, :] = kp_ref[...]
    kpad[hf:hf + tile, :] = kc_ref[...]
    kpad[hf + tile:, :] = kn_ref[...]
    vpad[0:hf, :] = vp_ref[...]
    vpad[hf:hf + tile, :] = vc_ref[...]
    vpad[hf + tile:, :] = vn_ref[...]

    scale = ATTN_HD ** -0.5
    col = lax.broadcasted_iota(jnp.int32, (ATTN_TQ, ATTN_TK), 1)
    lane = lax.broadcasted_iota(jnp.int32, (ATTN_TQ, LANES), 1)

    def body(t, carry):
        s0 = pl.multiple_of(t * ATTN_TQ, ATTN_TQ)
        kpos = col + (i * tile + s0 - hf)
        valid = (kpos >= 0) & (kpos < sub)
        lse_tile = jnp.zeros((ATTN_TQ, LANES), F32)
        for hh in range(ATTN_HPG):
            sl = slice(hh * ATTN_HD, (hh + 1) * ATTN_HD)
            q = q_ref[pl.ds(s0, ATTN_TQ), sl]
            kw = kpad[pl.ds(s0, ATTN_TK), sl]
            vw = vpad[pl.ds(s0, ATTN_TK), sl]
            s = lax.dot_general(q, kw, (((1,), (1,)), ((), ())), preferred_element_type=F32) * scale + bias_ref[hh]
            s = jnp.where(valid, s, MASK_VALUE)
            m = jnp.max(s, axis=-1, keepdims=True)
            p = jnp.exp(s - m)
            l = jnp.sum(p, axis=-1, keepdims=True)
            o = jnp.dot(p.astype(BF16), vw, preferred_element_type=F32) / l
            o_ref[pl.ds(s0, ATTN_TQ), sl] = o.astype(o_ref.dtype)
            lse_tile = jnp.where(lane == hh, m + jnp.log(l), lse_tile)
        lse_ref[pl.ds(s0, ATTN_TQ), :] = lse_tile
        return carry

    lax.fori_loop(0, tile // ATTN_TQ, body, 0, unroll=4)


def _attn_group(qkv, bias, dil, bsz, seq, *, tile=1024):
    sub = seq // dil
    tile = min(tile, sub)
    nblk = sub // tile
    hb = tile // ATTN_HALF
    last_h = sub // ATTN_HALF - 1
    assert sub % tile == 0 and tile % ATTN_TQ == 0
    cur = lambda which: pl.BlockSpec((None, None, tile, ATTN_OUT), lambda b, r, i: (which, b, i, r))
    prev = lambda which: pl.BlockSpec((None, None, ATTN_HALF, ATTN_OUT),
                                      lambda b, r, i: (which, b, jnp.maximum(i * hb - 1, 0), r))
    nxt = lambda which: pl.BlockSpec((None, None, ATTN_HALF, ATTN_OUT),
                                     lambda b, r, i: (which, b, jnp.minimum((i + 1) * hb, last_h), r))
    return pl.pallas_call(
        functools.partial(_attn_kernel, tile=tile, sub=sub),
        grid=(bsz, dil, nblk),
        in_specs=[cur(0), prev(1), cur(1), nxt(1), prev(2), cur(2), nxt(2),
                  pl.BlockSpec((ATTN_HPG, ATTN_TQ, ATTN_TK), lambda b, r, i: (0, 0, 0))],
        out_specs=(pl.BlockSpec((None, tile, ATTN_OUT), lambda b, r, i: (b, i, r)),
                   pl.BlockSpec((None, tile, LANES), lambda b, r, i: (b, i, r))),
        out_shape=(jax.ShapeDtypeStruct((bsz, sub, dil * ATTN_OUT), BF16),
                   jax.ShapeDtypeStruct((bsz, sub, dil * LANES), F32)),
        scratch_shapes=[pltpu.VMEM((tile + 2 * ATTN_HALF, ATTN_OUT), BF16)] * 2,
        compiler_params=_cparams(("parallel", "parallel", "parallel")),
        name=f"dilated_attn_d{dil}",
    )(qkv, qkv, qkv, qkv, qkv, qkv, qkv, bias)


CONV_HALO = BF16_ROWS
CONV_RC = 128
CONV_APRON = 8


SUBLANES = 8


def _shift_rows(x, s):
    r, c = x.shape
    x3 = x.reshape(r // SUBLANES, SUBLANES, c)
    rot = pltpu.roll(x3, (-s) % SUBLANES, 1)
    sub = lax.broadcasted_iota(jnp.int32, x3.shape, 1)
    if s > 0:
        y = jnp.where(sub < SUBLANES - s, rot, jnp.concatenate([rot[1:], rot[:1]], axis=0))
    else:
        y = jnp.where(sub >= -s, rot, jnp.concatenate([rot[-1:], rot[:-1]], axis=0))
    return y.reshape(r, c)


def _proj_conv_kernel(hp_ref, hc_ref, hn_ref, w_ref, cw_ref, cb_ref, o_ref, *, tm, per_b):
    i = pl.program_id(0)
    first = (i % per_b) == 0
    last = (i % per_b) == per_b - 1
    w = w_ref[...]
    halo_p = jnp.dot(hp_ref[...], w, preferred_element_type=F32)
    halo_n = jnp.dot(hn_ref[...], w, preferred_element_type=F32)
    halo_p = jnp.where(first, 0.0, halo_p)
    halo_n = jnp.where(last, 0.0, halo_n)
    pad = SSD_CONV // 2
    acc = {}

    def project(s):
        acc[s] = jnp.dot(hc_ref[s * MM_ROW_CHUNK:(s + 1) * MM_ROW_CHUNK, :], w, preferred_element_type=F32)

    def tile_rows(lo, hi):
        pieces, r = [], lo
        while r < hi:
            if r < 0:
                src, off, end = halo_p, CONV_HALO + r, min(hi, 0)
            elif r >= tm:
                src, off, end = halo_n, r - tm, hi
            else:
                s = r // MM_ROW_CHUNK
                src, off, end = acc[s], r - s * MM_ROW_CHUNK, min(hi, (s + 1) * MM_ROW_CHUNK)
            pieces.append(src[off:off + end - r])
            r = end
        return pieces[0] if len(pieces) == 1 else jnp.concatenate(pieces, axis=0)

    def conv_chunk(c):
        x = tile_rows(c * CONV_RC - CONV_APRON, (c + 1) * CONV_RC + CONV_APRON)
        tap = lambda off: cw_ref[pad + off:pad + off + 1, :]
        xm, xq = _shift_rows(x, -3), _shift_rows(x, 3)
        f0 = tap(-3) * xm + tap(0) * x + tap(3) * xq + cb_ref[...]
        f1 = tap(-2) * xm + tap(1) * x
        fm = tap(-1) * x + tap(2) * xq
        y = f0 + _shift_rows(f1, 1) + _shift_rows(fm, -1)
        o_ref[c * CONV_RC:(c + 1) * CONV_RC, :] = _silu(y[CONV_APRON:CONV_APRON + CONV_RC]).astype(o_ref.dtype)

    nsub = tm // MM_ROW_CHUNK
    per_sub = MM_ROW_CHUNK // CONV_RC
    project(0)
    for s in range(nsub):
        if s + 1 < nsub:
            project(s + 1)
        for c in range(s * per_sub, (s + 1) * per_sub):
            conv_chunk(c)


def _proj_conv_silu(h, w, cw, cb, seq, *, col0, tm=1024, tn=512):
    m, k = h.shape
    n = cw.shape[1]
    assert col0 % tn == 0
    cb0 = col0 // tn
    per_b = seq // tm
    hb = tm // CONV_HALO
    assert seq % tm == 0 and n % tn == 0 and SSD_CONV == 7
    return pl.pallas_call(
        functools.partial(_proj_conv_kernel, tm=tm, per_b=per_b),
        grid=(m // tm, n // tn),
        in_specs=[
            pl.BlockSpec((CONV_HALO, k), lambda i, j: (jnp.maximum(i * hb - 1, 0), 0)),
            pl.BlockSpec((tm, k), lambda i, j: (i, 0)),
            pl.BlockSpec((CONV_HALO, k), lambda i, j: (jnp.minimum((i + 1) * hb, m // CONV_HALO - 1), 0)),
            pl.BlockSpec((k, tn), lambda i, j: (0, cb0 + j)),
            pl.BlockSpec((SSD_CONV, tn), lambda i, j: (0, j)),
            pl.BlockSpec((1, tn), lambda i, j: (0, j)),
        ],
        out_specs=pl.BlockSpec((tm, tn), lambda i, j: (i, j)),
        out_shape=jax.ShapeDtypeStruct((m, n), BF16),
        compiler_params=_cparams(("parallel", "parallel")),
        name="proj_conv_silu",
    )(h, h, h, w, cw, cb.reshape(1, n))


def _softplus(x):
    return jnp.maximum(x, 0.0) + jnp.log1p(jnp.exp(-jnp.abs(x)))


def _ssd_prep_kernel(dtr_ref, dtb_ref, alog_ref, ccol_ref, rt_ref, esc_ref, wst_ref, etot_ref, *, t):
    half = LANES // 2
    log2e = math.log2(math.e)
    lane = lax.broadcasted_iota(jnp.int32, (t, LANES), 1)
    fwd = lane < SSD_HEADS
    tri = (lax.broadcasted_iota(jnp.int32, (t, t), 1) <= lax.broadcasted_iota(jnp.int32, (t, t), 0)).astype(BF16)
    neg_a = jnp.exp(alog_ref[...])
    for ci in range(dtr_ref.shape[0] // t):
        rs = slice(ci * t, (ci + 1) * t)
        dt = _softplus(dtr_ref[rs, :] + dtb_ref[...])
        la = jnp.where(lane < 2 * SSD_HEADS, -(dt * neg_a), 0.0)
        hi = la.astype(BF16)
        r1 = la - hi.astype(F32)
        mid = r1.astype(BF16)
        lo = (r1 - mid.astype(F32)).astype(BF16)
        packed = (hi.astype(F32) + pltpu.roll(mid.astype(F32), half, 1)).astype(BF16)
        res = jnp.dot(tri, jnp.concatenate([packed, lo], axis=1), preferred_element_type=F32)
        a0 = res[:, :LANES]
        acs = a0 + pltpu.roll(a0, half, 1) + res[:, LANES:]
        exb = acs - la
        ldt = jnp.log(dt)
        tot = acs[t - 1:t, :]
        ccol_ref[rs, :] = jnp.where(fwd, acs, exb) * log2e
        rt_ref[ci] = (jnp.where(fwd, acs - ldt, exb + ldt) * log2e).T
        esc_ref[rs, :] = jnp.exp(jnp.where(fwd, acs, tot - exb))
        wst_ref[rs, :] = jnp.exp(jnp.where(fwd, tot - acs, exb)) * dt
        etot_ref[ci] = jnp.broadcast_to(jnp.exp(tot), etot_ref.shape[1:])


def _split2(v):
    hi = v.astype(BF16)
    return jnp.concatenate([hi, (v - hi.astype(F32)).astype(BF16)], axis=1)


def _head_expand_tables():
    j = np.arange(2 * LANES)[:, None] % LANES
    c = np.arange(SSD_INNER)[None, :]
    fwd = (j == c // SSD_HD)
    bwd = (j == SSD_HEADS + c // SSD_HD)
    return jnp.asarray(fwd, BF16), jnp.asarray(bwd, BF16)


def _ssd_bwd_state_kernel(xs_ref, b_ref, wst_ref, etot_ref, selb_ref, gin_ref, g_sc):
    @pl.when(pl.program_id(0) == 0)
    def _():
        g_sc[...] = jnp.zeros(g_sc.shape, F32)

    for b in range(xs_ref.shape[0]):
        gin_ref[b] = g_sc[b].astype(gin_ref.dtype)
        both = jnp.concatenate([_split2(wst_ref[b]), _split2(etot_ref[b])], axis=0)
        spread = jnp.dot(both, selb_ref[...], preferred_element_type=F32)
        t = wst_ref.shape[1]
        xw = (xs_ref[b].astype(F32) * spread[:t]).astype(BF16)
        dec = spread[t:t + 1]
        for g in range(SSD_GROUPS):
            gs = slice(g * SSD_GW, (g + 1) * SSD_GW)
            bm_t = b_ref[b, :, g * SSD_STATE:(g + 1) * SSD_STATE].T
            st = jnp.dot(bm_t, xw[:, gs], preferred_element_type=F32)
            g_sc[b, :, gs] = g_sc[b, :, gs] * dec[:, gs] + st


def _ssd_main_kernel(xs_ref, b_ref, c_ref, ccol_ref, rt_ref, esc_ref, wst_ref, etot_ref, gin_ref,
                     dsk_ref, self_ref, selb_ref, y_ref, h_sc):
    @pl.when(pl.program_id(1) == 0)
    def _():
        h_sc[...] = jnp.zeros(h_sc.shape, F32)

    t = xs_ref.shape[0]
    ccol = ccol_ref[...]
    r_t = rt_ref[...]
    esc2 = _split2(esc_ref[...])
    wst2 = _split2(wst_ref[...])
    etot2 = _split2(etot_ref[...])

    li = lax.broadcasted_iota(jnp.int32, (t, t), 0)
    si = lax.broadcasted_iota(jnp.int32, (t, t), 1)
    mask_f = jnp.where(li >= si, 0.0, MASK_VALUE)
    mask_b = jnp.where(si >= li, 0.0, MASK_VALUE)
    lo_half = lax.broadcasted_iota(jnp.int32, (t, LANES), 1) < SSD_HD

    for g in range(SSD_GROUPS):
        gs = slice(g * SSD_GW, (g + 1) * SSD_GW)
        bm = b_ref[:, g * SSD_STATE:(g + 1) * SSD_STATE]
        cm = c_ref[:, g * SSD_STATE:(g + 1) * SSD_STATE]
        cb = lax.dot_general(cm, bm, (((1,), (1,)), ((), ())), preferred_element_type=F32)
        hf = h_sc[:, gs]
        yf_all = jnp.dot(cm, hf.astype(BF16), preferred_element_type=F32)
        yb_all = jnp.dot(cm, gin_ref[:, gs], preferred_element_type=F32)
        e_in = jnp.dot(esc2, self_ref[g], preferred_element_type=F32)
        e_out = jnp.dot(esc2, selb_ref[g], preferred_element_type=F32)
        w_state = jnp.dot(wst2, self_ref[g], preferred_element_type=F32)
        dec = jnp.dot(etot2, self_ref[g], preferred_element_type=F32)[0:1]
        xw = []
        for k in range(SSD_HPG // 2):
            ea = g * SSD_HPG + 2 * k
            sl = slice(g * SSD_GW + k * LANES, g * SSD_GW + (k + 1) * LANES)
            ks = slice(k * LANES, (k + 1) * LANES)
            xp = xs_ref[:, sl]
            mms = []
            for e in (ea, ea + 1):
                eb = SSD_HEADS + e
                w = (jnp.exp2(ccol[:, e:e + 1] - r_t[e:e + 1, :] + mask_f)
                     + jnp.exp2(r_t[eb:eb + 1, :] - ccol[:, eb:eb + 1] + mask_b))
                mms.append((cb * w).astype(BF16))
            zero = jnp.zeros_like(xp)
            xx = jnp.concatenate([jnp.where(lo_half, xp, zero), jnp.where(lo_half, zero, xp)], axis=0)
            ypair = jnp.dot(jnp.concatenate(mms, axis=1), xx, preferred_element_type=F32)
            xf = xp.astype(F32)
            y_ref[:, sl] = (ypair + yf_all[:, ks] * e_in[:, ks] + yb_all[:, ks] * e_out[:, ks]
                            + xf * dsk_ref[:, sl])
            xw.append((xf * w_state[:, ks]).astype(BF16))
        st = jnp.dot(bm.T, jnp.concatenate(xw, axis=1), preferred_element_type=F32)
        h_sc[:, gs] = hf * dec + st


def _ssd(conv_out, dt_raw, dtb, alog, dskip, bsz, seq):
    t = SSD_CHUNK
    nc = seq // t
    assert seq % t == 0
    gn = SSD_GROUPS * SSD_STATE
    b_blk = SSD_INNER // gn
    c_blk = b_blk + 1
    cpp = min(8, nc)
    assert nc % cpp == 0
    sub8 = 8
    const_spec = pl.BlockSpec((1, LANES), lambda b, i: (0, 0))
    rows_spec = pl.BlockSpec((None, cpp * t, LANES), lambda b, i: (b, i, 0))
    row_arr = jax.ShapeDtypeStruct((bsz, seq, LANES), F32)
    ccol, rt, esc, wst, etot = pl.pallas_call(
        functools.partial(_ssd_prep_kernel, t=t),
        grid=(bsz, nc // cpp),
        in_specs=[rows_spec, const_spec, const_spec],
        out_specs=(rows_spec, pl.BlockSpec((None, cpp, LANES, t), lambda b, i: (b, i, 0, 0)), rows_spec, rows_spec,
                   pl.BlockSpec((None, cpp, sub8, LANES), lambda b, i: (b, i, 0, 0))),
        out_shape=(row_arr, jax.ShapeDtypeStruct((bsz, nc, LANES, t), F32), row_arr, row_arr,
                   jax.ShapeDtypeStruct((bsz, nc, sub8, LANES), F32)),
        compiler_params=_cparams(("parallel", "parallel")),
        name="ssd_decay_terms",
    )(dt_raw, dtb, alog)

    rev = lambda c: nc - 1 - c
    sel_f, sel_b = _head_expand_tables()
    by_group = lambda a: a.reshape(a.shape[0], SSD_GROUPS, SSD_GW).transpose(1, 0, 2)
    sel_spec = pl.BlockSpec((SSD_GROUPS, sel_f.shape[0], SSD_GW), lambda b, c: (0, 0, 0))
    gin = pl.pallas_call(
        _ssd_bwd_state_kernel,
        grid=(nc,),
        in_specs=[
            pl.BlockSpec((bsz, t, SSD_INNER), lambda c: (0, rev(c), 0)),
            pl.BlockSpec((bsz, t, gn), lambda c: (0, rev(c), b_blk)),
            pl.BlockSpec((bsz, t, LANES), lambda c: (0, rev(c), 0)),
            pl.BlockSpec((bsz, None, sub8, LANES), lambda c: (0, rev(c), 0, 0)),
            pl.BlockSpec(sel_b.shape, lambda c: (0, 0)),
        ],
        out_specs=pl.BlockSpec((bsz, None, SSD_STATE, SSD_INNER), lambda c: (0, rev(c), 0, 0)),
        out_shape=jax.ShapeDtypeStruct((bsz, nc, SSD_STATE, SSD_INNER), BF16),
        scratch_shapes=[pltpu.VMEM((bsz, SSD_STATE, SSD_INNER), F32)],
        compiler_params=_cparams(("arbitrary",)),
        name="ssd_bwd_states",
    )(conv_out, conv_out, wst, etot, sel_b)

    wide_spec = pl.BlockSpec((1, SSD_INNER), lambda b, c: (0, 0))
    tok_spec = pl.BlockSpec((None, t, LANES), lambda b, c: (b, c, 0))
    return pl.pallas_call(
        _ssd_main_kernel,
        grid=(bsz, nc),
        in_specs=[
            pl.BlockSpec((None, t, SSD_INNER), lambda b, c: (b, c, 0)),
            pl.BlockSpec((None, t, gn), lambda b, c: (b, c, b_blk)),
            pl.BlockSpec((None, t, gn), lambda b, c: (b, c, c_blk)),
            tok_spec,
            pl.BlockSpec((None, None, LANES, t), lambda b, c: (b, c, 0, 0)),
            tok_spec, tok_spec,
            pl.BlockSpec((None, None, sub8, LANES), lambda b, c: (b, c, 0, 0)),
            pl.BlockSpec((None, None, SSD_STATE, SSD_INNER), lambda b, c: (b, c, 0, 0)),
            wide_spec, sel_spec, sel_spec,
        ],
        out_specs=pl.BlockSpec((None, t, SSD_INNER), lambda b, c: (b, c, 0)),
        out_shape=jax.ShapeDtypeStruct((bsz, seq, SSD_INNER), F32),
        scratch_shapes=[pltpu.VMEM((SSD_STATE, SSD_INNER), F32)],
        compiler_params=_cparams(("parallel", "arbitrary")),
        name="ssd_main",
    )(conv_out, conv_out, conv_out, ccol, rt, esc, wst, etot, gin, dskip, by_group(sel_f), by_group(sel_b))


def _dft_tables(seq):
    n1, n2 = seq // F_N2, F_N2
    c = np.arange(F_GD)
    ang = 2 * np.pi * np.outer(c, c) / F_GD
    chan = np.concatenate([np.cos(ang), -np.sin(ang)], axis=1) / math.sqrt(F_GD)
    k1 = np.arange(n1)
    a1 = 2 * np.pi * np.outer(k1, k1) / n1
    stage_a = np.block([[np.cos(a1), np.sin(a1)], [-np.sin(a1), np.cos(a1)]])
    stage_a = np.kron(stage_a, np.eye(2))
    s2 = np.arange(n2)
    at = 2 * np.pi * np.outer(s2, k1) / seq
    at = at.reshape(n2 // 2, 2, n1).transpose(0, 2, 1).reshape(n2 // 2, 2 * n1)
    tw_c, tw_s = np.cos(at)[..., None], np.sin(at)[..., None]
    a2 = 2 * np.pi * np.outer(s2, s2) / n2
    stage_b = np.concatenate([np.cos(a2), np.sin(a2)], axis=1)
    return (jnp.asarray(chan, BF16), jnp.asarray(stage_a, BF16), jnp.asarray(tw_c, F32),
            jnp.asarray(tw_s, F32), jnp.asarray(stage_b, BF16))


def _fft_a_kernel(z_ref, ma_ref, twc_ref, tws_ref, o_ref, zs, ys, *, n1, rb):
    ct = z_ref.shape[-1]
    half = rb // 2
    _put_rows(zs, pltpu.bitcast(z_ref[...].reshape(2 * n1 * rb, ct), jnp.uint32))
    for j in range(half):
        zz = pltpu.bitcast(_get_rows(zs, j, 2 * n1, half), BF16)
        y = jnp.dot(ma_ref[...], zz, preferred_element_type=F32)
        yr, yi = y[:2 * n1], y[2 * n1:]
        tc, ts = twc_ref[j], tws_ref[j]
        out = jnp.concatenate([yr * tc + yi * ts, yi * tc - yr * ts], axis=0).astype(BF16)
        _put_rows(ys, pltpu.bitcast(out, jnp.uint32), j, half)
    o_ref[...] = pltpu.bitcast(_get_rows(ys), BF16).reshape(2, n1, rb, ct)


def _fft_b_kernel(y_ref, mb_ref, o_ref, os_ref, *, n2, kb, scale):
    ct = y_ref.shape[-1]
    for kk in range(kb):
        yy = y_ref[:, kk].reshape(2 * n2, ct)
        _put_rows(os_ref, jnp.dot(mb_ref[...], yy, preferred_element_type=F32) * scale, kk, kb)
    o_ref[...] = _get_rows(os_ref).reshape(n2, kb, ct).astype(o_ref.dtype)


def _fourier_seq(z2, tabs, bsz, seq, *, rb=BF16_ROWS, kb=BF16_ROWS, ct=768):
    _, ma, twc, tws, mb = tabs
    n1, n2 = seq // F_N2, F_N2
    w = z2.shape[-1]
    nct = w // ct
    za = z2.reshape(2, bsz, n1, n2, w)
    ab_spec = pl.BlockSpec((2, None, n1, rb, ct), lambda b, a, j: (0, b, 0, a, j))
    ya = pl.pallas_call(
        functools.partial(_fft_a_kernel, n1=n1, rb=rb),
        grid=(bsz, n2 // rb, nct),
        in_specs=[
            ab_spec,
            pl.BlockSpec((4 * n1, 4 * n1), lambda b, a, j: (0, 0)),
            pl.BlockSpec((rb // 2, 2 * n1, 1), lambda b, a, j: (a, 0, 0)),
            pl.BlockSpec((rb // 2, 2 * n1, 1), lambda b, a, j: (a, 0, 0)),
        ],
        out_specs=ab_spec,
        out_shape=jax.ShapeDtypeStruct((2, bsz, n1, n2, w), BF16),
        scratch_shapes=[pltpu.VMEM((ct // LANES, n1 * rb, LANES), jnp.uint32)] * 2,
        compiler_params=_cparams(("parallel", "parallel", "parallel")),
        name="fourier_stage_a",
    )(za, ma, twc, tws)
    out = pl.pallas_call(
        functools.partial(_fft_b_kernel, n2=n2, kb=kb, scale=1.0 / math.sqrt(seq)),
        grid=(bsz, n1 // kb, nct),
        in_specs=[
            pl.BlockSpec((2, None, kb, n2, ct), lambda b, k, j: (0, b, k, 0, j)),
            pl.BlockSpec((n2, 2 * n2), lambda b, k, j: (0, 0)),
        ],
        out_specs=pl.BlockSpec((None, n2, kb, ct), lambda b, k, j: (b, 0, k, j)),
        out_shape=jax.ShapeDtypeStruct((bsz, n2, n1, w), BF16),
        scratch_shapes=[pltpu.VMEM((ct // LANES, n2 * kb, LANES), F32)],
        compiler_params=_cparams(("parallel", "parallel", "parallel")),
        name="fourier_stage_b",
    )(ya, mb)
    return out.reshape(bsz, seq, w)


def _merge_kernel(*refs, dils):
    ng = len(dils)
    x_ref = refs[0]
    o_refs = refs[1:1 + ng]
    l_refs = refs[1 + ng:1 + 2 * ng]
    y_ref, z_ref, yg_ref, f_ref, gt_ref, wa_ref, ws_ref, wf_ref, wo_ref, out_ref, o_sc, l_sc = refs[1 + 2 * ng:]
    tm, d = x_ref.shape

    yf = jnp.dot(f_ref[...], wf_ref[...], preferred_element_type=F32)

    ys = None
    for g in range(SSD_GROUPS):
        gs = slice(g * SSD_GW, (g + 1) * SSD_GW)
        yy = y_ref[:, gs] * _silu(z_ref[:, gs].astype(F32))
        part = jnp.dot(_rms(yy, yg_ref[:, gs]).astype(BF16), ws_ref[gs, :], preferred_element_type=F32)
        ys = part if ys is None else ys + part

    for gi, dil in enumerate(dils):
        rows = tm // dil
        idx = lambda r: slice(None) if dil == 1 else pl.ds(r, rows, stride=dil)
        for r in range(dil):
            l_sc[gi, idx(r), :] = l_refs[gi][:, r * LANES:(r + 1) * LANES]
            for hh in range(ATTN_HPG):
                c0 = r * ATTN_OUT + hh * ATTN_HD
                o_sc[gi * ATTN_HPG + hh, idx(r), :] = o_refs[gi][:, c0:c0 + ATTN_HD].astype(F32)

    ls = [l_sc[gi] for gi in range(ng)]
    mx = functools.reduce(jnp.maximum, ls)
    ws = [jnp.exp(v - mx) for v in ls]
    inv = 1.0 / functools.reduce(lambda a, b: a + b, ws)
    heads = []
    for hh in range(ATTN_HPG):
        acc = None
        for gi in range(ng):
            alpha = (ws[gi] * inv)[:, hh:hh + 1]
            term = alpha * o_sc[gi * ATTN_HPG + hh]
            acc = term if acc is None else acc + term
        heads.append(acc.astype(BF16))
    o_attn = jnp.concatenate(heads, axis=1)

    ya = jnp.dot(o_attn, wa_ref[...], preferred_element_type=F32)
    gates = gt_ref[...].astype(F32)
    merged = gates[:, :d] * ya + gates[:, d:2 * d] * ys + gates[:, 2 * d:] * yf
    out_ref[...] = x_ref[...] + jnp.dot(merged.astype(BF16), wo_ref[...], preferred_element_type=F32)


def _merge(x, os_, ls_, dils, y, z, y_gain, f, gates, wa, ws, wf, wo, bsz, seq, *, tm=512):
    d = x.shape[-1]
    assert all((tm // dil) % BF16_ROWS == 0 for dil in dils)
    tok = lambda width: pl.BlockSpec((None, tm, width), lambda b, i: (b, i, 0))
    full = lambda a: pl.BlockSpec(a.shape, lambda b, i: (0, 0), pipeline_mode=pl.Buffered(1))
    ng = len(dils)
    in_specs = [tok(d)]
    in_specs += [pl.BlockSpec((None, tm // dil, dil * ATTN_OUT), lambda b, i: (b, i, 0)) for dil in dils]
    in_specs += [pl.BlockSpec((None, tm // dil, dil * LANES), lambda b, i: (b, i, 0)) for dil in dils]
    in_specs += [tok(y.shape[-1]), tok(z.shape[-1]), full(y_gain), tok(f.shape[-1]), tok(gates.shape[-1]),
                 full(wa), full(ws), full(wf), full(wo)]
    return pl.pallas_call(
        functools.partial(_merge_kernel, dils=tuple(dils)),
        grid=(bsz, seq // tm),
        in_specs=in_specs,
        out_specs=tok(d),
        out_shape=jax.ShapeDtypeStruct((bsz, seq, d), F32),
        scratch_shapes=[pltpu.VMEM((ng * ATTN_HPG, tm, ATTN_HD), F32), pltpu.VMEM((ng, tm, LANES), F32)],
        compiler_params=_cparams(("parallel", "parallel")),
        name="branch_merge",
    )(x, *os_, *ls_, y, z, y_gain, f, gates, wa, ws, wf, wo)


def _xattn_kernel(x_ref, g_ref, wq_ref, qg_ref, k_ref, v_ref, wo_ref, out_ref):
    tm, d = x_ref.shape
    hd = d // MEM_HEADS
    scale = hd ** -0.5
    rc = min(tm, XATTN_ROW_CHUNK)
    for c in range(tm // rc):
        rs = slice(c * rc, (c + 1) * rc)
        x = x_ref[rs, :]
        h = _rms(x, g_ref[...]).astype(BF16)
        q = jnp.dot(h, wq_ref[...], preferred_element_type=F32)
        heads = []
        for hh in range(MEM_HEADS):
            sl = slice(hh * hd, (hh + 1) * hd)
            qn = _rms(q[:, sl], qg_ref[...]).astype(BF16)
            s = lax.dot_general(qn, k_ref[:, sl], (((1,), (1,)), ((), ())), preferred_element_type=F32) * scale
            m = jnp.max(s, axis=-1, keepdims=True)
            p = jnp.exp(s - m)
            l = jnp.sum(p, axis=-1, keepdims=True)
            heads.append((jnp.dot(p.astype(BF16), v_ref[:, sl], preferred_element_type=F32) / l).astype(BF16))
        o = jnp.concatenate(heads, axis=1)
        out_ref[rs, :] = x + jnp.dot(o, wo_ref[...], preferred_element_type=F32)


def _xattn(x, g, wq, qg, k, v, wo, *, tm=512):
    bsz, seq, d = x.shape
    mt = k.shape[1]
    full = lambda a: pl.BlockSpec(a.shape, lambda b, i: (0, 0))
    g = g.reshape(1, d)
    qg = qg.reshape(1, -1)
    return pl.pallas_call(
        _xattn_kernel,
        grid=(bsz, seq // tm),
        in_specs=[
            pl.BlockSpec((None, tm, d), lambda b, i: (b, i, 0)),
            full(g), full(wq), full(qg),
            pl.BlockSpec((None, mt, d), lambda b, i: (b, 0, 0)),
            pl.BlockSpec((None, mt, d), lambda b, i: (b, 0, 0)),
            full(wo),
        ],
        out_specs=pl.BlockSpec((None, tm, d), lambda b, i: (b, i, 0)),
        out_shape=jax.ShapeDtypeStruct((bsz, seq, d), F32),
        compiler_params=_cparams(("parallel", "parallel")),
        name="mem_xattn",
    )(x, g, wq, qg, k, v, wo)


def _ffn_kernel(x_ref, g_ref, wg_ref, wu_ref, wd_ref, out_ref, *, tf):
    x = x_ref[...]
    h = _rms(x, g_ref[...]).astype(BF16)
    acc = x
    for c in range(wg_ref.shape[1] // tf):
        sl = slice(c * tf, (c + 1) * tf)
        gt = jnp.dot(h, wg_ref[:, sl], preferred_element_type=F32)
        up = jnp.dot(h, wu_ref[:, sl], preferred_element_type=F32)
        a = (_silu(gt) * up).astype(BF16)
        acc = acc + jnp.dot(a, wd_ref[sl, :], preferred_element_type=F32)
    out_ref[...] = acc


def _ffn(x, g, wg, wu, wd, *, tm=256):
    m, d = x.shape
    dff = wg.shape[1]
    tf = dff // 2 if (dff // 2) % LANES == 0 else dff
    full = lambda a: pl.BlockSpec(a.shape, lambda i: (0, 0))
    g = g.reshape(1, d)
    return pl.pallas_call(
        functools.partial(_ffn_kernel, tf=tf),
        grid=(m // tm,),
        in_specs=[pl.BlockSpec((tm, d), lambda i: (i, 0)), full(g), full(wg), full(wu), full(wd)],
        out_specs=pl.BlockSpec((tm, d), lambda i: (i, 0)),
        out_shape=jax.ShapeDtypeStruct((m, d), F32),
        compiler_params=_cparams(("parallel",)),
        name="swiglu_ffn",
    )(x, g, wg, wu, wd)


def _pad_lanes(a):
    a = a.reshape(-1, 2 * SSD_HEADS)
    return jnp.pad(a, ((0, 0), (0, LANES - 2 * SSD_HEADS)))


def kernel(x, mem, rel_bias, mix_norm_g, w_in, gate_bias, attn_q_norm_g, attn_k_norm_g, conv_w, conv_b, dt_bias, a_log, d_skip, ssd_norm_g, w_branch_attn, w_branch_ssd, w_branch_fourier, w_mix_out, xattn_norm_g, mem_norm_g, w_xq, w_xk, w_xv, xattn_q_norm_g, xattn_k_norm_g, w_xo, ffn_norm_g, w_ffn_gate, w_ffn_up, w_ffn_down):
    bsz, seq, d = x.shape
    depth = w_in.shape[0]
    m = bsz * seq
    xf = x.reshape(m, d)
    memf = mem.reshape(bsz * mem.shape[1], d)
    tabs = _dft_tables(seq)
    dils = [dil for _, dil in ATTN_GROUPS]
    biases = [_attn_bias(rel_bias, gi, dil) for gi, dil in enumerate(dils)]
    offs = np.cumsum([0, ATTN_WIDTH, ATTN_WIDTH, ATTN_WIDTH, SSD_INNER, SSD_CONV_CH, 2 * SSD_HEADS, F_WIDTH, 3 * d])
    bf = lambda a: a.astype(BF16)

    for l in range(depth):
        wl = bf(w_in[l])
        seg = lambda i: wl[:, offs[i]:offs[i + 1]]
        h = _rmsnorm(xf, mix_norm_g[l])
        gains = jnp.stack([attn_q_norm_g[l], attn_k_norm_g[l], jnp.ones_like(attn_q_norm_g[l])]).reshape(3, 1, ATTN_HD)
        os_, ls_ = [], []
        for gi, dil in enumerate(dils):
            qkv = _qkv_group(h, wl, gains, gi, dil, bsz, seq)
            o_g, lse_g = _attn_group(qkv, biases[gi], dil, bsz, seq)
            os_.append(o_g)
            ls_.append(lse_g)

        z = _matmul(h, wl, out_dtype=BF16, tn=512, tm=2048, col0=int(offs[3]), n=SSD_INNER)
        dt_raw = _matmul(h, _pad_lanes(seg(5)), out_dtype=F32, tn=LANES, tm=2048)
        z2 = _matmul(h, seg(6), out_dtype=BF16, tn=768, tm=2048, epilogue="chandft", extra=tabs[0])
        gates = _matmul(h, seg(7), out_dtype=BF16, tn=1024, tm=2048, epilogue="sigmoid_bias", extra=gate_bias[l])

        conv_out = _proj_conv_silu(h, wl, conv_w[l], conv_b[l], seq, col0=int(offs[4])).reshape(bsz, seq, SSD_CONV_CH)
        y_ssd = _ssd(conv_out, dt_raw.reshape(bsz, seq, -1), _pad_lanes(dt_bias[l]), _pad_lanes(a_log[l]),
                     jnp.repeat(d_skip[l], SSD_HD).reshape(1, SSD_INNER), bsz, seq)

        f_re = _fourier_seq(z2, tabs, bsz, seq)

        x3 = _merge(xf.reshape(bsz, seq, d), os_, ls_, dils, y_ssd, z.reshape(bsz, seq, SSD_INNER),
                    ssd_norm_g[l].reshape(1, SSD_INNER), f_re, gates.reshape(bsz, seq, 3 * d),
                    bf(w_branch_attn[l]), bf(w_branch_ssd[l]), bf(w_branch_fourier[l]), bf(w_mix_out[l]), bsz, seq)

        hd_m = d // MEM_HEADS
        hm = _rmsnorm(memf, mem_norm_g[l])
        km = _matmul(hm, bf(w_xk[l]), out_dtype=BF16, tn=512, epilogue="headnorm",
                     extra=xattn_k_norm_g[l], head_dim=hd_m)
        vm = _matmul(hm, bf(w_xv[l]), out_dtype=BF16, tn=512)
        x3 = _xattn(x3, xattn_norm_g[l], bf(w_xq[l]), xattn_q_norm_g[l], km.reshape(bsz, -1, d),
                    vm.reshape(bsz, -1, d), bf(w_xo[l]))

        xf = _ffn(x3.reshape(m, d), ffn_norm_g[l], bf(w_ffn_gate[l]), bf(w_ffn_up[l]), bf(w_ffn_down[l]))

    return xf.reshape(bsz, seq, d)
```

```python
import functools
import math

import numpy as np
import jax
import jax.numpy as jnp
from jax import lax
from jax.experimental import pallas as pl
from jax.experimental.pallas import tpu as pltpu

F32 = jnp.float32
BF16 = jnp.bfloat16

NORM_EPS = 1e-6
MASK_VALUE = -1e30

ATTN_GROUPS = ((128, 1), (512, 4), (2048, 16))
ATTN_HPG = 4
ATTN_HD = 128
ATTN_HEADS = ATTN_HPG * len(ATTN_GROUPS)
ATTN_WIDTH = ATTN_HEADS * ATTN_HD
ATTN_OUT = ATTN_HPG * ATTN_HD
ATTN_HALF = 64
ATTN_TQ = 2 * ATTN_HALF
ATTN_TK = ATTN_TQ + 2 * ATTN_HALF
NUM_BUCKETS = 32
MAX_DISTANCE = 1024

SSD_HEADS = 32
SSD_HD = 64
SSD_GROUPS = 4
SSD_HPG = SSD_HEADS // SSD_GROUPS
SSD_STATE = 128
SSD_INNER = SSD_HEADS * SSD_HD
SSD_GW = SSD_INNER // SSD_GROUPS
SSD_CONV = 7
SSD_CHUNK = 128
SSD_CONV_CH = SSD_INNER + 2 * SSD_GROUPS * SSD_STATE
LANES = 128
BF16_ROWS = 16

F_GROUPS = 6
F_GD = 256
F_WIDTH = F_GROUPS * F_GD
F_N2 = 128

MEM_HEADS = 4
XATTN_ROW_CHUNK = 256

VMEM_LIMIT = 56 * 1024 * 1024


def _cparams(sem):
    return pltpu.CompilerParams(dimension_semantics=sem, vmem_limit_bytes=VMEM_LIMIT)


def _silu(x):
    return x * jax.nn.sigmoid(x)


def _rms(x, g):
    ms = jnp.mean(x * x, axis=-1, keepdims=True)
    return x * lax.rsqrt(ms + NORM_EPS) * g


def _get_rows(ref3, start=0, size=None, stride=1):
    idx = slice(None) if size is None else pl.ds(start, size, stride=stride)
    return jnp.concatenate([ref3[c, idx, :] for c in range(ref3.shape[0])], axis=1)


def _put_rows(ref3, val, start=0, stride=1):
    size = val.shape[0]
    idx = slice(None) if (stride == 1 and size == ref3.shape[1]) else pl.ds(start, size, stride=stride)
    for c in range(ref3.shape[0]):
        ref3[c, idx, :] = val[:, c * LANES:(c + 1) * LANES]


def _rmsnorm_kernel(x_ref, g_ref, o_ref):
    o_ref[...] = _rms(x_ref[...], g_ref[...]).astype(o_ref.dtype)


def _rmsnorm(x, g, *, tm=1024):
    m, k = x.shape
    tm = min(tm, m)
    return pl.pallas_call(
        _rmsnorm_kernel,
        grid=(m // tm,),
        in_specs=[pl.BlockSpec((tm, k), lambda i: (i, 0)), pl.BlockSpec((1, k), lambda i: (0, 0))],
        out_specs=pl.BlockSpec((tm, k), lambda i: (i, 0)),
        out_shape=jax.ShapeDtypeStruct((m, k), BF16),
        compiler_params=_cparams(("parallel",)),
        name="rmsnorm",
    )(x, g.reshape(1, k))


MM_ROW_CHUNK = 512


def _mm_kernel(*refs, epilogue, head_dim):
    h_ref, w_ref = refs[:2]
    o_ref = refs[-1]
    tm = h_ref.shape[0]
    rc = min(MM_ROW_CHUNK, tm)
    for c in range(tm // rc):
        rs = slice(c * rc, (c + 1) * rc)
        acc = jnp.dot(h_ref[rs, :], w_ref[...], preferred_element_type=F32)
        tn = acc.shape[1]
        if epilogue == "headnorm":
            hg_ref = refs[2]
            for hh in range(tn // head_dim):
                sl = slice(hh * head_dim, (hh + 1) * head_dim)
                o_ref[rs, sl] = _rms(acc[:, sl], hg_ref[...]).astype(o_ref.dtype)
        elif epilogue == "sigmoid_bias":
            b_ref = refs[2]
            o_ref[rs, :] = jax.nn.sigmoid(acc + b_ref[...]).astype(o_ref.dtype)
        elif epilogue == "chandft":
            dft_ref = refs[2]
            a16 = acc.astype(BF16)
            for gg in range(tn // F_GD):
                sl = slice(gg * F_GD, (gg + 1) * F_GD)
                r = jnp.dot(a16[:, sl], dft_ref[...], preferred_element_type=F32)
                o_ref[0, rs, sl] = r[:, :F_GD].astype(o_ref.dtype)
                o_ref[1, rs, sl] = r[:, F_GD:].astype(o_ref.dtype)
        else:
            o_ref[rs, :] = acc.astype(o_ref.dtype)


def _matmul(h, w, *, out_dtype, tn, tm=1024, epilogue="plain", extra=None, head_dim=None, col0=0, n=None):
    m, k = h.shape
    n = w.shape[1] if n is None else n
    tm = min(tm, m)
    assert m % tm == 0 and n % tn == 0 and col0 % tn == 0
    cb0 = col0 // tn
    in_specs = [pl.BlockSpec((tm, k), lambda i, j: (i, 0)), pl.BlockSpec((k, tn), lambda i, j: (0, cb0 + j))]
    args = [h, w]
    if epilogue == "headnorm":
        in_specs.append(pl.BlockSpec((1, head_dim), lambda i, j: (0, 0)))
        args.append(extra.reshape(1, head_dim))
    elif epilogue == "sigmoid_bias":
        in_specs.append(pl.BlockSpec((1, tn), lambda i, j: (0, j)))
        args.append(extra.reshape(1, n))
    elif epilogue == "chandft":
        in_specs.append(pl.BlockSpec(extra.shape, lambda i, j: (0, 0)))
        args.append(extra)
    if epilogue == "chandft":
        out_shape = jax.ShapeDtypeStruct((2, m, n), out_dtype)
        out_spec = pl.BlockSpec((2, tm, tn), lambda i, j: (0, i, j))
    else:
        out_shape = jax.ShapeDtypeStruct((m, n), out_dtype)
        out_spec = pl.BlockSpec((tm, tn), lambda i, j: (i, j))
    return pl.pallas_call(
        functools.partial(_mm_kernel, epilogue=epilogue, head_dim=head_dim),
        grid=(m // tm, n // tn),
        in_specs=in_specs,
        out_specs=out_spec,
        out_shape=out_shape,
        compiler_params=_cparams(("parallel", "parallel")),
        name="matmul_" + epilogue,
    )(*args)


def _qkv_kernel(h_ref, w_ref, hg_ref, o_ref, *scratch, dil):
    j = pl.program_id(1)
    tm = h_ref.shape[0]
    width = w_ref.shape[1]
    nh = width // ATTN_HD
    rc = min(MM_ROW_CHUNK, tm)
    rows = rc // dil
    for c in range(tm // rc):
        acc = jnp.dot(h_ref[c * rc:(c + 1) * rc, :], w_ref[...], preferred_element_type=F32)
        if dil > 1:
            scr = scratch[0]
            for hh in range(nh):
                scr[hh, c * rc:(c + 1) * rc, :] = acc[:, hh * ATTN_HD:(hh + 1) * ATTN_HD]
        for r in range(dil):
            for hh in range(nh):
                if dil == 1:
                    ph = acc[:, hh * ATTN_HD:(hh + 1) * ATTN_HD]
                else:
                    ph = scr[hh, pl.ds(c * rc + r, rows, stride=dil), :]
                val = jnp.where(j < 2, _rms(ph, hg_ref[...]), ph)
                c0 = r * width + hh * ATTN_HD
                o_ref[c * rows:(c + 1) * rows, c0:c0 + ATTN_HD] = val.astype(o_ref.dtype)


def _qkv_group(h, w, gains, gi, dil, bsz, seq, *, tm=2048):
    m, k = h.shape
    sub = seq // dil
    per_b = seq // tm
    rows = tm // dil
    assert seq % tm == 0 and rows % BF16_ROWS == 0
    return pl.pallas_call(
        functools.partial(_qkv_kernel, dil=dil),
        grid=(m // tm, 3),
        in_specs=[
            pl.BlockSpec((tm, k), lambda i, j: (i, 0)),
            pl.BlockSpec((k, ATTN_OUT), lambda i, j: (0, j * len(ATTN_GROUPS) + gi)),
            pl.BlockSpec((None, 1, ATTN_HD), lambda i, j: (j, 0, 0)),
        ],
        out_specs=pl.BlockSpec((None, None, rows, dil * ATTN_OUT), lambda i, j: (j, i // per_b, i % per_b, 0)),
        out_shape=jax.ShapeDtypeStruct((3, bsz, sub, dil * ATTN_OUT), BF16),
        scratch_shapes=[pltpu.VMEM((ATTN_HPG, tm, ATTN_HD), F32)] if dil > 1 else [],
        compiler_params=_cparams(("parallel", "parallel")),
        name=f"qkv_proj_d{dil}",
    )(h, w, gains)


def _t5_bucket_np(rel):
    half_b = NUM_BUCKETS // 2
    exact = half_b // 2
    dist = np.abs(rel)
    log_ratio = np.log(np.maximum(dist, 1) / exact) / np.log(MAX_DISTANCE / exact)
    far = np.minimum(exact + (log_ratio * (half_b - exact)).astype(np.int32), half_b - 1)
    return np.where(rel > 0, half_b, 0) + np.where(dist < exact, dist, far)


def _attn_bias(rel_bias, gi, dil):
    i = np.arange(ATTN_TQ)[:, None]
    j = np.arange(ATTN_TK)[None, :]
    rel = j - ATTN_HALF - i
    idx = _t5_bucket_np(rel * dil)
    onehot = jnp.asarray(np.eye(NUM_BUCKETS, dtype=np.float32)[idx])
    tab = rel_bias[:, gi * ATTN_HPG:(gi + 1) * ATTN_HPG].astype(F32)
    b = jnp.einsum("qkn,nh->hqk", onehot, tab, precision=lax.Precision.HIGHEST)
    band = np.broadcast_to(np.abs(rel) <= ATTN_HALF, (ATTN_TQ, ATTN_TK))
    keep = np.stack([band & ((j >= ATTN_HALF) | (not first)) & ((j < ATTN_TK - ATTN_HALF) | (not last))
                     for last in (False, True) for first in (False, True)])
    return jnp.where(jnp.asarray(keep)[:, None], b[None], MASK_VALUE)


def _attn_kernel(q_ref, kp_ref, kc_ref, kn_ref, vp_ref, vc_ref, vn_ref, bias_ref, o_ref, lse_ref, kpad, vpad,
                 *, tile, sub):
    i = pl.program_id(2)
    hf = ATTN_HALF
    kpad[0:hf, :] = kp_ref[...]
    kpad[hf:hf + tile, :] = kc_ref[...]
    kpad[hf + tile:, :] = kn_ref[...]
    vpad[0:hf, :] = vp_ref[...]
    vpad[hf:hf + tile, :] = vc_ref[...]
    vpad[hf + tile:, :] = vn_ref[...]

    scale = ATTN_HD ** -0.5
    lane = lax.broadcasted_iota(jnp.int32, (ATTN_TQ, LANES), 1)
    tiles = tile // ATTN_TQ

    def body(t, carry):
        s0 = pl.multiple_of(t * ATTN_TQ, ATTN_TQ)
        gt = i * tiles + t
        variant = (gt == 0).astype(jnp.int32) + 2 * (gt == sub // ATTN_TQ - 1).astype(jnp.int32)
        lse_tile = jnp.zeros((ATTN_TQ, LANES), F32)
        for hh in range(ATTN_HPG):
            sl = slice(hh * ATTN_HD, (hh + 1) * ATTN_HD)
            q = q_ref[pl.ds(s0, ATTN_TQ), sl]
            kw = kpad[pl.ds(s0, ATTN_TK), sl]
            vw = vpad[pl.ds(s0, ATTN_TK), sl]
            s = (lax.dot_general(q, kw, (((1,), (1,)), ((), ())), preferred_element_type=F32) * scale
                 + bias_ref[variant, hh])
            m = jnp.max(s, axis=-1, keepdims=True)
            p = jnp.exp(s - m)
            l = jnp.sum(p, axis=-1, keepdims=True)
            o = jnp.dot(p.astype(BF16), vw, preferred_element_type=F32) / l
            o_ref[pl.ds(s0, ATTN_TQ), sl] = o.astype(o_ref.dtype)
            lse_tile = jnp.where(lane == hh, m + jnp.log(l), lse_tile)
        lse_ref[pl.ds(s0, ATTN_TQ), :] = lse_tile
        return carry

    lax.fori_loop(0, tile // ATTN_TQ, body, 0, unroll=4)


def _attn_group(qkv, bias, dil, bsz, seq, *, tile=1024):
    sub = seq // dil
    tile = min(tile, sub)
    nblk = sub // tile
    hb = tile // ATTN_HALF
    last_h = sub // ATTN_HALF - 1
    assert sub % tile == 0 and tile % ATTN_TQ == 0
    cur = lambda which: pl.BlockSpec((None, None, tile, ATTN_OUT), lambda b, r, i: (which, b, i, r))
    prev = lambda which: pl.BlockSpec((None, None, ATTN_HALF, ATTN_OUT),
                                      lambda b, r, i: (which, b, jnp.maximum(i * hb - 1, 0), r))
    nxt = lambda which: pl.BlockSpec((None, None, ATTN_HALF, ATTN_OUT),
                                     lambda b, r, i: (which, b, jnp.minimum((i + 1) * hb, last_h), r))
    return pl.pallas_call(
        functools.partial(_attn_kernel, tile=tile, sub=sub),
        grid=(bsz, dil, nblk),
        in_specs=[cur(0), prev(1), cur(1), nxt(1), prev(2), cur(2), nxt(2),
                  pl.BlockSpec(bias.shape, lambda b, r, i: (0, 0, 0, 0))],
        out_specs=(pl.BlockSpec((None, tile, ATTN_OUT), lambda b, r, i: (b, i, r)),
                   pl.BlockSpec((None, tile, LANES), lambda b, r, i: (b, i, r))),
        out_shape=(jax.ShapeDtypeStruct((bsz, sub, dil * ATTN_OUT), BF16),
                   jax.ShapeDtypeStruct((bsz, sub, dil * LANES), F32)),
        scratch_shapes=[pltpu.VMEM((tile + 2 * ATTN_HALF, ATTN_OUT), BF16)] * 2,
        compiler_params=_cparams(("parallel", "parallel", "parallel")),
        name=f"dilated_attn_d{dil}",
    )(qkv, qkv, qkv, qkv, qkv, qkv, qkv, bias)


CONV_HALO = BF16_ROWS
CONV_RC = 128
CONV_APRON = 8


SUBLANES = 8


def _shift_rows(x, s):
    r, c = x.shape
    x3 = x.reshape(r // SUBLANES, SUBLANES, c)
    rot = pltpu.roll(x3, (-s) % SUBLANES, 1)
    sub = lax.broadcasted_iota(jnp.int32, x3.shape, 1)
    if s > 0:
        y = jnp.where(sub < SUBLANES - s, rot, jnp.concatenate([rot[1:], rot[:1]], axis=0))
    else:
        y = jnp.where(sub >= -s, rot, jnp.concatenate([rot[-1:], rot[:-1]], axis=0))
    return y.reshape(r, c)


def _proj_conv_kernel(hp_ref, hc_ref, hn_ref, w_ref, cw_ref, cb_ref, o_ref, *, tm, per_b):
    i = pl.program_id(0)
    first = (i % per_b) == 0
    last = (i % per_b) == per_b - 1
    w = w_ref[...]
    halo_p = jnp.dot(hp_ref[...], w, preferred_element_type=F32)
    halo_n = jnp.dot(hn_ref[...], w, preferred_element_type=F32)
    halo_p = jnp.where(first, 0.0, halo_p)
    halo_n = jnp.where(last, 0.0, halo_n)
    pad = SSD_CONV // 2
    acc = {}

    def project(s):
        acc[s] = jnp.dot(hc_ref[s * MM_ROW_CHUNK:(s + 1) * MM_ROW_CHUNK, :], w, preferred_element_type=F32)

    def tile_rows(lo, hi):
        pieces, r = [], lo
        while r < hi:
            if r < 0:
                src, off, end = halo_p, CONV_HALO + r, min(hi, 0)
            elif r >= tm:
                src, off, end = halo_n, r - tm, hi
            else:
                s = r // MM_ROW_CHUNK
                src, off, end = acc[s], r - s * MM_ROW_CHUNK, min(hi, (s + 1) * MM_ROW_CHUNK)
            pieces.append(src[off:off + end - r])
            r = end
        return pieces[0] if len(pieces) == 1 else jnp.concatenate(pieces, axis=0)

    def conv_chunk(c):
        x = tile_rows(c * CONV_RC - CONV_APRON, (c + 1) * CONV_RC + CONV_APRON)
        tap = lambda off: cw_ref[pad + off:pad + off + 1, :]
        xm, xq = _shift_rows(x, -3), _shift_rows(x, 3)
        f0 = tap(-3) * xm + tap(0) * x + tap(3) * xq + cb_ref[...]
        f1 = tap(-2) * xm + tap(1) * x
        fm = tap(-1) * x + tap(2) * xq
        y = f0 + _shift_rows(f1, 1) + _shift_rows(fm, -1)
        o_ref[c * CONV_RC:(c + 1) * CONV_RC, :] = _silu(y[CONV_APRON:CONV_APRON + CONV_RC]).astype(o_ref.dtype)

    nsub = tm // MM_ROW_CHUNK
    per_sub = MM_ROW_CHUNK // CONV_RC
    project(0)
    for s in range(nsub):
        if s + 1 < nsub:
            project(s + 1)
        for c in range(s * per_sub, (s + 1) * per_sub):
            conv_chunk(c)


def _proj_conv_silu(h, w, cw, cb, seq, *, col0, tm=1024, tn=512):
    m, k = h.shape
    n = cw.shape[1]
    assert col0 % tn == 0
    cb0 = col0 // tn
    per_b = seq // tm
    hb = tm // CONV_HALO
    assert seq % tm == 0 and n % tn == 0 and SSD_CONV == 7
    return pl.pallas_call(
        functools.partial(_proj_conv_kernel, tm=tm, per_b=per_b),
        grid=(m // tm, n // tn),
        in_specs=[
            pl.BlockSpec((CONV_HALO, k), lambda i, j: (jnp.maximum(i * hb - 1, 0), 0)),
            pl.BlockSpec((tm, k), lambda i, j: (i, 0)),
            pl.BlockSpec((CONV_HALO, k), lambda i, j: (jnp.minimum((i + 1) * hb, m // CONV_HALO - 1), 0)),
            pl.BlockSpec((k, tn), lambda i, j: (0, cb0 + j)),
            pl.BlockSpec((SSD_CONV, tn), lambda i, j: (0, j)),
            pl.BlockSpec((1, tn), lambda i, j: (0, j)),
        ],
        out_specs=pl.BlockSpec((tm, tn), lambda i, j: (i, j)),
        out_shape=jax.ShapeDtypeStruct((m, n), BF16),
        compiler_params=_cparams(("parallel", "parallel")),
        name="proj_conv_silu",
    )(h, h, h, w, cw, cb.reshape(1, n))


def _softplus(x):
    return jnp.maximum(x, 0.0) + jnp.log1p(jnp.exp(-jnp.abs(x)))


def _ssd_prep_kernel(dtr_ref, dtb_ref, alog_ref, ccol_ref, rt_ref, esc_ref, wst_ref, etot_ref, *, t):
    half = LANES // 2
    log2e = math.log2(math.e)
    lane = lax.broadcasted_iota(jnp.int32, (t, LANES), 1)
    fwd = lane < SSD_HEADS
    tri = (lax.broadcasted_iota(jnp.int32, (t, t), 1) <= lax.broadcasted_iota(jnp.int32, (t, t), 0)).astype(BF16)
    neg_a = jnp.exp(alog_ref[...])
    for ci in range(dtr_ref.shape[0] // t):
        rs = slice(ci * t, (ci + 1) * t)
        dt = _softplus(dtr_ref[rs, :] + dtb_ref[...])
        la = jnp.where(lane < 2 * SSD_HEADS, -(dt * neg_a), 0.0)
        hi = la.astype(BF16)
        r1 = la - hi.astype(F32)
        mid = r1.astype(BF16)
        lo = (r1 - mid.astype(F32)).astype(BF16)
        packed = (hi.astype(F32) + pltpu.roll(mid.astype(F32), half, 1)).astype(BF16)
        res = jnp.dot(tri, jnp.concatenate([packed, lo], axis=1), preferred_element_type=F32)
        a0 = res[:, :LANES]
        acs = a0 + pltpu.roll(a0, half, 1) + res[:, LANES:]
        exb = acs - la
        ldt = jnp.log(dt)
        tot = acs[t - 1:t, :]
        ccol_ref[rs, :] = jnp.where(fwd, acs, exb) * log2e
        rt_ref[ci] = (jnp.where(fwd, acs - ldt, exb + ldt) * log2e).T
        esc_ref[rs, :] = jnp.exp(jnp.where(fwd, acs, tot - exb))
        wst_ref[rs, :] = jnp.exp(jnp.where(fwd, tot - acs, exb)) * dt
        etot_ref[ci] = jnp.broadcast_to(jnp.exp(tot), etot_ref.shape[1:])


def _split2(v):
    hi = v.astype(BF16)
    return jnp.concatenate([hi, (v - hi.astype(F32)).astype(BF16)], axis=1)


def _head_expand_tables():
    j = np.arange(2 * LANES)[:, None] % LANES
    c = np.arange(SSD_INNER)[None, :]
    fwd = (j == c // SSD_HD)
    bwd = (j == SSD_HEADS + c // SSD_HD)
    return jnp.asarray(fwd, BF16), jnp.asarray(bwd, BF16)


def _ssd_bwd_state_kernel(xs_ref, b_ref, wst_ref, etot_ref, selb_ref, gin_ref, g_sc):
    @pl.when(pl.program_id(0) == 0)
    def _():
        g_sc[...] = jnp.zeros(g_sc.shape, F32)

    cps = etot_ref.shape[1]
    t = wst_ref.shape[1] // cps
    for sc, b in [(sc, b) for sc in reversed(range(cps)) for b in range(xs_ref.shape[0])]:
        rows = slice(sc * t, (sc + 1) * t)
        gin_ref[b, sc] = g_sc[b].astype(gin_ref.dtype)
        both = jnp.concatenate([_split2(wst_ref[b, rows, :]), _split2(etot_ref[b, sc])], axis=0)
        spread = jnp.dot(both, selb_ref[...], preferred_element_type=F32)
        xw = (xs_ref[b, rows, :].astype(F32) * spread[:t]).astype(BF16)
        dec = spread[t:t + 1]
        for g in range(SSD_GROUPS):
            gs = slice(g * SSD_GW, (g + 1) * SSD_GW)
            bm_t = b_ref[b, rows, g * SSD_STATE:(g + 1) * SSD_STATE].T
            st = jnp.dot(bm_t, xw[:, gs], preferred_element_type=F32)
            g_sc[b, :, gs] = g_sc[b, :, gs] * dec[:, gs] + st


def _ssd_main_kernel(xs_ref, b_ref, c_ref, ccol_ref, rt_ref, esc_ref, wst_ref, etot_ref, gin_ref,
                     dsk_ref, self_ref, selb_ref, y_ref, h_sc):
    @pl.when(pl.program_id(1) == 0)
    def _():
        h_sc[...] = jnp.zeros(h_sc.shape, F32)

    t = rt_ref.shape[-1]
    li = lax.broadcasted_iota(jnp.int32, (t, t), 0)
    si = lax.broadcasted_iota(jnp.int32, (t, t), 1)
    mask_f = jnp.where(li >= si, 0.0, MASK_VALUE)
    mask_b = jnp.where(si >= li, 0.0, MASK_VALUE)
    lo_half = lax.broadcasted_iota(jnp.int32, (t, LANES), 1) < SSD_HD

    for sc, g in [(sc, g) for sc in range(rt_ref.shape[0]) for g in range(SSD_GROUPS)]:
        rows = slice(sc * t, (sc + 1) * t)
        ccol = ccol_ref[rows, :]
        r_t = rt_ref[sc]
        esc2 = _split2(esc_ref[rows, :])
        wst2 = _split2(wst_ref[rows, :])
        etot2 = _split2(etot_ref[sc])
        gs = slice(g * SSD_GW, (g + 1) * SSD_GW)
        bm = b_ref[rows, g * SSD_STATE:(g + 1) * SSD_STATE]
        cm = c_ref[rows, g * SSD_STATE:(g + 1) * SSD_STATE]
        cb = lax.dot_general(cm, bm, (((1,), (1,)), ((), ())), preferred_element_type=F32)
        hf = h_sc[:, gs]
        yf_all = jnp.dot(cm, hf.astype(BF16), preferred_element_type=F32)
        yb_all = jnp.dot(cm, gin_ref[sc, :, gs], preferred_element_type=F32)
        e_in = jnp.dot(esc2, self_ref[g], preferred_element_type=F32)
        e_out = jnp.dot(esc2, selb_ref[g], preferred_element_type=F32)
        w_state = jnp.dot(wst2, self_ref[g], preferred_element_type=F32)
        dec = jnp.dot(etot2, self_ref[g], preferred_element_type=F32)[0:1]
        xw = []
        for k in range(SSD_HPG // 2):
            ea = g * SSD_HPG + 2 * k
            sl = slice(g * SSD_GW + k * LANES, g * SSD_GW + (k + 1) * LANES)
            ks = slice(k * LANES, (k + 1) * LANES)
            xp = xs_ref[rows, sl]
            mms = []
            for e in (ea, ea + 1):
                eb = SSD_HEADS + e
                w = (jnp.exp2(ccol[:, e:e + 1] - r_t[e:e + 1, :] + mask_f)
                     + jnp.exp2(r_t[eb:eb + 1, :] - ccol[:, eb:eb + 1] + mask_b))
                mms.append((cb * w).astype(BF16))
            zero = jnp.zeros_like(xp)
            xx = jnp.concatenate([jnp.where(lo_half, xp, zero), jnp.where(lo_half, zero, xp)], axis=0)
            ypair = jnp.dot(jnp.concatenate(mms, axis=1), xx, preferred_element_type=F32)
            xf = xp.astype(F32)
            y_ref[rows, sl] = (ypair + yf_all[:, ks] * e_in[:, ks] + yb_all[:, ks] * e_out[:, ks]
                               + xf * dsk_ref[:, sl])
            xw.append((xf * w_state[:, ks]).astype(BF16))
        st = jnp.dot(bm.T, jnp.concatenate(xw, axis=1), preferred_element_type=F32)
        h_sc[:, gs] = hf * dec + st


def _ssd(conv_out, dt_raw, dtb, alog, dskip, bsz, seq):
    t = SSD_CHUNK
    nc = seq // t
    assert seq % t == 0
    gn = SSD_GROUPS * SSD_STATE
    b_blk = SSD_INNER // gn
    c_blk = b_blk + 1
    cpp = min(8, nc)
    assert nc % cpp == 0
    sub8 = 8
    const_spec = pl.BlockSpec((1, LANES), lambda b, i: (0, 0))
    rows_spec = pl.BlockSpec((None, cpp * t, LANES), lambda b, i: (b, i, 0))
    row_arr = jax.ShapeDtypeStruct((bsz, seq, LANES), F32)
    ccol, rt, esc, wst, etot = pl.pallas_call(
        functools.partial(_ssd_prep_kernel, t=t),
        grid=(bsz, nc // cpp),
        in_specs=[rows_spec, const_spec, const_spec],
        out_specs=(rows_spec, pl.BlockSpec((None, cpp, LANES, t), lambda b, i: (b, i, 0, 0)), rows_spec, rows_spec,
                   pl.BlockSpec((None, cpp, sub8, LANES), lambda b, i: (b, i, 0, 0))),
        out_shape=(row_arr, jax.ShapeDtypeStruct((bsz, nc, LANES, t), F32), row_arr, row_arr,
                   jax.ShapeDtypeStruct((bsz, nc, sub8, LANES), F32)),
        compiler_params=_cparams(("parallel", "parallel")),
        name="ssd_decay_terms",
    )(dt_raw, dtb, alog)

    cps = next(n for n in (4, 2, 1) if nc % n == 0)
    rev = lambda c: nc // cps - 1 - c
    sel_f, sel_b = _head_expand_tables()
    by_group = lambda a: a.reshape(a.shape[0], SSD_GROUPS, SSD_GW).transpose(1, 0, 2)
    sel_spec = pl.BlockSpec((SSD_GROUPS, sel_f.shape[0], SSD_GW), lambda b, c: (0, 0, 0))
    gin = pl.pallas_call(
        _ssd_bwd_state_kernel,
        grid=(nc // cps,),
        in_specs=[
            pl.BlockSpec((bsz, cps * t, SSD_INNER), lambda c: (0, rev(c), 0)),
            pl.BlockSpec((bsz, cps * t, gn), lambda c: (0, rev(c), b_blk)),
            pl.BlockSpec((bsz, cps * t, LANES), lambda c: (0, rev(c), 0)),
            pl.BlockSpec((bsz, cps, sub8, LANES), lambda c: (0, rev(c), 0, 0)),
            pl.BlockSpec(sel_b.shape, lambda c: (0, 0)),
        ],
        out_specs=pl.BlockSpec((bsz, cps, SSD_STATE, SSD_INNER), lambda c: (0, rev(c), 0, 0)),
        out_shape=jax.ShapeDtypeStruct((bsz, nc, SSD_STATE, SSD_INNER), BF16),
        scratch_shapes=[pltpu.VMEM((bsz, SSD_STATE, SSD_INNER), F32)],
        compiler_params=_cparams(("arbitrary",)),
        name="ssd_bwd_states",
    )(conv_out, conv_out, wst, etot, sel_b)

    wide_spec = pl.BlockSpec((1, SSD_INNER), lambda b, c: (0, 0))
    tok_spec = pl.BlockSpec((None, cps * t, LANES), lambda b, c: (b, c, 0))
    return pl.pallas_call(
        _ssd_main_kernel,
        grid=(bsz, nc // cps),
        in_specs=[
            pl.BlockSpec((None, cps * t, SSD_INNER), lambda b, c: (b, c, 0)),
            pl.BlockSpec((None, cps * t, gn), lambda b, c: (b, c, b_blk)),
            pl.BlockSpec((None, cps * t, gn), lambda b, c: (b, c, c_blk)),
            tok_spec,
            pl.BlockSpec((None, cps, LANES, t), lambda b, c: (b, c, 0, 0)),
            tok_spec, tok_spec,
            pl.BlockSpec((None, cps, sub8, LANES), lambda b, c: (b, c, 0, 0)),
            pl.BlockSpec((None, cps, SSD_STATE, SSD_INNER), lambda b, c: (b, c, 0, 0)),
            wide_spec, sel_spec, sel_spec,
        ],
        out_specs=pl.BlockSpec((None, cps * t, SSD_INNER), lambda b, c: (b, c, 0)),
        out_shape=jax.ShapeDtypeStruct((bsz, seq, SSD_INNER), F32),
        scratch_shapes=[pltpu.VMEM((SSD_STATE, SSD_INNER), F32)],
        compiler_params=_cparams(("parallel", "arbitrary")),
        name="ssd_main",
    )(conv_out, conv_out, conv_out, ccol, rt, esc, wst, etot, gin, dskip, by_group(sel_f), by_group(sel_b))


def _dft_tables(seq):
    n1, n2 = seq // F_N2, F_N2
    c = np.arange(F_GD)
    ang = 2 * np.pi * np.outer(c, c) / F_GD
    chan = np.concatenate([np.cos(ang), -np.sin(ang)], axis=1) / math.sqrt(F_GD)
    k1 = np.arange(n1)
    a1 = 2 * np.pi * np.outer(k1, k1) / n1
    stage_a = np.block([[np.cos(a1), np.sin(a1)], [-np.sin(a1), np.cos(a1)]])
    stage_a = np.kron(stage_a, np.eye(2))
    s2 = np.arange(n2)
    at = 2 * np.pi * np.outer(s2, k1) / seq
    at = at.reshape(n2 // 2, 2, n1).transpose(0, 2, 1).reshape(n2 // 2, 2 * n1)
    tw_c, tw_s = np.cos(at)[..., None], np.sin(at)[..., None]
    a2 = 2 * np.pi * np.outer(s2, s2) / n2
    stage_b = np.concatenate([np.cos(a2), np.sin(a2)], axis=1)
    return (jnp.asarray(chan, BF16), jnp.asarray(stage_a, BF16), jnp.asarray(tw_c, F32),
            jnp.asarray(tw_s, F32), jnp.asarray(stage_b, BF16))


def _fft_a_kernel(z_ref, ma_ref, twc_ref, tws_ref, o_ref, zs, ys, *, n1, rb):
    ct = z_ref.shape[-1]
    half = rb // 2
    _put_rows(zs, pltpu.bitcast(z_ref[...].reshape(2 * n1 * rb, ct), jnp.uint32))
    for j in range(half):
        zz = pltpu.bitcast(_get_rows(zs, j, 2 * n1, half), BF16)
        y = jnp.dot(ma_ref[...], zz, preferred_element_type=F32)
        yr, yi = y[:2 * n1], y[2 * n1:]
        tc, ts = twc_ref[j], tws_ref[j]
        out = jnp.concatenate([yr * tc + yi * ts, yi * tc - yr * ts], axis=0).astype(BF16)
        _put_rows(ys, pltpu.bitcast(out, jnp.uint32), j, half)
    o_ref[...] = pltpu.bitcast(_get_rows(ys), BF16).reshape(2, n1, rb, ct)


def _fft_b_kernel(y_ref, mb_ref, o_ref, os_ref, *, n2, kb, scale):
    ct = y_ref.shape[-1]
    for kk in range(kb):
        yy = y_ref[:, kk].reshape(2 * n2, ct)
        _put_rows(os_ref, jnp.dot(mb_ref[...], yy, preferred_element_type=F32) * scale, kk, kb)
    o_ref[...] = _get_rows(os_ref).reshape(n2, kb, ct).astype(o_ref.dtype)


def _fourier_seq(z2, tabs, bsz, seq, *, rb=BF16_ROWS, kb=BF16_ROWS, ct=768):
    _, ma, twc, tws, mb = tabs
    n1, n2 = seq // F_N2, F_N2
    w = z2.shape[-1]
    nct = w // ct
    za = z2.reshape(2, bsz, n1, n2, w)
    ab_spec = pl.BlockSpec((2, None, n1, rb, ct), lambda b, a, j: (0, b, 0, a, j))
    ya = pl.pallas_call(
        functools.partial(_fft_a_kernel, n1=n1, rb=rb),
        grid=(bsz, n2 // rb, nct),
        in_specs=[
            ab_spec,
            pl.BlockSpec((4 * n1, 4 * n1), lambda b, a, j: (0, 0)),
            pl.BlockSpec((rb // 2, 2 * n1, 1), lambda b, a, j: (a, 0, 0)),
            pl.BlockSpec((rb // 2, 2 * n1, 1), lambda b, a, j: (a, 0, 0)),
        ],
        out_specs=ab_spec,
        out_shape=jax.ShapeDtypeStruct((2, bsz, n1, n2, w), BF16),
        scratch_shapes=[pltpu.VMEM((ct // LANES, n1 * rb, LANES), jnp.uint32)] * 2,
        compiler_params=_cparams(("parallel", "parallel", "parallel")),
        name="fourier_stage_a",
    )(za, ma, twc, tws)
    out = pl.pallas_call(
        functools.partial(_fft_b_kernel, n2=n2, kb=kb, scale=1.0 / math.sqrt(seq)),
        grid=(bsz, n1 // kb, nct),
        in_specs=[
            pl.BlockSpec((2, None, kb, n2, ct), lambda b, k, j: (0, b, k, 0, j)),
            pl.BlockSpec((n2, 2 * n2), lambda b, k, j: (0, 0)),
        ],
        out_specs=pl.BlockSpec((None, n2, kb, ct), lambda b, k, j: (b, 0, k, j)),
        out_shape=jax.ShapeDtypeStruct((bsz, n2, n1, w), BF16),
        scratch_shapes=[pltpu.VMEM((ct // LANES, n2 * kb, LANES), F32)],
        compiler_params=_cparams(("parallel", "parallel", "parallel")),
        name="fourier_stage_b",
    )(ya, mb)
    return out.reshape(bsz, seq, w)


def _merge_kernel(*refs, dils):
    ng = len(dils)
    x_ref = refs[0]
    o_refs = refs[1:1 + ng]
    l_refs = refs[1 + ng:1 + 2 * ng]
    y_ref, z_ref, yg_ref, f_ref, gt_ref, wa_ref, ws_ref, wf_ref, wo_ref, out_ref, o_sc, l_sc = refs[1 + 2 * ng:]
    tm, d = x_ref.shape

    yf = jnp.dot(f_ref[...], wf_ref[...], preferred_element_type=F32)

    ys = None
    for g in range(SSD_GROUPS):
        gs = slice(g * SSD_GW, (g + 1) * SSD_GW)
        yy = y_ref[:, gs] * _silu(z_ref[:, gs].astype(F32))
        part = jnp.dot(_rms(yy, yg_ref[:, gs]).astype(BF16), ws_ref[gs, :], preferred_element_type=F32)
        ys = part if ys is None else ys + part

    for gi, dil in enumerate(dils):
        rows = tm // dil
        idx = lambda r: slice(None) if dil == 1 else pl.ds(r, rows, stride=dil)
        for r in range(dil):
            l_sc[gi, idx(r), :] = l_refs[gi][:, r * LANES:(r + 1) * LANES]
            for hh in range(ATTN_HPG):
                c0 = r * ATTN_OUT + hh * ATTN_HD
                o_sc[gi * ATTN_HPG + hh, idx(r), :] = o_refs[gi][:, c0:c0 + ATTN_HD].astype(F32)

    ls = [l_sc[gi] for gi in range(ng)]
    mx = functools.reduce(jnp.maximum, ls)
    ws = [jnp.exp(v - mx) for v in ls]
    inv = 1.0 / functools.reduce(lambda a, b: a + b, ws)
    heads = []
    for hh in range(ATTN_HPG):
        acc = None
        for gi in range(ng):
            alpha = (ws[gi] * inv)[:, hh:hh + 1]
            term = alpha * o_sc[gi * ATTN_HPG + hh]
            acc = term if acc is None else acc + term
        heads.append(acc.astype(BF16))
    o_attn = jnp.concatenate(heads, axis=1)

    ya = jnp.dot(o_attn, wa_ref[...], preferred_element_type=F32)
    gates = gt_ref[...].astype(F32)
    merged = gates[:, :d] * ya + gates[:, d:2 * d] * ys + gates[:, 2 * d:] * yf
    out_ref[...] = x_ref[...] + jnp.dot(merged.astype(BF16), wo_ref[...], preferred_element_type=F32)


def _merge(x, os_, ls_, dils, y, z, y_gain, f, gates, wa, ws, wf, wo, bsz, seq, *, tm=512):
    d = x.shape[-1]
    assert all((tm // dil) % BF16_ROWS == 0 for dil in dils)
    tok = lambda width: pl.BlockSpec((None, tm, width), lambda b, i: (b, i, 0))
    full = lambda a: pl.BlockSpec(a.shape, lambda b, i: (0, 0), pipeline_mode=pl.Buffered(1))
    ng = len(dils)
    in_specs = [tok(d)]
    in_specs += [pl.BlockSpec((None, tm // dil, dil * ATTN_OUT), lambda b, i: (b, i, 0)) for dil in dils]
    in_specs += [pl.BlockSpec((None, tm // dil, dil * LANES), lambda b, i: (b, i, 0)) for dil in dils]
    in_specs += [tok(y.shape[-1]), tok(z.shape[-1]), full(y_gain), tok(f.shape[-1]), tok(gates.shape[-1]),
                 full(wa), full(ws), full(wf), full(wo)]
    return pl.pallas_call(
        functools.partial(_merge_kernel, dils=tuple(dils)),
        grid=(bsz, seq // tm),
        in_specs=in_specs,
        out_specs=tok(d),
        out_shape=jax.ShapeDtypeStruct((bsz, seq, d), F32),
        scratch_shapes=[pltpu.VMEM((ng * ATTN_HPG, tm, ATTN_HD), F32), pltpu.VMEM((ng, tm, LANES), F32)],
        compiler_params=_cparams(("parallel", "parallel")),
        name="branch_merge",
    )(x, *os_, *ls_, y, z, y_gain, f, gates, wa, ws, wf, wo)


def _xattn_kernel(x_ref, g_ref, wq_ref, qg_ref, k_ref, v_ref, wo_ref, out_ref):
    tm, d = x_ref.shape
    hd = d // MEM_HEADS
    scale = hd ** -0.5
    rc = min(tm, XATTN_ROW_CHUNK)
    for c in range(tm // rc):
        rs = slice(c * rc, (c + 1) * rc)
        x = x_ref[rs, :]
        h = _rms(x, g_ref[...]).astype(BF16)
        q = jnp.dot(h, wq_ref[...], preferred_element_type=F32)
        heads = []
        for hh in range(MEM_HEADS):
            sl = slice(hh * hd, (hh + 1) * hd)
            qn = _rms(q[:, sl], qg_ref[...]).astype(BF16)
            s = lax.dot_general(qn, k_ref[:, sl], (((1,), (1,)), ((), ())), preferred_element_type=F32) * scale
            m = jnp.max(s, axis=-1, keepdims=True)
            p = jnp.exp(s - m)
            l = jnp.sum(p, axis=-1, keepdims=True)
            heads.append((jnp.dot(p.astype(BF16), v_ref[:, sl], preferred_element_type=F32) / l).astype(BF16))
        o = jnp.concatenate(heads, axis=1)
        out_ref[rs, :] = x + jnp.dot(o, wo_ref[...], preferred_element_type=F32)


def _xattn(x, g, wq, qg, k, v, wo, *, tm=512):
    bsz, seq, d = x.shape
    mt = k.shape[1]
    full = lambda a: pl.BlockSpec(a.shape, lambda b, i: (0, 0))
    g = g.reshape(1, d)
    qg = qg.reshape(1, -1)
    return pl.pallas_call(
        _xattn_kernel,
        grid=(bsz, seq // tm),
        in_specs=[
            pl.BlockSpec((None, tm, d), lambda b, i: (b, i, 0)),
            full(g), full(wq), full(qg),
            pl.BlockSpec((None, mt, d), lambda b, i: (b, 0, 0)),
            pl.BlockSpec((None, mt, d), lambda b, i: (b, 0, 0)),
            full(wo),
        ],
        out_specs=pl.BlockSpec((None, tm, d), lambda b, i: (b, i, 0)),
        out_shape=jax.ShapeDtypeStruct((bsz, seq, d), F32),
        compiler_params=_cparams(("parallel", "parallel")),
        name="mem_xattn",
    )(x, g, wq, qg, k, v, wo)


def _ffn_kernel(x_ref, g_ref, wg_ref, wu_ref, wd_ref, out_ref, *, tf):
    x = x_ref[...]
    h = _rms(x, g_ref[...]).astype(BF16)
    acc = x
    for c in range(wg_ref.shape[1] // tf):
        sl = slice(c * tf, (c + 1) * tf)
        gt = jnp.dot(h, wg_ref[:, sl], preferred_element_type=F32)
        up = jnp.dot(h, wu_ref[:, sl], preferred_element_type=F32)
        a = (_silu(gt) * up).astype(BF16)
        acc = acc + jnp.dot(a, wd_ref[sl, :], preferred_element_type=F32)
    out_ref[...] = acc


def _ffn(x, g, wg, wu, wd, *, tm=256):
    m, d = x.shape
    dff = wg.shape[1]
    tf = dff // 2 if (dff // 2) % LANES == 0 else dff
    full = lambda a: pl.BlockSpec(a.shape, lambda i: (0, 0))
    g = g.reshape(1, d)
    return pl.pallas_call(
        functools.partial(_ffn_kernel, tf=tf),
        grid=(m // tm,),
        in_specs=[pl.BlockSpec((tm, d), lambda i: (i, 0)), full(g), full(wg), full(wu), full(wd)],
        out_specs=pl.BlockSpec((tm, d), lambda i: (i, 0)),
        out_shape=jax.ShapeDtypeStruct((m, d), F32),
        compiler_params=_cparams(("parallel",)),
        name="swiglu_ffn",
    )(x, g, wg, wu, wd)


def _pad_lanes(a):
    a = a.reshape(-1, 2 * SSD_HEADS)
    return jnp.pad(a, ((0, 0), (0, LANES - 2 * SSD_HEADS)))


def kernel(x, mem, rel_bias, mix_norm_g, w_in, gate_bias, attn_q_norm_g, attn_k_norm_g, conv_w, conv_b, dt_bias, a_log, d_skip, ssd_norm_g, w_branch_attn, w_branch_ssd, w_branch_fourier, w_mix_out, xattn_norm_g, mem_norm_g, w_xq, w_xk, w_xv, xattn_q_norm_g, xattn_k_norm_g, w_xo, ffn_norm_g, w_ffn_gate, w_ffn_up, w_ffn_down):
    bsz, seq, d = x.shape
    depth = w_in.shape[0]
    m = bsz * seq
    xf = x.reshape(m, d)
    memf = mem.reshape(bsz * mem.shape[1], d)
    tabs = _dft_tables(seq)
    dils = [dil for _, dil in ATTN_GROUPS]
    biases = [_attn_bias(rel_bias, gi, dil) for gi, dil in enumerate(dils)]
    offs = np.cumsum([0, ATTN_WIDTH, ATTN_WIDTH, ATTN_WIDTH, SSD_INNER, SSD_CONV_CH, 2 * SSD_HEADS, F_WIDTH, 3 * d])
    bf = lambda a: a.astype(BF16)

    for l in range(depth):
        wl = bf(w_in[l])
        seg = lambda i: wl[:, offs[i]:offs[i + 1]]
        h = _rmsnorm(xf, mix_norm_g[l])
        gains = jnp.stack([attn_q_norm_g[l], attn_k_norm_g[l], jnp.ones_like(attn_q_norm_g[l])]).reshape(3, 1, ATTN_HD)
        os_, ls_ = [], []
        for gi, dil in enumerate(dils):
            qkv = _qkv_group(h, wl, gains, gi, dil, bsz, seq)
            o_g, lse_g = _attn_group(qkv, biases[gi], dil, bsz, seq)
            os_.append(o_g)
            ls_.append(lse_g)

        z = _matmul(h, wl, out_dtype=BF16, tn=512, tm=2048, col0=int(offs[3]), n=SSD_INNER)
        dt_raw = _matmul(h, _pad_lanes(seg(5)), out_dtype=F32, tn=LANES, tm=2048)
        z2 = _matmul(h, seg(6), out_dtype=BF16, tn=768, tm=2048, epilogue="chandft", extra=tabs[0])
        gates = _matmul(h, seg(7), out_dtype=BF16, tn=1024, tm=2048, epilogue="sigmoid_bias", extra=gate_bias[l])

        conv_out = _proj_conv_silu(h, wl, conv_w[l], conv_b[l], seq, col0=int(offs[4])).reshape(bsz, seq, SSD_CONV_CH)
        y_ssd = _ssd(conv_out, dt_raw.reshape(bsz, seq, -1), _pad_lanes(dt_bias[l]), _pad_lanes(a_log[l]),
                     jnp.repeat(d_skip[l], SSD_HD).reshape(1, SSD_INNER), bsz, seq)

        f_re = _fourier_seq(z2, tabs, bsz, seq)

        x3 = _merge(xf.reshape(bsz, seq, d), os_, ls_, dils, y_ssd, z.reshape(bsz, seq, SSD_INNER),
                    ssd_norm_g[l].reshape(1, SSD_INNER), f_re, gates.reshape(bsz, seq, 3 * d),
                    bf(w_branch_attn[l]), bf(w_branch_ssd[l]), bf(w_branch_fourier[l]), bf(w_mix_out[l]), bsz, seq)

        hd_m = d // MEM_HEADS
        hm = _rmsnorm(memf, mem_norm_g[l])
        km = _matmul(hm, bf(w_xk[l]), out_dtype=BF16, tn=512, epilogue="headnorm",
                     extra=xattn_k_norm_g[l], head_dim=hd_m)
        vm = _matmul(hm, bf(w_xv[l]), out_dtype=BF16, tn=512)
        x3 = _xattn(x3, xattn_norm_g[l], bf(w_xq[l]), xattn_q_norm_g[l], km.reshape(bsz, -1, d),
                    vm.reshape(bsz, -1, d), bf(w_xo[l]))

        xf = _ffn(x3.reshape(m, d), ffn_norm_g[l], bf(w_ffn_gate[l]), bf(w_ffn_up[l]), bf(w_ffn_down[l]))

    return xf.reshape(bsz, seq, d)
```

```python
import functools
import math

import numpy as np
import jax
import jax.numpy as jnp
from jax import lax
from jax.experimental import pallas as pl
from jax.experimental.pallas import tpu as pltpu

F32 = jnp.float32
BF16 = jnp.bfloat16

NORM_EPS = 1e-6
MASK_VALUE = -1e30

ATTN_GROUPS = ((128, 1), (512, 4), (2048, 16))
ATTN_HPG = 4
ATTN_HD = 128
ATTN_HEADS = ATTN_HPG * len(ATTN_GROUPS)
ATTN_WIDTH = ATTN_HEADS * ATTN_HD
ATTN_OUT = ATTN_HPG * ATTN_HD
ATTN_HALF = 64
ATTN_TQ = 2 * ATTN_HALF
ATTN_TK = ATTN_TQ + 2 * ATTN_HALF
NUM_BUCKETS = 32
MAX_DISTANCE = 1024

SSD_HEADS = 32
SSD_HD = 64
SSD_GROUPS = 4
SSD_HPG = SSD_HEADS // SSD_GROUPS
SSD_STATE = 128
SSD_INNER = SSD_HEADS * SSD_HD
SSD_GW = SSD_INNER // SSD_GROUPS
SSD_CONV = 7
SSD_CHUNK = 128
SSD_CONV_CH = SSD_INNER + 2 * SSD_GROUPS * SSD_STATE
LANES = 128
BF16_ROWS = 16

F_GROUPS = 6
F_GD = 256
F_WIDTH = F_GROUPS * F_GD
F_N2 = 128

MEM_HEADS = 4
XATTN_ROW_CHUNK = 256

VMEM_LIMIT = 56 * 1024 * 1024


def _cparams(sem):
    return pltpu.CompilerParams(dimension_semantics=sem, vmem_limit_bytes=VMEM_LIMIT)


def _silu(x):
    return x * jax.nn.sigmoid(x)


def _rms(x, g):
    ms = jnp.mean(x * x, axis=-1, keepdims=True)
    return x * lax.rsqrt(ms + NORM_EPS) * g


def _get_rows(ref3, start=0, size=None, stride=1):
    idx = slice(None) if size is None else pl.ds(start, size, stride=stride)
    return jnp.concatenate([ref3[c, idx, :] for c in range(ref3.shape[0])], axis=1)


def _put_rows(ref3, val, start=0, stride=1):
    size = val.shape[0]
    idx = slice(None) if (stride == 1 and size == ref3.shape[1]) else pl.ds(start, size, stride=stride)
    for c in range(ref3.shape[0]):
        ref3[c, idx, :] = val[:, c * LANES:(c + 1) * LANES]


def _rmsnorm_kernel(x_ref, g_ref, o_ref):
    o_ref[...] = _rms(x_ref[...], g_ref[...]).astype(o_ref.dtype)


def _rmsnorm(x, g, *, tm=1024):
    m, k = x.shape
    tm = min(tm, m)
    return pl.pallas_call(
        _rmsnorm_kernel,
        grid=(m // tm,),
        in_specs=[pl.BlockSpec((tm, k), lambda i: (i, 0)), pl.BlockSpec((1, k), lambda i: (0, 0))],
        out_specs=pl.BlockSpec((tm, k), lambda i: (i, 0)),
        out_shape=jax.ShapeDtypeStruct((m, k), BF16),
        compiler_params=_cparams(("parallel",)),
        name="rmsnorm",
    )(x, g.reshape(1, k))


MM_ROW_CHUNK = 512


def _mm_kernel(*refs, epilogue, head_dim):
    h_ref, w_ref = refs[:2]
    o_ref = refs[-1]
    tm = h_ref.shape[0]
    rc = min(MM_ROW_CHUNK, tm)
    w = w_ref[...].astype(BF16)
    for c in range(tm // rc):
        rs = slice(c * rc, (c + 1) * rc)
        acc = jnp.dot(h_ref[rs, :], w, preferred_element_type=F32)
        tn = acc.shape[1]
        if epilogue == "headnorm":
            hg_ref = refs[2]
            for hh in range(tn // head_dim):
                sl = slice(hh * head_dim, (hh + 1) * head_dim)
                o_ref[rs, sl] = _rms(acc[:, sl], hg_ref[...]).astype(o_ref.dtype)
        elif epilogue == "sigmoid_bias":
            b_ref = refs[2]
            o_ref[rs, :] = jax.nn.sigmoid(acc + b_ref[...]).astype(o_ref.dtype)
        elif epilogue == "chandft":
            dft_ref = refs[2]
            a16 = acc.astype(BF16)
            for gg in range(tn // F_GD):
                sl = slice(gg * F_GD, (gg + 1) * F_GD)
                r = jnp.dot(a16[:, sl], dft_ref[...], preferred_element_type=F32)
                o_ref[0, rs, sl] = r[:, :F_GD].astype(o_ref.dtype)
                o_ref[1, rs, sl] = r[:, F_GD:].astype(o_ref.dtype)
        else:
            o_ref[rs, :] = acc.astype(o_ref.dtype)


def _w_spec(w, k, tn, layer, col_block):
    if w.ndim == 2:
        return pl.BlockSpec((k, tn), lambda i, j: (0, col_block(j)))
    return pl.BlockSpec((None, k, tn), lambda i, j: (layer, 0, col_block(j)))


def _matmul(h, w, *, out_dtype, tn, tm=1024, epilogue="plain", extra=None, head_dim=None, col0=0, n=None,
            layer=None):
    m, k = h.shape
    n = w.shape[-1] if n is None else n
    tm = min(tm, m)
    assert m % tm == 0 and n % tn == 0 and col0 % tn == 0
    cb0 = col0 // tn
    in_specs = [pl.BlockSpec((tm, k), lambda i, j: (i, 0)), _w_spec(w, k, tn, layer, lambda j: cb0 + j)]
    args = [h, w]
    if epilogue == "headnorm":
        in_specs.append(pl.BlockSpec((1, head_dim), lambda i, j: (0, 0)))
        args.append(extra.reshape(1, head_dim))
    elif epilogue == "sigmoid_bias":
        in_specs.append(pl.BlockSpec((1, tn), lambda i, j: (0, j)))
        args.append(extra.reshape(1, n))
    elif epilogue == "chandft":
        in_specs.append(pl.BlockSpec(extra.shape, lambda i, j: (0, 0)))
        args.append(extra)
    if epilogue == "chandft":
        out_shape = jax.ShapeDtypeStruct((2, m, n), out_dtype)
        out_spec = pl.BlockSpec((2, tm, tn), lambda i, j: (0, i, j))
    else:
        out_shape = jax.ShapeDtypeStruct((m, n), out_dtype)
        out_spec = pl.BlockSpec((tm, tn), lambda i, j: (i, j))
    return pl.pallas_call(
        functools.partial(_mm_kernel, epilogue=epilogue, head_dim=head_dim),
        grid=(m // tm, n // tn),
        in_specs=in_specs,
        out_specs=out_spec,
        out_shape=out_shape,
        compiler_params=_cparams(("parallel", "parallel")),
        name="matmul_" + epilogue,
    )(*args)


def _qkv_kernel(h_ref, w_ref, hg_ref, o_ref, *scratch, dil):
    j = pl.program_id(1)
    tm = h_ref.shape[0]
    width = w_ref.shape[1]
    nh = width // ATTN_HD
    rc = min(MM_ROW_CHUNK, tm)
    rows = rc // dil
    w = w_ref[...].astype(BF16)
    for c in range(tm // rc):
        acc = jnp.dot(h_ref[c * rc:(c + 1) * rc, :], w, preferred_element_type=F32)
        if dil > 1:
            scr = scratch[0]
            for hh in range(nh):
                scr[hh, c * rc:(c + 1) * rc, :] = acc[:, hh * ATTN_HD:(hh + 1) * ATTN_HD]
        for r in range(dil):
            for hh in range(nh):
                if dil == 1:
                    ph = acc[:, hh * ATTN_HD:(hh + 1) * ATTN_HD]
                else:
                    ph = scr[hh, pl.ds(c * rc + r, rows, stride=dil), :]
                val = jnp.where(j < 2, _rms(ph, hg_ref[...]), ph)
                c0 = r * width + hh * ATTN_HD
                o_ref[c * rows:(c + 1) * rows, c0:c0 + ATTN_HD] = val.astype(o_ref.dtype)


def _qkv_group(h, w, layer, gains, gi, dil, bsz, seq, *, tm=2048):
    m, k = h.shape
    sub = seq // dil
    per_b = seq // tm
    rows = tm // dil
    assert seq % tm == 0 and rows % BF16_ROWS == 0
    return pl.pallas_call(
        functools.partial(_qkv_kernel, dil=dil),
        grid=(m // tm, 3),
        in_specs=[
            pl.BlockSpec((tm, k), lambda i, j: (i, 0)),
            _w_spec(w, k, ATTN_OUT, layer, lambda j: j * len(ATTN_GROUPS) + gi),
            pl.BlockSpec((None, 1, ATTN_HD), lambda i, j: (j, 0, 0)),
        ],
        out_specs=pl.BlockSpec((None, None, rows, dil * ATTN_OUT), lambda i, j: (j, i // per_b, i % per_b, 0)),
        out_shape=jax.ShapeDtypeStruct((3, bsz, sub, dil * ATTN_OUT), BF16),
        scratch_shapes=[pltpu.VMEM((ATTN_HPG, tm, ATTN_HD), F32)] if dil > 1 else [],
        compiler_params=_cparams(("parallel", "parallel")),
        name=f"qkv_proj_d{dil}",
    )(h, w, gains)


def _t5_bucket_np(rel):
    half_b = NUM_BUCKETS // 2
    exact = half_b // 2
    dist = np.abs(rel)
    log_ratio = np.log(np.maximum(dist, 1) / exact) / np.log(MAX_DISTANCE / exact)
    far = np.minimum(exact + (log_ratio * (half_b - exact)).astype(np.int32), half_b - 1)
    return np.where(rel > 0, half_b, 0) + np.where(dist < exact, dist, far)


def _attn_bias(rel_bias, gi, dil):
    i = np.arange(ATTN_TQ)[:, None]
    j = np.arange(ATTN_TK)[None, :]
    rel = j - ATTN_HALF - i
    idx = _t5_bucket_np(rel * dil)
    onehot = jnp.asarray(np.eye(NUM_BUCKETS, dtype=np.float32)[idx])
    tab = rel_bias[:, gi * ATTN_HPG:(gi + 1) * ATTN_HPG].astype(F32)
    b = jnp.einsum("qkn,nh->hqk", onehot, tab, precision=lax.Precision.HIGHEST)
    band = np.broadcast_to(np.abs(rel) <= ATTN_HALF, (ATTN_TQ, ATTN_TK))
    keep = np.stack([band & ((j >= ATTN_HALF) | (not first)) & ((j < ATTN_TK - ATTN_HALF) | (not last))
                     for last in (False, True) for first in (False, True)])
    return jnp.where(jnp.asarray(keep)[:, None], b[None], MASK_VALUE)


def _attn_kernel(q_ref, kp_ref, kc_ref, kn_ref, vp_ref, vc_ref, vn_ref, bias_ref, o_ref, lse_ref, kpad, vpad,
                 *, tile, sub):
    i = pl.program_id(2)
    hf = ATTN_HALF
    kpad[0:hf, :] = kp_ref[...]
    kpad[hf:hf + tile, :] = kc_ref[...]
    kpad[hf + tile:, :] = kn_ref[...]
    vpad[0:hf, :] = vp_ref[...]
    vpad[hf:hf + tile, :] = vc_ref[...]
    vpad[hf + tile:, :] = vn_ref[...]

    scale = ATTN_HD ** -0.5
    lane = lax.broadcasted_iota(jnp.int32, (ATTN_TQ, LANES), 1)
    tiles = tile // ATTN_TQ

    def body(t, carry):
        s0 = pl.multiple_of(t * ATTN_TQ, ATTN_TQ)
        gt = i * tiles + t
        variant = (gt == 0).astype(jnp.int32) + 2 * (gt == sub // ATTN_TQ - 1).astype(jnp.int32)
        lse_tile = jnp.zeros((ATTN_TQ, LANES), F32)
        for hh in range(ATTN_HPG):
            sl = slice(hh * ATTN_HD, (hh + 1) * ATTN_HD)
            q = q_ref[pl.ds(s0, ATTN_TQ), sl]
            kw = kpad[pl.ds(s0, ATTN_TK), sl]
            vw = vpad[pl.ds(s0, ATTN_TK), sl]
            s = (lax.dot_general(q, kw, (((1,), (1,)), ((), ())), preferred_element_type=F32) * scale
                 + bias_ref[variant, hh])
            m = jnp.max(s, axis=-1, keepdims=True)
            p = jnp.exp(s - m)
            l = jnp.sum(p, axis=-1, keepdims=True)
            o = jnp.dot(p.astype(BF16), vw, preferred_element_type=F32) / l
            o_ref[pl.ds(s0, ATTN_TQ), sl] = o.astype(o_ref.dtype)
            lse_tile = jnp.where(lane == hh, m + jnp.log(l), lse_tile)
        lse_ref[pl.ds(s0, ATTN_TQ), :] = lse_tile
        return carry

    lax.fori_loop(0, tile // ATTN_TQ, body, 0, unroll=4)


def _attn_group(qkv, bias, dil, bsz, seq, *, tile=1024):
    sub = seq // dil
    tile = min(tile, sub)
    nblk = sub // tile
    hb = tile // ATTN_HALF
    last_h = sub // ATTN_HALF - 1
    assert sub % tile == 0 and tile % ATTN_TQ == 0
    cur = lambda which: pl.BlockSpec((None, None, tile, ATTN_OUT), lambda b, r, i: (which, b, i, r))
    prev = lambda which: pl.BlockSpec((None, None, ATTN_HALF, ATTN_OUT),
                                      lambda b, r, i: (which, b, jnp.maximum(i * hb - 1, 0), r))
    nxt = lambda which: pl.BlockSpec((None, None, ATTN_HALF, ATTN_OUT),
                                     lambda b, r, i: (which, b, jnp.minimum((i + 1) * hb, last_h), r))
    return pl.pallas_call(
        functools.partial(_attn_kernel, tile=tile, sub=sub),
        grid=(bsz, dil, nblk),
        in_specs=[cur(0), prev(1), cur(1), nxt(1), prev(2), cur(2), nxt(2),
                  pl.BlockSpec(bias.shape, lambda b, r, i: (0, 0, 0, 0))],
        out_specs=(pl.BlockSpec((None, tile, ATTN_OUT), lambda b, r, i: (b, i, r)),
                   pl.BlockSpec((None, tile, LANES), lambda b, r, i: (b, i, r))),
        out_shape=(jax.ShapeDtypeStruct((bsz, sub, dil * ATTN_OUT), BF16),
                   jax.ShapeDtypeStruct((bsz, sub, dil * LANES), F32)),
        scratch_shapes=[pltpu.VMEM((tile + 2 * ATTN_HALF, ATTN_OUT), BF16)] * 2,
        compiler_params=_cparams(("parallel", "parallel", "parallel")),
        name=f"dilated_attn_d{dil}",
    )(qkv, qkv, qkv, qkv, qkv, qkv, qkv, bias)


CONV_HALO = BF16_ROWS
CONV_RC = 128
CONV_APRON = 8


SUBLANES = 8


def _shift_rows(x, s):
    r, c = x.shape
    x3 = x.reshape(r // SUBLANES, SUBLANES, c)
    rot = pltpu.roll(x3, (-s) % SUBLANES, 1)
    sub = lax.broadcasted_iota(jnp.int32, x3.shape, 1)
    if s > 0:
        y = jnp.where(sub < SUBLANES - s, rot, jnp.concatenate([rot[1:], rot[:1]], axis=0))
    else:
        y = jnp.where(sub >= -s, rot, jnp.concatenate([rot[-1:], rot[:-1]], axis=0))
    return y.reshape(r, c)


def _proj_conv_kernel(hp_ref, hc_ref, hn_ref, w_ref, cw_ref, cb_ref, o_ref, *, tm, per_b):
    i = pl.program_id(0)
    first = (i % per_b) == 0
    last = (i % per_b) == per_b - 1
    w = w_ref[...].astype(BF16)
    halo_p =jnp.dot(hp_ref[...], w, preferred_element_type=F32)
    halo_n = jnp.dot(hn_ref[...], w, preferred_element_type=F32)
    halo_p = jnp.where(first, 0.0, halo_p)
    halo_n = jnp.where(last, 0.0, halo_n)
    pad = SSD_CONV // 2
    acc = {}

    def project(s):
        acc[s] = jnp.dot(hc_ref[s * MM_ROW_CHUNK:(s + 1) * MM_ROW_CHUNK, :], w, preferred_element_type=F32)

    def tile_rows(lo, hi):
        pieces, r = [], lo
        while r < hi:
            if r < 0:
                src, off, end = halo_p, CONV_HALO + r, min(hi, 0)
            elif r >= tm:
                src, off, end = halo_n, r - tm, hi
            else:
                s = r // MM_ROW_CHUNK
                src, off, end = acc[s], r - s * MM_ROW_CHUNK, min(hi, (s + 1) * MM_ROW_CHUNK)
            pieces.append(src[off:off + end - r])
            r = end
        return pieces[0] if len(pieces) == 1 else jnp.concatenate(pieces, axis=0)

    def conv_chunk(c):
        x = tile_rows(c * CONV_RC - CONV_APRON, (c + 1) * CONV_RC + CONV_APRON)
        tap = lambda off: cw_ref[pad + off:pad + off + 1, :]
        xm, xq = _shift_rows(x, -3), _shift_rows(x, 3)
        f0 = tap(-3) * xm + tap(0) * x + tap(3) * xq + cb_ref[...]
        f1 = tap(-2) * xm + tap(1) * x
        fm = tap(-1) * x + tap(2) * xq
        y = f0 + _shift_rows(f1, 1) + _shift_rows(fm, -1)
        o_ref[c * CONV_RC:(c + 1) * CONV_RC, :] = _silu(y[CONV_APRON:CONV_APRON + CONV_RC]).astype(o_ref.dtype)

    nsub = tm // MM_ROW_CHUNK
    per_sub = MM_ROW_CHUNK // CONV_RC
    project(0)
    for s in range(nsub):
        if s + 1 < nsub:
            project(s + 1)
        for c in range(s * per_sub, (s + 1) * per_sub):
            conv_chunk(c)


def _proj_conv_silu(h, w, layer, cw, cb, seq, *, col0, tm=1024, tn=512):
    m, k = h.shape
    n = cw.shape[1]
    assert col0 % tn == 0
    cb0 = col0 // tn
    per_b = seq // tm
    hb = tm // CONV_HALO
    assert seq % tm == 0 and n % tn == 0 and SSD_CONV == 7
    return pl.pallas_call(
        functools.partial(_proj_conv_kernel, tm=tm, per_b=per_b),
        grid=(m // tm, n // tn),
        in_specs=[
            pl.BlockSpec((CONV_HALO, k), lambda i, j: (jnp.maximum(i * hb - 1, 0), 0)),
            pl.BlockSpec((tm, k), lambda i, j: (i, 0)),
            pl.BlockSpec((CONV_HALO, k), lambda i, j: (jnp.minimum((i + 1) * hb, m // CONV_HALO - 1), 0)),
            _w_spec(w, k, tn, layer, lambda j: cb0 + j),
            pl.BlockSpec((SSD_CONV, tn), lambda i, j: (0, j)),
            pl.BlockSpec((1, tn), lambda i, j: (0, j)),
        ],
        out_specs=pl.BlockSpec((tm, tn), lambda i, j: (i, j)),
        out_shape=jax.ShapeDtypeStruct((m, n), BF16),
        compiler_params=_cparams(("parallel", "parallel")),
        name="proj_conv_silu",
    )(h, h, h, w, cw, cb.reshape(1, n))


def _softplus(x):
    return jnp.maximum(x, 0.0) + jnp.log1p(jnp.exp(-jnp.abs(x)))


def _ssd_prep_kernel(dtr_ref, dtb_ref, alog_ref, ccol_ref, rt_ref, esc_ref, wst_ref, etot_ref, *, t):
    half = LANES // 2
    log2e = math.log2(math.e)
    lane = lax.broadcasted_iota(jnp.int32, (t, LANES), 1)
    fwd = lane < SSD_HEADS
    tri = (lax.broadcasted_iota(jnp.int32, (t, t), 1) <= lax.broadcasted_iota(jnp.int32, (t, t), 0)).astype(BF16)
    neg_a = jnp.exp(alog_ref[...])
    for ci in range(dtr_ref.shape[0] // t):
        rs = slice(ci * t, (ci + 1) * t)
        dt = _softplus(dtr_ref[rs, :] + dtb_ref[...])
        la = jnp.where(lane < 2 * SSD_HEADS, -(dt * neg_a), 0.0)
        hi = la.astype(BF16)
        r1 = la - hi.astype(F32)
        mid = r1.astype(BF16)
        lo = (r1 - mid.astype(F32)).astype(BF16)
        packed = (hi.astype(F32) + pltpu.roll(mid.astype(F32), half, 1)).astype(BF16)
        res = jnp.dot(tri, jnp.concatenate([packed, lo], axis=1), preferred_element_type=F32)
        a0 = res[:, :LANES]
        acs = a0 + pltpu.roll(a0, half, 1) + res[:, LANES:]
        exb = acs - la
        ldt = jnp.log(dt)
        tot = acs[t - 1:t, :]
        ccol_ref[rs, :] = jnp.where(fwd, acs, exb) * log2e
        rt_ref[ci] = (jnp.where(fwd, acs - ldt, exb + ldt) * log2e).T
        esc_ref[rs, :] = jnp.exp(jnp.where(fwd, acs, tot - exb))
        wst_ref[rs, :] = jnp.exp(jnp.where(fwd, tot - acs, exb)) * dt
        etot_ref[ci] = jnp.broadcast_to(jnp.exp(tot), etot_ref.shape[1:])


def _split2(v):
    hi = v.astype(BF16)
    return jnp.concatenate([hi, (v - hi.astype(F32)).astype(BF16)], axis=1)


def _head_expand_tables():
    j = np.arange(2 * LANES)[:, None] % LANES
    c = np.arange(SSD_INNER)[None, :]
    fwd = (j == c // SSD_HD)
    bwd = (j == SSD_HEADS + c // SSD_HD)
    return jnp.asarray(fwd, BF16), jnp.asarray(bwd, BF16)


def _ssd_bwd_state_kernel(xs_ref, b_ref, wst_ref, etot_ref, selb_ref, gin_ref, g_sc):
    @pl.when(pl.program_id(0) == 0)
    def _():
        g_sc[...] = jnp.zeros(g_sc.shape, F32)

    cps = etot_ref.shape[1]
    t = wst_ref.shape[1] // cps
    for sc, b in [(sc, b) for sc in reversed(range(cps)) for b in range(xs_ref.shape[0])]:
        rows = slice(sc * t, (sc + 1) * t)
        gin_ref[b, sc] = g_sc[b].astype(gin_ref.dtype)
        both = jnp.concatenate([_split2(wst_ref[b, rows, :]), _split2(etot_ref[b, sc])], axis=0)
        spread = jnp.dot(both, selb_ref[...], preferred_element_type=F32)
        xw = (xs_ref[b, rows, :].astype(F32) * spread[:t]).astype(BF16)
        dec = spread[t:t + 1]
        for g in range(SSD_GROUPS):
            gs = slice(g * SSD_GW, (g + 1) * SSD_GW)
            bm_t = b_ref[b, rows, g * SSD_STATE:(g + 1) * SSD_STATE].T
            st = jnp.dot(bm_t, xw[:, gs], preferred_element_type=F32)
            g_sc[b, :, gs] = g_sc[b, :, gs] * dec[:, gs] + st


def _ssd_main_kernel(xs_ref, b_ref, c_ref, ccol_ref, rt_ref, esc_ref, wst_ref, etot_ref, gin_ref,
                     dsk_ref, self_ref, selb_ref, y_ref, h_sc):
    @pl.when(pl.program_id(1) == 0)
    def _():
        h_sc[...] = jnp.zeros(h_sc.shape, F32)

    t = rt_ref.shape[-1]
    li = lax.broadcasted_iota(jnp.int32, (t, t), 0)
    si = lax.broadcasted_iota(jnp.int32, (t, t), 1)
    mask_f = jnp.where(li >= si, 0.0, MASK_VALUE)
    mask_b = jnp.where(si >= li, 0.0, MASK_VALUE)
    lo_half = lax.broadcasted_iota(jnp.int32, (t, LANES), 1) < SSD_HD

    for sc, g in [(sc, g) for sc in range(rt_ref.shape[0]) for g in range(SSD_GROUPS)]:
        rows = slice(sc * t, (sc + 1) * t)
        ccol = ccol_ref[rows, :]
        r_t = rt_ref[sc]
        esc2 = _split2(esc_ref[rows, :])
        wst2 = _split2(wst_ref[rows, :])
        etot2 = _split2(etot_ref[sc])
        gs = slice(g * SSD_GW, (g + 1) * SSD_GW)
        bm = b_ref[rows, g * SSD_STATE:(g + 1) * SSD_STATE]
        cm = c_ref[rows, g * SSD_STATE:(g + 1) * SSD_STATE]
        cb = lax.dot_general(cm, bm, (((1,), (1,)), ((), ())), preferred_element_type=F32)
        hf = h_sc[:, gs]
        yf_all = jnp.dot(cm, hf.astype(BF16), preferred_element_type=F32)
        yb_all = jnp.dot(cm, gin_ref[sc, :, gs], preferred_element_type=F32)
        e_in = jnp.dot(esc2, self_ref[g], preferred_element_type=F32)
        e_out = jnp.dot(esc2, selb_ref[g], preferred_element_type=F32)
        w_state = jnp.dot(wst2, self_ref[g], preferred_element_type=F32)
        dec = jnp.dot(etot2, self_ref[g], preferred_element_type=F32)[0:1]
        xw = []
        for k in range(SSD_HPG // 2):
            ea = g * SSD_HPG + 2 * k
            sl = slice(g * SSD_GW + k * LANES, g * SSD_GW + (k + 1) * LANES)
            ks = slice(k * LANES, (k + 1) * LANES)
            xp = xs_ref[rows, sl]
            mms = []
            for e in (ea, ea + 1):
                eb = SSD_HEADS + e
                w = (jnp.exp2(ccol[:, e:e + 1] - r_t[e:e + 1, :] + mask_f)
                     + jnp.exp2(r_t[eb:eb + 1, :] - ccol[:, eb:eb + 1] + mask_b))
                mms.append((cb * w).astype(BF16))
            zero = jnp.zeros_like(xp)
            xx = jnp.concatenate([jnp.where(lo_half, xp, zero), jnp.where(lo_half, zero, xp)], axis=0)
            ypair = jnp.dot(jnp.concatenate(mms, axis=1), xx, preferred_element_type=F32)
            xf = xp.astype(F32)
            y_ref[rows, sl] = (ypair + yf_all[:, ks] * e_in[:, ks] + yb_all[:, ks] * e_out[:, ks]
                               + xf * dsk_ref[:, sl])
            xw.append((xf * w_state[:, ks]).astype(BF16))
        st = jnp.dot(bm.T, jnp.concatenate(xw, axis=1), preferred_element_type=F32)
        h_sc[:, gs] = hf * dec + st


def _ssd(conv_out, dt_raw, dtb, alog, dskip, bsz, seq):
    t = SSD_CHUNK
    nc = seq // t
    assert seq % t == 0
    gn = SSD_GROUPS * SSD_STATE
    b_blk = SSD_INNER // gn
    c_blk = b_blk + 1
    cpp = min(8, nc)
    assert nc % cpp == 0
    sub8 = 8
    const_spec = pl.BlockSpec((1, LANES), lambda b, i: (0, 0))
    rows_spec = pl.BlockSpec((None, cpp * t, LANES), lambda b, i: (b, i, 0))
    row_arr = jax.ShapeDtypeStruct((bsz, seq, LANES), F32)
    ccol, rt, esc, wst, etot = pl.pallas_call(
        functools.partial(_ssd_prep_kernel, t=t),
        grid=(bsz, nc // cpp),
        in_specs=[rows_spec, const_spec, const_spec],
        out_specs=(rows_spec, pl.BlockSpec((None, cpp, LANES, t), lambda b, i: (b, i, 0, 0)), rows_spec, rows_spec,
                   pl.BlockSpec((None, cpp, sub8, LANES), lambda b, i: (b, i, 0, 0))),
        out_shape=(row_arr, jax.ShapeDtypeStruct((bsz, nc, LANES, t), F32), row_arr, row_arr,
                   jax.ShapeDtypeStruct((bsz, nc, sub8, LANES), F32)),
        compiler_params=_cparams(("parallel", "parallel")),
        name="ssd_decay_terms",
    )(dt_raw, dtb, alog)

    cps = next(n for n in (4, 2, 1) if nc % n == 0)
    rev = lambda c: nc // cps - 1 - c
    sel_f, sel_b = _head_expand_tables()
    by_group = lambda a: a.reshape(a.shape[0], SSD_GROUPS, SSD_GW).transpose(1, 0, 2)
    sel_spec = pl.BlockSpec((SSD_GROUPS, sel_f.shape[0], SSD_GW), lambda b, c: (0, 0, 0))
    gin = pl.pallas_call(
        _ssd_bwd_state_kernel,
        grid=(nc // cps,),
        in_specs=[
            pl.BlockSpec((bsz, cps * t, SSD_INNER), lambda c: (0, rev(c), 0)),
            pl.BlockSpec((bsz, cps * t, gn), lambda c: (0, rev(c), b_blk)),
            pl.BlockSpec((bsz, cps * t, LANES), lambda c: (0, rev(c), 0)),
            pl.BlockSpec((bsz, cps, sub8, LANES), lambda c: (0, rev(c), 0, 0)),
            pl.BlockSpec(sel_b.shape, lambda c: (0, 0)),
        ],
        out_specs=pl.BlockSpec((bsz, cps, SSD_STATE, SSD_INNER), lambda c: (0, rev(c), 0, 0)),
        out_shape=jax.ShapeDtypeStruct((bsz, nc, SSD_STATE, SSD_INNER), BF16),
        scratch_shapes=[pltpu.VMEM((bsz, SSD_STATE, SSD_INNER), F32)],
        compiler_params=_cparams(("arbitrary",)),
        name="ssd_bwd_states",
    )(conv_out, conv_out, wst, etot, sel_b)

    wide_spec = pl.BlockSpec((1, SSD_INNER), lambda b, c: (0, 0))
    tok_spec = pl.BlockSpec((None, cps * t, LANES), lambda b, c: (b, c, 0))
    return pl.pallas_call(
        _ssd_main_kernel,
        grid=(bsz, nc // cps),
        in_specs=[
            pl.BlockSpec((None, cps * t, SSD_INNER), lambda b, c: (b, c, 0)),
            pl.BlockSpec((None, cps * t, gn), lambda b, c: (b, c, b_blk)),
            pl.BlockSpec((None, cps * t, gn), lambda b, c: (b, c, c_blk)),
            tok_spec,
            pl.BlockSpec((None, cps, LANES, t), lambda b, c: (b, c, 0, 0)),
            tok_spec, tok_spec,
            pl.BlockSpec((None, cps, sub8, LANES), lambda b, c: (b, c, 0, 0)),
            pl.BlockSpec((None, cps, SSD_STATE, SSD_INNER), lambda b, c: (b, c, 0, 0)),
            wide_spec, sel_spec, sel_spec,
        ],
        out_specs=pl.BlockSpec((None, cps * t, SSD_INNER), lambda b, c: (b, c, 0)),
        out_shape=jax.ShapeDtypeStruct((bsz, seq, SSD_INNER), F32),
        scratch_shapes=[pltpu.VMEM((SSD_STATE, SSD_INNER), F32)],
        compiler_params=_cparams(("parallel", "arbitrary")),
        name="ssd_main",
    )(conv_out, conv_out, conv_out, ccol, rt, esc, wst, etot, gin, dskip, by_group(sel_f), by_group(sel_b))


def _dft_tables(seq):
    n1, n2 = seq // F_N2, F_N2
    c = np.arange(F_GD)
    ang = 2 * np.pi * np.outer(c, c) / F_GD
    chan = np.concatenate([np.cos(ang), -np.sin(ang)], axis=1) / math.sqrt(F_GD)
    k1 = np.arange(n1)
    a1 = 2 * np.pi * np.outer(k1, k1) / n1
    stage_a = np.block([[np.cos(a1), np.sin(a1)], [-np.sin(a1), np.cos(a1)]])
    stage_a = np.kron(stage_a, np.eye(2))
    s2 = np.arange(n2)
    at = 2 * np.pi * np.outer(s2, k1) / seq
    at = at.reshape(n2 // 2, 2, n1).transpose(0, 2, 1).reshape(n2 // 2, 2 * n1)
    tw_c, tw_s = np.cos(at)[..., None], np.sin(at)[..., None]
    a2 = 2 * np.pi * np.outer(s2, s2) / n2
    stage_b = np.concatenate([np.cos(a2), np.sin(a2)], axis=1)
    return (jnp.asarray(chan, BF16), jnp.asarray(stage_a, BF16), jnp.asarray(tw_c, F32),
            jnp.asarray(tw_s, F32), jnp.asarray(stage_b, BF16))


def _fft_a_kernel(z_ref, ma_ref, twc_ref, tws_ref, o_ref, zs, ys, *, n1, rb):
    ct = z_ref.shape[-1]
    half = rb // 2
    _put_rows(zs, pltpu.bitcast(z_ref[...].reshape(2 * n1 * rb, ct), jnp.uint32))
    for j in range(half):
        zz = pltpu.bitcast(_get_rows(zs, j, 2 * n1, half), BF16)
        y = jnp.dot(ma_ref[...], zz, preferred_element_type=F32)
        yr, yi = y[:2 * n1], y[2 * n1:]
        tc, ts = twc_ref[j], tws_ref[j]
        out = jnp.concatenate([yr * tc + yi * ts, yi * tc - yr * ts], axis=0).astype(BF16)
        _put_rows(ys, pltpu.bitcast(out, jnp.uint32), j, half)
    o_ref[...] = pltpu.bitcast(_get_rows(ys), BF16).reshape(2, n1, rb, ct)


def _fft_b_kernel(y_ref, mb_ref, o_ref, os_ref, *, n2, kb, scale):
    ct = y_ref.shape[-1]
    for kk in range(kb):
        yy = y_ref[:, kk].reshape(2 * n2, ct)
        _put_rows(os_ref, jnp.dot(mb_ref[...], yy, preferred_element_type=F32) * scale, kk, kb)
    o_ref[...] = _get_rows(os_ref).reshape(n2, kb, ct).astype(o_ref.dtype)


def _fourier_seq(z2, tabs, bsz, seq, *, rb=BF16_ROWS, kb=BF16_ROWS, ct=768):
    _, ma, twc, tws, mb = tabs
    n1, n2 = seq // F_N2, F_N2
    w = z2.shape[-1]
    nct = w // ct
    za = z2.reshape(2, bsz, n1, n2, w)
    ab_spec = pl.BlockSpec((2, None, n1, rb, ct), lambda b, a, j: (0, b, 0, a, j))
    ya = pl.pallas_call(
        functools.partial(_fft_a_kernel, n1=n1, rb=rb),
        grid=(bsz, n2 // rb, nct),
        in_specs=[
            ab_spec,
            pl.BlockSpec((4 * n1, 4 * n1), lambda b, a, j: (0, 0)),
            pl.BlockSpec((rb // 2, 2 * n1, 1), lambda b, a, j: (a, 0, 0)),
            pl.BlockSpec((rb // 2, 2 * n1, 1), lambda b, a, j: (a, 0, 0)),
        ],
        out_specs=ab_spec,
        out_shape=jax.ShapeDtypeStruct((2, bsz, n1, n2, w), BF16),
        scratch_shapes=[pltpu.VMEM((ct // LANES, n1 * rb, LANES), jnp.uint32)] * 2,
        compiler_params=_cparams(("parallel", "parallel", "parallel")),
        name="fourier_stage_a",
    )(za, ma, twc, tws)
    out = pl.pallas_call(
        functools.partial(_fft_b_kernel, n2=n2, kb=kb, scale=1.0 / math.sqrt(seq)),
        grid=(bsz, n1 // kb, nct),
        in_specs=[
            pl.BlockSpec((2, None, kb, n2, ct), lambda b, k, j: (0, b, k, 0, j)),
            pl.BlockSpec((n2, 2 * n2), lambda b, k, j: (0, 0)),
        ],
        out_specs=pl.BlockSpec((None, n2, kb, ct), lambda b, k, j: (b, 0, k, j)),
        out_shape=jax.ShapeDtypeStruct((bsz, n2, n1, w), BF16),
        scratch_shapes=[pltpu.VMEM((ct // LANES, n2 * kb, LANES), F32)],
        compiler_params=_cparams(("parallel", "parallel", "parallel")),
        name="fourier_stage_b",
    )(ya, mb)
    return out.reshape(bsz, seq, w)


def _merge_kernel(*refs, dils):
    ng = len(dils)
    x_ref = refs[0]
    o_refs = refs[1:1 + ng]
    l_refs = refs[1 + ng:1 + 2 * ng]
    y_ref, z_ref, yg_ref, f_ref, gt_ref, wa_ref, ws_ref, wf_ref, wo_ref, out_ref, o_sc, l_sc = refs[1 + 2 * ng:]
    tm, d = x_ref.shape

    yf = jnp.dot(f_ref[...], wf_ref[...], preferred_element_type=F32)

    ys = None
    for g in range(SSD_GROUPS):
        gs = slice(g * SSD_GW, (g + 1) * SSD_GW)
        yy = y_ref[:, gs] * _silu(z_ref[:, gs].astype(F32))
        part = jnp.dot(_rms(yy, yg_ref[:, gs]).astype(BF16), ws_ref[gs, :], preferred_element_type=F32)
        ys = part if ys is None else ys + part

    for gi, dil in enumerate(dils):
        rows = tm // dil
        idx = lambda r: slice(None) if dil == 1 else pl.ds(r, rows, stride=dil)
        for r in range(dil):
            l_sc[gi, idx(r), :] = l_refs[gi][:, r * LANES:(r + 1) * LANES]
            for hh in range(ATTN_HPG):
                c0 = r * ATTN_OUT + hh * ATTN_HD
                o_sc[gi * ATTN_HPG + hh, idx(r), :] = o_refs[gi][:, c0:c0 + ATTN_HD].astype(F32)

    ls = [l_sc[gi] for gi in range(ng)]
    mx = functools.reduce(jnp.maximum, ls)
    ws = [jnp.exp(v - mx) for v in ls]
    inv = 1.0 / functools.reduce(lambda a, b: a + b, ws)
    heads = []
    for hh in range(ATTN_HPG):
        acc = None
        for gi in range(ng):
            alpha = (ws[gi] * inv)[:, hh:hh + 1]
            term = alpha * o_sc[gi * ATTN_HPG + hh]
            acc = term if acc is None else acc + term
        heads.append(acc.astype(BF16))
    o_attn = jnp.concatenate(heads, axis=1)

    ya = jnp.dot(o_attn, wa_ref[...], preferred_element_type=F32)
    gates = gt_ref[...].astype(F32)
    merged = gates[:, :d] * ya + gates[:, d:2 * d] * ys + gates[:, 2 * d:] * yf
    out_ref[...] = x_ref[...] + jnp.dot(merged.astype(BF16), wo_ref[...], preferred_element_type=F32)


def _merge(x, os_, ls_, dils, y, z, y_gain, f, gates, wa, ws, wf, wo, bsz, seq, *, tm=512):
    d = x.shape[-1]
    assert all((tm // dil) % BF16_ROWS == 0 for dil in dils)
    tok = lambda width: pl.BlockSpec((None, tm, width), lambda b, i: (b, i, 0))
    full = lambda a: pl.BlockSpec(a.shape, lambda b, i: (0, 0), pipeline_mode=pl.Buffered(1))
    ng = len(dils)
    in_specs = [tok(d)]
    in_specs += [pl.BlockSpec((None, tm // dil, dil * ATTN_OUT), lambda b, i: (b, i, 0)) for dil in dils]
    in_specs += [pl.BlockSpec((None, tm // dil, dil * LANES), lambda b, i: (b, i, 0)) for dil in dils]
    in_specs += [tok(y.shape[-1]), tok(z.shape[-1]), full(y_gain), tok(f.shape[-1]), tok(gates.shape[-1]),
                 full(wa), full(ws), full(wf), full(wo)]
    return pl.pallas_call(
        functools.partial(_merge_kernel, dils=tuple(dils)),
        grid=(bsz, seq // tm),
        in_specs=in_specs,
        out_specs=tok(d),
        out_shape=jax.ShapeDtypeStruct((bsz, seq, d), F32),
        scratch_shapes=[pltpu.VMEM((ng * ATTN_HPG, tm, ATTN_HD), F32), pltpu.VMEM((ng, tm, LANES), F32)],
        compiler_params=_cparams(("parallel", "parallel")),
        name="branch_merge",
    )(x, *os_, *ls_, y, z, y_gain, f, gates, wa, ws, wf, wo)


def _xattn_kernel(x_ref, g_ref, wq_ref, qg_ref, k_ref, v_ref, wo_ref, out_ref):
    tm, d = x_ref.shape
    hd = d // MEM_HEADS
    scale = hd ** -0.5
    rc = min(tm, XATTN_ROW_CHUNK)
    for c in range(tm // rc):
        rs = slice(c * rc, (c + 1) * rc)
        x = x_ref[rs, :]
        h = _rms(x, g_ref[...]).astype(BF16)
        q = jnp.dot(h, wq_ref[...], preferred_element_type=F32)
        heads = []
        for hh in range(MEM_HEADS):
            sl = slice(hh * hd, (hh + 1) * hd)
            qn = _rms(q[:, sl], qg_ref[...]).astype(BF16)
            s = lax.dot_general(qn, k_ref[:, sl], (((1,), (1,)), ((), ())), preferred_element_type=F32) * scale
            m = jnp.max(s, axis=-1, keepdims=True)
            p = jnp.exp(s - m)
            l = jnp.sum(p, axis=-1, keepdims=True)
            heads.append((jnp.dot(p.astype(BF16), v_ref[:, sl], preferred_element_type=F32) / l).astype(BF16))
        o = jnp.concatenate(heads, axis=1)
        out_ref[rs, :] = x + jnp.dot(o, wo_ref[...], preferred_element_type=F32)


def _xattn(x, g, wq, qg, k, v, wo, *, tm=512):
    bsz, seq, d = x.shape
    mt = k.shape[1]
    full = lambda a: pl.BlockSpec(a.shape, lambda b, i: (0, 0))
    g = g.reshape(1, d)
    qg = qg.reshape(1, -1)
    return pl.pallas_call(
        _xattn_kernel,
        grid=(bsz, seq // tm),
        in_specs=[
            pl.BlockSpec((None, tm, d), lambda b, i: (b, i, 0)),
            full(g), full(wq), full(qg),
            pl.BlockSpec((None, mt, d), lambda b, i: (b, 0, 0)),
            pl.BlockSpec((None, mt, d), lambda b, i: (b, 0, 0)),
            full(wo),
        ],
        out_specs=pl.BlockSpec((None, tm, d), lambda b, i: (b, i, 0)),
        out_shape=jax.ShapeDtypeStruct((bsz, seq, d), F32),
        compiler_params=_cparams(("parallel", "parallel")),
        name="mem_xattn",
    )(x, g, wq, qg, k, v, wo)


def _ffn_kernel(x_ref, g_ref, wg_ref, wu_ref, wd_ref, out_ref, *, tf):
    x = x_ref[...]
    h = _rms(x, g_ref[...]).astype(BF16)
    acc = x
    for c in range(wg_ref.shape[1] // tf):
        sl = slice(c * tf, (c + 1) * tf)
        gt = jnp.dot(h, wg_ref[:, sl], preferred_element_type=F32)
        up = jnp.dot(h, wu_ref[:, sl], preferred_element_type=F32)
        a = (_silu(gt) * up).astype(BF16)
        acc = acc + jnp.dot(a, wd_ref[sl, :], preferred_element_type=F32)
    out_ref[...] = acc


def _ffn(x, g, wg, wu, wd, *, tm=512):
    m, d = x.shape
    dff = wg.shape[1]
    tf = dff // 2 if (dff // 2) % LANES == 0 else dff
    full = lambda a: pl.BlockSpec(a.shape, lambda i: (0, 0), pipeline_mode=pl.Buffered(1))
    g = g.reshape(1, d)
    return pl.pallas_call(
        functools.partial(_ffn_kernel, tf=tf),
        grid=(m // tm,),
        in_specs=[pl.BlockSpec((tm, d), lambda i: (i, 0)), full(g), full(wg), full(wu), full(wd)],
        out_specs=pl.BlockSpec((tm, d), lambda i: (i, 0)),
        out_shape=jax.ShapeDtypeStruct((m, d), F32),
        compiler_params=_cparams(("parallel",)),
        name="swiglu_ffn",
    )(x, g, wg, wu, wd)


def _pad_lanes(a):
    a = a.reshape(-1, 2 * SSD_HEADS)
    return jnp.pad(a, ((0, 0), (0, LANES - 2 * SSD_HEADS)))


def kernel(x, mem, rel_bias, mix_norm_g, w_in, gate_bias, attn_q_norm_g, attn_k_norm_g, conv_w, conv_b, dt_bias, a_log, d_skip, ssd_norm_g, w_branch_attn, w_branch_ssd, w_branch_fourier, w_mix_out, xattn_norm_g, mem_norm_g, w_xq, w_xk, w_xv, xattn_q_norm_g, xattn_k_norm_g, w_xo, ffn_norm_g, w_ffn_gate, w_ffn_up, w_ffn_down):
    bsz, seq, d = x.shape
    depth = w_in.shape[0]
    m = bsz * seq
    xf = x.reshape(m, d)
    memf = mem.reshape(bsz * mem.shape[1], d)
    tabs = _dft_tables(seq)
    dils = [dil for _, dil in ATTN_GROUPS]
    biases = [_attn_bias(rel_bias, gi, dil) for gi, dil in enumerate(dils)]
    offs = np.cumsum([0, ATTN_WIDTH, ATTN_WIDTH, ATTN_WIDTH, SSD_INNER, SSD_CONV_CH, 2 * SSD_HEADS, F_WIDTH, 3 * d])
    bf = lambda a: a.astype(BF16)

    for l in range(depth):
        seg = lambda i: bf(w_in[l, :, offs[i]:offs[i + 1]])
        h = _rmsnorm(xf, mix_norm_g[l])
        gains = jnp.stack([attn_q_norm_g[l], attn_k_norm_g[l], jnp.ones_like(attn_q_norm_g[l])]).reshape(3, 1, ATTN_HD)
        os_, ls_ = [], []
        for gi, dil in enumerate(dils):
            qkv = _qkv_group(h, w_in, l, gains, gi, dil, bsz, seq)
            o_g, lse_g = _attn_group(qkv, biases[gi], dil, bsz, seq)
            os_.append(o_g)
            ls_.append(lse_g)

        z = _matmul(h, w_in, out_dtype=BF16, tn=512, tm=2048, col0=int(offs[3]), n=SSD_INNER, layer=l)
        dt_raw = _matmul(h, _pad_lanes(seg(5)), out_dtype=F32, tn=LANES, tm=2048)
        z2 = _matmul(h, seg(6), out_dtype=BF16, tn=768, tm=2048, epilogue="chandft", extra=tabs[0])
        gates = _matmul(h, seg(7), out_dtype=BF16, tn=1024, tm=2048, epilogue="sigmoid_bias", extra=gate_bias[l])

        conv_out = _proj_conv_silu(h, w_in, l, conv_w[l], conv_b[l], seq,
                                   col0=int(offs[4])).reshape(bsz, seq, SSD_CONV_CH)
        y_ssd = _ssd(conv_out, dt_raw.reshape(bsz, seq, -1), _pad_lanes(dt_bias[l]), _pad_lanes(a_log[l]),
                     jnp.repeat(d_skip[l], SSD_HD).reshape(1, SSD_INNER), bsz, seq)

        f_re = _fourier_seq(z2, tabs, bsz, seq)

        x3 = _merge(xf.reshape(bsz, seq, d), os_, ls_, dils, y_ssd, z.reshape(bsz, seq, SSD_INNER),
                    ssd_norm_g[l].reshape(1, SSD_INNER), f_re, gates.reshape(bsz, seq, 3 * d),
                    bf(w_branch_attn[l]), bf(w_branch_ssd[l]), bf(w_branch_fourier[l]), bf(w_mix_out[l]), bsz, seq)

        hd_m = d // MEM_HEADS
        hm = _rmsnorm(memf, mem_norm_g[l])
        km = _matmul(hm, bf(w_xk[l]), out_dtype=BF16, tn=512, epilogue="headnorm",
                     extra=xattn_k_norm_g[l], head_dim=hd_m)
        vm = _matmul(hm, bf(w_xv[l]), out_dtype=BF16, tn=512)
        x3 = _xattn(x3, xattn_norm_g[l], bf(w_xq[l]), xattn_q_norm_g[l], km.reshape(bsz, -1, d),
                    vm.reshape(bsz, -1, d), bf(w_xo[l]))

        xf = _ffn(x3.reshape(m, d), ffn_norm_g[l], bf(w_ffn_gate[l]), bf(w_ffn_up[l]), bf(w_ffn_down[l]))

    return xf.reshape(bsz, seq, d)
```

```python
import functools
import math

import numpy as np
import jax
import jax.numpy as jnp
from jax import lax
from jax.experimental import pallas as pl
from jax.experimental.pallas import tpu as pltpu

F32 = jnp.float32
BF16 = jnp.bfloat16

NORM_EPS = 1e-6
MASK_VALUE = -1e30

ATTN_GROUPS = ((128, 1), (512, 4), (2048, 16))
ATTN_HPG = 4
ATTN_HD = 128
ATTN_HEADS = ATTN_HPG * len(ATTN_GROUPS)
ATTN_WIDTH = ATTN_HEADS * ATTN_HD
ATTN_OUT = ATTN_HPG * ATTN_HD
ATTN_HALF = 64
ATTN_TQ = 2 * ATTN_HALF
ATTN_TK = ATTN_TQ + 2 * ATTN_HALF
NUM_BUCKETS = 32
MAX_DISTANCE = 1024

SSD_HEADS = 32
SSD_HD = 64
SSD_GROUPS = 4
SSD_HPG = SSD_HEADS // SSD_GROUPS
SSD_STATE = 128
SSD_INNER = SSD_HEADS * SSD_HD
SSD_GW = SSD_INNER // SSD_GROUPS
SSD_CONV = 7
SSD_CHUNK = 128
SSD_CONV_CH = SSD_INNER + 2 * SSD_GROUPS * SSD_STATE
LANES = 128
BF16_ROWS = 16

F_GROUPS = 6
F_GD = 256
F_WIDTH = F_GROUPS * F_GD
F_N2 = 128

MEM_HEADS = 4
XATTN_ROW_CHUNK = 256

VMEM_LIMIT = 56 * 1024 * 1024


def _cparams(sem):
    return pltpu.CompilerParams(dimension_semantics=sem, vmem_limit_bytes=VMEM_LIMIT)


def _silu(x):
    return x * jax.nn.sigmoid(x)


def _rms(x, g):
    ms = jnp.mean(x * x, axis=-1, keepdims=True)
    return x * lax.rsqrt(ms + NORM_EPS) * g


def _get_rows(ref3, start=0, size=None, stride=1):
    idx = slice(None) if size is None else pl.ds(start, size, stride=stride)
    return jnp.concatenate([ref3[c, idx, :] for c in range(ref3.shape[0])], axis=1)


def _put_rows(ref3, val, start=0, stride=1):
    size = val.shape[0]
    idx = slice(None) if (stride == 1 and size == ref3.shape[1]) else pl.ds(start, size, stride=stride)
    for c in range(ref3.shape[0]):
        ref3[c, idx, :] = val[:, c * LANES:(c + 1) * LANES]


def _rmsnorm_kernel(x_ref, g_ref, o_ref):
    o_ref[...] = _rms(x_ref[...], g_ref[...]).astype(o_ref.dtype)


def _rmsnorm(x, g, *, tm=1024):
    m, k = x.shape
    tm = min(tm, m)
    return pl.pallas_call(
        _rmsnorm_kernel,
        grid=(m // tm,),
        in_specs=[pl.BlockSpec((tm, k), lambda i: (i, 0)), pl.BlockSpec((1, k), lambda i: (0, 0))],
        out_specs=pl.BlockSpec((tm, k), lambda i: (i, 0)),
        out_shape=jax.ShapeDtypeStruct((m, k), BF16),
        compiler_params=_cparams(("parallel",)),
        name="rmsnorm",
    )(x, g.reshape(1, k))


MM_ROW_CHUNK = 512


def _mm_kernel(*refs, epilogue, head_dim):
    h_ref, w_ref = refs[:2]
    o_ref = refs[-1]
    tm = h_ref.shape[0]
    rc = min(MM_ROW_CHUNK, tm)
    for c in range(tm // rc):
        rs = slice(c * rc, (c + 1) * rc)
        acc = jnp.dot(h_ref[rs, :], w_ref[...], preferred_element_type=F32)
        tn = acc.shape[1]
        if epilogue == "headnorm":
            hg_ref = refs[2]
            for hh in range(tn // head_dim):
                sl = slice(hh * head_dim, (hh + 1) * head_dim)
                o_ref[rs, sl] = _rms(acc[:, sl], hg_ref[...]).astype(o_ref.dtype)
        elif epilogue == "sigmoid_bias":
            b_ref = refs[2]
            o_ref[rs, :] = jax.nn.sigmoid(acc + b_ref[...]).astype(o_ref.dtype)
        elif epilogue == "chandft":
            dft_ref = refs[2]
            a16 = acc.astype(BF16)
            for gg in range(tn // F_GD):
                sl = slice(gg * F_GD, (gg + 1) * F_GD)
                r = jnp.dot(a16[:, sl], dft_ref[...], preferred_element_type=F32)
                o_ref[0, rs, sl] = r[:, :F_GD].astype(o_ref.dtype)
                o_ref[1, rs, sl] = r[:, F_GD:].astype(o_ref.dtype)
        else:
            o_ref[rs, :] = acc.astype(o_ref.dtype)


def _matmul(h, w, *, out_dtype, tn, tm=1024, epilogue="plain", extra=None, head_dim=None, col0=0, n=None):
    m, k = h.shape
    n = w.shape[1] if n is None else n
    tm = min(tm, m)
    assert m % tm == 0 and n % tn == 0 and col0 % tn == 0
    cb0 = col0 // tn
    in_specs = [pl.BlockSpec((tm, k), lambda i, j: (i, 0)), pl.BlockSpec((k, tn), lambda i, j: (0, cb0 + j))]
    args = [h, w]
    if epilogue == "headnorm":
        in_specs.append(pl.BlockSpec((1, head_dim), lambda i, j: (0, 0)))
        args.append(extra.reshape(1, head_dim))
    elif epilogue == "sigmoid_bias":
        in_specs.append(pl.BlockSpec((1, tn), lambda i, j: (0, j)))
        args.append(extra.reshape(1, n))
    elif epilogue == "chandft":
        in_specs.append(pl.BlockSpec(extra.shape, lambda i, j: (0, 0)))
        args.append(extra)
    if epilogue == "chandft":
        out_shape = jax.ShapeDtypeStruct((2, m, n), out_dtype)
        out_spec = pl.BlockSpec((2, tm, tn), lambda i, j: (0, i, j))
    else:
        out_shape = jax.ShapeDtypeStruct((m, n), out_dtype)
        out_spec = pl.BlockSpec((tm, tn), lambda i, j: (i, j))
    return pl.pallas_call(
        functools.partial(_mm_kernel, epilogue=epilogue, head_dim=head_dim),
        grid=(m // tm, n // tn),
        in_specs=in_specs,
        out_specs=out_spec,
        out_shape=out_shape,
        compiler_params=_cparams(("parallel", "parallel")),
        name="matmul_" + epilogue,
    )(*args)


def _qkv_kernel(h_ref, w_ref, hg_ref, o_ref, *scratch, dil):
    j = pl.program_id(1)
    tm = h_ref.shape[0]
    width = w_ref.shape[1]
    nh = width // ATTN_HD
    rc = min(MM_ROW_CHUNK, tm)
    rows = rc // dil
    for c in range(tm // rc):
        acc = jnp.dot(h_ref[c * rc:(c + 1) * rc, :], w_ref[...], preferred_element_type=F32)
        if dil > 1:
            scr = scratch[0]
            for hh in range(nh):
                scr[hh, c * rc:(c + 1) * rc, :] = acc[:, hh * ATTN_HD:(hh + 1) * ATTN_HD]
        for r in range(dil):
            for hh in range(nh):
                if dil == 1:
                    ph = acc[:, hh * ATTN_HD:(hh + 1) * ATTN_HD]
                else:
                    ph = scr[hh, pl.ds(c * rc + r, rows, stride=dil), :]
                val = jnp.where(j < 2, _rms(ph, hg_ref[...]), ph)
                c0 = r * width + hh * ATTN_HD
                o_ref[c * rows:(c + 1) * rows, c0:c0 + ATTN_HD] = val.astype(o_ref.dtype)


def _qkv_group(h, w, gains, gi, dil, bsz, seq, *, tm=2048):
    m, k = h.shape
    sub = seq // dil
    per_b = seq // tm
    rows = tm // dil
    assert seq % tm == 0 and rows % BF16_ROWS == 0
    return pl.pallas_call(
        functools.partial(_qkv_kernel, dil=dil),
        grid=(m // tm, 3),
        in_specs=[
            pl.BlockSpec((tm, k), lambda i, j: (i, 0)),
            pl.BlockSpec((k, ATTN_OUT), lambda i, j: (0, j * len(ATTN_GROUPS) + gi)),
            pl.BlockSpec((None, 1, ATTN_HD), lambda i, j: (j, 0, 0)),
        ],
        out_specs=pl.BlockSpec((None, None, rows, dil * ATTN_OUT), lambda i, j: (j, i // per_b, i % per_b, 0)),
        out_shape=jax.ShapeDtypeStruct((3, bsz, sub, dil * ATTN_OUT), BF16),
        scratch_shapes=[pltpu.VMEM((ATTN_HPG, tm, ATTN_HD), F32)] if dil > 1 else [],
        compiler_params=_cparams(("parallel", "parallel")),
        name=f"qkv_proj_d{dil}",
    )(h, w, gains)


def _t5_bucket_np(rel):
    half_b = NUM_BUCKETS // 2
    exact = half_b // 2
    dist = np.abs(rel)
    log_ratio = np.log(np.maximum(dist, 1) / exact) / np.log(MAX_DISTANCE / exact)
    far = np.minimum(exact + (log_ratio * (half_b - exact)).astype(np.int32), half_b - 1)
    return np.where(rel > 0, half_b, 0) + np.where(dist < exact, dist, far)


def _attn_bias(rel_bias, gi, dil):
    i = np.arange(ATTN_TQ)[:, None]
    j = np.arange(ATTN_TK)[None, :]
    rel = j - ATTN_HALF - i
    idx = _t5_bucket_np(rel * dil)
    onehot = jnp.asarray(np.eye(NUM_BUCKETS, dtype=np.float32)[idx])
    tab = rel_bias[:, gi * ATTN_HPG:(gi + 1) * ATTN_HPG].astype(F32)
    b = jnp.einsum("qkn,nh->hqk", onehot, tab, precision=lax.Precision.HIGHEST)
    band = np.broadcast_to(np.abs(rel) <= ATTN_HALF, (ATTN_TQ, ATTN_TK))
    keep = np.stack([band & ((j >= ATTN_HALF) | (not first)) & ((j < ATTN_TK - ATTN_HALF) | (not last))
                     for last in (False, True) for first in (False, True)])
    return jnp.where(jnp.asarray(keep)[:, None], b[None], MASK_VALUE)


def _attn_kernel(q_ref, kp_ref, kc_ref, kn_ref, vp_ref, vc_ref, vn_ref, bias_ref, o_ref, lse_ref, kpad, vpad,
                 *, tile, sub):
    i = pl.program_id(2)
    hf = ATTN_HALF
    kpad[0:hf, :] = kp_ref[...]
    kpad[hf:hf + tile, :] = kc_ref[...]
    kpad[hf + tile:, :] = kn_ref[...]
    vpad[0:hf, :] = vp_ref[...]
    vpad[hf:hf + tile, :] = vc_ref[...]
    vpad[hf + tile:, :] = vn_ref[...]

    scale = ATTN_HD ** -0.5
    lane = lax.broadcasted_iota(jnp.int32, (ATTN_TQ, LANES), 1)
    tiles = tile // ATTN_TQ

    def body(t, carry):
        s0 = pl.multiple_of(t * ATTN_TQ, ATTN_TQ)
        gt = i * tiles + t
        variant = (gt == 0).astype(jnp.int32) + 2 * (gt == sub // ATTN_TQ - 1).astype(jnp.int32)
        lse_tile = jnp.zeros((ATTN_TQ, LANES), F32)
        for hh in range(ATTN_HPG):
            sl = slice(hh * ATTN_HD, (hh + 1) * ATTN_HD)
            q = q_ref[pl.ds(s0, ATTN_TQ), sl]
            kw = kpad[pl.ds(s0, ATTN_TK), sl]
            vw = vpad[pl.ds(s0, ATTN_TK), sl]
            s = (lax.dot_general(q, kw, (((1,), (1,)), ((), ())), preferred_element_type=F32) * scale
                 + bias_ref[variant, hh])
            m = jnp.max(s, axis=-1, keepdims=True)
            p = jnp.exp(s - m)
            l = jnp.sum(p, axis=-1, keepdims=True)
            o = jnp.dot(p.astype(BF16), vw, preferred_element_type=F32) / l
            o_ref[pl.ds(s0, ATTN_TQ), sl] = o.astype(o_ref.dtype)
            lse_tile = jnp.where(lane == hh, m + jnp.log(l), lse_tile)
        lse_ref[pl.ds(s0, ATTN_TQ), :] = lse_tile
        return carry

    lax.fori_loop(0, tile // ATTN_TQ, body, 0, unroll=4)


def _attn_group(qkv, bias, dil, bsz, seq, *, tile=1024):
    sub = seq // dil
    tile = min(tile, sub)
    nblk = sub // tile
    hb = tile // ATTN_HALF
    last_h = sub // ATTN_HALF - 1
    assert sub % tile == 0 and tile % ATTN_TQ == 0
    cur = lambda which: pl.BlockSpec((None, None, tile, ATTN_OUT), lambda b, r, i: (which, b, i, r))
    prev = lambda which: pl.BlockSpec((None, None, ATTN_HALF, ATTN_OUT),
                                      lambda b, r, i: (which, b, jnp.maximum(i * hb - 1, 0), r))
    nxt = lambda which: pl.BlockSpec((None, None, ATTN_HALF, ATTN_OUT),
                                     lambda b, r, i: (which, b, jnp.minimum((i + 1) * hb, last_h), r))
    return pl.pallas_call(
        functools.partial(_attn_kernel, tile=tile, sub=sub),
        grid=(bsz, dil, nblk),
        in_specs=[cur(0), prev(1), cur(1), nxt(1), prev(2), cur(2), nxt(2),
                  pl.BlockSpec(bias.shape, lambda b, r, i: (0, 0, 0, 0))],
        out_specs=(pl.BlockSpec((None, tile, ATTN_OUT), lambda b, r, i: (b, i, r)),
                   pl.BlockSpec((None, tile, LANES), lambda b, r, i: (b, i, r))),
        out_shape=(jax.ShapeDtypeStruct((bsz, sub, dil * ATTN_OUT), BF16),
                   jax.ShapeDtypeStruct((bsz, sub, dil * LANES), F32)),
        scratch_shapes=[pltpu.VMEM((tile + 2 * ATTN_HALF, ATTN_OUT), BF16)] * 2,
        compiler_params=_cparams(("parallel", "parallel", "parallel")),
        name=f"dilated_attn_d{dil}",
    )(qkv, qkv, qkv, qkv, qkv, qkv, qkv, bias)


CONV_HALO = BF16_ROWS
CONV_RC = 128
CONV_APRON = 8


SUBLANES = 8


def _shift_rows(x, s):
    r, c = x.shape
    x3 = x.reshape(r // SUBLANES, SUBLANES, c)
    rot = pltpu.roll(x3, (-s) % SUBLANES, 1)
    sub = lax.broadcasted_iota(jnp.int32, x3.shape, 1)
    if s > 0:
        y = jnp.where(sub < SUBLANES - s, rot, jnp.concatenate([rot[1:], rot[:1]], axis=0))
    else:
        y = jnp.where(sub >= -s, rot, jnp.concatenate([rot[-1:], rot[:-1]], axis=0))
    return y.reshape(r, c)


def _proj_conv_kernel(hp_ref, hc_ref, hn_ref, w_ref, cw_ref, cb_ref, o_ref, *, tm, per_b):
    i = pl.program_id(0)
    first = (i % per_b) == 0
    last = (i % per_b) == per_b - 1
    w = w_ref[...]
    halo_p = jnp.dot(hp_ref[...], w, preferred_element_type=F32)
    halo_n = jnp.dot(hn_ref[...], w, preferred_element_type=F32)
    halo_p = jnp.where(first, 0.0, halo_p)
    halo_n = jnp.where(last, 0.0, halo_n)
    pad = SSD_CONV // 2
    acc = {}

    def project(s):
        acc[s] = jnp.dot(hc_ref[s * MM_ROW_CHUNK:(s + 1) * MM_ROW_CHUNK, :], w, preferred_element_type=F32)

    def tile_rows(lo, hi):
        pieces, r = [], lo
        while r < hi:
            if r < 0:
                src, off, end = halo_p, CONV_HALO + r, min(hi, 0)
            elif r >= tm:
                src, off, end = halo_n, r - tm, hi
            else:
                s = r // MM_ROW_CHUNK
                src, off, end = acc[s], r - s * MM_ROW_CHUNK, min(hi, (s + 1) * MM_ROW_CHUNK)
            pieces.append(src[off:off + end - r])
            r = end
        return pieces[0] if len(pieces) == 1 else jnp.concatenate(pieces, axis=0)

    def conv_chunk(c):
        x = tile_rows(c * CONV_RC - CONV_APRON, (c + 1) * CONV_RC + CONV_APRON)
        tap = lambda off: cw_ref[pad + off:pad + off + 1, :]
        xm, xq = _shift_rows(x, -3), _shift_rows(x, 3)
        f0 = tap(-3) * xm + tap(0) * x + tap(3) * xq + cb_ref[...]
        f1 = tap(-2) * xm + tap(1) * x
        fm = tap(-1) * x + tap(2) * xq
        y = f0 + _shift_rows(f1, 1) + _shift_rows(fm, -1)
        o_ref[c * CONV_RC:(c + 1) * CONV_RC, :] = _silu(y[CONV_APRON:CONV_APRON + CONV_RC]).astype(o_ref.dtype)

    nsub = tm // MM_ROW_CHUNK
    per_sub = MM_ROW_CHUNK // CONV_RC
    project(0)
    for s in range(nsub):
        if s + 1 < nsub:
            project(s + 1)
        for c in range(s * per_sub, (s + 1) * per_sub):
            conv_chunk(c)


def _proj_conv_silu(h, w, cw, cb, seq, *, col0, tm=2048, tn=512):
    m, k = h.shape
    n = cw.shape[1]
    assert col0 % tn == 0
    cb0 = col0 // tn
    per_b = seq // tm
    hb = tm // CONV_HALO
    assert seq % tm == 0 and n % tn == 0 and SSD_CONV == 7
    return pl.pallas_call(
        functools.partial(_proj_conv_kernel, tm=tm, per_b=per_b),
        grid=(m // tm, n // tn),
        in_specs=[
            pl.BlockSpec((CONV_HALO, k), lambda i, j: (jnp.maximum(i * hb - 1, 0), 0)),
            pl.BlockSpec((tm, k), lambda i, j: (i, 0)),
            pl.BlockSpec((CONV_HALO, k), lambda i, j: (jnp.minimum((i + 1) * hb, m // CONV_HALO - 1), 0)),
            pl.BlockSpec((k, tn), lambda i, j: (0, cb0 + j)),
            pl.BlockSpec((SSD_CONV, tn), lambda i, j: (0, j)),
            pl.BlockSpec((1, tn), lambda i, j: (0, j)),
        ],
        out_specs=pl.BlockSpec((tm, tn), lambda i, j: (i, j)),
        out_shape=jax.ShapeDtypeStruct((m, n), BF16),
        compiler_params=_cparams(("parallel", "parallel")),
        name="proj_conv_silu",
    )(h, h, h, w, cw, cb.reshape(1, n))


def _softplus(x):
    return jnp.maximum(x, 0.0) + jnp.log1p(jnp.exp(-jnp.abs(x)))


def _ssd_prep_kernel(dtr_ref, dtb_ref, alog_ref, ccol_ref, rt_ref, esc_ref, wst_ref, etot_ref, *, t):
    half = LANES // 2
    log2e = math.log2(math.e)
    lane = lax.broadcasted_iota(jnp.int32, (t, LANES), 1)
    fwd = lane < SSD_HEADS
    tri = (lax.broadcasted_iota(jnp.int32, (t, t), 1) <= lax.broadcasted_iota(jnp.int32, (t, t), 0)).astype(BF16)
    neg_a = jnp.exp(alog_ref[...])
    for ci in range(dtr_ref.shape[0] // t):
        rs = slice(ci * t, (ci + 1) * t)
        dt = _softplus(dtr_ref[rs, :] + dtb_ref[...])
        la = jnp.where(lane < 2 * SSD_HEADS, -(dt * neg_a), 0.0)
        hi = la.astype(BF16)
        r1 = la - hi.astype(F32)
        mid = r1.astype(BF16)
        lo = (r1 - mid.astype(F32)).astype(BF16)
        packed = (hi.astype(F32) + pltpu.roll(mid.astype(F32), half, 1)).astype(BF16)
        res = jnp.dot(tri, jnp.concatenate([packed, lo], axis=1), preferred_element_type=F32)
        a0 = res[:, :LANES]
        acs = a0 + pltpu.roll(a0, half, 1) + res[:, LANES:]
        exb = acs - la
        ldt = jnp.log(dt)
        tot = acs[t - 1:t, :]
        ccol_ref[rs, :] = jnp.where(fwd, acs, exb) * log2e
        rt_ref[ci] = (jnp.where(fwd, acs - ldt, exb + ldt) * log2e).T
        esc_ref[rs, :] = jnp.exp(jnp.where(fwd, acs, tot - exb))
        wst_ref[rs, :] = jnp.exp(jnp.where(fwd, tot - acs, exb)) * dt
        etot_ref[ci] = jnp.broadcast_to(jnp.exp(tot), etot_ref.shape[1:])


def _split2(v):
    hi = v.astype(BF16)
    return jnp.concatenate([hi, (v - hi.astype(F32)).astype(BF16)], axis=1)


def _head_expand_tables():
    j = np.arange(2 * LANES)[:, None] % LANES
    c = np.arange(SSD_INNER)[None, :]
    fwd = (j == c // SSD_HD)
    bwd = (j == SSD_HEADS + c // SSD_HD)
    return jnp.asarray(fwd, BF16), jnp.asarray(bwd, BF16)


def _ssd_bwd_state_kernel(xs_ref, b_ref, wst_ref, etot_ref, selb_ref, gin_ref, g_sc):
    @pl.when(pl.program_id(0) == 0)
    def _():
        g_sc[...] = jnp.zeros(g_sc.shape, F32)

    cps = etot_ref.shape[1]
    t = wst_ref.shape[1] // cps
    for sc, b in [(sc, b) for sc in reversed(range(cps)) for b in range(xs_ref.shape[0])]:
        rows = slice(sc * t, (sc + 1) * t)
        gin_ref[b, sc] = g_sc[b].astype(gin_ref.dtype)
        both = jnp.concatenate([_split2(wst_ref[b, rows, :]), _split2(etot_ref[b, sc])], axis=0)
        spread = jnp.dot(both, selb_ref[...], preferred_element_type=F32)
        xw = (xs_ref[b, rows, :].astype(F32) * spread[:t]).astype(BF16)
        dec = spread[t:t + 1]
        for g in range(SSD_GROUPS):
            gs = slice(g * SSD_GW, (g + 1) * SSD_GW)
            bm_t = b_ref[b, rows, g * SSD_STATE:(g + 1) * SSD_STATE].T
            st = jnp.dot(bm_t, xw[:, gs], preferred_element_type=F32)
            g_sc[b, :, gs] = g_sc[b, :, gs] * dec[:, gs] + st


def _ssd_main_kernel(xs_ref, b_ref, c_ref, ccol_ref, rt_ref, esc_ref, wst_ref, etot_ref, gin_ref,
                     dsk_ref, self_ref, selb_ref, y_ref, h_sc):
    @pl.when(pl.program_id(1) == 0)
    def _():
        h_sc[...] = jnp.zeros(h_sc.shape, F32)

    t = rt_ref.shape[-1]
    li = lax.broadcasted_iota(jnp.int32, (t, t), 0)
    si = lax.broadcasted_iota(jnp.int32, (t, t), 1)
    mask_f = jnp.where(li >= si, 0.0, MASK_VALUE)
    mask_b = jnp.where(si >= li, 0.0, MASK_VALUE)
    lo_half = lax.broadcasted_iota(jnp.int32, (t, LANES), 1) < SSD_HD

    for sc, g in [(sc, g) for sc in range(rt_ref.shape[0]) for g in range(SSD_GROUPS)]:
        rows = slice(sc * t, (sc + 1) * t)
        ccol = ccol_ref[rows, :]
        r_t = rt_ref[sc]
        esc2 = _split2(esc_ref[rows, :])
        wst2 = _split2(wst_ref[rows, :])
        etot2 = _split2(etot_ref[sc])
        gs = slice(g * SSD_GW, (g + 1) * SSD_GW)
        bm = b_ref[rows, g * SSD_STATE:(g + 1) * SSD_STATE]
        cm = c_ref[rows, g * SSD_STATE:(g + 1) * SSD_STATE]
        cb = lax.dot_general(cm, bm, (((1,), (1,)), ((), ())), preferred_element_type=F32)
        hf = h_sc[:, gs]
        yf_all = jnp.dot(cm, hf.astype(BF16), preferred_element_type=F32)
        yb_all = jnp.dot(cm, gin_ref[sc, :, gs], preferred_element_type=F32)
        e_in = jnp.dot(esc2, self_ref[g], preferred_element_type=F32)
        e_out = jnp.dot(esc2, selb_ref[g], preferred_element_type=F32)
        w_state = jnp.dot(wst2, self_ref[g], preferred_element_type=F32)
        dec = jnp.dot(etot2, self_ref[g], preferred_element_type=F32)[0:1]
        xw = []
        for k in range(SSD_HPG // 2):
            ea = g * SSD_HPG + 2 * k
            sl = slice(g * SSD_GW + k * LANES, g * SSD_GW + (k + 1) * LANES)
            ks = slice(k * LANES, (k + 1) * LANES)
            xp = xs_ref[rows, sl]
            mms = []
            for e in (ea, ea + 1):
                eb = SSD_HEADS + e
                w = (jnp.exp2(ccol[:, e:e + 1] - r_t[e:e + 1, :] + mask_f)
                     + jnp.exp2(r_t[eb:eb + 1, :] - ccol[:, eb:eb + 1] + mask_b))
                mms.append((cb * w).astype(BF16))
            zero = jnp.zeros_like(xp)
            xx = jnp.concatenate([jnp.where(lo_half, xp, zero), jnp.where(lo_half, zero, xp)], axis=0)
            ypair = jnp.dot(jnp.concatenate(mms, axis=1), xx, preferred_element_type=F32)
            xf = xp.astype(F32)
            y_ref[rows, sl] = (ypair + yf_all[:, ks] * e_in[:, ks] + yb_all[:, ks] * e_out[:, ks]
                               + xf * dsk_ref[:, sl])
            xw.append((xf * w_state[:, ks]).astype(BF16))
        st = jnp.dot(bm.T, jnp.concatenate(xw, axis=1), preferred_element_type=F32)
        h_sc[:, gs] = hf * dec + st


def _ssd(conv_out, dt_raw, dtb, alog, dskip, bsz, seq):
    t = SSD_CHUNK
    nc = seq // t
    assert seq % t == 0
    gn = SSD_GROUPS * SSD_STATE
    b_blk = SSD_INNER // gn
    c_blk = b_blk + 1
    cpp = min(8, nc)
    assert nc % cpp == 0
    sub8 = 8
    const_spec = pl.BlockSpec((1, LANES), lambda b, i: (0, 0))
    rows_spec = pl.BlockSpec((None, cpp * t, LANES), lambda b, i: (b, i, 0))
    row_arr = jax.ShapeDtypeStruct((bsz, seq, LANES), F32)
    ccol, rt, esc, wst, etot = pl.pallas_call(
        functools.partial(_ssd_prep_kernel, t=t),
        grid=(bsz, nc // cpp),
        in_specs=[rows_spec, const_spec, const_spec],
        out_specs=(rows_spec, pl.BlockSpec((None, cpp, LANES, t), lambda b, i: (b, i, 0, 0)), rows_spec, rows_spec,
                   pl.BlockSpec((None, cpp, sub8, LANES), lambda b, i: (b, i, 0, 0))),
        out_shape=(row_arr, jax.ShapeDtypeStruct((bsz, nc, LANES, t), F32), row_arr, row_arr,
                   jax.ShapeDtypeStruct((bsz, nc, sub8, LANES), F32)),
        compiler_params=_cparams(("parallel", "parallel")),
        name="ssd_decay_terms",
    )(dt_raw, dtb, alog)

    cps = next(n for n in (4, 2, 1) if nc % n == 0)
    rev = lambda c: nc // cps - 1 - c
    sel_f, sel_b = _head_expand_tables()
    by_group = lambda a: a.reshape(a.shape[0], SSD_GROUPS, SSD_GW).transpose(1, 0, 2)
    sel_spec = pl.BlockSpec((SSD_GROUPS, sel_f.shape[0], SSD_GW), lambda b, c: (0, 0, 0))
    gin = pl.pallas_call(
        _ssd_bwd_state_kernel,
        grid=(nc // cps,),
        in_specs=[
            pl.BlockSpec((bsz, cps * t, SSD_INNER), lambda c: (0, rev(c), 0)),
            pl.BlockSpec((bsz, cps * t, gn), lambda c: (0, rev(c), b_blk)),
            pl.BlockSpec((bsz, cps * t, LANES), lambda c: (0, rev(c), 0)),
            pl.BlockSpec((bsz, cps, sub8, LANES), lambda c: (0, rev(c), 0, 0)),
            pl.BlockSpec(sel_b.shape, lambda c: (0, 0)),
        ],
        out_specs=pl.BlockSpec((bsz, cps, SSD_STATE, SSD_INNER), lambda c: (0, rev(c), 0, 0)),
        out_shape=jax.ShapeDtypeStruct((bsz, nc, SSD_STATE, SSD_INNER), BF16),
        scratch_shapes=[pltpu.VMEM((bsz, SSD_STATE, SSD_INNER), F32)],
        compiler_params=_cparams(("arbitrary",)),
        name="ssd_bwd_states",
    )(conv_out, conv_out, wst, etot, sel_b)

    wide_spec = pl.BlockSpec((1, SSD_INNER), lambda b, c: (0, 0))
    tok_spec = pl.BlockSpec((None, cps * t, LANES), lambda b, c: (b, c, 0))
    return pl.pallas_call(
        _ssd_main_kernel,
        grid=(bsz, nc // cps),
        in_specs=[
            pl.BlockSpec((None, cps * t, SSD_INNER), lambda b, c: (b, c, 0)),
            pl.BlockSpec((None, cps * t, gn), lambda b, c: (b, c, b_blk)),
            pl.BlockSpec((None, cps * t, gn), lambda b, c: (b, c, c_blk)),
            tok_spec,
            pl.BlockSpec((None, cps, LANES, t), lambda b, c: (b, c, 0, 0)),
            tok_spec, tok_spec,
            pl.BlockSpec((None, cps, sub8, LANES), lambda b, c: (b, c, 0, 0)),
            pl.BlockSpec((None, cps, SSD_STATE, SSD_INNER), lambda b, c: (b, c, 0, 0)),
            wide_spec, sel_spec, sel_spec,
        ],
        out_specs=pl.BlockSpec((None, cps * t, SSD_INNER), lambda b, c: (b, c, 0)),
        out_shape=jax.ShapeDtypeStruct((bsz, seq, SSD_INNER), F32),
        scratch_shapes=[pltpu.VMEM((SSD_STATE, SSD_INNER), F32)],
        compiler_params=_cparams(("parallel", "arbitrary")),
        name="ssd_main",
    )(conv_out, conv_out, conv_out, ccol, rt, esc, wst, etot, gin, dskip, by_group(sel_f), by_group(sel_b))


def _dft_tables(seq):
    n1, n2 = seq // F_N2, F_N2
    c = np.arange(F_GD)
    ang = 2 * np.pi * np.outer(c, c) / F_GD
    chan = np.concatenate([np.cos(ang), -np.sin(ang)], axis=1) / math.sqrt(F_GD)
    k1 = np.arange(n1)
    a1 = 2 * np.pi * np.outer(k1, k1) / n1
    stage_a = np.block([[np.cos(a1), np.sin(a1)], [-np.sin(a1), np.cos(a1)]])
    stage_a = np.kron(stage_a, np.eye(2))
    s2 = np.arange(n2)
    at = 2 * np.pi * np.outer(s2, k1) / seq
    at = at.reshape(n2 // 2, 2, n1).transpose(0, 2, 1).reshape(n2 // 2, 2 * n1)
    tw_c, tw_s = np.cos(at)[..., None], np.sin(at)[..., None]
    a2 = 2 * np.pi * np.outer(s2, s2) / n2
    stage_b = np.concatenate([np.cos(a2), np.sin(a2)], axis=1)
    return (jnp.asarray(chan, BF16), jnp.asarray(stage_a, BF16), jnp.asarray(tw_c, F32),
            jnp.asarray(tw_s, F32), jnp.asarray(stage_b, BF16))


def _fft_a_kernel(z_ref, ma_ref, twc_ref, tws_ref, o_ref, zs, ys, *, n1, rb):
    ct = z_ref.shape[-1]
    half = rb // 2
    _put_rows(zs, pltpu.bitcast(z_ref[...].reshape(2 * n1 * rb, ct), jnp.uint32))
    for j in range(half):
        zz = pltpu.bitcast(_get_rows(zs, j, 2 * n1, half), BF16)
        y = jnp.dot(ma_ref[...], zz, preferred_element_type=F32)
        yr, yi = y[:2 * n1], y[2 * n1:]
        tc, ts = twc_ref[j], tws_ref[j]
        out = jnp.concatenate([yr * tc + yi * ts, yi * tc - yr * ts], axis=0).astype(BF16)
        _put_rows(ys, pltpu.bitcast(out, jnp.uint32), j, half)
    o_ref[...] = pltpu.bitcast(_get_rows(ys), BF16).reshape(2, n1, rb, ct)


def _fft_b_kernel(y_ref, mb_ref, o_ref, os_ref, *, n2, kb, scale):
    ct = y_ref.shape[-1]
    for kk in range(kb):
        yy = y_ref[:, kk].reshape(2 * n2, ct)
        _put_rows(os_ref, jnp.dot(mb_ref[...], yy, preferred_element_type=F32) * scale, kk, kb)
    o_ref[...] = _get_rows(os_ref).reshape(n2, kb, ct).astype(o_ref.dtype)


def _fourier_seq(z2, tabs, bsz, seq, *, rb=BF16_ROWS, kb=BF16_ROWS, ct=768):
    _, ma, twc, tws, mb = tabs
    n1, n2 = seq // F_N2, F_N2
    w = z2.shape[-1]
    nct = w // ct
    za = z2.reshape(2, bsz, n1, n2, w)
    ab_spec = pl.BlockSpec((2, None, n1, rb, ct), lambda b, a, j: (0, b, 0, a, j))
    ya = pl.pallas_call(
        functools.partial(_fft_a_kernel, n1=n1, rb=rb),
        grid=(bsz, n2 // rb, nct),
        in_specs=[
            ab_spec,
            pl.BlockSpec((4 * n1, 4 * n1), lambda b, a, j: (0, 0)),
            pl.BlockSpec((rb // 2, 2 * n1, 1), lambda b, a, j: (a, 0, 0)),
            pl.BlockSpec((rb // 2, 2 * n1, 1), lambda b, a, j: (a, 0, 0)),
        ],
        out_specs=ab_spec,
        out_shape=jax.ShapeDtypeStruct((2, bsz, n1, n2, w), BF16),
        scratch_shapes=[pltpu.VMEM((ct // LANES, n1 * rb, LANES), jnp.uint32)] * 2,
        compiler_params=_cparams(("parallel", "parallel", "parallel")),
        name="fourier_stage_a",
    )(za, ma, twc, tws)
    out = pl.pallas_call(
        functools.partial(_fft_b_kernel, n2=n2, kb=kb, scale=1.0 / math.sqrt(seq)),
        grid=(bsz, n1 // kb, nct),
        in_specs=[
            pl.BlockSpec((2, None, kb, n2, ct), lambda b, k, j: (0, b, k, 0, j)),
            pl.BlockSpec((n2, 2 * n2), lambda b, k, j: (0, 0)),
        ],
        out_specs=pl.BlockSpec((None, n2, kb, ct), lambda b, k, j: (b, 0, k, j)),
        out_shape=jax.ShapeDtypeStruct((bsz, n2, n1, w), BF16),
        scratch_shapes=[pltpu.VMEM((ct // LANES, n2 * kb, LANES), F32)],
        compiler_params=_cparams(("parallel", "parallel", "parallel")),
        name="fourier_stage_b",
    )(ya, mb)
    return out.reshape(bsz, seq, w)


def _merge_kernel(*refs, dils):
    ng = len(dils)
    x_ref = refs[0]
    o_refs = refs[1:1 + ng]
    l_refs = refs[1 + ng:1 + 2 * ng]
    y_ref, z_ref, yg_ref, f_ref, gt_ref, wa_ref, ws_ref, wf_ref, wo_ref, out_ref, o_sc, l_sc = refs[1 + 2 * ng:]
    tm, d = x_ref.shape

    yf = jnp.dot(f_ref[...], wf_ref[...], preferred_element_type=F32)

    ys = None
    for g in range(SSD_GROUPS):
        gs = slice(g * SSD_GW, (g + 1) * SSD_GW)
        yy = y_ref[:, gs] * _silu(z_ref[:, gs].astype(F32))
        part = jnp.dot(_rms(yy, yg_ref[:, gs]).astype(BF16), ws_ref[gs, :], preferred_element_type=F32)
        ys = part if ys is None else ys + part

    for gi, dil in enumerate(dils):
        rows = tm // dil
        idx = lambda r: slice(None) if dil == 1 else pl.ds(r, rows, stride=dil)
        for r in range(dil):
            l_sc[gi, idx(r), :] = l_refs[gi][:, r * LANES:(r + 1) * LANES]
            for hh in range(ATTN_HPG):
                c0 = r * ATTN_OUT + hh * ATTN_HD
                o_sc[gi * ATTN_HPG + hh, idx(r), :] = o_refs[gi][:, c0:c0 + ATTN_HD].astype(F32)

    ls = [l_sc[gi] for gi in range(ng)]
    mx = functools.reduce(jnp.maximum, ls)
    ws = [jnp.exp(v - mx) for v in ls]
    inv = 1.0 / functools.reduce(lambda a, b: a + b, ws)
    heads = []
    for hh in range(ATTN_HPG):
        acc = None
        for gi in range(ng):
            alpha = (ws[gi] * inv)[:, hh:hh + 1]
            term = alpha * o_sc[gi * ATTN_HPG + hh]
            acc = term if acc is None else acc + term
        heads.append(acc.astype(BF16))
    o_attn = jnp.concatenate(heads, axis=1)

    ya = jnp.dot(o_attn, wa_ref[...], preferred_element_type=F32)
    gates = gt_ref[...].astype(F32)
    merged = gates[:, :d] * ya + gates[:, d:2 * d] * ys + gates[:, 2 * d:] * yf
    out_ref[...] = x_ref[...] + jnp.dot(merged.astype(BF16), wo_ref[...], preferred_element_type=F32)


def _merge(x, os_, ls_, dils, y, z, y_gain, f, gates, wa, ws, wf, wo, bsz, seq, *, tm=512):
    d = x.shape[-1]
    assert all((tm // dil) % BF16_ROWS == 0 for dil in dils)
    tok = lambda width: pl.BlockSpec((None, tm, width), lambda b, i: (b, i, 0))
    full = lambda a: pl.BlockSpec(a.shape, lambda b, i: (0, 0), pipeline_mode=pl.Buffered(1))
    ng = len(dils)
    in_specs = [tok(d)]
    in_specs += [pl.BlockSpec((None, tm // dil, dil * ATTN_OUT), lambda b, i: (b, i, 0)) for dil in dils]
    in_specs += [pl.BlockSpec((None, tm // dil, dil * LANES), lambda b, i: (b, i, 0)) for dil in dils]
    in_specs += [tok(y.shape[-1]), tok(z.shape[-1]), full(y_gain), tok(f.shape[-1]), tok(gates.shape[-1]),
                 full(wa), full(ws), full(wf), full(wo)]
    return pl.pallas_call(
        functools.partial(_merge_kernel, dils=tuple(dils)),
        grid=(bsz, seq // tm),
        in_specs=in_specs,
        out_specs=tok(d),
        out_shape=jax.ShapeDtypeStruct((bsz, seq, d), F32),
        scratch_shapes=[pltpu.VMEM((ng * ATTN_HPG, tm, ATTN_HD), F32), pltpu.VMEM((ng, tm, LANES), F32)],
        compiler_params=_cparams(("parallel", "parallel")),
        name="branch_merge",
    )(x, *os_, *ls_, y, z, y_gain, f, gates, wa, ws, wf, wo)


def _xattn_kernel(x_ref, g_ref, wq_ref, qg_ref, k_ref, v_ref, wo_ref, out_ref):
    tm, d = x_ref.shape
    hd = d // MEM_HEADS
    scale = hd ** -0.5
    rc = min(tm, XATTN_ROW_CHUNK)
    for c in range(tm // rc):
        rs = slice(c * rc, (c + 1) * rc)
        x = x_ref[rs, :]
        h = _rms(x, g_ref[...]).astype(BF16)
        q = jnp.dot(h, wq_ref[...], preferred_element_type=F32)
        heads = []
        for hh in range(MEM_HEADS):
            sl = slice(hh * hd, (hh + 1) * hd)
            qn = _rms(q[:, sl], qg_ref[...]).astype(BF16)
            s = lax.dot_general(qn, k_ref[:, sl], (((1,), (1,)), ((), ())), preferred_element_type=F32) * scale
            m = jnp.max(s, axis=-1, keepdims=True)
            p = jnp.exp(s - m)
            l = jnp.sum(p, axis=-1, keepdims=True)
            heads.append((jnp.dot(p.astype(BF16), v_ref[:, sl], preferred_element_type=F32) / l).astype(BF16))
        o = jnp.concatenate(heads, axis=1)
        out_ref[rs, :] = x + jnp.dot(o, wo_ref[...], preferred_element_type=F32)


def _xattn(x, g, wq, qg, k, v, wo, *, tm=512):
    bsz, seq, d = x.shape
    mt = k.shape[1]
    full = lambda a: pl.BlockSpec(a.shape, lambda b, i: (0, 0))
    g = g.reshape(1, d)
    qg = qg.reshape(1, -1)
    return pl.pallas_call(
        _xattn_kernel,
        grid=(bsz, seq // tm),
        in_specs=[
            pl.BlockSpec((None, tm, d), lambda b, i: (b, i, 0)),
            full(g), full(wq), full(qg),
            pl.BlockSpec((None, mt, d), lambda b, i: (b, 0, 0)),
            pl.BlockSpec((None, mt, d), lambda b, i: (b, 0, 0)),
            full(wo),
        ],
        out_specs=pl.BlockSpec((None, tm, d), lambda b, i: (b, i, 0)),
        out_shape=jax.ShapeDtypeStruct((bsz, seq, d), F32),
        compiler_params=_cparams(("parallel", "parallel")),
        name="mem_xattn",
    )(x, g, wq, qg, k, v, wo)


def _ffn_kernel(x_ref, g_ref, wg_ref, wu_ref, wd_ref, *rest, tf):
    x = x_ref[...]
    h = _rms(x, g_ref[...]).astype(BF16)
    acc = x
    for c in range(wg_ref.shape[1] // tf):
        sl = slice(c * tf, (c + 1) * tf)
        gt = jnp.dot(h, wg_ref[:, sl], preferred_element_type=F32)
        up = jnp.dot(h, wu_ref[:, sl], preferred_element_type=F32)
        a = (_silu(gt) * up).astype(BF16)
        acc = acc + jnp.dot(a, wd_ref[sl, :], preferred_element_type=F32)
    if len(rest) == 1:
        rest[0][...] = acc
    else:
        gn_ref, out_ref, hn_ref = rest
        out_ref[...] = acc
        hn_ref[...] = _rms(acc, gn_ref[...]).astype(hn_ref.dtype)


def _ffn(x, g, wg, wu, wd, next_gain=None, *, tm=512):
    m, d = x.shape
    dff = wg.shape[1]
    tf = dff // 2 if (dff // 2) % LANES == 0 else dff
    full = lambda a: pl.BlockSpec(a.shape, lambda i: (0, 0), pipeline_mode=pl.Buffered(1))
    tok = pl.BlockSpec((tm, d), lambda i: (i, 0))
    args = [x, g.reshape(1, d), wg, wu, wd]
    out_shape, out_specs = jax.ShapeDtypeStruct((m, d), F32), tok
    if next_gain is not None:
        args.append(next_gain.reshape(1, d))
        out_shape, out_specs = (out_shape, jax.ShapeDtypeStruct((m, d), BF16)), (tok, tok)
    return pl.pallas_call(
        functools.partial(_ffn_kernel, tf=tf),
        grid=(m // tm,),
        in_specs=[tok] + [full(a) for a in args[1:]],
        out_specs=out_specs,
        out_shape=out_shape,
        compiler_params=_cparams(("parallel",)),
        name="swiglu_ffn",
    )(*args)


def _pad_lanes(a):
    a = a.reshape(-1, 2 * SSD_HEADS)
    return jnp.pad(a, ((0, 0), (0, LANES - 2 * SSD_HEADS)))


def kernel(x, mem, rel_bias, mix_norm_g, w_in, gate_bias, attn_q_norm_g, attn_k_norm_g, conv_w, conv_b, dt_bias, a_log, d_skip, ssd_norm_g, w_branch_attn, w_branch_ssd, w_branch_fourier, w_mix_out, xattn_norm_g, mem_norm_g, w_xq, w_xk, w_xv, xattn_q_norm_g, xattn_k_norm_g, w_xo, ffn_norm_g, w_ffn_gate, w_ffn_up, w_ffn_down):
    bsz, seq, d = x.shape
    depth = w_in.shape[0]
    m = bsz * seq
    xf = x.reshape(m, d)
    memf = mem.reshape(bsz * mem.shape[1], d)
    tabs = _dft_tables(seq)
    dils = [dil for _, dil in ATTN_GROUPS]
    biases = [_attn_bias(rel_bias, gi, dil) for gi, dil in enumerate(dils)]
    offs = np.cumsum([0, ATTN_WIDTH, ATTN_WIDTH, ATTN_WIDTH, SSD_INNER, SSD_CONV_CH, 2 * SSD_HEADS, F_WIDTH, 3 * d])
    bf = lambda a: a.astype(BF16)

    for l in range(depth):
        wl = bf(w_in[l])
        seg = lambda i: wl[:, offs[i]:offs[i + 1]]
        if l == 0:
            h = _rmsnorm(xf, mix_norm_g[l])
        gains = jnp.stack([attn_q_norm_g[l], attn_k_norm_g[l], jnp.ones_like(attn_q_norm_g[l])]).reshape(3, 1, ATTN_HD)
        os_, ls_ = [], []
        for gi, dil in enumerate(dils):
            qkv = _qkv_group(h, wl, gains, gi, dil, bsz, seq)
            o_g, lse_g = _attn_group(qkv, biases[gi], dil, bsz, seq)
            os_.append(o_g)
            ls_.append(lse_g)

        z = _matmul(h, wl, out_dtype=BF16, tn=512, tm=2048, col0=int(offs[3]), n=SSD_INNER)
        dt_raw = _matmul(h, _pad_lanes(seg(5)), out_dtype=F32, tn=LANES, tm=2048)
        z2 = _matmul(h, seg(6), out_dtype=BF16, tn=768, tm=2048, epilogue="chandft", extra=tabs[0])
        gates = _matmul(h, seg(7), out_dtype=BF16, tn=1024, tm=2048, epilogue="sigmoid_bias", extra=gate_bias[l])

        conv_out = _proj_conv_silu(h, wl, conv_w[l], conv_b[l], seq, col0=int(offs[4])).reshape(bsz, seq, SSD_CONV_CH)
        y_ssd = _ssd(conv_out, dt_raw.reshape(bsz, seq, -1), _pad_lanes(dt_bias[l]), _pad_lanes(a_log[l]),
                     jnp.repeat(d_skip[l], SSD_HD).reshape(1, SSD_INNER), bsz, seq)

        f_re = _fourier_seq(z2, tabs, bsz, seq)

        x3 = _merge(xf.reshape(bsz, seq, d), os_, ls_, dils, y_ssd, z.reshape(bsz, seq, SSD_INNER),
                    ssd_norm_g[l].reshape(1, SSD_INNER), f_re, gates.reshape(bsz, seq, 3 * d),
                    bf(w_branch_attn[l]), bf(w_branch_ssd[l]), bf(w_branch_fourier[l]), bf(w_mix_out[l]), bsz, seq)

        hd_m = d // MEM_HEADS
        hm = _rmsnorm(memf, mem_norm_g[l])
        km = _matmul(hm, bf(w_xk[l]), out_dtype=BF16, tn=512, epilogue="headnorm",
                     extra=xattn_k_norm_g[l], head_dim=hd_m)
        vm = _matmul(hm, bf(w_xv[l]), out_dtype=BF16, tn=512)
        x3 = _xattn(x3, xattn_norm_g[l], bf(w_xq[l]), xattn_q_norm_g[l], km.reshape(bsz, -1, d),
                    vm.reshape(bsz, -1, d), bf(w_xo[l]))

        ffn_args = (x3.reshape(m, d), ffn_norm_g[l], bf(w_ffn_gate[l]), bf(w_ffn_up[l]), bf(w_ffn_down[l]))
        if l + 1 < depth:
            xf, h = _ffn(*ffn_args, mix_norm_g[l + 1])
        else:
            xf = _ffn(*ffn_args)

    return xf.reshape(bsz, seq, d)
```

```python
import functools
import math

import numpy as np
import jax
import jax.numpy as jnp
from jax import lax
from jax.experimental import pallas as pl
from jax.experimental.pallas import tpu as pltpu

F32 = jnp.float32
BF16 = jnp.bfloat16

NORM_EPS = 1e-6
MASK_VALUE = -1e30

ATTN_GROUPS = ((128, 1), (512, 4), (2048, 16))
ATTN_HPG = 4
ATTN_HD = 128
ATTN_HEADS = ATTN_HPG * len(ATTN_GROUPS)
ATTN_WIDTH = ATTN_HEADS * ATTN_HD
ATTN_OUT = ATTN_HPG * ATTN_HD
ATTN_HALF = 64
ATTN_TQ = 2 * ATTN_HALF
ATTN_TK = ATTN_TQ + 2 * ATTN_HALF
NUM_BUCKETS = 32
MAX_DISTANCE = 1024

SSD_HEADS = 32
SSD_HD = 64
SSD_GROUPS = 4
SSD_HPG = SSD_HEADS // SSD_GROUPS
SSD_STATE = 128
SSD_INNER = SSD_HEADS * SSD_HD
SSD_GW = SSD_INNER // SSD_GROUPS
SSD_CONV = 7
SSD_CHUNK = 128
SSD_CONV_CH = SSD_INNER + 2 * SSD_GROUPS * SSD_STATE
LANES = 128
SUBLANES = 8
BF16_ROWS = 16

F_GROUPS = 6
F_GD = 256
F_WIDTH = F_GROUPS * F_GD
F_N2 = 128

MEM_HEADS = 4
XATTN_ROW_CHUNK = 256

VMEM_LIMIT = 56 * 1024 * 1024


def _cparams(sem):
    return pltpu.CompilerParams(dimension_semantics=sem, vmem_limit_bytes=VMEM_LIMIT)


def _silu(x):
    return x * jax.nn.sigmoid(x)


def _rms(x, g):
    ms = jnp.mean(x * x, axis=-1, keepdims=True)
    return x * lax.rsqrt(ms + NORM_EPS) * g


def _get_rows(ref3, start=0, size=None, stride=1):
    idx = slice(None) if size is None else pl.ds(start, size, stride=stride)
    return jnp.concatenate([ref3[c, idx, :] for c in range(ref3.shape[0])], axis=1)


def _put_rows(ref3, val, start=0, stride=1):
    size = val.shape[0]
    idx = slice(None) if (stride == 1 and size == ref3.shape[1]) else pl.ds(start, size, stride=stride)
    for c in range(ref3.shape[0]):
        ref3[c, idx, :] = val[:, c * LANES:(c + 1) * LANES]


def _rmsnorm_kernel(x_ref, g_ref, o_ref):
    o_ref[...] = _rms(x_ref[...], g_ref[...]).astype(o_ref.dtype)


def _rmsnorm(x, g, *, tm=1024):
    m, k = x.shape
    tm = min(tm, m)
    return pl.pallas_call(
        _rmsnorm_kernel,
        grid=(m // tm,),
        in_specs=[pl.BlockSpec((tm, k), lambda i: (i, 0)), pl.BlockSpec((1, k), lambda i: (0, 0))],
        out_specs=pl.BlockSpec((tm, k), lambda i: (i, 0)),
        out_shape=jax.ShapeDtypeStruct((m, k), BF16),
        compiler_params=_cparams(("parallel",)),
        name="rmsnorm",
    )(x, g.reshape(1, k))


MM_ROW_CHUNK = 256


def _mm_kernel(*refs, epilogue, head_dim, row_chunk):
    h_ref, w_ref = refs[:2]
    o_ref = refs[-1]
    tm = h_ref.shape[0]
    rc = min(row_chunk, tm)
    for c in range(tm // rc):
        rs = slice(c * rc, (c + 1) * rc)
        acc = jnp.dot(h_ref[rs, :], w_ref[...], preferred_element_type=F32)
        tn = acc.shape[1]
        if epilogue == "headnorm":
            hg_ref = refs[2]
            for hh in range(tn // head_dim):
                sl = slice(hh * head_dim, (hh + 1) * head_dim)
                o_ref[rs, sl] = _rms(acc[:, sl], hg_ref[...]).astype(o_ref.dtype)
        elif epilogue == "sigmoid_bias":
            b_ref = refs[2]
            o_ref[rs, :] = jax.nn.sigmoid(acc + b_ref[...]).astype(o_ref.dtype)
        elif epilogue == "chandft":
            dft_ref = refs[2]
            a16 = acc.astype(BF16)
            for gg in range(tn // F_GD):
                sl = slice(gg * F_GD, (gg + 1) * F_GD)
                r = jnp.dot(a16[:, sl], dft_ref[...], preferred_element_type=F32)
                o_ref[0, rs, sl] = r[:, :F_GD].astype(o_ref.dtype)
                o_ref[1, rs, sl] = r[:, F_GD:].astype(o_ref.dtype)
        else:
            o_ref[rs, :] = acc.astype(o_ref.dtype)


def _matmul(h, w, *, out_dtype, tn, tm=1024, epilogue="plain", extra=None, head_dim=None, col0=0, n=None,
            row_chunk=MM_ROW_CHUNK):
    m, k = h.shape
    n = w.shape[1] if n is None else n
    tm = min(tm, m)
    assert m % tm == 0 and n % tn == 0 and col0 % tn == 0
    cb0 = col0 // tn
    in_specs = [pl.BlockSpec((tm, k), lambda i, j: (i, 0)), pl.BlockSpec((k, tn), lambda i, j: (0, cb0 + j))]
    args = [h, w]
    if epilogue == "headnorm":
        in_specs.append(pl.BlockSpec((1, head_dim), lambda i, j: (0, 0)))
        args.append(extra.reshape(1, head_dim))
    elif epilogue == "sigmoid_bias":
        in_specs.append(pl.BlockSpec((1, tn), lambda i, j: (0, j)))
        args.append(extra.reshape(1, n))
    elif epilogue == "chandft":
        in_specs.append(pl.BlockSpec(extra.shape, lambda i, j: (0, 0)))
        args.append(extra)
    if epilogue == "chandft":
        out_shape = jax.ShapeDtypeStruct((2, m, n), out_dtype)
        out_spec = pl.BlockSpec((2, tm, tn), lambda i, j: (0, i, j))
    else:
        out_shape = jax.ShapeDtypeStruct((m, n), out_dtype)
        out_spec = pl.BlockSpec((tm, tn), lambda i, j: (i, j))
    return pl.pallas_call(
        functools.partial(_mm_kernel, epilogue=epilogue, head_dim=head_dim, row_chunk=row_chunk),
        grid=(m // tm, n // tn),
        in_specs=in_specs,
        out_specs=out_spec,
        out_shape=out_shape,
        compiler_params=_cparams(("parallel", "parallel")),
        name="matmul_" + epilogue,
    )(*args)


def _qkv_kernel(h_ref, w_ref, hg_ref, o_ref, *scratch, dil):
    j = pl.program_id(1)
    tm = h_ref.shape[0]
    width = w_ref.shape[1]
    nh = width // ATTN_HD
    rc = min(MM_ROW_CHUNK, tm)
    rows = rc // dil
    for c in range(tm // rc):
        acc = jnp.dot(h_ref[c * rc:(c + 1) * rc, :], w_ref[...], preferred_element_type=F32)
        if dil > 1:
            scr = scratch[0]
            for hh in range(nh):
                scr[hh, c * rc:(c + 1) * rc, :] = acc[:, hh * ATTN_HD:(hh + 1) * ATTN_HD]
        for r in range(dil):
            for hh in range(nh):
                if dil == 1:
                    ph = acc[:, hh * ATTN_HD:(hh + 1) * ATTN_HD]
                else:
                    ph = scr[hh, pl.ds(c * rc + r, rows, stride=dil), :]
                val = jnp.where(j < 2, _rms(ph, hg_ref[...]), ph)
                c0 = r * width + hh * ATTN_HD
                o_ref[c * rows:(c + 1) * rows, c0:c0 + ATTN_HD] = val.astype(o_ref.dtype)


def _qkv_group(h, w, gains, gi, dil, bsz, seq, *, tm=2048):
    m, k = h.shape
    sub = seq // dil
    per_b = seq // tm
    rows = tm // dil
    assert seq % tm == 0 and rows % BF16_ROWS == 0
    return pl.pallas_call(
        functools.partial(_qkv_kernel, dil=dil),
        grid=(m // tm, 3),
        in_specs=[
            pl.BlockSpec((tm, k), lambda i, j: (i, 0)),
            pl.BlockSpec((k, ATTN_OUT), lambda i, j: (0, j * len(ATTN_GROUPS) + gi)),
            pl.BlockSpec((None, 1, ATTN_HD), lambda i, j: (j, 0, 0)),
        ],
        out_specs=pl.BlockSpec((None, None, rows, dil * ATTN_OUT), lambda i, j: (j, i // per_b, i % per_b, 0)),
        out_shape=jax.ShapeDtypeStruct((3, bsz, sub, dil * ATTN_OUT), BF16),
        scratch_shapes=[pltpu.VMEM((ATTN_HPG, tm, ATTN_HD), F32)] if dil > 1 else [],
        compiler_params=_cparams(("parallel", "parallel")),
        name=f"qkv_proj_d{dil}",
    )(h, w, gains)


def _t5_bucket_np(rel):
    half_b = NUM_BUCKETS // 2
    exact = half_b // 2
    dist = np.abs(rel)
    log_ratio = np.log(np.maximum(dist, 1) / exact) / np.log(MAX_DISTANCE / exact)
    far = np.minimum(exact + (log_ratio * (half_b - exact)).astype(np.int32), half_b - 1)
    return np.where(rel > 0, half_b, 0) + np.where(dist < exact, dist, far)


def _attn_bias(rel_bias, gi, dil):
    i = np.arange(ATTN_TQ)[:, None]
    j = np.arange(ATTN_TK)[None, :]
    rel = j - ATTN_HALF - i
    idx = _t5_bucket_np(rel * dil)
    onehot = jnp.asarray(np.eye(NUM_BUCKETS, dtype=np.float32)[idx])
    tab = rel_bias[:, gi * ATTN_HPG:(gi + 1) * ATTN_HPG].astype(F32)
    b = jnp.einsum("qkn,nh->hqk", onehot, tab, precision=lax.Precision.HIGHEST)
    band = np.broadcast_to(np.abs(rel) <= ATTN_HALF, (ATTN_TQ, ATTN_TK))
    keep = np.stack([band & ((j >= ATTN_HALF) | (not first)) & ((j < ATTN_TK - ATTN_HALF) | (not last))
                     for last in (False, True) for first in (False, True)])
    return jnp.where(jnp.asarray(keep)[:, None], b[None], MASK_VALUE)


def _attn_kernel(q_ref, kp_ref, kc_ref, kn_ref, vp_ref, vc_ref, vn_ref, bias_ref, o_ref, lse_ref, kpad, vpad,
                 *, tile, sub):
    i = pl.program_id(2)
    hf = ATTN_HALF
    kpad[0:hf, :] = kp_ref[...]
    kpad[hf:hf + tile, :] = kc_ref[...]
    kpad[hf + tile:, :] = kn_ref[...]
    vpad[0:hf, :] = vp_ref[...]
    vpad[hf:hf + tile, :] = vc_ref[...]
    vpad[hf + tile:, :] = vn_ref[...]

    scale = ATTN_HD ** -0.5
    lane = lax.broadcasted_iota(jnp.int32, (ATTN_TQ, LANES), 1)
    tiles = tile // ATTN_TQ

    def body(t, carry):
        s0 = pl.multiple_of(t * ATTN_TQ, ATTN_TQ)
        gt = i * tiles + t
        variant = (gt == 0).astype(jnp.int32) + 2 * (gt == sub // ATTN_TQ - 1).astype(jnp.int32)
        lse_tile = jnp.zeros((ATTN_TQ, LANES), F32)
        for hh in range(ATTN_HPG):
            sl = slice(hh * ATTN_HD, (hh + 1) * ATTN_HD)
            q = q_ref[pl.ds(s0, ATTN_TQ), sl]
            kw = kpad[pl.ds(s0, ATTN_TK), sl]
            vw = vpad[pl.ds(s0, ATTN_TK), sl]
            s = (lax.dot_general(q, kw, (((1,), (1,)), ((), ())), preferred_element_type=F32) * scale
                 + bias_ref[variant, hh])
            m = jnp.max(s, axis=-1, keepdims=True)
            p = jnp.exp(s - m)
            l = jnp.sum(p, axis=-1, keepdims=True)
            o = jnp.dot(p.astype(BF16), vw, preferred_element_type=F32) / l
            o_ref[pl.ds(s0, ATTN_TQ), sl] = o.astype(o_ref.dtype)
            lse_tile = jnp.where(lane == hh, m + jnp.log(l), lse_tile)
        lse_ref[pl.ds(s0, ATTN_TQ), :] = lse_tile
        return carry

    lax.fori_loop(0, tile // ATTN_TQ, body, 0, unroll=4)


def _attn_group(qkv, bias, dil, bsz, seq, *, tile=1024):
    sub = seq // dil
    tile = min(tile, sub)
    nblk = sub // tile
    hb = tile // ATTN_HALF
    last_h = sub // ATTN_HALF - 1
    assert sub % tile == 0 and tile % ATTN_TQ == 0
    cur = lambda which: pl.BlockSpec((None, None, tile, ATTN_OUT), lambda b, r, i: (which, b, i, r))
    prev = lambda which: pl.BlockSpec((None, None, ATTN_HALF, ATTN_OUT),
                                      lambda b, r, i: (which, b, jnp.maximum(i * hb - 1, 0), r))
    nxt = lambda which: pl.BlockSpec((None, None, ATTN_HALF, ATTN_OUT),
                                     lambda b, r, i: (which, b, jnp.minimum((i + 1) * hb, last_h), r))
    return pl.pallas_call(
        functools.partial(_attn_kernel, tile=tile, sub=sub),
        grid=(bsz, dil, nblk),
        in_specs=[cur(0), prev(1), cur(1), nxt(1), prev(2), cur(2), nxt(2),
                  pl.BlockSpec(bias.shape, lambda b, r, i: (0, 0, 0, 0))],
        out_specs=(pl.BlockSpec((None, tile, ATTN_OUT), lambda b, r, i: (b, i, r)),
                   pl.BlockSpec((None, tile, LANES), lambda b, r, i: (b, i, r))),
        out_shape=(jax.ShapeDtypeStruct((bsz, sub, dil * ATTN_OUT), BF16),
                   jax.ShapeDtypeStruct((bsz, sub, dil * LANES), F32)),
        scratch_shapes=[pltpu.VMEM((tile + 2 * ATTN_HALF, ATTN_OUT), BF16)] * 2,
        compiler_params=_cparams(("parallel", "parallel", "parallel")),
        name=f"dilated_attn_d{dil}",
    )(qkv, qkv, qkv, qkv, qkv, qkv, qkv, bias)


CONV_HALO = BF16_ROWS
CONV_RC = 128
CONV_APRON = 8


def _shift_rows(x, s):
    r, c = x.shape
    x3 = x.reshape(r // SUBLANES, SUBLANES, c)
    rot = pltpu.roll(x3, (-s) % SUBLANES, 1)
    sub = lax.broadcasted_iota(jnp.int32, x3.shape, 1)
    if s > 0:
        y = jnp.where(sub < SUBLANES - s, rot, jnp.concatenate([rot[1:], rot[:1]], axis=0))
    else:
        y = jnp.where(sub >= -s, rot, jnp.concatenate([rot[-1:], rot[:-1]], axis=0))
    return y.reshape(r, c)


def _proj_conv_kernel(hp_ref, hc_ref, hn_ref, w_ref, cw_ref, cb_ref, o_ref, *, tm, per_b):
    i = pl.program_id(0)
    first = (i % per_b) == 0
    last = (i % per_b) == per_b - 1
    w = w_ref[...]
    halo_p = jnp.dot(hp_ref[...], w, preferred_element_type=F32)
    halo_n = jnp.dot(hn_ref[...], w, preferred_element_type=F32)
    halo_p = jnp.where(first, 0.0, halo_p)
    halo_n = jnp.where(last, 0.0, halo_n)
    pad = SSD_CONV // 2
    acc = {}

    def project(s):
        acc[s] = jnp.dot(hc_ref[s * MM_ROW_CHUNK:(s + 1) * MM_ROW_CHUNK, :], w, preferred_element_type=F32)

    def tile_rows(lo, hi):
        pieces, r = [], lo
        while r < hi:
            if r < 0:
                src, off, end = halo_p, CONV_HALO + r, min(hi, 0)
            elif r >= tm:
                src, off, end = halo_n, r - tm, hi
            else:
                s = r // MM_ROW_CHUNK
                src, off, end = acc[s], r - s * MM_ROW_CHUNK, min(hi, (s + 1) * MM_ROW_CHUNK)
            pieces.append(src[off:off + end - r])
            r = end
        return pieces[0] if len(pieces) == 1 else jnp.concatenate(pieces, axis=0)

    def conv_chunk(c):
        x = tile_rows(c * CONV_RC - CONV_APRON, (c + 1) * CONV_RC + CONV_APRON)
        tap = lambda off: cw_ref[pad + off:pad + off + 1, :]
        xm, xq = _shift_rows(x, -3), _shift_rows(x, 3)
        f0 = tap(-3) * xm + tap(0) * x + tap(3) * xq + cb_ref[...]
        f1 = tap(-2) * xm + tap(1) * x
        fm = tap(-1) * x + tap(2) * xq
        y = f0 + _shift_rows(f1, 1) + _shift_rows(fm, -1)
        o_ref[c * CONV_RC:(c + 1) * CONV_RC, :] = _silu(y[CONV_APRON:CONV_APRON + CONV_RC]).astype(o_ref.dtype)

    nsub = tm // MM_ROW_CHUNK
    per_sub = MM_ROW_CHUNK // CONV_RC
    project(0)
    for s in range(nsub):
        if s + 1 < nsub:
            project(s + 1)
        for c in range(s * per_sub, (s + 1) * per_sub):
            conv_chunk(c)


def _proj_conv_silu(h, w, cw, cb, seq, *, col0, tm=2048, tn=512):
    m, k = h.shape
    n = cw.shape[1]
    assert col0 % tn == 0
    cb0 = col0 // tn
    per_b = seq // tm
    hb = tm // CONV_HALO
    assert seq % tm == 0 and n % tn == 0 and SSD_CONV == 7
    return pl.pallas_call(
        functools.partial(_proj_conv_kernel, tm=tm, per_b=per_b),
        grid=(m // tm, n // tn),
        in_specs=[
            pl.BlockSpec((CONV_HALO, k), lambda i, j: (jnp.maximum(i * hb - 1, 0), 0)),
            pl.BlockSpec((tm, k), lambda i, j: (i, 0)),
            pl.BlockSpec((CONV_HALO, k), lambda i, j: (jnp.minimum((i + 1) * hb, m // CONV_HALO - 1), 0)),
            pl.BlockSpec((k, tn), lambda i, j: (0, cb0 + j)),
            pl.BlockSpec((SSD_CONV, tn), lambda i, j: (0, j)),
            pl.BlockSpec((1, tn), lambda i, j: (0, j)),
        ],
        out_specs=pl.BlockSpec((tm, tn), lambda i, j: (i, j)),
        out_shape=jax.ShapeDtypeStruct((m, n), BF16),
        compiler_params=_cparams(("parallel", "parallel")),
        name="proj_conv_silu",
    )(h, h, h, w, cw, cb.reshape(1, n))


def _softplus(x):
    return jnp.maximum(x, 0.0) + jnp.log1p(jnp.exp(-jnp.abs(x)))


def _ssd_prep_kernel(dtr_ref, dtb_ref, alog_ref, ccol_ref, rt_ref, esc_ref, wst_ref, etot_ref, *, t):
    half = LANES // 2
    log2e = math.log2(math.e)
    lane = lax.broadcasted_iota(jnp.int32, (t, LANES), 1)
    fwd = lane < SSD_HEADS
    tri = (lax.broadcasted_iota(jnp.int32, (t, t), 1) <= lax.broadcasted_iota(jnp.int32, (t, t), 0)).astype(BF16)
    neg_a = jnp.exp(alog_ref[...])
    for ci in range(dtr_ref.shape[0] // t):
        rs = slice(ci * t, (ci + 1) * t)
        dt = _softplus(dtr_ref[rs, :] + dtb_ref[...])
        la = jnp.where(lane < 2 * SSD_HEADS, -(dt * neg_a), 0.0)
        hi = la.astype(BF16)
        r1 = la - hi.astype(F32)
        mid = r1.astype(BF16)
        lo = (r1 - mid.astype(F32)).astype(BF16)
        packed = (hi.astype(F32) + pltpu.roll(mid.astype(F32), half, 1)).astype(BF16)
        res = jnp.dot(tri, jnp.concatenate([packed, lo], axis=1), preferred_element_type=F32)
        a0 = res[:, :LANES]
        acs = a0 + pltpu.roll(a0, half, 1) + res[:, LANES:]
        exb = acs - la
        ldt = jnp.log(dt)
        tot = acs[t - 1:t, :]
        ccol_ref[rs, :] = jnp.where(fwd, acs, exb) * log2e
        rt_ref[ci] = (jnp.where(fwd, acs - ldt, exb + ldt) * log2e).T
        esc_ref[rs, :] = jnp.exp(jnp.where(fwd, acs, tot - exb))
        wst_ref[rs, :] = jnp.exp(jnp.where(fwd, tot - acs, exb)) * dt
        etot_ref[ci] = jnp.broadcast_to(jnp.exp(tot), etot_ref.shape[1:])


def _split2(v):
    hi = v.astype(BF16)
    return jnp.concatenate([hi, (v - hi.astype(F32)).astype(BF16)], axis=1)


def _head_expand_tables():
    j = np.arange(2 * LANES)[:, None] % LANES
    c = np.arange(SSD_INNER)[None, :]
    fwd = (j == c // SSD_HD)
    bwd = (j == SSD_HEADS + c // SSD_HD)
    return jnp.asarray(fwd, BF16), jnp.asarray(bwd, BF16)


def _ssd_bwd_state_kernel(xs_ref, b_ref, wst_ref, etot_ref, selb_ref, gin_ref, g_sc):
    @pl.when(pl.program_id(0) == 0)
    def _():
        g_sc[...] = jnp.zeros(g_sc.shape, F32)

    cps = etot_ref.shape[1]
    t = wst_ref.shape[1] // cps
    for sc, b in [(sc, b) for sc in reversed(range(cps)) for b in range(xs_ref.shape[0])]:
        rows = slice(sc * t, (sc + 1) * t)
        gin_ref[b, sc] = g_sc[b].astype(gin_ref.dtype)
        both = jnp.concatenate([_split2(wst_ref[b, rows, :]), _split2(etot_ref[b, sc])], axis=0)
        spread = jnp.dot(both, selb_ref[...], preferred_element_type=F32)
        xw = (xs_ref[b, rows, :].astype(F32) * spread[:t]).astype(BF16)
        dec = spread[t:t + 1]
        for g in range(SSD_GROUPS):
            gs = slice(g * SSD_GW, (g + 1) * SSD_GW)
            bm_t = b_ref[b, rows, g * SSD_STATE:(g + 1) * SSD_STATE].T
            st = jnp.dot(bm_t, xw[:, gs], preferred_element_type=F32)
            g_sc[b, :, gs] = g_sc[b, :, gs] * dec[:, gs] + st


def _ssd_main_kernel(xs_ref, b_ref, c_ref, ccol_ref, rt_ref, esc_ref, wst_ref, etot_ref, gin_ref,
                     dsk_ref, self_ref, selb_ref, y_ref, h_sc):
    @pl.when(pl.program_id(1) == 0)
    def _():
        h_sc[...] = jnp.zeros(h_sc.shape, F32)

    t = rt_ref.shape[-1]
    li = lax.broadcasted_iota(jnp.int32, (t, t), 0)
    si = lax.broadcasted_iota(jnp.int32, (t, t), 1)
    mask_f = jnp.where(li >= si, 0.0, MASK_VALUE)
    mask_b = jnp.where(si >= li, 0.0, MASK_VALUE)
    lo_half = lax.broadcasted_iota(jnp.int32, (t, LANES), 1) < SSD_HD

    for sc, g in [(sc, g) for sc in range(rt_ref.shape[0]) for g in range(SSD_GROUPS)]:
        rows = slice(sc * t, (sc + 1) * t)
        ccol = ccol_ref[rows, :]
        r_t = rt_ref[sc]
        esc2 = _split2(esc_ref[rows, :])
        wst2 = _split2(wst_ref[rows, :])
        etot2 = _split2(etot_ref[sc])
        gs = slice(g * SSD_GW, (g + 1) * SSD_GW)
        bm = b_ref[rows, g * SSD_STATE:(g + 1) * SSD_STATE]
        cm = c_ref[rows, g * SSD_STATE:(g + 1) * SSD_STATE]
        cb = lax.dot_general(cm, bm, (((1,), (1,)), ((), ())), preferred_element_type=F32)
        hf = h_sc[:, gs]
        yf_all = jnp.dot(cm, hf.astype(BF16), preferred_element_type=F32)
        yb_all = jnp.dot(cm, gin_ref[sc, :, gs], preferred_element_type=F32)
        e_in = jnp.dot(esc2, self_ref[g], preferred_element_type=F32)
        e_out = jnp.dot(esc2, selb_ref[g], preferred_element_type=F32)
        w_state = jnp.dot(wst2, self_ref[g], preferred_element_type=F32)
        dec = jnp.dot(etot2, self_ref[g], preferred_element_type=F32)[0:1]
        xw = []
        for k in range(SSD_HPG // 2):
            ea = g * SSD_HPG + 2 * k
            sl = slice(g * SSD_GW + k * LANES, g * SSD_GW + (k + 1) * LANES)
            ks = slice(k * LANES, (k + 1) * LANES)
            xp = xs_ref[rows, sl]
            mms = []
            for e in (ea, ea + 1):
                eb = SSD_HEADS + e
                w = (jnp.exp2(ccol[:, e:e + 1] - r_t[e:e + 1, :] + mask_f)
                     + jnp.exp2(r_t[eb:eb + 1, :] - ccol[:, eb:eb + 1] + mask_b))
                mms.append((cb * w).astype(BF16))
            zero = jnp.zeros_like(xp)
            xx = jnp.concatenate([jnp.where(lo_half, xp, zero), jnp.where(lo_half, zero, xp)], axis=0)
            ypair = jnp.dot(jnp.concatenate(mms, axis=1), xx, preferred_element_type=F32)
            xf = xp.astype(F32)
            y_ref[rows, sl] = (ypair + yf_all[:, ks] * e_in[:, ks] + yb_all[:, ks] * e_out[:, ks]
                               + xf * dsk_ref[:, sl])
            xw.append((xf * w_state[:, ks]).astype(BF16))
        st = jnp.dot(bm.T, jnp.concatenate(xw, axis=1), preferred_element_type=F32)
        h_sc[:, gs] = hf * dec + st


def _ssd(conv_out, dt_raw, dtb, alog, dskip, bsz, seq):
    t = SSD_CHUNK
    nc = seq // t
    assert seq % t == 0
    gn = SSD_GROUPS * SSD_STATE
    b_blk = SSD_INNER // gn
    c_blk = b_blk + 1
    cpp = min(8, nc)
    assert nc % cpp == 0
    sub8 = SUBLANES
    const_spec = pl.BlockSpec((1, LANES), lambda b, i: (0, 0))
    rows_spec = pl.BlockSpec((None, cpp * t, LANES), lambda b, i: (b, i, 0))
    row_arr = jax.ShapeDtypeStruct((bsz, seq, LANES), F32)
    ccol, rt, esc, wst, etot = pl.pallas_call(
        functools.partial(_ssd_prep_kernel, t=t),
        grid=(bsz, nc // cpp),
        in_specs=[rows_spec, const_spec, const_spec],
        out_specs=(rows_spec, pl.BlockSpec((None, cpp, LANES, t), lambda b, i: (b, i, 0, 0)), rows_spec, rows_spec,
                   pl.BlockSpec((None, cpp, sub8, LANES), lambda b, i: (b, i, 0, 0))),
        out_shape=(row_arr, jax.ShapeDtypeStruct((bsz, nc, LANES, t), F32), row_arr, row_arr,
                   jax.ShapeDtypeStruct((bsz, nc, sub8, LANES), F32)),
        compiler_params=_cparams(("parallel", "parallel")),
        name="ssd_decay_terms",
    )(dt_raw, dtb, alog)

    cps = next(n for n in (4, 2, 1) if nc % n == 0)
    rev = lambda c: nc // cps - 1 - c
    sel_f, sel_b = _head_expand_tables()
    by_group = lambda a: a.reshape(a.shape[0], SSD_GROUPS, SSD_GW).transpose(1, 0, 2)
    sel_spec = pl.BlockSpec((SSD_GROUPS, sel_f.shape[0], SSD_GW), lambda b, c: (0, 0, 0))
    gin = pl.pallas_call(
        _ssd_bwd_state_kernel,
        grid=(nc // cps,),
        in_specs=[
            pl.BlockSpec((bsz, cps * t, SSD_INNER), lambda c: (0, rev(c), 0)),
            pl.BlockSpec((bsz, cps * t, gn), lambda c: (0, rev(c), b_blk)),
            pl.BlockSpec((bsz, cps * t, LANES), lambda c: (0, rev(c), 0)),
            pl.BlockSpec((bsz, cps, sub8, LANES), lambda c: (0, rev(c), 0, 0)),
            pl.BlockSpec(sel_b.shape, lambda c: (0, 0)),
        ],
        out_specs=pl.BlockSpec((bsz, cps, SSD_STATE, SSD_INNER), lambda c: (0, rev(c), 0, 0)),
        out_shape=jax.ShapeDtypeStruct((bsz, nc, SSD_STATE, SSD_INNER), BF16),
        scratch_shapes=[pltpu.VMEM((bsz, SSD_STATE, SSD_INNER), F32)],
        compiler_params=_cparams(("arbitrary",)),
        name="ssd_bwd_states",
    )(conv_out, conv_out, wst, etot, sel_b)

    wide_spec = pl.BlockSpec((1, SSD_INNER), lambda b, c: (0, 0))
    tok_spec = pl.BlockSpec((None, cps * t, LANES), lambda b, c: (b, c, 0))
    return pl.pallas_call(
        _ssd_main_kernel,
        grid=(bsz, nc // cps),
        in_specs=[
            pl.BlockSpec((None, cps * t, SSD_INNER), lambda b, c: (b, c, 0)),
            pl.BlockSpec((None, cps * t, gn), lambda b, c: (b, c, b_blk)),
            pl.BlockSpec((None, cps * t, gn), lambda b, c: (b, c, c_blk)),
            tok_spec,
            pl.BlockSpec((None, cps, LANES, t), lambda b, c: (b, c, 0, 0)),
            tok_spec, tok_spec,
            pl.BlockSpec((None, cps, sub8, LANES), lambda b, c: (b, c, 0, 0)),
            pl.BlockSpec((None, cps, SSD_STATE, SSD_INNER), lambda b, c: (b, c, 0, 0)),
            wide_spec, sel_spec, sel_spec,
        ],
        out_specs=pl.BlockSpec((None, cps * t, SSD_INNER), lambda b, c: (b, c, 0)),
        out_shape=jax.ShapeDtypeStruct((bsz, seq, SSD_INNER), F32),
        scratch_shapes=[pltpu.VMEM((SSD_STATE, SSD_INNER), F32)],
        compiler_params=_cparams(("parallel", "arbitrary")),
        name="ssd_main",
    )(conv_out, conv_out, conv_out, ccol, rt, esc, wst, etot, gin, dskip, by_group(sel_f), by_group(sel_b))


def _dft_tables(seq):
    n1, n2 = seq // F_N2, F_N2
    c = np.arange(F_GD)
    ang = 2 * np.pi * np.outer(c, c) / F_GD
    chan = np.concatenate([np.cos(ang), -np.sin(ang)], axis=1) / math.sqrt(F_GD)
    k1 = np.arange(n1)
    a1 = 2 * np.pi * np.outer(k1, k1) / n1
    stage_a = np.block([[np.cos(a1), np.sin(a1)], [-np.sin(a1), np.cos(a1)]])
    stage_a = np.kron(stage_a, np.eye(2))
    s2 = np.arange(n2)
    at = 2 * np.pi * np.outer(s2, k1) / seq
    at = at.reshape(n2 // 2, 2, n1).transpose(0, 2, 1).reshape(n2 // 2, 2 * n1)
    tw_c, tw_s = np.cos(at)[..., None], np.sin(at)[..., None]
    a2 = 2 * np.pi * np.outer(s2, s2) / n2
    stage_b = np.concatenate([np.cos(a2), np.sin(a2)], axis=1)
    return (jnp.asarray(chan, BF16), jnp.asarray(stage_a, BF16), jnp.asarray(tw_c, F32),
            jnp.asarray(tw_s, F32), jnp.asarray(stage_b, BF16))


def _fft_a_kernel(z_ref, ma_ref, twc_ref, tws_ref, o_ref, zs, ys, *, n1, rb):
    ct = z_ref.shape[-1]
    half = rb // 2
    _put_rows(zs, pltpu.bitcast(z_ref[...].reshape(2 * n1 * rb, ct), jnp.uint32))
    for j in range(half):
        zz = pltpu.bitcast(_get_rows(zs, j, 2 * n1, half), BF16)
        y = jnp.dot(ma_ref[...], zz, preferred_element_type=F32)
        yr, yi = y[:2 * n1], y[2 * n1:]
        tc, ts = twc_ref[j], tws_ref[j]
        out = jnp.concatenate([yr * tc + yi * ts, yi * tc - yr * ts], axis=0).astype(BF16)
        _put_rows(ys, pltpu.bitcast(out, jnp.uint32), j, half)
    o_ref[...] = pltpu.bitcast(_get_rows(ys), BF16).reshape(2, n1, rb, ct)


def _fft_b_kernel(y_ref, mb_ref, o_ref, os_ref, *, n2, kb, scale):
    ct = y_ref.shape[-1]
    for kk in range(kb):
        yy = y_ref[:, kk].reshape(2 * n2, ct)
        _put_rows(os_ref, jnp.dot(mb_ref[...], yy, preferred_element_type=F32) * scale, kk, kb)
    o_ref[...] = _get_rows(os_ref).reshape(n2, kb, ct).astype(o_ref.dtype)


def _fourier_seq(z2, tabs, bsz, seq, *, rb=BF16_ROWS, kb=BF16_ROWS, ct=1536):
    _, ma, twc, tws, mb = tabs
    n1, n2 = seq // F_N2, F_N2
    w = z2.shape[-1]
    nct = w // ct
    za = z2.reshape(2, bsz, n1, n2, w)
    ab_spec = pl.BlockSpec((2, None, n1, rb, ct), lambda b, a, j: (0, b, 0, a, j))
    ya = pl.pallas_call(
        functools.partial(_fft_a_kernel, n1=n1, rb=rb),
        grid=(bsz, n2 // rb, nct),
        in_specs=[
            ab_spec,
            pl.BlockSpec((4 * n1, 4 * n1), lambda b, a, j: (0, 0)),
            pl.BlockSpec((rb // 2, 2 * n1, 1), lambda b, a, j: (a, 0, 0)),
            pl.BlockSpec((rb // 2, 2 * n1, 1), lambda b, a, j: (a, 0, 0)),
        ],
        out_specs=ab_spec,
        out_shape=jax.ShapeDtypeStruct((2, bsz, n1, n2, w), BF16),
        scratch_shapes=[pltpu.VMEM((ct // LANES, n1 * rb, LANES), jnp.uint32)] * 2,
        compiler_params=_cparams(("parallel", "parallel", "parallel")),
        name="fourier_stage_a",
    )(za, ma, twc, tws)
    out = pl.pallas_call(
        functools.partial(_fft_b_kernel, n2=n2, kb=kb, scale=1.0 / math.sqrt(seq)),
        grid=(bsz, n1 // kb, nct),
        in_specs=[
            pl.BlockSpec((2, None, kb, n2, ct), lambda b, k, j: (0, b, k, 0, j)),
            pl.BlockSpec((n2, 2 * n2), lambda b, k, j: (0, 0)),
        ],
        out_specs=pl.BlockSpec((None, n2, kb, ct), lambda b, k, j: (b, 0, k, j)),
        out_shape=jax.ShapeDtypeStruct((bsz, n2, n1, w), BF16),
        scratch_shapes=[pltpu.VMEM((ct // LANES, n2 * kb, LANES), F32)],
        compiler_params=_cparams(("parallel", "parallel", "parallel")),
        name="fourier_stage_b",
    )(ya, mb)
    return out.reshape(bsz, seq, w)


def _merge_kernel(*refs, dils):
    ng = len(dils)
    x_ref = refs[0]
    o_refs = refs[1:1 + ng]
    l_refs = refs[1 + ng:1 + 2 * ng]
    y_ref, z_ref, yg_ref, f_ref, gt_ref, wa_ref, ws_ref, wf_ref, wo_ref, out_ref, o_sc, l_sc = refs[1 + 2 * ng:]
    tm, d = x_ref.shape

    yf = jnp.dot(f_ref[...], wf_ref[...], preferred_element_type=F32)

    ys = None
    for g in range(SSD_GROUPS):
        gs = slice(g * SSD_GW, (g + 1) * SSD_GW)
        yy = y_ref[:, gs] * _silu(z_ref[:, gs].astype(F32))
        part = jnp.dot(_rms(yy, yg_ref[:, gs]).astype(BF16), ws_ref[gs, :], preferred_element_type=F32)
        ys = part if ys is None else ys + part

    for gi, dil in enumerate(dils):
        rows = tm // dil
        idx = lambda r: slice(None) if dil == 1 else pl.ds(r, rows, stride=dil)
        for r in range(dil):
            l_sc[gi, idx(r), :] = l_refs[gi][:, r * LANES:(r + 1) * LANES]
            for hh in range(ATTN_HPG):
                c0 = r * ATTN_OUT + hh * ATTN_HD
                o_sc[gi * ATTN_HPG + hh, idx(r), :] = o_refs[gi][:, c0:c0 + ATTN_HD].astype(F32)

    ls = [l_sc[gi] for gi in range(ng)]
    mx = functools.reduce(jnp.maximum, ls)
    ws = [jnp.exp(v - mx) for v in ls]
    inv = 1.0 / functools.reduce(lambda a, b: a + b, ws)
    heads = []
    for hh in range(ATTN_HPG):
        acc = None
        for gi in range(ng):
            alpha = (ws[gi] * inv)[:, hh:hh + 1]
            term = alpha * o_sc[gi * ATTN_HPG + hh]
            acc = term if acc is None else acc + term
        heads.append(acc.astype(BF16))
    o_attn = jnp.concatenate(heads, axis=1)

    ya = jnp.dot(o_attn, wa_ref[...], preferred_element_type=F32)
    gates = gt_ref[...].astype(F32)
    merged = gates[:, :d] * ya + gates[:, d:2 * d] * ys + gates[:, 2 * d:] * yf
    out_ref[...] = x_ref[...] + jnp.dot(merged.astype(BF16), wo_ref[...], preferred_element_type=F32)


def _merge(x, os_, ls_, dils, y, z, y_gain, f, gates, wa, ws, wf, wo, bsz, seq, *, tm=512):
    d = x.shape[-1]
    assert all((tm // dil) % BF16_ROWS == 0 for dil in dils)
    tok = lambda width: pl.BlockSpec((None, tm, width), lambda b, i: (b, i, 0))
    full = lambda a: pl.BlockSpec(a.shape, lambda b, i: (0, 0), pipeline_mode=pl.Buffered(1))
    ng = len(dils)
    in_specs = [tok(d)]
    in_specs += [pl.BlockSpec((None, tm // dil, dil * ATTN_OUT), lambda b, i: (b, i, 0)) for dil in dils]
    in_specs += [pl.BlockSpec((None, tm // dil, dil * LANES), lambda b, i: (b, i, 0)) for dil in dils]
    in_specs += [tok(y.shape[-1]), tok(z.shape[-1]), full(y_gain), tok(f.shape[-1]), tok(gates.shape[-1]),
                 full(wa), full(ws), full(wf), full(wo)]
    return pl.pallas_call(
        functools.partial(_merge_kernel, dils=tuple(dils)),
        grid=(bsz, seq // tm),
        in_specs=in_specs,
        out_specs=tok(d),
        out_shape=jax.ShapeDtypeStruct((bsz, seq, d), F32),
        scratch_shapes=[pltpu.VMEM((ng * ATTN_HPG, tm, ATTN_HD), F32), pltpu.VMEM((ng, tm, LANES), F32)],
        compiler_params=_cparams(("parallel", "parallel")),
        name="branch_merge",
    )(x, *os_, *ls_, y, z, y_gain, f, gates, wa, ws, wf, wo)


def _xattn_kernel(x_ref, g_ref, wq_ref, qg_ref, k_ref, v_ref, wo_ref, out_ref):
    tm, d = x_ref.shape
    hd = d // MEM_HEADS
    scale = hd ** -0.5
    rc = min(tm, XATTN_ROW_CHUNK)
    for c in range(tm // rc):
        rs = slice(c * rc, (c + 1) * rc)
        x = x_ref[rs, :]
        h = _rms(x, g_ref[...]).astype(BF16)
        q = jnp.dot(h, wq_ref[...], preferred_element_type=F32)
        heads = []
        for hh in range(MEM_HEADS):
            sl = slice(hh * hd, (hh + 1) * hd)
            qn = _rms(q[:, sl], qg_ref[...]).astype(BF16)
            s = lax.dot_general(qn, k_ref[:, sl], (((1,), (1,)), ((), ())), preferred_element_type=F32) * scale
            m = jnp.max(s, axis=-1, keepdims=True)
            p = jnp.exp(s - m)
            l = jnp.sum(p, axis=-1, keepdims=True)
            heads.append((jnp.dot(p.astype(BF16), v_ref[:, sl], preferred_element_type=F32) / l).astype(BF16))
        o = jnp.concatenate(heads, axis=1)
        out_ref[rs, :] = x + jnp.dot(o, wo_ref[...], preferred_element_type=F32)


def _xattn(x, g, wq, qg, k, v, wo, *, tm=512):
    bsz, seq, d = x.shape
    mt = k.shape[1]
    full = lambda a: pl.BlockSpec(a.shape, lambda b, i: (0, 0))
    g = g.reshape(1, d)
    qg = qg.reshape(1, -1)
    return pl.pallas_call(
        _xattn_kernel,
        grid=(bsz, seq // tm),
        in_specs=[
            pl.BlockSpec((None, tm, d), lambda b, i: (b, i, 0)),
            full(g), full(wq), full(qg),
            pl.BlockSpec((None, mt, d), lambda b, i: (b, 0, 0)),
            pl.BlockSpec((None, mt, d), lambda b, i: (b, 0, 0)),
            full(wo),
        ],
        out_specs=pl.BlockSpec((None, tm, d), lambda b, i: (b, i, 0)),
        out_shape=jax.ShapeDtypeStruct((bsz, seq, d), F32),
        compiler_params=_cparams(("parallel", "parallel")),
        name="mem_xattn",
    )(x, g, wq, qg, k, v, wo)


def _ffn_kernel(x_ref, g_ref, wg_ref, wu_ref, wd_ref, *rest, tf):
    x = x_ref[...]
    h = _rms(x, g_ref[...]).astype(BF16)
    acc = x
    for c in range(wg_ref.shape[1] // tf):
        sl = slice(c * tf, (c + 1) * tf)
        gt = jnp.dot(h, wg_ref[:, sl], preferred_element_type=F32)
        up = jnp.dot(h, wu_ref[:, sl], preferred_element_type=F32)
        a = (_silu(gt) * up).astype(BF16)
        acc = acc + jnp.dot(a, wd_ref[sl, :], preferred_element_type=F32)
    if len(rest) == 1:
        rest[0][...] = acc
    else:
        gn_ref, out_ref, hn_ref = rest
        out_ref[...] = acc
        hn_ref[...] = _rms(acc, gn_ref[...]).astype(hn_ref.dtype)


def _ffn(x, g, wg, wu, wd, next_gain=None, *, tm=512):
    m, d = x.shape
    dff = wg.shape[1]
    tf = dff // 2 if (dff // 2) % LANES == 0 else dff
    full = lambda a: pl.BlockSpec(a.shape, lambda i: (0, 0), pipeline_mode=pl.Buffered(1))
    tok = pl.BlockSpec((tm, d), lambda i: (i, 0))
    args = [x, g.reshape(1, d), wg, wu, wd]
    out_shape, out_specs = jax.ShapeDtypeStruct((m, d), F32), tok
    if next_gain is not None:
        args.append(next_gain.reshape(1, d))
        out_shape, out_specs = (out_shape, jax.ShapeDtypeStruct((m, d), BF16)), (tok, tok)
    return pl.pallas_call(
        functools.partial(_ffn_kernel, tf=tf),
        grid=(m // tm,),
        in_specs=[tok] + [full(a) for a in args[1:]],
        out_specs=out_specs,
        out_shape=out_shape,
        compiler_params=_cparams(("parallel",)),
        name="swiglu_ffn",
    )(*args)


def _pad_lanes(a):
    a = a.reshape(-1, 2 * SSD_HEADS)
    return jnp.pad(a, ((0, 0), (0, LANES - 2 * SSD_HEADS)))


def kernel(x, mem, rel_bias, mix_norm_g, w_in, gate_bias, attn_q_norm_g, attn_k_norm_g, conv_w, conv_b, dt_bias, a_log, d_skip, ssd_norm_g, w_branch_attn, w_branch_ssd, w_branch_fourier, w_mix_out, xattn_norm_g, mem_norm_g, w_xq, w_xk, w_xv, xattn_q_norm_g, xattn_k_norm_g, w_xo, ffn_norm_g, w_ffn_gate, w_ffn_up, w_ffn_down):
    bsz, seq, d = x.shape
    depth = w_in.shape[0]
    m = bsz * seq
    xf = x.reshape(m, d)
    memf = mem.reshape(bsz * mem.shape[1], d)
    tabs = _dft_tables(seq)
    dils = [dil for _, dil in ATTN_GROUPS]
    biases = [_attn_bias(rel_bias, gi, dil) for gi, dil in enumerate(dils)]
    offs = np.cumsum([0, ATTN_WIDTH, ATTN_WIDTH, ATTN_WIDTH, SSD_INNER, SSD_CONV_CH, 2 * SSD_HEADS, F_WIDTH, 3 * d])
    bf = lambda a: a.astype(BF16)

    for l in range(depth):
        wl = bf(w_in[l])
        seg = lambda i: wl[:, offs[i]:offs[i + 1]]
        if l == 0:
            h = _rmsnorm(xf, mix_norm_g[l])
        gains = jnp.stack([attn_q_norm_g[l], attn_k_norm_g[l], jnp.ones_like(attn_q_norm_g[l])]).reshape(3, 1, ATTN_HD)
        os_, ls_ = [], []
        for gi, dil in enumerate(dils):
            qkv = _qkv_group(h, wl, gains, gi, dil, bsz, seq)
            o_g, lse_g = _attn_group(qkv, biases[gi], dil, bsz, seq)
            os_.append(o_g)
            ls_.append(lse_g)

        z = _matmul(h, wl, out_dtype=BF16, tn=512, tm=2048, col0=int(offs[3]), n=SSD_INNER)
        dt_raw = _matmul(h, _pad_lanes(seg(5)), out_dtype=F32, tn=LANES, tm=2048)
        z2 = _matmul(h, seg(6), out_dtype=BF16, tn=768, tm=2048, epilogue="chandft", extra=tabs[0],
                     row_chunk=2 * MM_ROW_CHUNK)
        gates = _matmul(h, seg(7), out_dtype=BF16, tn=1024, tm=2048, epilogue="sigmoid_bias", extra=gate_bias[l])

        conv_out = _proj_conv_silu(h, wl, conv_w[l], conv_b[l], seq, col0=int(offs[4])).reshape(bsz, seq, SSD_CONV_CH)
        y_ssd = _ssd(conv_out, dt_raw.reshape(bsz, seq, -1), _pad_lanes(dt_bias[l]), _pad_lanes(a_log[l]),
                     jnp.repeat(d_skip[l], SSD_HD).reshape(1, SSD_INNER), bsz, seq)

        f_re = _fourier_seq(z2, tabs, bsz, seq)

        x3 = _merge(xf.reshape(bsz, seq, d), os_, ls_, dils, y_ssd, z.reshape(bsz, seq, SSD_INNER),
                    ssd_norm_g[l].reshape(1, SSD_INNER), f_re, gates.reshape(bsz, seq, 3 * d),
                    bf(w_branch_attn[l]), bf(w_branch_ssd[l]), bf(w_branch_fourier[l]), bf(w_mix_out[l]), bsz, seq)

        hd_m = d // MEM_HEADS
        hm = _rmsnorm(memf, mem_norm_g[l])
        km = _matmul(hm, bf(w_xk[l]), out_dtype=BF16, tn=512, epilogue="headnorm",
                     extra=xattn_k_norm_g[l], head_dim=hd_m)
        vm = _matmul(hm, bf(w_xv[l]), out_dtype=BF16, tn=512)
        x3 = _xattn(x3, xattn_norm_g[l], bf(w_xq[l]), xattn_q_norm_g[l], km.reshape(bsz, -1, d),
                    vm.reshape(bsz, -1, d), bf(w_xo[l]))

        ffn_args = (x3.reshape(m, d), ffn_norm_g[l], bf(w_ffn_gate[l]), bf(w_ffn_up[l]), bf(w_ffn_down[l]))
        if l + 1 < depth:
            xf, h = _ffn(*ffn_args, mix_norm_g[l + 1])
        else:
            xf = _ffn(*ffn_args)

    return xf.reshape(bsz, seq, d)
```

```python
import functools
import math

import numpy as np
import jax
import jax.numpy as jnp
from jax import lax
from jax.experimental import pallas as pl
from jax.experimental.pallas import tpu as pltpu

F32 = jnp.float32
BF16 = jnp.bfloat16

NORM_EPS = 1e-6
MASK_VALUE = -1e30

ATTN_GROUPS = ((128, 1), (512, 4), (2048, 16))
ATTN_HPG = 4
ATTN_HD = 128
ATTN_HEADS = ATTN_HPG * len(ATTN_GROUPS)
ATTN_WIDTH = ATTN_HEADS * ATTN_HD
ATTN_OUT = ATTN_HPG * ATTN_HD
ATTN_HALF = 64
ATTN_TQ = 2 * ATTN_HALF
ATTN_TK = ATTN_TQ + 2 * ATTN_HALF
NUM_BUCKETS = 32
MAX_DISTANCE = 1024

SSD_HEADS = 32
SSD_HD = 64
SSD_GROUPS = 4
SSD_HPG = SSD_HEADS // SSD_GROUPS
SSD_STATE = 128
SSD_INNER = SSD_HEADS * SSD_HD
SSD_GW = SSD_INNER // SSD_GROUPS
SSD_CONV = 7
SSD_CHUNK = 128
SSD_CONV_CH = SSD_INNER + 2 * SSD_GROUPS * SSD_STATE
LANES = 128
SUBLANES = 8
BF16_ROWS = 16

F_GROUPS = 6
F_GD = 256
F_WIDTH = F_GROUPS * F_GD
F_N2 = 128

MEM_HEADS = 4
XATTN_ROW_CHUNK = 256

VMEM_LIMIT = 56 * 1024 * 1024


def _cparams(sem):
    return pltpu.CompilerParams(dimension_semantics=sem, vmem_limit_bytes=VMEM_LIMIT)


def _silu(x):
    return x * jax.nn.sigmoid(x)


def _rms(x, g):
    ms = jnp.mean(x * x, axis=-1, keepdims=True)
    return x * lax.rsqrt(ms + NORM_EPS) * g


def _get_rows(ref3, start=0, size=None, stride=1):
    idx = slice(None) if size is None else pl.ds(start, size, stride=stride)
    return jnp.concatenate([ref3[c, idx, :] for c in range(ref3.shape[0])], axis=1)


def _put_rows(ref3, val, start=0, stride=1):
    size = val.shape[0]
    idx = slice(None) if (stride == 1 and size == ref3.shape[1]) else pl.ds(start, size, stride=stride)
    for c in range(ref3.shape[0]):
        ref3[c, idx, :] = val[:, c * LANES:(c + 1) * LANES]


def _rmsnorm_kernel(x_ref, g_ref, o_ref):
    o_ref[...] = _rms(x_ref[...], g_ref[...]).astype(o_ref.dtype)


def _rmsnorm(x, g, *, tm=1024):
    m, k = x.shape
    tm = min(tm, m)
    return pl.pallas_call(
        _rmsnorm_kernel,
        grid=(m // tm,),
        in_specs=[pl.BlockSpec((tm, k), lambda i: (i, 0)), pl.BlockSpec((1, k), lambda i: (0, 0))],
        out_specs=pl.BlockSpec((tm, k), lambda i: (i, 0)),
        out_shape=jax.ShapeDtypeStruct((m, k), BF16),
        compiler_params=_cparams(("parallel",)),
        name="rmsnorm",
    )(x, g.reshape(1, k))


MM_ROW_CHUNK = 256


def _mm_kernel(*refs, epilogue, head_dim, row_chunk):
    h_ref, w_ref = refs[:2]
    o_ref = refs[-1]
    tm = h_ref.shape[0]
    rc = min(row_chunk, tm)
    for c in range(tm // rc):
        rs = slice(c * rc, (c + 1) * rc)
        acc = jnp.dot(h_ref[rs, :], w_ref[...], preferred_element_type=F32)
        tn = acc.shape[1]
        if epilogue == "headnorm":
            hg_ref = refs[2]
            for hh in range(tn // head_dim):
                sl = slice(hh * head_dim, (hh + 1) * head_dim)
                o_ref[rs, sl] = _rms(acc[:, sl], hg_ref[...]).astype(o_ref.dtype)
        elif epilogue == "sigmoid_bias":
            b_ref = refs[2]
            o_ref[rs, :] = jax.nn.sigmoid(acc + b_ref[...]).astype(o_ref.dtype)
        elif epilogue == "chandft":
            dft_ref = refs[2]
            a16 = acc.astype(BF16)
            for gg in range(tn // F_GD):
                sl = slice(gg * F_GD, (gg + 1) * F_GD)
                r = jnp.dot(a16[:, sl], dft_ref[...], preferred_element_type=F32)
                o_ref[0, rs, sl] = r[:, :F_GD].astype(o_ref.dtype)
                o_ref[1, rs, sl] = r[:, F_GD:].astype(o_ref.dtype)
        else:
            o_ref[rs, :] = acc.astype(o_ref.dtype)


def _matmul(h, w, *, out_dtype, tn, tm=1024, epilogue="plain", extra=None, head_dim=None, col0=0, n=None,
            row_chunk=MM_ROW_CHUNK):
    m, k = h.shape
    n = w.shape[1] if n is None else n
    tm = min(tm, m)
    assert m % tm == 0 and n % tn == 0 and col0 % tn == 0
    cb0 = col0 // tn
    in_specs = [pl.BlockSpec((tm, k), lambda i, j: (i, 0)), pl.BlockSpec((k, tn), lambda i, j: (0, cb0 + j))]
    args = [h, w]
    if epilogue == "headnorm":
        in_specs.append(pl.BlockSpec((1, head_dim), lambda i, j: (0, 0)))
        args.append(extra.reshape(1, head_dim))
    elif epilogue == "sigmoid_bias":
        in_specs.append(pl.BlockSpec((1, tn), lambda i, j: (0, j)))
        args.append(extra.reshape(1, n))
    elif epilogue == "chandft":
        in_specs.append(pl.BlockSpec(extra.shape, lambda i, j: (0, 0)))
        args.append(extra)
    if epilogue == "chandft":
        out_shape = jax.ShapeDtypeStruct((2, m, n), out_dtype)
        out_spec = pl.BlockSpec((2, tm, tn), lambda i, j: (0, i, j))
    else:
        out_shape = jax.ShapeDtypeStruct((m, n), out_dtype)
        out_spec = pl.BlockSpec((tm, tn), lambda i, j: (i, j))
    return pl.pallas_call(
        functools.partial(_mm_kernel, epilogue=epilogue, head_dim=head_dim, row_chunk=row_chunk),
        grid=(m // tm, n // tn),
        in_specs=in_specs,
        out_specs=out_spec,
        out_shape=out_shape,
        compiler_params=_cparams(("parallel", "parallel")),
        name="matmul_" + epilogue,
    )(*args)


def _qkv_kernel(h_ref, w_ref, hg_ref, o_ref, *scratch, dil):
    j = pl.program_id(1)
    tm = h_ref.shape[0]
    width = w_ref.shape[1]
    nh = width // ATTN_HD
    rc = min(max(MM_ROW_CHUNK, 2 * BF16_ROWS * dil), tm)
    rows = rc // dil
    for c in range(tm // rc):
        acc = jnp.dot(h_ref[c * rc:(c + 1) * rc, :], w_ref[...], preferred_element_type=F32)
        if dil > 1:
            scr = scratch[0]
            for hh in range(nh):
                scr[hh, c * rc:(c + 1) * rc, :] = acc[:, hh * ATTN_HD:(hh + 1) * ATTN_HD]
        for r in range(dil):
            for hh in range(nh):
                if dil == 1:
                    ph = acc[:, hh * ATTN_HD:(hh + 1) * ATTN_HD]
                else:
                    ph = scr[hh, pl.ds(c * rc + r, rows, stride=dil), :]
                val = jnp.where(j < 2, _rms(ph, hg_ref[...]), ph)
                c0 = r * width + hh * ATTN_HD
                o_ref[c * rows:(c + 1) * rows, c0:c0 + ATTN_HD] = val.astype(o_ref.dtype)


def _qkv_group(h, w, gains, gi, dil, bsz, seq, *, tm=2048):
    m, k = h.shape
    sub = seq // dil
    per_b = seq // tm
    rows = tm // dil
    assert seq % tm == 0 and rows % BF16_ROWS == 0
    return pl.pallas_call(
        functools.partial(_qkv_kernel, dil=dil),
        grid=(m // tm, 3),
        in_specs=[
            pl.BlockSpec((tm, k), lambda i, j: (i, 0)),
            pl.BlockSpec((k, ATTN_OUT), lambda i, j: (0, j * len(ATTN_GROUPS) + gi)),
            pl.BlockSpec((None, 1, ATTN_HD), lambda i, j: (j, 0, 0)),
        ],
        out_specs=pl.BlockSpec((None, None, rows, dil * ATTN_OUT), lambda i, j: (j, i // per_b, i % per_b, 0)),
        out_shape=jax.ShapeDtypeStruct((3, bsz, sub, dil * ATTN_OUT), BF16),
        scratch_shapes=[pltpu.VMEM((ATTN_HPG, tm, ATTN_HD), F32)] if dil > 1 else [],
        compiler_params=_cparams(("parallel", "parallel")),
        name=f"qkv_proj_d{dil}",
    )(h, w, gains)


def _t5_bucket_np(rel):
    half_b = NUM_BUCKETS // 2
    exact = half_b // 2
    dist = np.abs(rel)
    log_ratio = np.log(np.maximum(dist, 1) / exact) / np.log(MAX_DISTANCE / exact)
    far = np.minimum(exact + (log_ratio * (half_b - exact)).astype(np.int32), half_b - 1)
    return np.where(rel > 0, half_b, 0) + np.where(dist < exact, dist, far)


def _attn_bias(rel_bias, gi, dil):
    i = np.arange(ATTN_TQ)[:, None]
    j = np.arange(ATTN_TK)[None, :]
    rel = j - ATTN_HALF - i
    idx = _t5_bucket_np(rel * dil)
    onehot = jnp.asarray(np.eye(NUM_BUCKETS, dtype=np.float32)[idx])
    tab = rel_bias[:, gi * ATTN_HPG:(gi + 1) * ATTN_HPG].astype(F32)
    b = jnp.einsum("qkn,nh->hqk", onehot, tab, precision=lax.Precision.HIGHEST)
    band = np.broadcast_to(np.abs(rel) <= ATTN_HALF, (ATTN_TQ, ATTN_TK))
    keep = np.stack([band & ((j >= ATTN_HALF) | (not first)) & ((j < ATTN_TK - ATTN_HALF) | (not last))
                     for last in (False, True) for first in (False, True)])
    return jnp.where(jnp.asarray(keep)[:, None], b[None], MASK_VALUE)


def _attn_kernel(q_ref, kp_ref, kc_ref, kn_ref, vp_ref, vc_ref, vn_ref, bias_ref, o_ref, lse_ref, kpad, vpad,
                 *, tile, sub):
    i = pl.program_id(2)
    hf = ATTN_HALF
    kpad[0:hf, :] = kp_ref[...]
    kpad[hf:hf + tile, :] = kc_ref[...]
    kpad[hf + tile:, :] = kn_ref[...]
    vpad[0:hf, :] = vp_ref[...]
    vpad[hf:hf + tile, :] = vc_ref[...]
    vpad[hf + tile:, :] = vn_ref[...]

    scale = ATTN_HD ** -0.5
    lane = lax.broadcasted_iota(jnp.int32, (ATTN_TQ, LANES), 1)
    tiles = tile // ATTN_TQ

    def body(t, carry):
        s0 = pl.multiple_of(t * ATTN_TQ, ATTN_TQ)
        gt = i * tiles + t
        variant = (gt == 0).astype(jnp.int32) + 2 * (gt == sub // ATTN_TQ - 1).astype(jnp.int32)
        lse_tile = jnp.zeros((ATTN_TQ, LANES), F32)
        for hh in range(ATTN_HPG):
            sl = slice(hh * ATTN_HD, (hh + 1) * ATTN_HD)
            q = q_ref[pl.ds(s0, ATTN_TQ), sl]
            kw = kpad[pl.ds(s0, ATTN_TK), sl]
            vw = vpad[pl.ds(s0, ATTN_TK), sl]
            s = (lax.dot_general(q, kw, (((1,), (1,)), ((), ())), preferred_element_type=F32) * scale
                 + bias_ref[variant, hh])
            m = jnp.max(s, axis=-1, keepdims=True)
            p = jnp.exp(s - m)
            l = jnp.sum(p, axis=-1, keepdims=True)
            o = jnp.dot(p.astype(BF16), vw, preferred_element_type=F32) / l
            o_ref[pl.ds(s0, ATTN_TQ), sl] = o.astype(o_ref.dtype)
            lse_tile = jnp.where(lane == hh, m + jnp.log(l), lse_tile)
        lse_ref[pl.ds(s0, ATTN_TQ), :] = lse_tile
        return carry

    lax.fori_loop(0, tile // ATTN_TQ, body, 0, unroll=4)


def _attn_group(qkv, bias, dil, bsz, seq, *, tile=1024):
    sub = seq // dil
    tile = min(tile, sub)
    nblk = sub // tile
    hb = tile // ATTN_HALF
    last_h = sub // ATTN_HALF - 1
    assert sub % tile == 0 and tile % ATTN_TQ == 0
    cur = lambda which: pl.BlockSpec((None, None, tile, ATTN_OUT), lambda b, r, i: (which, b, i, r))
    prev = lambda which: pl.BlockSpec((None, None, ATTN_HALF, ATTN_OUT),
                                      lambda b, r, i: (which, b, jnp.maximum(i * hb - 1, 0), r))
    nxt = lambda which: pl.BlockSpec((None, None, ATTN_HALF, ATTN_OUT),
                                     lambda b, r, i: (which, b, jnp.minimum((i + 1) * hb, last_h), r))
    return pl.pallas_call(
        functools.partial(_attn_kernel, tile=tile, sub=sub),
        grid=(bsz, dil, nblk),
        in_specs=[cur(0), prev(1), cur(1), nxt(1), prev(2), cur(2), nxt(2),
                  pl.BlockSpec(bias.shape, lambda b, r, i: (0, 0, 0, 0))],
        out_specs=(pl.BlockSpec((None, tile, ATTN_OUT), lambda b, r, i: (b, i, r)),
                   pl.BlockSpec((None, tile, LANES), lambda b, r, i: (b, i, r))),
        out_shape=(jax.ShapeDtypeStruct((bsz, sub, dil * ATTN_OUT), BF16),
                   jax.ShapeDtypeStruct((bsz, sub, dil * LANES), F32)),
        scratch_shapes=[pltpu.VMEM((tile + 2 * ATTN_HALF, ATTN_OUT), BF16)] * 2,
        compiler_params=_cparams(("parallel", "parallel", "parallel")),
        name=f"dilated_attn_d{dil}",
    )(qkv, qkv, qkv, qkv, qkv, qkv, qkv, bias)


CONV_HALO = BF16_ROWS
CONV_RC = 128
CONV_APRON = 8
CONV_PROJ_ROWS = 512


def _shift_rows(x, s):
    r, c = x.shape
    x3 = x.reshape(r // SUBLANES, SUBLANES, c)
    rot = pltpu.roll(x3, (-s) % SUBLANES, 1)
    sub = lax.broadcasted_iota(jnp.int32, x3.shape, 1)
    if s > 0:
        y = jnp.where(sub < SUBLANES - s, rot, jnp.concatenate([rot[1:], rot[:1]], axis=0))
    else:
        y = jnp.where(sub >= -s, rot, jnp.concatenate([rot[-1:], rot[:-1]], axis=0))
    return y.reshape(r, c)


def _proj_conv_kernel(hp_ref, hc_ref, hn_ref, w_ref, cw_ref, cb_ref, o_ref, *, tm, per_b):
    i = pl.program_id(0)
    first = (i % per_b) == 0
    last = (i % per_b) == per_b - 1
    w = w_ref[...]
    halo_p = jnp.dot(hp_ref[...], w, preferred_element_type=F32)
    halo_n = jnp.dot(hn_ref[...], w, preferred_element_type=F32)
    halo_p = jnp.where(first, 0.0, halo_p)
    halo_n = jnp.where(last, 0.0, halo_n)
    pad = SSD_CONV // 2
    acc = {}

    def project(s):
        acc[s] = jnp.dot(hc_ref[s * CONV_PROJ_ROWS:(s + 1) * CONV_PROJ_ROWS, :], w, preferred_element_type=F32)

    def tile_rows(lo, hi):
        pieces, r = [], lo
        while r < hi:
            if r < 0:
                src, off, end = halo_p, CONV_HALO + r, min(hi, 0)
            elif r >= tm:
                src, off, end = halo_n, r - tm, hi
            else:
                s = r // CONV_PROJ_ROWS
                src, off, end = acc[s], r - s * CONV_PROJ_ROWS, min(hi, (s + 1) * CONV_PROJ_ROWS)
            pieces.append(src[off:off + end - r])
            r = end
        return pieces[0] if len(pieces) == 1 else jnp.concatenate(pieces, axis=0)

    def conv_chunk(c):
        x = tile_rows(c * CONV_RC - CONV_APRON, (c + 1) * CONV_RC + CONV_APRON)
        tap = lambda off: cw_ref[pad + off:pad + off + 1, :]
        xm, xq = _shift_rows(x, -3), _shift_rows(x, 3)
        f0 = tap(-3) * xm + tap(0) * x + tap(3) * xq + cb_ref[...]
        f1 = tap(-2) * xm + tap(1) * x
        fm = tap(-1) * x + tap(2) * xq
        y = f0 + _shift_rows(f1, 1) + _shift_rows(fm, -1)
        o_ref[c * CONV_RC:(c + 1) * CONV_RC, :] = _silu(y[CONV_APRON:CONV_APRON + CONV_RC]).astype(o_ref.dtype)

    nsub = tm // CONV_PROJ_ROWS
    per_sub = CONV_PROJ_ROWS // CONV_RC
    project(0)
    for s in range(nsub):
        if s + 1 < nsub:
            project(s + 1)
        for c in range(s * per_sub, (s + 1) * per_sub):
            conv_chunk(c)


def _proj_conv_silu(h, w, cw, cb, seq, *, col0, tm=2048, tn=512):
    m, k = h.shape
    n = cw.shape[1]
    assert col0 % tn == 0
    cb0 = col0 // tn
    per_b = seq // tm
    hb = tm // CONV_HALO
    assert seq % tm == 0 and n % tn == 0 and SSD_CONV == 7
    return pl.pallas_call(
        functools.partial(_proj_conv_kernel, tm=tm, per_b=per_b),
        grid=(m // tm, n // tn),
        in_specs=[
            pl.BlockSpec((CONV_HALO, k), lambda i, j: (jnp.maximum(i * hb - 1, 0), 0)),
            pl.BlockSpec((tm, k), lambda i, j: (i, 0)),
            pl.BlockSpec((CONV_HALO, k), lambda i, j: (jnp.minimum((i + 1) * hb, m // CONV_HALO - 1), 0)),
            pl.BlockSpec((k, tn), lambda i, j: (0, cb0 + j)),
            pl.BlockSpec((SSD_CONV, tn), lambda i, j: (0, j)),
            pl.BlockSpec((1, tn), lambda i, j: (0, j)),
        ],
        out_specs=pl.BlockSpec((tm, tn), lambda i, j: (i, j)),
        out_shape=jax.ShapeDtypeStruct((m, n), BF16),
        compiler_params=_cparams(("parallel", "parallel")),
        name="proj_conv_silu",
    )(h, h, h, w, cw, cb.reshape(1, n))


def _softplus(x):
    return jnp.maximum(x, 0.0) + jnp.log1p(jnp.exp(-jnp.abs(x)))


def _ssd_prep_kernel(dtr_ref, dtb_ref, alog_ref, ccol_ref, rt_ref, esc_ref, wst_ref, etot_ref, *, t):
    half = LANES // 2
    log2e = math.log2(math.e)
    lane = lax.broadcasted_iota(jnp.int32, (t, LANES), 1)
    fwd = lane < SSD_HEADS
    tri = (lax.broadcasted_iota(jnp.int32, (t, t), 1) <= lax.broadcasted_iota(jnp.int32, (t, t), 0)).astype(BF16)
    neg_a = jnp.exp(alog_ref[...])
    for ci in range(dtr_ref.shape[0] // t):
        rs = slice(ci * t, (ci + 1) * t)
        dt = _softplus(dtr_ref[rs, :] + dtb_ref[...])
        la = jnp.where(lane < 2 * SSD_HEADS, -(dt * neg_a), 0.0)
        hi = la.astype(BF16)
        r1 = la - hi.astype(F32)
        mid = r1.astype(BF16)
        lo = (r1 - mid.astype(F32)).astype(BF16)
        packed = (hi.astype(F32) + pltpu.roll(mid.astype(F32), half, 1)).astype(BF16)
        res = jnp.dot(tri, jnp.concatenate([packed, lo], axis=1), preferred_element_type=F32)
        a0 = res[:, :LANES]
        acs = a0 + pltpu.roll(a0, half, 1) + res[:, LANES:]
        exb = acs - la
        ldt = jnp.log(dt)
        tot = acs[t - 1:t, :]
        ccol_ref[rs, :] = jnp.where(fwd, acs, exb) * log2e
        rt_ref[ci] = (jnp.where(fwd, acs - ldt, exb + ldt) * log2e).T
        esc_ref[rs, :] = jnp.exp(jnp.where(fwd, acs, tot - exb))
        wst_ref[rs, :] = jnp.exp(jnp.where(fwd, tot - acs, exb)) * dt
        etot_ref[ci] = jnp.broadcast_to(jnp.exp(tot), etot_ref.shape[1:])


def _split2(v):
    hi = v.astype(BF16)
    return jnp.concatenate([hi, (v - hi.astype(F32)).astype(BF16)], axis=1)


def _head_expand_tables():
    j = np.arange(2 * LANES)[:, None] % LANES
    c = np.arange(SSD_INNER)[None, :]
    fwd = (j == c // SSD_HD)
    bwd = (j == SSD_HEADS + c // SSD_HD)
    return jnp.asarray(fwd, BF16), jnp.asarray(bwd, BF16)


def _ssd_bwd_state_kernel(xs_ref, b_ref, wst_ref, etot_ref, selb_ref, gin_ref, g_sc):
    @pl.when(pl.program_id(0) == 0)
    def _():
        g_sc[...] = jnp.zeros(g_sc.shape, F32)

    cps = etot_ref.shape[1]
    t = wst_ref.shape[1] // cps
    for sc, b in [(sc, b) for sc in reversed(range(cps)) for b in range(xs_ref.shape[0])]:
        rows = slice(sc * t, (sc + 1) * t)
        gin_ref[b, sc] = g_sc[b].astype(gin_ref.dtype)
        both = jnp.concatenate([_split2(wst_ref[b, rows, :]), _split2(etot_ref[b, sc])], axis=0)
        spread = jnp.dot(both, selb_ref[...], preferred_element_type=F32)
        xw = (xs_ref[b, rows, :].astype(F32) * spread[:t]).astype(BF16)
        dec = spread[t:t + 1]
        for g in range(SSD_GROUPS):
            gs = slice(g * SSD_GW, (g + 1) * SSD_GW)
            bm_t = b_ref[b, rows, g * SSD_STATE:(g + 1) * SSD_STATE].T
            st = jnp.dot(bm_t, xw[:, gs], preferred_element_type=F32)
            g_sc[b, :, gs] = g_sc[b, :, gs] * dec[:, gs] + st


def _ssd_main_kernel(xs_ref, b_ref, c_ref, ccol_ref, rt_ref, esc_ref, wst_ref, etot_ref, gin_ref,
                     dsk_ref, self_ref, selb_ref, y_ref, h_sc):
    @pl.when(pl.program_id(1) == 0)
    def _():
        h_sc[...] = jnp.zeros(h_sc.shape, F32)

    t = rt_ref.shape[-1]
    li = lax.broadcasted_iota(jnp.int32, (t, t), 0)
    si = lax.broadcasted_iota(jnp.int32, (t, t), 1)
    mask_f = jnp.where(li >= si, 0.0, MASK_VALUE)
    mask_b = jnp.where(si >= li, 0.0, MASK_VALUE)
    lo_half = lax.broadcasted_iota(jnp.int32, (t, LANES), 1) < SSD_HD

    for sc, g in [(sc, g) for sc in range(rt_ref.shape[0]) for g in range(SSD_GROUPS)]:
        rows = slice(sc * t, (sc + 1) * t)
        ccol = ccol_ref[rows, :]
        r_t = rt_ref[sc]
        esc2 = _split2(esc_ref[rows, :])
        wst2 = _split2(wst_ref[rows, :])
        etot2 = _split2(etot_ref[sc])
        gs = slice(g * SSD_GW, (g + 1) * SSD_GW)
        bm = b_ref[rows, g * SSD_STATE:(g + 1) * SSD_STATE]
        cm = c_ref[rows, g * SSD_STATE:(g + 1) * SSD_STATE]
        cb = lax.dot_general(cm, bm, (((1,), (1,)), ((), ())), preferred_element_type=F32)
        hf = h_sc[:, gs]
        yf_all = jnp.dot(cm, hf.astype(BF16), preferred_element_type=F32)
        yb_all = jnp.dot(cm, gin_ref[sc, :, gs], preferred_element_type=F32)
        e_in = jnp.dot(esc2, self_ref[g], preferred_element_type=F32)
        e_out = jnp.dot(esc2, selb_ref[g], preferred_element_type=F32)
        w_state = jnp.dot(wst2, self_ref[g], preferred_element_type=F32)
        dec = jnp.dot(etot2, self_ref[g], preferred_element_type=F32)[0:1]
        xw = []
        for k in range(SSD_HPG // 2):
            ea = g * SSD_HPG + 2 * k
            sl = slice(g * SSD_GW + k * LANES, g * SSD_GW + (k + 1) * LANES)
            ks = slice(k * LANES, (k + 1) * LANES)
            xp = xs_ref[rows, sl]
            mms = []
            for e in (ea, ea + 1):
                eb = SSD_HEADS + e
                w = (jnp.exp2(ccol[:, e:e + 1] - r_t[e:e + 1, :] + mask_f)
                     + jnp.exp2(r_t[eb:eb + 1, :] - ccol[:, eb:eb + 1] + mask_b))
                mms.append((cb * w).astype(BF16))
            zero = jnp.zeros_like(xp)
            xx = jnp.concatenate([jnp.where(lo_half, xp, zero), jnp.where(lo_half, zero, xp)], axis=0)
            ypair = jnp.dot(jnp.concatenate(mms, axis=1), xx, preferred_element_type=F32)
            xf = xp.astype(F32)
            y_ref[rows, sl] = (ypair + yf_all[:, ks] * e_in[:, ks] + yb_all[:, ks] * e_out[:, ks]
                               + xf * dsk_ref[:, sl])
            xw.append((xf * w_state[:, ks]).astype(BF16))
        st = jnp.dot(bm.T, jnp.concatenate(xw, axis=1), preferred_element_type=F32)
        h_sc[:, gs] = hf * dec + st


def _ssd(conv_out, dt_raw, dtb, alog, dskip, bsz, seq):
    t = SSD_CHUNK
    nc = seq // t
    assert seq % t == 0
    gn = SSD_GROUPS * SSD_STATE
    b_blk = SSD_INNER // gn
    c_blk = b_blk + 1
    cpp = min(8, nc)
    assert nc % cpp == 0
    sub8 = SUBLANES
    const_spec = pl.BlockSpec((1, LANES), lambda b, i: (0, 0))
    rows_spec = pl.BlockSpec((None, cpp * t, LANES), lambda b, i: (b, i, 0))
    row_arr = jax.ShapeDtypeStruct((bsz, seq, LANES), F32)
    ccol, rt, esc, wst, etot = pl.pallas_call(
        functools.partial(_ssd_prep_kernel, t=t),
        grid=(bsz, nc // cpp),
        in_specs=[rows_spec, const_spec, const_spec],
        out_specs=(rows_spec, pl.BlockSpec((None, cpp, LANES, t), lambda b, i: (b, i, 0, 0)), rows_spec, rows_spec,
                   pl.BlockSpec((None, cpp, sub8, LANES), lambda b, i: (b, i, 0, 0))),
        out_shape=(row_arr, jax.ShapeDtypeStruct((bsz, nc, LANES, t), F32), row_arr, row_arr,
                   jax.ShapeDtypeStruct((bsz, nc, sub8, LANES), F32)),
        compiler_params=_cparams(("parallel", "parallel")),
        name="ssd_decay_terms",
    )(dt_raw, dtb, alog)

    cps = next(n for n in (4, 2, 1) if nc % n == 0)
    rev = lambda c: nc // cps - 1 - c
    sel_f, sel_b = _head_expand_tables()
    by_group = lambda a: a.reshape(a.shape[0], SSD_GROUPS, SSD_GW).transpose(1, 0, 2)
    sel_spec = pl.BlockSpec((SSD_GROUPS, sel_f.shape[0], SSD_GW), lambda b, c: (0, 0, 0))
    gin = pl.pallas_call(
        _ssd_bwd_state_kernel,
        grid=(nc // cps,),
        in_specs=[
            pl.BlockSpec((bsz, cps * t, SSD_INNER), lambda c: (0, rev(c), 0)),
            pl.BlockSpec((bsz, cps * t, gn), lambda c: (0, rev(c), b_blk)),
            pl.BlockSpec((bsz, cps * t, LANES), lambda c: (0, rev(c), 0)),
            pl.BlockSpec((bsz, cps, sub8, LANES), lambda c: (0, rev(c), 0, 0)),
            pl.BlockSpec(sel_b.shape, lambda c: (0, 0)),
        ],
        out_specs=pl.BlockSpec((bsz, cps, SSD_STATE, SSD_INNER), lambda c: (0, rev(c), 0, 0)),
        out_shape=jax.ShapeDtypeStruct((bsz, nc, SSD_STATE, SSD_INNER), BF16),
        scratch_shapes=[pltpu.VMEM((bsz, SSD_STATE, SSD_INNER), F32)],
        compiler_params=_cparams(("arbitrary",)),
        name="ssd_bwd_states",
    )(conv_out, conv_out, wst, etot, sel_b)

    wide_spec = pl.BlockSpec((1, SSD_INNER), lambda b, c: (0, 0))
    tok_spec = pl.BlockSpec((None, cps * t, LANES), lambda b, c: (b, c, 0))
    return pl.pallas_call(
        _ssd_main_kernel,
        grid=(bsz, nc // cps),
        in_specs=[
            pl.BlockSpec((None, cps * t, SSD_INNER), lambda b, c: (b, c, 0)),
            pl.BlockSpec((None, cps * t, gn), lambda b, c: (b, c, b_blk)),
            pl.BlockSpec((None, cps * t, gn), lambda b, c: (b, c, c_blk)),
            tok_spec,
            pl.BlockSpec((None, cps, LANES, t), lambda b, c: (b, c, 0, 0)),
            tok_spec, tok_spec,
            pl.BlockSpec((None, cps, sub8, LANES), lambda b, c: (b, c, 0, 0)),
            pl.BlockSpec((None, cps, SSD_STATE, SSD_INNER), lambda b, c: (b, c, 0, 0)),
            wide_spec, sel_spec, sel_spec,
        ],
        out_specs=pl.BlockSpec((None, cps * t, SSD_INNER), lambda b, c: (b, c, 0)),
        out_shape=jax.ShapeDtypeStruct((bsz, seq, SSD_INNER), F32),
        scratch_shapes=[pltpu.VMEM((SSD_STATE, SSD_INNER), F32)],
        compiler_params=_cparams(("parallel", "arbitrary")),
        name="ssd_main",
    )(conv_out, conv_out, conv_out, ccol, rt, esc, wst, etot, gin, dskip, by_group(sel_f), by_group(sel_b))


def _dft_tables(seq):
    n1, n2 = seq // F_N2, F_N2
    c = np.arange(F_GD)
    ang = 2 * np.pi * np.outer(c, c) / F_GD
    chan = np.concatenate([np.cos(ang), -np.sin(ang)], axis=1) / math.sqrt(F_GD)
    k1 = np.arange(n1)
    a1 = 2 * np.pi * np.outer(k1, k1) / n1
    stage_a = np.block([[np.cos(a1), np.sin(a1)], [-np.sin(a1), np.cos(a1)]])
    stage_a = np.kron(stage_a, np.eye(2))
    s2 = np.arange(n2)
    at = 2 * np.pi * np.outer(s2, k1) / seq
    at = at.reshape(n2 // 2, 2, n1).transpose(0, 2, 1).reshape(n2 // 2, 2 * n1)
    tw_c, tw_s = np.cos(at)[..., None], np.sin(at)[..., None]
    a2 = 2 * np.pi * np.outer(s2, s2) / n2
    stage_b = np.concatenate([np.cos(a2), np.sin(a2)], axis=1)
    return (jnp.asarray(chan, BF16), jnp.asarray(stage_a, BF16), jnp.asarray(tw_c, F32),
            jnp.asarray(tw_s, F32), jnp.asarray(stage_b, BF16))


def _fft_a_kernel(z_ref, ma_ref, twc_ref, tws_ref, o_ref, zs, ys, *, n1, rb):
    ct = z_ref.shape[-1]
    half = rb // 2
    _put_rows(zs, pltpu.bitcast(z_ref[...].reshape(2 * n1 * rb, ct), jnp.uint32))
    for j in range(half):
        zz = pltpu.bitcast(_get_rows(zs, j, 2 * n1, half), BF16)
        y = jnp.dot(ma_ref[...], zz, preferred_element_type=F32)
        yr, yi = y[:2 * n1], y[2 * n1:]
        tc, ts = twc_ref[j], tws_ref[j]
        out = jnp.concatenate([yr * tc + yi * ts, yi * tc - yr * ts], axis=0).astype(BF16)
        _put_rows(ys, pltpu.bitcast(out, jnp.uint32), j, half)
    o_ref[...] = pltpu.bitcast(_get_rows(ys), BF16).reshape(2, n1, rb, ct)


def _fft_b_kernel(y_ref, mb_ref, o_ref, os_ref, *, n2, kb, scale):
    ct = y_ref.shape[-1]
    for kk in range(kb):
        yy = y_ref[:, kk].reshape(2 * n2, ct)
        _put_rows(os_ref, jnp.dot(mb_ref[...], yy, preferred_element_type=F32) * scale, kk, kb)
    o_ref[...] = _get_rows(os_ref).reshape(n2, kb, ct).astype(o_ref.dtype)


def _fourier_seq(z2, tabs, bsz, seq, *, rb=BF16_ROWS, kb=BF16_ROWS, ct_a=1536, ct_b=768):
    _, ma, twc, tws, mb = tabs
    n1, n2 = seq // F_N2, F_N2
    w = z2.shape[-1]
    ct, nct = ct_a, w // ct_a
    za = z2.reshape(2, bsz, n1, n2, w)
    ab_spec = pl.BlockSpec((2, None, n1, rb, ct), lambda b, a, j: (0, b, 0, a, j))
    ya = pl.pallas_call(
        functools.partial(_fft_a_kernel, n1=n1, rb=rb),
        grid=(bsz, n2 // rb, nct),
        in_specs=[
            ab_spec,
            pl.BlockSpec((4 * n1, 4 * n1), lambda b, a, j: (0, 0)),
            pl.BlockSpec((rb // 2, 2 * n1, 1), lambda b, a, j: (a, 0, 0)),
            pl.BlockSpec((rb // 2, 2 * n1, 1), lambda b, a, j: (a, 0, 0)),
        ],
        out_specs=ab_spec,
        out_shape=jax.ShapeDtypeStruct((2, bsz, n1, n2, w), BF16),
        scratch_shapes=[pltpu.VMEM((ct // LANES, n1 * rb, LANES), jnp.uint32)] * 2,
        compiler_params=_cparams(("parallel", "parallel", "parallel")),
        name="fourier_stage_a",
    )(za, ma, twc, tws)
    ct, nct = ct_b, w // ct_b
    out = pl.pallas_call(
        functools.partial(_fft_b_kernel, n2=n2, kb=kb, scale=1.0 / math.sqrt(seq)),
        grid=(bsz, n1 // kb, nct),
        in_specs=[
            pl.BlockSpec((2, None, kb, n2, ct), lambda b, k, j: (0, b, k, 0, j)),
            pl.BlockSpec((n2, 2 * n2), lambda b, k, j: (0, 0)),
        ],
        out_specs=pl.BlockSpec((None, n2, kb, ct), lambda b, k, j: (b, 0, k, j)),
        out_shape=jax.ShapeDtypeStruct((bsz, n2, n1, w), BF16),
        scratch_shapes=[pltpu.VMEM((ct // LANES, n2 * kb, LANES), F32)],
        compiler_params=_cparams(("parallel", "parallel", "parallel")),
        name="fourier_stage_b",
    )(ya, mb)
    return out.reshape(bsz, seq, w)


def _merge_kernel(*refs, dils):
    ng = len(dils)
    x_ref = refs[0]
    o_refs = refs[1:1 + ng]
    l_refs = refs[1 + ng:1 + 2 * ng]
    y_ref, z_ref, yg_ref, f_ref, gt_ref, wa_ref, ws_ref, wf_ref, wo_ref, out_ref, o_sc, l_sc = refs[1 + 2 * ng:]
    tm, d = x_ref.shape

    yf = jnp.dot(f_ref[...], wf_ref[...], preferred_element_type=F32)

    ys = None
    for g in range(SSD_GROUPS):
        gs = slice(g * SSD_GW, (g + 1) * SSD_GW)
        yy = y_ref[:, gs] * _silu(z_ref[:, gs].astype(F32))
        part = jnp.dot(_rms(yy, yg_ref[:, gs]).astype(BF16), ws_ref[gs, :], preferred_element_type=F32)
        ys = part if ys is None else ys + part

    for gi, dil in enumerate(dils):
        rows = tm // dil
        idx = lambda r: slice(None) if dil == 1 else pl.ds(r, rows, stride=dil)
        for r in range(dil):
            l_sc[gi, idx(r), :] = l_refs[gi][:, r * LANES:(r + 1) * LANES]
            for hh in range(ATTN_HPG):
                c0 = r * ATTN_OUT + hh * ATTN_HD
                o_sc[gi * ATTN_HPG + hh, idx(r), :] = o_refs[gi][:, c0:c0 + ATTN_HD].astype(F32)

    ls = [l_sc[gi] for gi in range(ng)]
    mx = functools.reduce(jnp.maximum, ls)
    ws = [jnp.exp(v - mx) for v in ls]
    inv = 1.0 / functools.reduce(lambda a, b: a + b, ws)
    heads = []
    for hh in range(ATTN_HPG):
        acc = None
        for gi in range(ng):
            alpha = (ws[gi] * inv)[:, hh:hh + 1]
            term = alpha * o_sc[gi * ATTN_HPG + hh]
            acc = term if acc is None else acc + term
        heads.append(acc.astype(BF16))
    o_attn = jnp.concatenate(heads, axis=1)

    ya = jnp.dot(o_attn, wa_ref[...], preferred_element_type=F32)
    gates = gt_ref[...].astype(F32)
    merged = gates[:, :d] * ya + gates[:, d:2 * d] * ys + gates[:, 2 * d:] * yf
    out_ref[...] = x_ref[...] + jnp.dot(merged.astype(BF16), wo_ref[...], preferred_element_type=F32)


def _merge(x, os_, ls_, dils, y, z, y_gain, f, gates, wa, ws, wf, wo, bsz, seq, *, tm=512):
    d = x.shape[-1]
    assert all((tm // dil) % BF16_ROWS == 0 for dil in dils)
    tok = lambda width: pl.BlockSpec((None, tm, width), lambda b, i: (b, i, 0))
    full = lambda a: pl.BlockSpec(a.shape, lambda b, i: (0, 0), pipeline_mode=pl.Buffered(1))
    ng = len(dils)
    in_specs = [tok(d)]
    in_specs += [pl.BlockSpec((None, tm // dil, dil * ATTN_OUT), lambda b, i: (b, i, 0)) for dil in dils]
    in_specs += [pl.BlockSpec((None, tm // dil, dil * LANES), lambda b, i: (b, i, 0)) for dil in dils]
    in_specs += [tok(y.shape[-1]), tok(z.shape[-1]), full(y_gain), tok(f.shape[-1]), tok(gates.shape[-1]),
                 full(wa), full(ws), full(wf), full(wo)]
    return pl.pallas_call(
        functools.partial(_merge_kernel, dils=tuple(dils)),
        grid=(bsz, seq // tm),
        in_specs=in_specs,
        out_specs=tok(d),
        out_shape=jax.ShapeDtypeStruct((bsz, seq, d), F32),
        scratch_shapes=[pltpu.VMEM((ng * ATTN_HPG, tm, ATTN_HD), F32), pltpu.VMEM((ng, tm, LANES), F32)],
        compiler_params=_cparams(("parallel", "parallel")),
        name="branch_merge",
    )(x, *os_, *ls_, y, z, y_gain, f, gates, wa, ws, wf, wo)


def _xattn_kernel(x_ref, g_ref, wq_ref, qg_ref, k_ref, v_ref, wo_ref, out_ref):
    tm, d = x_ref.shape
    hd = d // MEM_HEADS
    scale = hd ** -0.5
    rc = min(tm, XATTN_ROW_CHUNK)
    for c in range(tm // rc):
        rs = slice(c * rc, (c + 1) * rc)
        x = x_ref[rs, :]
        h = _rms(x, g_ref[...]).astype(BF16)
        q = jnp.dot(h, wq_ref[...], preferred_element_type=F32)
        heads = []
        for hh in range(MEM_HEADS):
            sl = slice(hh * hd, (hh + 1) * hd)
            qn = _rms(q[:, sl], qg_ref[...]).astype(BF16)
            s = lax.dot_general(qn, k_ref[:, sl], (((1,), (1,)), ((), ())), preferred_element_type=F32) * scale
            m = jnp.max(s, axis=-1, keepdims=True)
            p = jnp.exp(s - m)
            l = jnp.sum(p, axis=-1, keepdims=True)
            heads.append((jnp.dot(p.astype(BF16), v_ref[:, sl], preferred_element_type=F32) / l).astype(BF16))
        o = jnp.concatenate(heads, axis=1)
        out_ref[rs, :] = x + jnp.dot(o, wo_ref[...], preferred_element_type=F32)


def _xattn(x, g, wq, qg, k, v, wo, *, tm=512):
    bsz, seq, d = x.shape
    mt = k.shape[1]
    full = lambda a: pl.BlockSpec(a.shape, lambda b, i: (0, 0))
    g = g.reshape(1, d)
    qg = qg.reshape(1, -1)
    return pl.pallas_call(
        _xattn_kernel,
        grid=(bsz, seq // tm),
        in_specs=[
            pl.BlockSpec((None, tm, d), lambda b, i: (b, i, 0)),
            full(g), full(wq), full(qg),
            pl.BlockSpec((None, mt, d), lambda b, i: (b, 0, 0)),
            pl.BlockSpec((None, mt, d), lambda b, i: (b, 0, 0)),
            full(wo),
        ],
        out_specs=pl.BlockSpec((None, tm, d), lambda b, i: (b, i, 0)),
        out_shape=jax.ShapeDtypeStruct((bsz, seq, d), F32),
        compiler_params=_cparams(("parallel", "parallel")),
        name="mem_xattn",
    )(x, g, wq, qg, k, v, wo)


def _ffn_kernel(x_ref, g_ref, wg_ref, wu_ref, wd_ref, *rest, tf):
    x = x_ref[...]
    h = _rms(x, g_ref[...]).astype(BF16)
    acc = x
    for c in range(wg_ref.shape[1] // tf):
        sl = slice(c * tf, (c + 1) * tf)
        gt = jnp.dot(h, wg_ref[:, sl], preferred_element_type=F32)
        up = jnp.dot(h, wu_ref[:, sl], preferred_element_type=F32)
        a = (_silu(gt) * up).astype(BF16)
        acc = acc + jnp.dot(a, wd_ref[sl, :], preferred_element_type=F32)
    if len(rest) == 1:
        rest[0][...] = acc
    else:
        gn_ref, out_ref, hn_ref = rest
        out_ref[...] = acc
        hn_ref[...] = _rms(acc, gn_ref[...]).astype(hn_ref.dtype)


def _ffn(x, g, wg, wu, wd, next_gain=None, *, tm=512):
    m, d = x.shape
    dff = wg.shape[1]
    tf = dff // 2 if (dff // 2) % LANES == 0 else dff
    full = lambda a: pl.BlockSpec(a.shape, lambda i: (0, 0), pipeline_mode=pl.Buffered(1))
    tok = pl.BlockSpec((tm, d), lambda i: (i, 0))
    args = [x, g.reshape(1, d), wg, wu, wd]
    out_shape, out_specs = jax.ShapeDtypeStruct((m, d), F32), tok
    if next_gain is not None:
        args.append(next_gain.reshape(1, d))
        out_shape, out_specs = (out_shape, jax.ShapeDtypeStruct((m, d), BF16)), (tok, tok)
    return pl.pallas_call(
        functools.partial(_ffn_kernel, tf=tf),
        grid=(m // tm,),
        in_specs=[tok] + [full(a) for a in args[1:]],
        out_specs=out_specs,
        out_shape=out_shape,
        compiler_params=_cparams(("parallel",)),
        name="swiglu_ffn",
    )(*args)


def _pad_lanes(a):
    a = a.reshape(-1, 2 * SSD_HEADS)
    return jnp.pad(a, ((0, 0), (0, LANES - 2 * SSD_HEADS)))


def kernel(x, mem, rel_bias, mix_norm_g, w_in, gate_bias, attn_q_norm_g, attn_k_norm_g, conv_w, conv_b, dt_bias, a_log, d_skip, ssd_norm_g, w_branch_attn, w_branch_ssd, w_branch_fourier, w_mix_out, xattn_norm_g, mem_norm_g, w_xq, w_xk, w_xv, xattn_q_norm_g, xattn_k_norm_g, w_xo, ffn_norm_g, w_ffn_gate, w_ffn_up, w_ffn_down):
    bsz, seq, d = x.shape
    depth = w_in.shape[0]
    m = bsz * seq
    xf = x.reshape(m, d)
    memf = mem.reshape(bsz * mem.shape[1], d)
    tabs = _dft_tables(seq)
    dils = [dil for _, dil in ATTN_GROUPS]
    biases = [_attn_bias(rel_bias, gi, dil) for gi, dil in enumerate(dils)]
    offs = np.cumsum([0, ATTN_WIDTH, ATTN_WIDTH, ATTN_WIDTH, SSD_INNER, SSD_CONV_CH, 2 * SSD_HEADS, F_WIDTH, 3 * d])
    bf = lambda a: a.astype(BF16)

    for l in range(depth):
        wl = bf(w_in[l])
        seg = lambda i: wl[:, offs[i]:offs[i + 1]]
        if l == 0:
            h = _rmsnorm(xf, mix_norm_g[l])
        gains = jnp.stack([attn_q_norm_g[l], attn_k_norm_g[l], jnp.ones_like(attn_q_norm_g[l])]).reshape(3, 1, ATTN_HD)
        os_, ls_ = [], []
        for gi, dil in enumerate(dils):
            qkv = _qkv_group(h, wl, gains, gi, dil, bsz, seq)
            o_g, lse_g = _attn_group(qkv, biases[gi], dil, bsz, seq)
            os_.append(o_g)
            ls_.append(lse_g)

        z = _matmul(h, wl, out_dtype=BF16, tn=512, tm=2048, col0=int(offs[3]), n=SSD_INNER)
        dt_raw = _matmul(h, _pad_lanes(seg(5)), out_dtype=F32, tn=LANES, tm=2048)
        z2 = _matmul(h, seg(6), out_dtype=BF16, tn=768, tm=2048, epilogue="chandft", extra=tabs[0],
                     row_chunk=2 * MM_ROW_CHUNK)
        gates = _matmul(h, seg(7), out_dtype=BF16, tn=1024, tm=2048, epilogue="sigmoid_bias", extra=gate_bias[l])

        conv_out = _proj_conv_silu(h, wl, conv_w[l], conv_b[l], seq, col0=int(offs[4])).reshape(bsz, seq, SSD_CONV_CH)
        y_ssd = _ssd(conv_out, dt_raw.reshape(bsz, seq, -1), _pad_lanes(dt_bias[l]), _pad_lanes(a_log[l]),
                     jnp.repeat(d_skip[l], SSD_HD).reshape(1, SSD_INNER), bsz, seq)

        f_re = _fourier_seq(z2, tabs, bsz, seq)

        x3 = _merge(xf.reshape(bsz, seq, d), os_, ls_, dils, y_ssd, z.reshape(bsz, seq, SSD_INNER),
                    ssd_norm_g[l].reshape(1, SSD_INNER), f_re, gates.reshape(bsz, seq, 3 * d),
                    bf(w_branch_attn[l]), bf(w_branch_ssd[l]), bf(w_branch_fourier[l]), bf(w_mix_out[l]), bsz, seq)

        hd_m = d // MEM_HEADS
        hm = _rmsnorm(memf, mem_norm_g[l])
        km = _matmul(hm, bf(w_xk[l]), out_dtype=BF16, tn=512, epilogue="headnorm",
                     extra=xattn_k_norm_g[l], head_dim=hd_m)
        vm = _matmul(hm, bf(w_xv[l]), out_dtype=BF16, tn=512)
        x3 = _xattn(x3, xattn_norm_g[l], bf(w_xq[l]), xattn_q_norm_g[l], km.reshape(bsz, -1, d),
                    vm.reshape(bsz, -1, d), bf(w_xo[l]))

        ffn_args = (x3.reshape(m, d), ffn_norm_g[l], bf(w_ffn_gate[l]), bf(w_ffn_up[l]), bf(w_ffn_down[l]))
        if l + 1 < depth:
            xf, h = _ffn(*ffn_args, mix_norm_g[l + 1])
        else:
            xf = _ffn(*ffn_args)

    return xf.reshape(bsz, seq, d)
```

```python
import functools
import math

import numpy as np
import jax
import jax.numpy as jnp
from jax import lax
from jax.experimental import pallas as pl
from jax.experimental.pallas import tpu as pltpu

F32 = jnp.float32
BF16 = jnp.bfloat16

NORM_EPS = 1e-6
MASK_VALUE = -1e30

ATTN_GROUPS = ((128, 1), (512, 4), (2048, 16))
ATTN_HPG = 4
ATTN_HD = 128
ATTN_HEADS = ATTN_HPG * len(ATTN_GROUPS)
ATTN_WIDTH = ATTN_HEADS * ATTN_HD
ATTN_OUT = ATTN_HPG * ATTN_HD
ATTN_HALF = 64
ATTN_TQ = 2 * ATTN_HALF
ATTN_TK = ATTN_TQ + 2 * ATTN_HALF
NUM_BUCKETS = 32
MAX_DISTANCE = 1024

SSD_HEADS = 32
SSD_HD = 64
SSD_GROUPS = 4
SSD_HPG = SSD_HEADS // SSD_GROUPS
SSD_STATE = 128
SSD_INNER = SSD_HEADS * SSD_HD
SSD_GW = SSD_INNER // SSD_GROUPS
SSD_CONV = 7
SSD_CHUNK = 128
SSD_CONV_CH = SSD_INNER + 2 * SSD_GROUPS * SSD_STATE
LANES = 128
SUBLANES = 8
BF16_ROWS = 16

F_GROUPS = 6
F_GD = 256
F_WIDTH = F_GROUPS * F_GD
F_N2 = 128

MEM_HEADS = 4
XATTN_ROW_CHUNK = 256

VMEM_LIMIT = 56 * 1024 * 1024


def _cparams(sem):
    return pltpu.CompilerParams(dimension_semantics=sem, vmem_limit_bytes=VMEM_LIMIT)


def _silu(x):
    return x * jax.nn.sigmoid(x)


def _rms(x, g):
    ms = jnp.mean(x * x, axis=-1, keepdims=True)
    return x * lax.rsqrt(ms + NORM_EPS) * g


def _get_rows(ref3, start=0, size=None, stride=1):
    idx = slice(None) if size is None else pl.ds(start, size, stride=stride)
    return jnp.concatenate([ref3[c, idx, :] for c in range(ref3.shape[0])], axis=1)


def _put_rows(ref3, val, start=0, stride=1):
    size = val.shape[0]
    idx = slice(None) if (stride == 1 and size == ref3.shape[1]) else pl.ds(start, size, stride=stride)
    for c in range(ref3.shape[0]):
        ref3[c, idx, :] = val[:, c * LANES:(c + 1) * LANES]


def _rmsnorm_kernel(x_ref, g_ref, o_ref):
    o_ref[...] = _rms(x_ref[...], g_ref[...]).astype(o_ref.dtype)


def _rmsnorm(x, g, *, tm=1024):
    m, k = x.shape
    tm = min(tm, m)
    return pl.pallas_call(
        _rmsnorm_kernel,
        grid=(m // tm,),
        in_specs=[pl.BlockSpec((tm, k), lambda i: (i, 0)), pl.BlockSpec((1, k), lambda i: (0, 0))],
        out_specs=pl.BlockSpec((tm, k), lambda i: (i, 0)),
        out_shape=jax.ShapeDtypeStruct((m, k), BF16),
        compiler_params=_cparams(("parallel",)),
        name="rmsnorm",
    )(x, g.reshape(1, k))


MM_ROW_CHUNK = 256


def _mm_kernel(*refs, epilogue, head_dim, row_chunk):
    h_ref, w_ref = refs[:2]
    o_ref = refs[-1]
    tm = h_ref.shape[0]
    rc = min(row_chunk, tm)
    for c in range(tm // rc):
        rs = slice(c * rc, (c + 1) * rc)
        acc = jnp.dot(h_ref[rs, :], w_ref[...], preferred_element_type=F32)
        tn = acc.shape[1]
        if epilogue == "headnorm":
            hg_ref = refs[2]
            for hh in range(tn // head_dim):
                sl = slice(hh * head_dim, (hh + 1) * head_dim)
                o_ref[rs, sl] = _rms(acc[:, sl], hg_ref[...]).astype(o_ref.dtype)
        elif epilogue == "sigmoid_bias":
            b_ref = refs[2]
            o_ref[rs, :] = jax.nn.sigmoid(acc + b_ref[...]).astype(o_ref.dtype)
        elif epilogue == "chandft":
            dft_ref = refs[2]
            a16 = acc.astype(BF16)
            for gg in range(tn // F_GD):
                sl = slice(gg * F_GD, (gg + 1) * F_GD)
                r = jnp.dot(a16[:, sl], dft_ref[...], preferred_element_type=F32)
                o_ref[0, rs, sl] = r[:, :F_GD].astype(o_ref.dtype)
                o_ref[1, rs, sl] = r[:, F_GD:].astype(o_ref.dtype)
        else:
            o_ref[rs, :] = acc.astype(o_ref.dtype)


def _matmul(h, w, *, out_dtype, tn, tm=1024, epilogue="plain", extra=None, head_dim=None, col0=0, n=None,
            row_chunk=MM_ROW_CHUNK):
    m, k = h.shape
    n = w.shape[1] if n is None else n
    tm = min(tm, m)
    assert m % tm == 0 and n % tn == 0 and col0 % tn == 0
    cb0 = col0 // tn
    in_specs = [pl.BlockSpec((tm, k), lambda i, j: (i, 0)), pl.BlockSpec((k, tn), lambda i, j: (0, cb0 + j))]
    args = [h, w]
    if epilogue == "headnorm":
        in_specs.append(pl.BlockSpec((1, head_dim), lambda i, j: (0, 0)))
        args.append(extra.reshape(1, head_dim))
    elif epilogue == "sigmoid_bias":
        in_specs.append(pl.BlockSpec((1, tn), lambda i, j: (0, j)))
        args.append(extra.reshape(1, n))
    elif epilogue == "chandft":
        in_specs.append(pl.BlockSpec(extra.shape, lambda i, j: (0, 0)))
        args.append(extra)
    if epilogue == "chandft":
        out_shape = jax.ShapeDtypeStruct((2, m, n), out_dtype)
        out_spec = pl.BlockSpec((2, tm, tn), lambda i, j: (0, i, j))
    else:
        out_shape = jax.ShapeDtypeStruct((m, n), out_dtype)
        out_spec = pl.BlockSpec((tm, tn), lambda i, j: (i, j))
    return pl.pallas_call(
        functools.partial(_mm_kernel, epilogue=epilogue, head_dim=head_dim, row_chunk=row_chunk),
        grid=(m // tm, n // tn),
        in_specs=in_specs,
        out_specs=out_spec,
        out_shape=out_shape,
        compiler_params=_cparams(("parallel", "parallel")),
        name="matmul_" + epilogue,
    )(*args)


def _qkv_kernel(h_ref, w_ref, hg_ref, o_ref, *scratch, dil):
    j = pl.program_id(1)
    tm = h_ref.shape[0]
    width = w_ref.shape[1]
    nh = width // ATTN_HD
    rc = min(max(MM_ROW_CHUNK, 2 * BF16_ROWS * dil), tm)
    rows = rc // dil
    for c in range(tm // rc):
        acc = jnp.dot(h_ref[c * rc:(c + 1) * rc, :], w_ref[...], preferred_element_type=F32)
        if dil > 1:
            scr = scratch[0]
            for hh in range(nh):
                scr[hh, c * rc:(c + 1) * rc, :] = acc[:, hh * ATTN_HD:(hh + 1) * ATTN_HD]
        for r in range(dil):
            for hh in range(nh):
                if dil == 1:
                    ph = acc[:, hh * ATTN_HD:(hh + 1) * ATTN_HD]
                else:
                    ph = scr[hh, pl.ds(c * rc + r, rows, stride=dil), :]
                val = jnp.where(j < 2, _rms(ph, hg_ref[...]), ph)
                c0 = r * width + hh * ATTN_HD
                o_ref[c * rows:(c + 1) * rows, c0:c0 + ATTN_HD] = val.astype(o_ref.dtype)


def _qkv_group(h, w, gains, gi, dil, bsz, seq, *, tm=2048):
    m, k = h.shape
    sub = seq // dil
    per_b = seq // tm
    rows = tm // dil
    assert seq % tm == 0 and rows % BF16_ROWS == 0
    return pl.pallas_call(
        functools.partial(_qkv_kernel, dil=dil),
        grid=(m // tm, 3),
        in_specs=[
            pl.BlockSpec((tm, k), lambda i, j: (i, 0)),
            pl.BlockSpec((k, ATTN_OUT), lambda i, j: (0, j * len(ATTN_GROUPS) + gi)),
            pl.BlockSpec((None, 1, ATTN_HD), lambda i, j: (j, 0, 0)),
        ],
        out_specs=pl.BlockSpec((None, None, rows, dil * ATTN_OUT), lambda i, j: (j, i // per_b, i % per_b, 0)),
        out_shape=jax.ShapeDtypeStruct((3, bsz, sub, dil * ATTN_OUT), BF16),
        scratch_shapes=[pltpu.VMEM((ATTN_HPG, tm, ATTN_HD), F32)] if dil > 1 else [],
        compiler_params=_cparams(("parallel", "parallel")),
        name=f"qkv_proj_d{dil}",
    )(h, w, gains)


def _t5_bucket_np(rel):
    half_b = NUM_BUCKETS // 2
    exact = half_b // 2
    dist = np.abs(rel)
    log_ratio = np.log(np.maximum(dist, 1) / exact) / np.log(MAX_DISTANCE / exact)
    far = np.minimum(exact + (log_ratio * (half_b - exact)).astype(np.int32), half_b - 1)
    return np.where(rel > 0, half_b, 0) + np.where(dist < exact, dist, far)


def _attn_bias(rel_bias, gi, dil):
    i = np.arange(ATTN_TQ)[:, None]
    j = np.arange(ATTN_TK)[None, :]
    rel = j - ATTN_HALF - i
    idx = _t5_bucket_np(rel * dil)
    onehot = jnp.asarray(np.eye(NUM_BUCKETS, dtype=np.float32)[idx])
    tab = rel_bias[:, gi * ATTN_HPG:(gi + 1) * ATTN_HPG].astype(F32)
    b = jnp.einsum("qkn,nh->hqk", onehot, tab, precision=lax.Precision.HIGHEST)
    band = np.broadcast_to(np.abs(rel) <= ATTN_HALF, (ATTN_TQ, ATTN_TK))
    keep = np.stack([band & ((j >= ATTN_HALF) | (not first)) & ((j < ATTN_TK - ATTN_HALF) | (not last))
                     for last in (False, True) for first in (False, True)])
    return jnp.where(jnp.asarray(keep)[:, None], b[None], MASK_VALUE)


def _attn_kernel(q_ref, kp_ref, kc_ref, kn_ref, vp_ref, vc_ref, vn_ref, bias_ref, o_ref, lse_ref, kpad, vpad,
                 *, tile, sub):
    i = pl.program_id(2)
    hf = ATTN_HALF
    kpad[0:hf, :] = kp_ref[...]
    kpad[hf:hf + tile, :] = kc_ref[...]
    kpad[hf + tile:, :] = kn_ref[...]
    vpad[0:hf, :] = vp_ref[...]
    vpad[hf:hf + tile, :] = vc_ref[...]
    vpad[hf + tile:, :] = vn_ref[...]

    scale = ATTN_HD ** -0.5
    lane = lax.broadcasted_iota(jnp.int32, (ATTN_TQ, LANES), 1)
    tiles = tile // ATTN_TQ

    def body(t, carry):
        s0 = pl.multiple_of(t * ATTN_TQ, ATTN_TQ)
        gt = i * tiles + t
        variant = (gt == 0).astype(jnp.int32) + 2 * (gt == sub // ATTN_TQ - 1).astype(jnp.int32)
        lse_tile = jnp.zeros((ATTN_TQ, LANES), F32)
        for hh in range(ATTN_HPG):
            sl = slice(hh * ATTN_HD, (hh + 1) * ATTN_HD)
            q = q_ref[pl.ds(s0, ATTN_TQ), sl]
            kw = kpad[pl.ds(s0, ATTN_TK), sl]
            vw = vpad[pl.ds(s0, ATTN_TK), sl]
            s = (lax.dot_general(q, kw, (((1,), (1,)), ((), ())), preferred_element_type=F32) * scale
                 + bias_ref[variant, hh])
            m = jnp.max(s, axis=-1, keepdims=True)
            p = jnp.exp(s - m)
            l = jnp.sum(p, axis=-1, keepdims=True)
            o = jnp.dot(p.astype(BF16), vw, preferred_element_type=F32) / l
            o_ref[pl.ds(s0, ATTN_TQ), sl] = o.astype(o_ref.dtype)
            lse_tile = jnp.where(lane == hh, m + jnp.log(l), lse_tile)
        lse_ref[pl.ds(s0, ATTN_TQ), :] = lse_tile
        return carry

    lax.fori_loop(0, tile // ATTN_TQ, body, 0, unroll=8)


def _attn_group(qkv, bias, dil, bsz, seq, *, tile=1024):
    sub = seq // dil
    tile = min(tile, sub)
    nblk = sub // tile
    hb = tile // ATTN_HALF
    last_h = sub // ATTN_HALF - 1
    assert sub % tile == 0 and tile % ATTN_TQ == 0
    cur = lambda which: pl.BlockSpec((None, None, tile, ATTN_OUT), lambda b, r, i: (which, b, i, r))
    prev = lambda which: pl.BlockSpec((None, None, ATTN_HALF, ATTN_OUT),
                                      lambda b, r, i: (which, b, jnp.maximum(i * hb - 1, 0), r))
    nxt = lambda which: pl.BlockSpec((None, None, ATTN_HALF, ATTN_OUT),
                                     lambda b, r, i: (which, b, jnp.minimum((i + 1) * hb, last_h), r))
    return pl.pallas_call(
        functools.partial(_attn_kernel, tile=tile, sub=sub),
        grid=(bsz, dil, nblk),
        in_specs=[cur(0), prev(1), cur(1), nxt(1), prev(2), cur(2), nxt(2),
                  pl.BlockSpec(bias.shape, lambda b, r, i: (0, 0, 0, 0))],
        out_specs=(pl.BlockSpec((None, tile, ATTN_OUT), lambda b, r, i: (b, i, r)),
                   pl.BlockSpec((None, tile, LANES), lambda b, r, i: (b, i, r))),
        out_shape=(jax.ShapeDtypeStruct((bsz, sub, dil * ATTN_OUT), BF16),
                   jax.ShapeDtypeStruct((bsz, sub, dil * LANES), F32)),
        scratch_shapes=[pltpu.VMEM((tile + 2 * ATTN_HALF, ATTN_OUT), BF16)] * 2,
        compiler_params=_cparams(("parallel", "parallel", "parallel")),
        name=f"dilated_attn_d{dil}",
    )(qkv, qkv, qkv, qkv, qkv, qkv, qkv, bias)


CONV_HALO = BF16_ROWS
CONV_RC = 128
CONV_APRON = 8
CONV_PROJ_ROWS = 512


def _shift_rows(x, s):
    r, c = x.shape
    x3 = x.reshape(r // SUBLANES, SUBLANES, c)
    rot = pltpu.roll(x3, (-s) % SUBLANES, 1)
    sub = lax.broadcasted_iota(jnp.int32, x3.shape, 1)
    if s > 0:
        y = jnp.where(sub < SUBLANES - s, rot, jnp.concatenate([rot[1:], rot[:1]], axis=0))
    else:
        y = jnp.where(sub >= -s, rot, jnp.concatenate([rot[-1:], rot[:-1]], axis=0))
    return y.reshape(r, c)


def _proj_conv_kernel(hp_ref, hc_ref, hn_ref, w_ref, cw_ref, cb_ref, o_ref, *, tm, per_b):
    i = pl.program_id(0)
    first = (i % per_b) == 0
    last = (i % per_b) == per_b - 1
    w = w_ref[...]
    halo_p = jnp.dot(hp_ref[...], w, preferred_element_type=F32)
    halo_n = jnp.dot(hn_ref[...], w, preferred_element_type=F32)
    halo_p = jnp.where(first, 0.0, halo_p)
    halo_n = jnp.where(last, 0.0, halo_n)
    pad = SSD_CONV // 2
    acc = {}

    def project(s):
        acc[s] = jnp.dot(hc_ref[s * CONV_PROJ_ROWS:(s + 1) * CONV_PROJ_ROWS, :], w, preferred_element_type=F32)

    def tile_rows(lo, hi):
        pieces, r = [], lo
        while r < hi:
            if r < 0:
                src, off, end = halo_p, CONV_HALO + r, min(hi, 0)
            elif r >= tm:
                src, off, end = halo_n, r - tm, hi
            else:
                s = r // CONV_PROJ_ROWS
                src, off, end = acc[s], r - s * CONV_PROJ_ROWS, min(hi, (s + 1) * CONV_PROJ_ROWS)
            pieces.append(src[off:off + end - r])
            r = end
        return pieces[0] if len(pieces) == 1 else jnp.concatenate(pieces, axis=0)

    def conv_chunk(c):
        x = tile_rows(c * CONV_RC - CONV_APRON, (c + 1) * CONV_RC + CONV_APRON)
        tap = lambda off: cw_ref[pad + off:pad + off + 1, :]
        xm, xq = _shift_rows(x, -3), _shift_rows(x, 3)
        f0 = tap(-3) * xm + tap(0) * x + tap(3) * xq + cb_ref[...]
        f1 = tap(-2) * xm + tap(1) * x
        fm = tap(-1) * x + tap(2) * xq
        y = f0 + _shift_rows(f1, 1) + _shift_rows(fm, -1)
        o_ref[c * CONV_RC:(c + 1) * CONV_RC, :] = _silu(y[CONV_APRON:CONV_APRON + CONV_RC]).astype(o_ref.dtype)

    nsub = tm // CONV_PROJ_ROWS
    per_sub = CONV_PROJ_ROWS // CONV_RC
    project(0)
    for s in range(nsub):
        if s + 1 < nsub:
            project(s + 1)
        for c in range(s * per_sub, (s + 1) * per_sub):
            conv_chunk(c)


def _proj_conv_silu(h, w, cw, cb, seq, *, col0, tm=2048, tn=512):
    m, k = h.shape
    n = cw.shape[1]
    assert col0 % tn == 0
    cb0 = col0 // tn
    per_b = seq // tm
    hb = tm // CONV_HALO
    assert seq % tm == 0 and n % tn == 0 and SSD_CONV == 7
    return pl.pallas_call(
        functools.partial(_proj_conv_kernel, tm=tm, per_b=per_b),
        grid=(m // tm, n // tn),
        in_specs=[
            pl.BlockSpec((CONV_HALO, k), lambda i, j: (jnp.maximum(i * hb - 1, 0), 0)),
            pl.BlockSpec((tm, k), lambda i, j: (i, 0)),
            pl.BlockSpec((CONV_HALO, k), lambda i, j: (jnp.minimum((i + 1) * hb, m // CONV_HALO - 1), 0)),
            pl.BlockSpec((k, tn), lambda i, j: (0, cb0 + j)),
            pl.BlockSpec((SSD_CONV, tn), lambda i, j: (0, j)),
            pl.BlockSpec((1, tn), lambda i, j: (0, j)),
        ],
        out_specs=pl.BlockSpec((tm, tn), lambda i, j: (i, j)),
        out_shape=jax.ShapeDtypeStruct((m, n), BF16),
        compiler_params=_cparams(("parallel", "parallel")),
        name="proj_conv_silu",
    )(h, h, h, w, cw, cb.reshape(1, n))


def _softplus(x):
    return jnp.maximum(x, 0.0) + jnp.log1p(jnp.exp(-jnp.abs(x)))


def _ssd_prep_kernel(dtr_ref, dtb_ref, alog_ref, ccol_ref, rt_ref, esc_ref, wst_ref, etot_ref, *, t):
    half = LANES // 2
    log2e = math.log2(math.e)
    lane = lax.broadcasted_iota(jnp.int32, (t, LANES), 1)
    fwd = lane < SSD_HEADS
    tri = (lax.broadcasted_iota(jnp.int32, (t, t), 1) <= lax.broadcasted_iota(jnp.int32, (t, t), 0)).astype(BF16)
    neg_a = jnp.exp(alog_ref[...])
    for ci in range(dtr_ref.shape[0] // t):
        rs = slice(ci * t, (ci + 1) * t)
        dt = _softplus(dtr_ref[rs, :] + dtb_ref[...])
        la = jnp.where(lane < 2 * SSD_HEADS, -(dt * neg_a), 0.0)
        hi = la.astype(BF16)
        r1 = la - hi.astype(F32)
        mid = r1.astype(BF16)
        lo = (r1 - mid.astype(F32)).astype(BF16)
        packed = (hi.astype(F32) + pltpu.roll(mid.astype(F32), half, 1)).astype(BF16)
        res = jnp.dot(tri, jnp.concatenate([packed, lo], axis=1), preferred_element_type=F32)
        a0 = res[:, :LANES]
        acs = a0 + pltpu.roll(a0, half, 1) + res[:, LANES:]
        exb = acs - la
        ldt = jnp.log(dt)
        tot = acs[t - 1:t, :]
        ccol_ref[rs, :] = jnp.where(fwd, acs, exb) * log2e
        rt_ref[ci] = (jnp.where(fwd, acs - ldt, exb + ldt) * log2e).T
        esc_ref[rs, :] = jnp.exp(jnp.where(fwd, acs, tot - exb))
        wst_ref[rs, :] = jnp.exp(jnp.where(fwd, tot - acs, exb)) * dt
        etot_ref[ci] = jnp.broadcast_to(jnp.exp(tot), etot_ref.shape[1:])


def _split2(v):
    hi = v.astype(BF16)
    return jnp.concatenate([hi, (v - hi.astype(F32)).astype(BF16)], axis=1)


def _head_expand_tables():
    j = np.arange(2 * LANES)[:, None] % LANES
    c = np.arange(SSD_INNER)[None, :]
    fwd = (j == c // SSD_HD)
    bwd = (j == SSD_HEADS + c // SSD_HD)
    return jnp.asarray(fwd, BF16), jnp.asarray(bwd, BF16)


def _ssd_bwd_state_kernel(xs_ref, b_ref, wst_ref, etot_ref, selb_ref, gin_ref, g_sc):
    @pl.when(pl.program_id(0) == 0)
    def _():
        g_sc[...] = jnp.zeros(g_sc.shape, F32)

    cps = etot_ref.shape[1]
    t = wst_ref.shape[1] // cps
    for sc, b in [(sc, b) for sc in reversed(range(cps)) for b in range(xs_ref.shape[0])]:
        rows = slice(sc * t, (sc + 1) * t)
        gin_ref[b, sc] = g_sc[b].astype(gin_ref.dtype)
        both = jnp.concatenate([_split2(wst_ref[b, rows, :]), _split2(etot_ref[b, sc])], axis=0)
        spread = jnp.dot(both, selb_ref[...], preferred_element_type=F32)
        xw = (xs_ref[b, rows, :].astype(F32) * spread[:t]).astype(BF16)
        dec = spread[t:t + 1]
        for g in range(SSD_GROUPS):
            gs = slice(g * SSD_GW, (g + 1) * SSD_GW)
            bm_t = b_ref[b, rows, g * SSD_STATE:(g + 1) * SSD_STATE].T
            st = jnp.dot(bm_t, xw[:, gs], preferred_element_type=F32)
            g_sc[b, :, gs] = g_sc[b, :, gs] * dec[:, gs] + st


def _ssd_main_kernel(xs_ref, b_ref, c_ref, ccol_ref, rt_ref, esc_ref, wst_ref, etot_ref, gin_ref,
                     dsk_ref, self_ref, selb_ref, y_ref, h_sc):
    @pl.when(pl.program_id(1) == 0)
    def _():
        h_sc[...] = jnp.zeros(h_sc.shape, F32)

    t = rt_ref.shape[-1]
    li = lax.broadcasted_iota(jnp.int32, (t, t), 0)
    si = lax.broadcasted_iota(jnp.int32, (t, t), 1)
    mask_f = jnp.where(li >= si, 0.0, MASK_VALUE)
    mask_b = jnp.where(si >= li, 0.0, MASK_VALUE)
    lo_half = lax.broadcasted_iota(jnp.int32, (t, LANES), 1) < SSD_HD

    for sc, g in [(sc, g) for sc in range(rt_ref.shape[0]) for g in range(SSD_GROUPS)]:
        rows = slice(sc * t, (sc + 1) * t)
        ccol = ccol_ref[rows, :]
        r_t = rt_ref[sc]
        esc2 = _split2(esc_ref[rows, :])
        wst2 = _split2(wst_ref[rows, :])
        etot2 = _split2(etot_ref[sc])
        gs = slice(g * SSD_GW, (g + 1) * SSD_GW)
        bm = b_ref[rows, g * SSD_STATE:(g + 1) * SSD_STATE]
        cm = c_ref[rows, g * SSD_STATE:(g + 1) * SSD_STATE]
        cb = lax.dot_general(cm, bm, (((1,), (1,)), ((), ())), preferred_element_type=F32)
        hf = h_sc[:, gs]
        yf_all = jnp.dot(cm, hf.astype(BF16), preferred_element_type=F32)
        yb_all = jnp.dot(cm, gin_ref[sc, :, gs], preferred_element_type=F32)
        e_in = jnp.dot(esc2, self_ref[g], preferred_element_type=F32)
        e_out = jnp.dot(esc2, selb_ref[g], preferred_element_type=F32)
        w_state = jnp.dot(wst2, self_ref[g], preferred_element_type=F32)
        dec = jnp.dot(etot2, self_ref[g], preferred_element_type=F32)[0:1]
        xw = []
        for k in range(SSD_HPG // 2):
            ea = g * SSD_HPG + 2 * k
            sl = slice(g * SSD_GW + k * LANES, g * SSD_GW + (k + 1) * LANES)
            ks = slice(k * LANES, (k + 1) * LANES)
            xp = xs_ref[rows, sl]
            mms = []
            for e in (ea, ea + 1):
                eb = SSD_HEADS + e
                w = (jnp.exp2(ccol[:, e:e + 1] - r_t[e:e + 1, :] + mask_f)
                     + jnp.exp2(r_t[eb:eb + 1, :] - ccol[:, eb:eb + 1] + mask_b))
                mms.append((cb * w).astype(BF16))
            zero = jnp.zeros_like(xp)
            xx = jnp.concatenate([jnp.where(lo_half, xp, zero), jnp.where(lo_half, zero, xp)], axis=0)
            ypair = jnp.dot(jnp.concatenate(mms, axis=1), xx, preferred_element_type=F32)
            xf = xp.astype(F32)
            y_ref[rows, sl] = (ypair + yf_all[:, ks] * e_in[:, ks] + yb_all[:, ks] * e_out[:, ks]
                               + xf * dsk_ref[:, sl])
            xw.append((xf * w_state[:, ks]).astype(BF16))
        st = jnp.dot(bm.T, jnp.concatenate(xw, axis=1), preferred_element_type=F32)
        h_sc[:, gs] = hf * dec + st


def _ssd(conv_out, dt_raw, dtb, alog, dskip, bsz, seq):
    t = SSD_CHUNK
    nc = seq // t
    assert seq % t == 0
    gn = SSD_GROUPS * SSD_STATE
    b_blk = SSD_INNER // gn
    c_blk = b_blk + 1
    cpp = min(8, nc)
    assert nc % cpp == 0
    sub8 = SUBLANES
    const_spec = pl.BlockSpec((1, LANES), lambda b, i: (0, 0))
    rows_spec = pl.BlockSpec((None, cpp * t, LANES), lambda b, i: (b, i, 0))
    row_arr = jax.ShapeDtypeStruct((bsz, seq, LANES), F32)
    ccol, rt, esc, wst, etot = pl.pallas_call(
        functools.partial(_ssd_prep_kernel, t=t),
        grid=(bsz, nc // cpp),
        in_specs=[rows_spec, const_spec, const_spec],
        out_specs=(rows_spec, pl.BlockSpec((None, cpp, LANES, t), lambda b, i: (b, i, 0, 0)), rows_spec, rows_spec,
                   pl.BlockSpec((None, cpp, sub8, LANES), lambda b, i: (b, i, 0, 0))),
        out_shape=(row_arr, jax.ShapeDtypeStruct((bsz, nc, LANES, t), F32), row_arr, row_arr,
                   jax.ShapeDtypeStruct((bsz, nc, sub8, LANES), F32)),
        compiler_params=_cparams(("parallel", "parallel")),
        name="ssd_decay_terms",
    )(dt_raw, dtb, alog)

    cps = next(n for n in (4, 2, 1) if nc % n == 0)
    rev = lambda c: nc // cps - 1 - c
    sel_f, sel_b = _head_expand_tables()
    by_group = lambda a: a.reshape(a.shape[0], SSD_GROUPS, SSD_GW).transpose(1, 0, 2)
    sel_spec = pl.BlockSpec((SSD_GROUPS, sel_f.shape[0], SSD_GW), lambda b, c: (0, 0, 0))
    gin = pl.pallas_call(
        _ssd_bwd_state_kernel,
        grid=(nc // cps,),
        in_specs=[
            pl.BlockSpec((bsz, cps * t, SSD_INNER), lambda c: (0, rev(c), 0)),
            pl.BlockSpec((bsz, cps * t, gn), lambda c: (0, rev(c), b_blk)),
            pl.BlockSpec((bsz, cps * t, LANES), lambda c: (0, rev(c), 0)),
            pl.BlockSpec((bsz, cps, sub8, LANES), lambda c: (0, rev(c), 0, 0)),
            pl.BlockSpec(sel_b.shape, lambda c: (0, 0)),
        ],
        out_specs=pl.BlockSpec((bsz, cps, SSD_STATE, SSD_INNER), lambda c: (0, rev(c), 0, 0)),
        out_shape=jax.ShapeDtypeStruct((bsz, nc, SSD_STATE, SSD_INNER), BF16),
        scratch_shapes=[pltpu.VMEM((bsz, SSD_STATE, SSD_INNER), F32)],
        compiler_params=_cparams(("arbitrary",)),
        name="ssd_bwd_states",
    )(conv_out, conv_out, wst, etot, sel_b)

    wide_spec = pl.BlockSpec((1, SSD_INNER), lambda b, c: (0, 0))
    tok_spec = pl.BlockSpec((None, cps * t, LANES), lambda b, c: (b, c, 0))
    return pl.pallas_call(
        _ssd_main_kernel,
        grid=(bsz, nc // cps),
        in_specs=[
            pl.BlockSpec((None, cps * t, SSD_INNER), lambda b, c: (b, c, 0)),
            pl.BlockSpec((None, cps * t, gn), lambda b, c: (b, c, b_blk)),
            pl.BlockSpec((None, cps * t, gn), lambda b, c: (b, c, c_blk)),
            tok_spec,
            pl.BlockSpec((None, cps, LANES, t), lambda b, c: (b, c, 0, 0)),
            tok_spec, tok_spec,
            pl.BlockSpec((None, cps, sub8, LANES), lambda b, c: (b, c, 0, 0)),
            pl.BlockSpec((None, cps, SSD_STATE, SSD_INNER), lambda b, c: (b, c, 0, 0)),
            wide_spec, sel_spec, sel_spec,
        ],
        out_specs=pl.BlockSpec((None, cps * t, SSD_INNER), lambda b, c: (b, c, 0)),
        out_shape=jax.ShapeDtypeStruct((bsz, seq, SSD_INNER), F32),
        scratch_shapes=[pltpu.VMEM((SSD_STATE, SSD_INNER), F32)],
        compiler_params=_cparams(("parallel", "arbitrary")),
        name="ssd_main",
    )(conv_out, conv_out, conv_out, ccol, rt, esc, wst, etot, gin, dskip, by_group(sel_f), by_group(sel_b))


def _dft_tables(seq):
    n1, n2 = seq // F_N2, F_N2
    c = np.arange(F_GD)
    ang = 2 * np.pi * np.outer(c, c) / F_GD
    chan = np.concatenate([np.cos(ang), -np.sin(ang)], axis=1) / math.sqrt(F_GD)
    k1 = np.arange(n1)
    a1 = 2 * np.pi * np.outer(k1, k1) / n1
    stage_a = np.block([[np.cos(a1), np.sin(a1)], [-np.sin(a1), np.cos(a1)]])
    stage_a = np.kron(stage_a, np.eye(2))
    s2 = np.arange(n2)
    at = 2 * np.pi * np.outer(s2, k1) / seq
    at = at.reshape(n2 // 2, 2, n1).transpose(0, 2, 1).reshape(n2 // 2, 2 * n1)
    tw_c, tw_s = np.cos(at)[..., None], np.sin(at)[..., None]
    a2 = 2 * np.pi * np.outer(s2, s2) / n2
    stage_b = np.concatenate([np.cos(a2), np.sin(a2)], axis=1)
    return (jnp.asarray(chan, BF16), jnp.asarray(stage_a, BF16), jnp.asarray(tw_c, F32),
            jnp.asarray(tw_s, F32), jnp.asarray(stage_b, BF16))


def _fft_a_kernel(z_ref, ma_ref, twc_ref, tws_ref, o_ref, zs, ys, *, n1, rb):
    ct = z_ref.shape[-1]
    half = rb // 2
    _put_rows(zs, pltpu.bitcast(z_ref[...].reshape(2 * n1 * rb, ct), jnp.uint32))
    for j in range(half):
        zz = pltpu.bitcast(_get_rows(zs, j, 2 * n1, half), BF16)
        y = jnp.dot(ma_ref[...], zz, preferred_element_type=F32)
        yr, yi = y[:2 * n1], y[2 * n1:]
        tc, ts = twc_ref[j], tws_ref[j]
        out = jnp.concatenate([yr * tc + yi * ts, yi * tc - yr * ts], axis=0).astype(BF16)
        _put_rows(ys, pltpu.bitcast(out, jnp.uint32), j, half)
    o_ref[...] = pltpu.bitcast(_get_rows(ys), BF16).reshape(2, n1, rb, ct)


def _fft_b_kernel(y_ref, mb_ref, o_ref, os_ref, *, n2, kb, scale):
    ct = y_ref.shape[-1]
    for kk in range(kb):
        yy = y_ref[:, kk].reshape(2 * n2, ct)
        _put_rows(os_ref, jnp.dot(mb_ref[...], yy, preferred_element_type=F32) * scale, kk, kb)
    o_ref[...] = _get_rows(os_ref).reshape(n2, kb, ct).astype(o_ref.dtype)


def _fourier_seq(z2, tabs, bsz, seq, *, rb=BF16_ROWS, kb=BF16_ROWS, ct_a=1536, ct_b=768):
    _, ma, twc, tws, mb = tabs
    n1, n2 = seq // F_N2, F_N2
    w = z2.shape[-1]
    ct, nct = ct_a, w // ct_a
    za = z2.reshape(2, bsz, n1, n2, w)
    ab_spec = pl.BlockSpec((2, None, n1, rb, ct), lambda b, a, j: (0, b, 0, a, j))
    ya = pl.pallas_call(
        functools.partial(_fft_a_kernel, n1=n1, rb=rb),
        grid=(bsz, n2 // rb, nct),
        in_specs=[
            ab_spec,
            pl.BlockSpec((4 * n1, 4 * n1), lambda b, a, j: (0, 0)),
            pl.BlockSpec((rb // 2, 2 * n1, 1), lambda b, a, j: (a, 0, 0)),
            pl.BlockSpec((rb // 2, 2 * n1, 1), lambda b, a, j: (a, 0, 0)),
        ],
        out_specs=ab_spec,
        out_shape=jax.ShapeDtypeStruct((2, bsz, n1, n2, w), BF16),
        scratch_shapes=[pltpu.VMEM((ct // LANES, n1 * rb, LANES), jnp.uint32)] * 2,
        compiler_params=_cparams(("parallel", "parallel", "parallel")),
        name="fourier_stage_a",
    )(za, ma, twc, tws)
    ct, nct = ct_b, w // ct_b
    out = pl.pallas_call(
        functools.partial(_fft_b_kernel, n2=n2, kb=kb, scale=1.0 / math.sqrt(seq)),
        grid=(bsz, n1 // kb, nct),
        in_specs=[
            pl.BlockSpec((2, None, kb, n2, ct), lambda b, k, j: (0, b, k, 0, j)),
            pl.BlockSpec((n2, 2 * n2), lambda b, k, j: (0, 0)),
        ],
        out_specs=pl.BlockSpec((None, n2, kb, ct), lambda b, k, j: (b, 0, k, j)),
        out_shape=jax.ShapeDtypeStruct((bsz, n2, n1, w), BF16),
        scratch_shapes=[pltpu.VMEM((ct // LANES, n2 * kb, LANES), F32)],
        compiler_params=_cparams(("parallel", "parallel", "parallel")),
        name="fourier_stage_b",
    )(ya, mb)
    return out.reshape(bsz, seq, w)


def _merge_kernel(*refs, dils):
    ng = len(dils)
    x_ref = refs[0]
    o_refs = refs[1:1 + ng]
    l_refs = refs[1 + ng:1 + 2 * ng]
    y_ref, z_ref, yg_ref, f_ref, gt_ref, wa_ref, ws_ref, wf_ref, wo_ref, out_ref, o_sc, l_sc = refs[1 + 2 * ng:]
    tm, d = x_ref.shape

    yf = jnp.dot(f_ref[...], wf_ref[...], preferred_element_type=F32)

    ys = None
    for g in range(SSD_GROUPS):
        gs = slice(g * SSD_GW, (g + 1) * SSD_GW)
        yy = y_ref[:, gs] * _silu(z_ref[:, gs].astype(F32))
        part = jnp.dot(_rms(yy, yg_ref[:, gs]).astype(BF16), ws_ref[gs, :], preferred_element_type=F32)
        ys = part if ys is None else ys + part

    for gi, dil in enumerate(dils):
        rows = tm // dil
        idx = lambda r: slice(None) if dil == 1 else pl.ds(r, rows, stride=dil)
        for r in range(dil):
            l_sc[gi, idx(r), :] = l_refs[gi][:, r * LANES:(r + 1) * LANES]
            for hh in range(ATTN_HPG):
                c0 = r * ATTN_OUT + hh * ATTN_HD
                o_sc[gi * ATTN_HPG + hh, idx(r), :] = o_refs[gi][:, c0:c0 + ATTN_HD].astype(F32)

    ls = [l_sc[gi] for gi in range(ng)]
    mx = functools.reduce(jnp.maximum, ls)
    ws = [jnp.exp(v - mx) for v in ls]
    inv = 1.0 / functools.reduce(lambda a, b: a + b, ws)
    heads = []
    for hh in range(ATTN_HPG):
        acc = None
        for gi in range(ng):
            alpha = (ws[gi] * inv)[:, hh:hh + 1]
            term = alpha * o_sc[gi * ATTN_HPG + hh]
            acc = term if acc is None else acc + term
        heads.append(acc.astype(BF16))
    o_attn = jnp.concatenate(heads, axis=1)

    ya = jnp.dot(o_attn, wa_ref[...], preferred_element_type=F32)
    gates = gt_ref[...].astype(F32)
    merged = gates[:, :d] * ya + gates[:, d:2 * d] * ys + gates[:, 2 * d:] * yf
    out_ref[...] = x_ref[...] + jnp.dot(merged.astype(BF16), wo_ref[...], preferred_element_type=F32)


def _merge(x, os_, ls_, dils, y, z, y_gain, f, gates, wa, ws, wf, wo, bsz, seq, *, tm=512):
    d = x.shape[-1]
    assert all((tm // dil) % BF16_ROWS == 0 for dil in dils)
    tok = lambda width: pl.BlockSpec((None, tm, width), lambda b, i: (b, i, 0))
    full = lambda a: pl.BlockSpec(a.shape, lambda b, i: (0, 0), pipeline_mode=pl.Buffered(1))
    ng = len(dils)
    in_specs = [tok(d)]
    in_specs += [pl.BlockSpec((None, tm // dil, dil * ATTN_OUT), lambda b, i: (b, i, 0)) for dil in dils]
    in_specs += [pl.BlockSpec((None, tm // dil, dil * LANES), lambda b, i: (b, i, 0)) for dil in dils]
    in_specs += [tok(y.shape[-1]), tok(z.shape[-1]), full(y_gain), tok(f.shape[-1]), tok(gates.shape[-1]),
                 full(wa), full(ws), full(wf), full(wo)]
    return pl.pallas_call(
        functools.partial(_merge_kernel, dils=tuple(dils)),
        grid=(bsz, seq // tm),
        in_specs=in_specs,
        out_specs=tok(d),
        out_shape=jax.ShapeDtypeStruct((bsz, seq, d), F32),
        scratch_shapes=[pltpu.VMEM((ng * ATTN_HPG, tm, ATTN_HD), F32), pltpu.VMEM((ng, tm, LANES), F32)],
        compiler_params=_cparams(("parallel", "parallel")),
        name="branch_merge",
    )(x, *os_, *ls_, y, z, y_gain, f, gates, wa, ws, wf, wo)


def _xattn_kernel(x_ref, g_ref, wq_ref, qg_ref, k_ref, v_ref, wo_ref, out_ref):
    tm, d = x_ref.shape
    hd = d // MEM_HEADS
    scale = hd ** -0.5
    rc = min(tm, XATTN_ROW_CHUNK)
    for c in range(tm // rc):
        rs = slice(c * rc, (c + 1) * rc)
        x = x_ref[rs, :]
        h = _rms(x, g_ref[...]).astype(BF16)
        q = jnp.dot(h, wq_ref[...], preferred_element_type=F32)
        heads = []
        for hh in range(MEM_HEADS):
            sl = slice(hh * hd, (hh + 1) * hd)
            qn = _rms(q[:, sl], qg_ref[...]).astype(BF16)
            s = lax.dot_general(qn, k_ref[:, sl], (((1,), (1,)), ((), ())), preferred_element_type=F32) * scale
            m = jnp.max(s, axis=-1, keepdims=True)
            p = jnp.exp(s - m)
            l = jnp.sum(p, axis=-1, keepdims=True)
            heads.append((jnp.dot(p.astype(BF16), v_ref[:, sl], preferred_element_type=F32) / l).astype(BF16))
        o = jnp.concatenate(heads, axis=1)
        out_ref[rs, :] = x + jnp.dot(o, wo_ref[...], preferred_element_type=F32)


def _xattn(x, g, wq, qg, k, v, wo, *, tm=1024):
    bsz, seq, d = x.shape
    mt = k.shape[1]
    full = lambda a: pl.BlockSpec(a.shape, lambda b, i: (0, 0), pipeline_mode=pl.Buffered(1))
    g = g.reshape(1, d)
    qg = qg.reshape(1, -1)
    return pl.pallas_call(
        _xattn_kernel,
        grid=(bsz, seq // tm),
        in_specs=[
            pl.BlockSpec((None, tm, d), lambda b, i: (b, i, 0)),
            full(g), full(wq), full(qg),
            pl.BlockSpec((None, mt, d), lambda b, i: (b, 0, 0)),
            pl.BlockSpec((None, mt, d), lambda b, i: (b, 0, 0)),
            full(wo),
        ],
        out_specs=pl.BlockSpec((None, tm, d), lambda b, i: (b, i, 0)),
        out_shape=jax.ShapeDtypeStruct((bsz, seq, d), F32),
        compiler_params=_cparams(("parallel", "parallel")),
        name="mem_xattn",
    )(x, g, wq, qg, k, v, wo)


def _ffn_kernel(x_ref, g_ref, wg_ref, wu_ref, wd_ref, *rest, tf):
    x = x_ref[...]
    h = _rms(x, g_ref[...]).astype(BF16)
    acc = x
    for c in range(wg_ref.shape[1] // tf):
        sl = slice(c * tf, (c + 1) * tf)
        gt = jnp.dot(h, wg_ref[:, sl], preferred_element_type=F32)
        up = jnp.dot(h, wu_ref[:, sl], preferred_element_type=F32)
        a = (_silu(gt) * up).astype(BF16)
        acc = acc + jnp.dot(a, wd_ref[sl, :], preferred_element_type=F32)
    if len(rest) == 1:
        rest[0][...] = acc
    else:
        gn_ref, out_ref, hn_ref = rest
        out_ref[...] = acc
        hn_ref[...] = _rms(acc, gn_ref[...]).astype(hn_ref.dtype)


def _ffn(x, g, wg, wu, wd, next_gain=None, *, tm=512):
    m, d = x.shape
    dff = wg.shape[1]
    tf = dff // 2 if (dff // 2) % LANES == 0 else dff
    full = lambda a: pl.BlockSpec(a.shape, lambda i: (0, 0), pipeline_mode=pl.Buffered(1))
    tok = pl.BlockSpec((tm, d), lambda i: (i, 0))
    args = [x, g.reshape(1, d), wg, wu, wd]
    out_shape, out_specs = jax.ShapeDtypeStruct((m, d), F32), tok
    if next_gain is not None:
        args.append(next_gain.reshape(1, d))
        out_shape, out_specs = (out_shape, jax.ShapeDtypeStruct((m, d), BF16)), (tok, tok)
    return pl.pallas_call(
        functools.partial(_ffn_kernel, tf=tf),
        grid=(m // tm,),
        in_specs=[tok] + [full(a) for a in args[1:]],
        out_specs=out_specs,
        out_shape=out_shape,
        compiler_params=_cparams(("parallel",)),
        name="swiglu_ffn",
    )(*args)


def _pad_lanes(a):
    a = a.reshape(-1, 2 * SSD_HEADS)
    return jnp.pad(a, ((0, 0), (0, LANES - 2 * SSD_HEADS)))


def kernel(x, mem, rel_bias, mix_norm_g, w_in, gate_bias, attn_q_norm_g, attn_k_norm_g, conv_w, conv_b, dt_bias, a_log, d_skip, ssd_norm_g, w_branch_attn, w_branch_ssd, w_branch_fourier, w_mix_out, xattn_norm_g, mem_norm_g, w_xq, w_xk, w_xv, xattn_q_norm_g, xattn_k_norm_g, w_xo, ffn_norm_g, w_ffn_gate, w_ffn_up, w_ffn_down):
    bsz, seq, d = x.shape
    depth = w_in.shape[0]
    m = bsz * seq
    xf = x.reshape(m, d)
    memf = mem.reshape(bsz * mem.shape[1], d)
    tabs = _dft_tables(seq)
    dils = [dil for _, dil in ATTN_GROUPS]
    biases = [_attn_bias(rel_bias, gi, dil) for gi, dil in enumerate(dils)]
    offs = np.cumsum([0, ATTN_WIDTH, ATTN_WIDTH, ATTN_WIDTH, SSD_INNER, SSD_CONV_CH, 2 * SSD_HEADS, F_WIDTH, 3 * d])
    bf = lambda a: a.astype(BF16)

    for l in range(depth):
        wl = bf(w_in[l])
        seg = lambda i: wl[:, offs[i]:offs[i + 1]]
        if l == 0:
            h = _rmsnorm(xf, mix_norm_g[l])
        gains = jnp.stack([attn_q_norm_g[l], attn_k_norm_g[l], jnp.ones_like(attn_q_norm_g[l])]).reshape(3, 1, ATTN_HD)
        os_, ls_ = [], []
        for gi, dil in enumerate(dils):
            qkv = _qkv_group(h, wl, gains, gi, dil, bsz, seq)
            o_g, lse_g = _attn_group(qkv, biases[gi], dil, bsz, seq)
            os_.append(o_g)
            ls_.append(lse_g)

        z = _matmul(h, wl, out_dtype=BF16, tn=512, tm=2048, col0=int(offs[3]), n=SSD_INNER)
        dt_raw = _matmul(h, _pad_lanes(seg(5)), out_dtype=F32, tn=LANES, tm=2048)
        z2 = _matmul(h, seg(6), out_dtype=BF16, tn=768, tm=2048, epilogue="chandft", extra=tabs[0],
                     row_chunk=2 * MM_ROW_CHUNK)
        gates = _matmul(h, seg(7), out_dtype=BF16, tn=1024, tm=2048, epilogue="sigmoid_bias", extra=gate_bias[l])

        conv_out = _proj_conv_silu(h, wl, conv_w[l], conv_b[l], seq, col0=int(offs[4])).reshape(bsz, seq, SSD_CONV_CH)
        y_ssd = _ssd(conv_out, dt_raw.reshape(bsz, seq, -1), _pad_lanes(dt_bias[l]), _pad_lanes(a_log[l]),
                     jnp.repeat(d_skip[l], SSD_HD).reshape(1, SSD_INNER), bsz, seq)

        f_re = _fourier_seq(z2, tabs, bsz, seq)

        x3 = _merge(xf.reshape(bsz, seq, d), os_, ls_, dils, y_ssd, z.reshape(bsz, seq, SSD_INNER),
                    ssd_norm_g[l].reshape(1, SSD_INNER), f_re, gates.reshape(bsz, seq, 3 * d),
                    bf(w_branch_attn[l]), bf(w_branch_ssd[l]), bf(w_branch_fourier[l]), bf(w_mix_out[l]), bsz, seq)

        hd_m = d // MEM_HEADS
        hm = _rmsnorm(memf, mem_norm_g[l])
        km = _matmul(hm, bf(w_xk[l]), out_dtype=BF16, tn=512, epilogue="headnorm",
                     extra=xattn_k_norm_g[l], head_dim=hd_m)
        vm = _matmul(hm, bf(w_xv[l]), out_dtype=BF16, tn=512)
        x3 = _xattn(x3, xattn_norm_g[l], bf(w_xq[l]), xattn_q_norm_g[l], km.reshape(bsz, -1, d),
                    vm.reshape(bsz, -1, d), bf(w_xo[l]))

        ffn_args = (x3.reshape(m, d), ffn_norm_g[l], bf(w_ffn_gate[l]), bf(w_ffn_up[l]), bf(w_ffn_down[l]))
        if l + 1 < depth:
            xf, h = _ffn(*ffn_args, mix_norm_g[l + 1])
        else:
            xf = _ffn(*ffn_args)

    return xf.reshape(bsz, seq, d)
```

```python
import functools
import math

import numpy as np
import jax
import jax.numpy as jnp
from jax import lax
from jax.experimental import pallas as pl
from jax.experimental.pallas import tpu as pltpu

F32 = jnp.float32
BF16 = jnp.bfloat16

NORM_EPS = 1e-6
MASK_VALUE = -1e30

ATTN_GROUPS = ((128, 1), (512, 4), (2048, 16))
ATTN_HPG = 4
ATTN_HD = 128
ATTN_HEADS = ATTN_HPG * len(ATTN_GROUPS)
ATTN_WIDTH = ATTN_HEADS * ATTN_HD
ATTN_OUT = ATTN_HPG * ATTN_HD
ATTN_HALF = 64
ATTN_TQ = 2 * ATTN_HALF
ATTN_TK = ATTN_TQ + 2 * ATTN_HALF
NUM_BUCKETS = 32
MAX_DISTANCE = 1024

SSD_HEADS = 32
SSD_HD = 64
SSD_GROUPS = 4
SSD_HPG = SSD_HEADS // SSD_GROUPS
SSD_STATE = 128
SSD_INNER = SSD_HEADS * SSD_HD
SSD_GW = SSD_INNER // SSD_GROUPS
SSD_CONV = 7
SSD_CHUNK = 128
SSD_CONV_CH = SSD_INNER + 2 * SSD_GROUPS * SSD_STATE
LANES = 128
SUBLANES = 8
BF16_ROWS = 16

F_GROUPS = 6
F_GD = 256
F_WIDTH = F_GROUPS * F_GD
F_N2 = 128

MEM_HEADS = 4
XATTN_ROW_CHUNK = 256

VMEM_LIMIT = 56 * 1024 * 1024


def _cparams(sem):
    return pltpu.CompilerParams(dimension_semantics=sem, vmem_limit_bytes=VMEM_LIMIT)


def _silu(x):
    return x * jax.nn.sigmoid(x)


def _rms(x, g):
    ms = jnp.mean(x * x, axis=-1, keepdims=True)
    return x * lax.rsqrt(ms + NORM_EPS) * g


def _get_rows(ref3, start=0, size=None, stride=1):
    idx = slice(None) if size is None else pl.ds(start, size, stride=stride)
    return jnp.concatenate([ref3[c, idx, :] for c in range(ref3.shape[0])], axis=1)


def _put_rows(ref3, val, start=0, stride=1):
    size = val.shape[0]
    idx = slice(None) if (stride == 1 and size == ref3.shape[1]) else pl.ds(start, size, stride=stride)
    for c in range(ref3.shape[0]):
        ref3[c, idx, :] = val[:, c * LANES:(c + 1) * LANES]


def _rmsnorm_kernel(x_ref, g_ref, o_ref):
    o_ref[...] = _rms(x_ref[...], g_ref[...]).astype(o_ref.dtype)


def _rmsnorm(x, g, *, tm=1024):
    m, k = x.shape
    tm = min(tm, m)
    return pl.pallas_call(
        _rmsnorm_kernel,
        grid=(m // tm,),
        in_specs=[pl.BlockSpec((tm, k), lambda i: (i, 0)), pl.BlockSpec((1, k), lambda i: (0, 0))],
        out_specs=pl.BlockSpec((tm, k), lambda i: (i, 0)),
        out_shape=jax.ShapeDtypeStruct((m, k), BF16),
        compiler_params=_cparams(("parallel",)),
        name="rmsnorm",
    )(x, g.reshape(1, k))


MM_ROW_CHUNK = 256


def _mm_kernel(*refs, epilogue, head_dim, row_chunk):
    h_ref, w_ref = refs[:2]
    o_ref = refs[-1]
    tm = h_ref.shape[0]
    rc = min(row_chunk, tm)
    for c in range(tm // rc):
        rs = slice(c * rc, (c + 1) * rc)
        acc = jnp.dot(h_ref[rs, :], w_ref[...], preferred_element_type=F32)
        tn = acc.shape[1]
        if epilogue == "headnorm":
            hg_ref = refs[2]
            for hh in range(tn // head_dim):
                sl = slice(hh * head_dim, (hh + 1) * head_dim)
                o_ref[rs, sl] = _rms(acc[:, sl], hg_ref[...]).astype(o_ref.dtype)
        elif epilogue == "sigmoid_bias":
            b_ref = refs[2]
            o_ref[rs, :] = jax.nn.sigmoid(acc + b_ref[...]).astype(o_ref.dtype)
        elif epilogue == "chandft":
            dft_ref = refs[2]
            a16 = acc.astype(BF16)
            for gg in range(tn // F_GD):
                sl = slice(gg * F_GD, (gg + 1) * F_GD)
                r = jnp.dot(a16[:, sl], dft_ref[...], preferred_element_type=F32)
                o_ref[0, rs, sl] = r[:, :F_GD].astype(o_ref.dtype)
                o_ref[1, rs, sl] = r[:, F_GD:].astype(o_ref.dtype)
        else:
            o_ref[rs, :] = acc.astype(o_ref.dtype)


def _matmul(h, w, *, out_dtype, tn, tm=1024, epilogue="plain", extra=None, head_dim=None, col0=0, n=None,
            row_chunk=MM_ROW_CHUNK):
    m, k = h.shape
    n = w.shape[1] if n is None else n
    tm = min(tm, m)
    assert m % tm == 0 and n % tn == 0 and col0 % tn == 0
    cb0 = col0 // tn
    in_specs = [pl.BlockSpec((tm, k), lambda i, j: (i, 0)), pl.BlockSpec((k, tn), lambda i, j: (0, cb0 + j))]
    args = [h, w]
    if epilogue == "headnorm":
        in_specs.append(pl.BlockSpec((1, head_dim), lambda i, j: (0, 0)))
        args.append(extra.reshape(1, head_dim))
    elif epilogue == "sigmoid_bias":
        in_specs.append(pl.BlockSpec((1, tn), lambda i, j: (0, j)))
        args.append(extra.reshape(1, n))
    elif epilogue == "chandft":
        in_specs.append(pl.BlockSpec(extra.shape, lambda i, j: (0, 0)))
        args.append(extra)
    if epilogue == "chandft":
        out_shape = jax.ShapeDtypeStruct((2, m, n), out_dtype)
        out_spec = pl.BlockSpec((2, tm, tn), lambda i, j: (0, i, j))
    else:
        out_shape = jax.ShapeDtypeStruct((m, n), out_dtype)
        out_spec = pl.BlockSpec((tm, tn), lambda i, j: (i, j))
    return pl.pallas_call(
        functools.partial(_mm_kernel, epilogue=epilogue, head_dim=head_dim, row_chunk=row_chunk),
        grid=(m // tm, n // tn),
        in_specs=in_specs,
        out_specs=out_spec,
        out_shape=out_shape,
        compiler_params=_cparams(("parallel", "parallel")),
        name="matmul_" + epilogue,
    )(*args)


def _qkv_kernel(h_ref, w_ref, hg_ref, o_ref, *scratch, dil):
    j = pl.program_id(1)
    tm = h_ref.shape[0]
    width = w_ref.shape[1]
    nh = width // ATTN_HD
    rc = min(max(MM_ROW_CHUNK, 2 * BF16_ROWS * dil), tm)
    rows = rc // dil
    for c in range(tm // rc):
        acc = jnp.dot(h_ref[c * rc:(c + 1) * rc, :], w_ref[...], preferred_element_type=F32)
        if dil > 1:
            scr = scratch[0]
            for hh in range(nh):
                scr[hh, c * rc:(c + 1) * rc, :] = acc[:, hh * ATTN_HD:(hh + 1) * ATTN_HD]
        for r in range(dil):
            for hh in range(nh):
                if dil == 1:
                    ph = acc[:, hh * ATTN_HD:(hh + 1) * ATTN_HD]
                else:
                    ph = scr[hh, pl.ds(c * rc + r, rows, stride=dil), :]
                val = jnp.where(j < 2, _rms(ph, hg_ref[...]), ph)
                c0 = r * width + hh * ATTN_HD
                o_ref[c * rows:(c + 1) * rows, c0:c0 + ATTN_HD] = val.astype(o_ref.dtype)


def _qkv_group(h, w, gains, gi, dil, bsz, seq, *, tm=2048):
    m, k = h.shape
    sub = seq // dil
    per_b = seq // tm
    rows = tm // dil
    assert seq % tm == 0 and rows % BF16_ROWS == 0
    return pl.pallas_call(
        functools.partial(_qkv_kernel, dil=dil),
        grid=(m // tm, 3),
        in_specs=[
            pl.BlockSpec((tm, k), lambda i, j: (i, 0)),
            pl.BlockSpec((k, ATTN_OUT), lambda i, j: (0, j * len(ATTN_GROUPS) + gi)),
            pl.BlockSpec((None, 1, ATTN_HD), lambda i, j: (j, 0, 0)),
        ],
        out_specs=pl.BlockSpec((None, None, rows, dil * ATTN_OUT), lambda i, j: (j, i // per_b, i % per_b, 0)),
        out_shape=jax.ShapeDtypeStruct((3, bsz, sub, dil * ATTN_OUT), BF16),
        scratch_shapes=[pltpu.VMEM((ATTN_HPG, tm, ATTN_HD), F32)] if dil > 1 else [],
        compiler_params=_cparams(("parallel", "parallel")),
        name=f"qkv_proj_d{dil}",
    )(h, w, gains)


def _t5_bucket_np(rel):
    half_b = NUM_BUCKETS // 2
    exact = half_b // 2
    dist = np.abs(rel)
    log_ratio = np.log(np.maximum(dist, 1) / exact) / np.log(MAX_DISTANCE / exact)
    far = np.minimum(exact + (log_ratio * (half_b - exact)).astype(np.int32), half_b - 1)
    return np.where(rel > 0, half_b, 0) + np.where(dist < exact, dist, far)


def _attn_bias(rel_bias, gi, dil):
    i = np.arange(ATTN_TQ)[:, None]
    j = np.arange(ATTN_TK)[None, :]
    rel = j - ATTN_HALF - i
    idx = _t5_bucket_np(rel * dil)
    onehot = jnp.asarray(np.eye(NUM_BUCKETS, dtype=np.float32)[idx])
    tab = rel_bias[:, gi * ATTN_HPG:(gi + 1) * ATTN_HPG].astype(F32)
    b = jnp.einsum("qkn,nh->hqk", onehot, tab, precision=lax.Precision.HIGHEST)
    band = np.broadcast_to(np.abs(rel) <= ATTN_HALF, (ATTN_TQ, ATTN_TK))
    keep = np.stack([band & ((j >= ATTN_HALF) | (not first)) & ((j < ATTN_TK - ATTN_HALF) | (not last))
                     for last in (False, True) for first in (False, True)])
    return jnp.where(jnp.asarray(keep)[:, None], b[None], MASK_VALUE)


def _attn_kernel(q_ref, kp_ref, kc_ref, kn_ref, vp_ref, vc_ref, vn_ref, bias_ref, o_ref, lse_ref, kpad, vpad,
                 *, tile, sub):
    i = pl.program_id(2)
    hf = ATTN_HALF
    kpad[0:hf, :] = kp_ref[...]
    kpad[hf:hf + tile, :] = kc_ref[...]
    kpad[hf + tile:, :] = kn_ref[...]
    vpad[0:hf, :] = vp_ref[...]
    vpad[hf:hf + tile, :] = vc_ref[...]
    vpad[hf + tile:, :] = vn_ref[...]

    scale = ATTN_HD ** -0.5
    lane = lax.broadcasted_iota(jnp.int32, (ATTN_TQ, LANES), 1)
    tiles = tile // ATTN_TQ

    def body(t, carry):
        s0 = pl.multiple_of(t * ATTN_TQ, ATTN_TQ)
        gt = i * tiles + t
        variant = (gt == 0).astype(jnp.int32) + 2 * (gt == sub // ATTN_TQ - 1).astype(jnp.int32)
        lse_tile = jnp.zeros((ATTN_TQ, LANES), F32)
        for hh in range(ATTN_HPG):
            sl = slice(hh * ATTN_HD, (hh + 1) * ATTN_HD)
            q = q_ref[pl.ds(s0, ATTN_TQ), sl]
            kw = kpad[pl.ds(s0, ATTN_TK), sl]
            vw = vpad[pl.ds(s0, ATTN_TK), sl]
            s = (lax.dot_general(q, kw, (((1,), (1,)), ((), ())), preferred_element_type=F32) * scale
                 + bias_ref[variant, hh])
            m = jnp.max(s, axis=-1, keepdims=True)
            p = jnp.exp(s - m)
            l = jnp.sum(p, axis=-1, keepdims=True)
            o = jnp.dot(p.astype(BF16), vw, preferred_element_type=F32) / l
            o_ref[pl.ds(s0, ATTN_TQ), sl] = o.astype(o_ref.dtype)
            lse_tile = jnp.where(lane == hh, m + jnp.log(l), lse_tile)
        lse_ref[pl.ds(s0, ATTN_TQ), :] = lse_tile
        return carry

    lax.fori_loop(0, tile // ATTN_TQ, body, 0, unroll=8)


def _attn_group(qkv, bias, dil, bsz, seq, *, tile=1024):
    sub = seq // dil
    tile = min(tile, sub)
    nblk = sub // tile
    hb = tile // ATTN_HALF
    last_h = sub // ATTN_HALF - 1
    assert sub % tile == 0 and tile % ATTN_TQ == 0
    cur = lambda which: pl.BlockSpec((None, None, tile, ATTN_OUT), lambda b, r, i: (which, b, i, r))
    prev = lambda which: pl.BlockSpec((None, None, ATTN_HALF, ATTN_OUT),
                                      lambda b, r, i: (which, b, jnp.maximum(i * hb - 1, 0), r))
    nxt = lambda which: pl.BlockSpec((None, None, ATTN_HALF, ATTN_OUT),
                                     lambda b, r, i: (which, b, jnp.minimum((i + 1) * hb, last_h), r))
    return pl.pallas_call(
        functools.partial(_attn_kernel, tile=tile, sub=sub),
        grid=(bsz, dil, nblk),
        in_specs=[cur(0), prev(1), cur(1), nxt(1), prev(2), cur(2), nxt(2),
                  pl.BlockSpec(bias.shape, lambda b, r, i: (0, 0, 0, 0))],
        out_specs=(pl.BlockSpec((None, tile, ATTN_OUT), lambda b, r, i: (b, i, r)),
                   pl.BlockSpec((None, tile, LANES), lambda b, r, i: (b, i, r))),
        out_shape=(jax.ShapeDtypeStruct((bsz, sub, dil * ATTN_OUT), BF16),
                   jax.ShapeDtypeStruct((bsz, sub, dil * LANES), F32)),
        scratch_shapes=[pltpu.VMEM((tile + 2 * ATTN_HALF, ATTN_OUT), BF16)] * 2,
        compiler_params=_cparams(("parallel", "parallel", "parallel")),
        name=f"dilated_attn_d{dil}",
    )(qkv, qkv, qkv, qkv, qkv, qkv, qkv, bias)


CONV_HALO = BF16_ROWS
CONV_RC = 128
CONV_APRON = 8
CONV_PROJ_ROWS = 512


def _shift_rows(x, s):
    r, c = x.shape
    x3 = x.reshape(r // SUBLANES, SUBLANES, c)
    rot = pltpu.roll(x3, (-s) % SUBLANES, 1)
    sub = lax.broadcasted_iota(jnp.int32, x3.shape, 1)
    if s > 0:
        y = jnp.where(sub < SUBLANES - s, rot, jnp.concatenate([rot[1:], rot[:1]], axis=0))
    else:
        y = jnp.where(sub >= -s, rot, jnp.concatenate([rot[-1:], rot[:-1]], axis=0))
    return y.reshape(r, c)


def _proj_conv_kernel(hp_ref, hc_ref, hn_ref, w_ref, cw_ref, cb_ref, o_ref, *, tm, per_b):
    i = pl.program_id(0)
    first = (i % per_b) == 0
    last = (i % per_b) == per_b - 1
    w = w_ref[...]
    halo_p = jnp.dot(hp_ref[...], w, preferred_element_type=F32)
    halo_n = jnp.dot(hn_ref[...], w, preferred_element_type=F32)
    halo_p = jnp.where(first, 0.0, halo_p)
    halo_n = jnp.where(last, 0.0, halo_n)
    pad = SSD_CONV // 2
    acc = {}

    def project(s):
        acc[s] = jnp.dot(hc_ref[s * CONV_PROJ_ROWS:(s + 1) * CONV_PROJ_ROWS, :], w, preferred_element_type=F32)

    def tile_rows(lo, hi):
        pieces, r = [], lo
        while r < hi:
            if r < 0:
                src, off, end = halo_p, CONV_HALO + r, min(hi, 0)
            elif r >= tm:
                src, off, end = halo_n, r - tm, hi
            else:
                s = r // CONV_PROJ_ROWS
                src, off, end = acc[s], r - s * CONV_PROJ_ROWS, min(hi, (s + 1) * CONV_PROJ_ROWS)
            pieces.append(src[off:off + end - r])
            r = end
        return pieces[0] if len(pieces) == 1 else jnp.concatenate(pieces, axis=0)

    def conv_chunk(c):
        x = tile_rows(c * CONV_RC - CONV_APRON, (c + 1) * CONV_RC + CONV_APRON)
        tap = lambda off: cw_ref[pad + off:pad + off + 1, :]
        xm, xq = _shift_rows(x, -3), _shift_rows(x, 3)
        f0 = tap(-3) * xm + tap(0) * x + tap(3) * xq + cb_ref[...]
        f1 = tap(-2) * xm + tap(1) * x
        fm = tap(-1) * x + tap(2) * xq
        y = f0 + _shift_rows(f1, 1) + _shift_rows(fm, -1)
        o_ref[c * CONV_RC:(c + 1) * CONV_RC, :] = _silu(y[CONV_APRON:CONV_APRON + CONV_RC]).astype(o_ref.dtype)

    nsub = tm // CONV_PROJ_ROWS
    per_sub = CONV_PROJ_ROWS // CONV_RC
    project(0)
    for s in range(nsub):
        if s + 1 < nsub:
            project(s + 1)
        for c in range(s * per_sub, (s + 1) * per_sub):
            conv_chunk(c)


def _proj_conv_silu(h, w, cw, cb, seq, *, col0, tm=2048, tn=512):
    m, k = h.shape
    n = cw.shape[1]
    assert col0 % tn == 0
    cb0 = col0 // tn
    per_b = seq // tm
    hb = tm // CONV_HALO
    assert seq % tm == 0 and n % tn == 0 and SSD_CONV == 7
    return pl.pallas_call(
        functools.partial(_proj_conv_kernel, tm=tm, per_b=per_b),
        grid=(m // tm, n // tn),
        in_specs=[
            pl.BlockSpec((CONV_HALO, k), lambda i, j: (jnp.maximum(i * hb - 1, 0), 0)),
            pl.BlockSpec((tm, k), lambda i, j: (i, 0)),
            pl.BlockSpec((CONV_HALO, k), lambda i, j: (jnp.minimum((i + 1) * hb, m // CONV_HALO - 1), 0)),
            pl.BlockSpec((k, tn), lambda i, j: (0, cb0 + j)),
            pl.BlockSpec((SSD_CONV, tn), lambda i, j: (0, j)),
            pl.BlockSpec((1, tn), lambda i, j: (0, j)),
        ],
        out_specs=pl.BlockSpec((tm, tn), lambda i, j: (i, j)),
        out_shape=jax.ShapeDtypeStruct((m, n), BF16),
        compiler_params=_cparams(("parallel", "parallel")),
        name="proj_conv_silu",
    )(h, h, h, w, cw, cb.reshape(1, n))


def _softplus(x):
    return jnp.maximum(x, 0.0) + jnp.log1p(jnp.exp(-jnp.abs(x)))


def _ssd_prep_kernel(dtr_ref, dtb_ref, alog_ref, ccol_ref, rt_ref, esc_ref, wst_ref, etot_ref, *, t):
    half = LANES // 2
    log2e = math.log2(math.e)
    lane = lax.broadcasted_iota(jnp.int32, (t, LANES), 1)
    fwd = lane < SSD_HEADS
    tri = (lax.broadcasted_iota(jnp.int32, (t, t), 1) <= lax.broadcasted_iota(jnp.int32, (t, t), 0)).astype(BF16)
    neg_a = jnp.exp(alog_ref[...])
    for ci in range(dtr_ref.shape[0] // t):
        rs = slice(ci * t, (ci + 1) * t)
        dt = _softplus(dtr_ref[rs, :] + dtb_ref[...])
        la = jnp.where(lane < 2 * SSD_HEADS, -(dt * neg_a), 0.0)
        hi = la.astype(BF16)
        r1 = la - hi.astype(F32)
        mid = r1.astype(BF16)
        lo = (r1 - mid.astype(F32)).astype(BF16)
        packed = (hi.astype(F32) + pltpu.roll(mid.astype(F32), half, 1)).astype(BF16)
        res = jnp.dot(tri, jnp.concatenate([packed, lo], axis=1), preferred_element_type=F32)
        a0 = res[:, :LANES]
        acs = a0 + pltpu.roll(a0, half, 1) + res[:, LANES:]
        exb = acs - la
        ldt = jnp.log(dt)
        tot = acs[t - 1:t, :]
        ccol_ref[rs, :] = jnp.where(fwd, acs, exb) * log2e
        rt_ref[ci] = (jnp.where(fwd, acs - ldt, exb + ldt) * log2e).T
        esc_ref[rs, :] = jnp.exp(jnp.where(fwd, acs, tot - exb))
        wst_ref[rs, :] = jnp.exp(jnp.where(fwd, tot - acs, exb)) * dt
        etot_ref[ci] = jnp.broadcast_to(jnp.exp(tot), etot_ref.shape[1:])


def _split2(v):
    hi = v.astype(BF16)
    return jnp.concatenate([hi, (v - hi.astype(F32)).astype(BF16)], axis=1)


def _head_expand_tables():
    j = np.arange(2 * LANES)[:, None] % LANES
    c = np.arange(SSD_INNER)[None, :]
    fwd = (j == c // SSD_HD)
    bwd = (j == SSD_HEADS + c // SSD_HD)
    return jnp.asarray(fwd, BF16), jnp.asarray(bwd, BF16)


def _ssd_bwd_state_kernel(xs_ref, b_ref, wst_ref, etot_ref, selb_ref, gin_ref, g_sc):
    @pl.when(pl.program_id(0) == 0)
    def _():
        g_sc[...] = jnp.zeros(g_sc.shape, F32)

    cps = etot_ref.shape[1]
    t = wst_ref.shape[1] // cps
    for sc, b in [(sc, b) for sc in reversed(range(cps)) for b in range(xs_ref.shape[0])]:
        rows = slice(sc * t, (sc + 1) * t)
        gin_ref[b, sc] = g_sc[b].astype(gin_ref.dtype)
        both = jnp.concatenate([_split2(wst_ref[b, rows, :]), _split2(etot_ref[b, sc])], axis=0)
        spread = jnp.dot(both, selb_ref[...], preferred_element_type=F32)
        xw = (xs_ref[b, rows, :].astype(F32) * spread[:t]).astype(BF16)
        dec = spread[t:t + 1]
        for g in range(SSD_GROUPS):
            gs = slice(g * SSD_GW, (g + 1) * SSD_GW)
            bm_t = b_ref[b, rows, g * SSD_STATE:(g + 1) * SSD_STATE].T
            st = jnp.dot(bm_t, xw[:, gs], preferred_element_type=F32)
            g_sc[b, :, gs] = g_sc[b, :, gs] * dec[:, gs] + st


def _ssd_main_kernel(xs_ref, b_ref, c_ref, ccol_ref, rt_ref, esc_ref, wst_ref, etot_ref, gin_ref,
                     dsk_ref, self_ref, selb_ref, y_ref, h_sc):
    @pl.when(pl.program_id(1) == 0)
    def _():
        h_sc[...] = jnp.zeros(h_sc.shape, F32)

    t = rt_ref.shape[-1]
    li = lax.broadcasted_iota(jnp.int32, (t, t), 0)
    si = lax.broadcasted_iota(jnp.int32, (t, t), 1)
    mask_f = jnp.where(li >= si, 0.0, MASK_VALUE)
    mask_b = jnp.where(si >= li, 0.0, MASK_VALUE)
    lo_half = lax.broadcasted_iota(jnp.int32, (t, LANES), 1) < SSD_HD

    for sc, g in [(sc, g) for sc in range(rt_ref.shape[0]) for g in range(SSD_GROUPS)]:
        rows = slice(sc * t, (sc + 1) * t)
        ccol = ccol_ref[rows, :]
        r_t = rt_ref[sc]
        esc2 = _split2(esc_ref[rows, :])
        wst2 = _split2(wst_ref[rows, :])
        etot2 = _split2(etot_ref[sc])
        gs = slice(g * SSD_GW, (g + 1) * SSD_GW)
        bm = b_ref[rows, g * SSD_STATE:(g + 1) * SSD_STATE]
        cm = c_ref[rows, g * SSD_STATE:(g + 1) * SSD_STATE]
        cb = lax.dot_general(cm, bm, (((1,), (1,)), ((), ())), preferred_element_type=F32)
        hf = h_sc[:, gs]
        yf_all = jnp.dot(cm, hf.astype(BF16), preferred_element_type=F32)
        yb_all = jnp.dot(cm, gin_ref[sc, :, gs], preferred_element_type=F32)
        e_in = jnp.dot(esc2, self_ref[g], preferred_element_type=F32)
        e_out = jnp.dot(esc2, selb_ref[g], preferred_element_type=F32)
        w_state = jnp.dot(wst2, self_ref[g], preferred_element_type=F32)
        dec = jnp.dot(etot2, self_ref[g], preferred_element_type=F32)[0:1]
        xw = []
        for k in range(SSD_HPG // 2):
            ea = g * SSD_HPG + 2 * k
            sl = slice(g * SSD_GW + k * LANES, g * SSD_GW + (k + 1) * LANES)
            ks = slice(k * LANES, (k + 1) * LANES)
            xp = xs_ref[rows, sl]
            mms = []
            for e in (ea, ea + 1):
                eb = SSD_HEADS + e
                w = (jnp.exp2(ccol[:, e:e + 1] - r_t[e:e + 1, :] + mask_f)
                     + jnp.exp2(r_t[eb:eb + 1, :] - ccol[:, eb:eb + 1] + mask_b))
                mms.append((cb * w).astype(BF16))
            zero = jnp.zeros_like(xp)
            xx = jnp.concatenate([jnp.where(lo_half, xp, zero), jnp.where(lo_half, zero, xp)], axis=0)
            ypair = jnp.dot(jnp.concatenate(mms, axis=1), xx, preferred_element_type=F32)
            xf = xp.astype(F32)
            y_ref[rows, sl] = (ypair + yf_all[:, ks] * e_in[:, ks] + yb_all[:, ks] * e_out[:, ks]
                               + xf * dsk_ref[:, sl])
            xw.append((xf * w_state[:, ks]).astype(BF16))
        st = jnp.dot(bm.T, jnp.concatenate(xw, axis=1), preferred_element_type=F32)
        h_sc[:, gs] = hf * dec + st


def _ssd(conv_out, dt_raw, dtb, alog, dskip, bsz, seq):
    t = SSD_CHUNK
    nc = seq // t
    assert seq % t == 0
    gn = SSD_GROUPS * SSD_STATE
    b_blk = SSD_INNER // gn
    c_blk = b_blk + 1
    cpp = min(8, nc)
    assert nc % cpp == 0
    sub8 = SUBLANES
    const_spec = pl.BlockSpec((1, LANES), lambda b, i: (0, 0))
    rows_spec = pl.BlockSpec((None, cpp * t, LANES), lambda b, i: (b, i, 0))
    row_arr = jax.ShapeDtypeStruct((bsz, seq, LANES), F32)
    ccol, rt, esc, wst, etot = pl.pallas_call(
        functools.partial(_ssd_prep_kernel, t=t),
        grid=(bsz, nc // cpp),
        in_specs=[rows_spec, const_spec, const_spec],
        out_specs=(rows_spec, pl.BlockSpec((None, cpp, LANES, t), lambda b, i: (b, i, 0, 0)), rows_spec, rows_spec,
                   pl.BlockSpec((None, cpp, sub8, LANES), lambda b, i: (b, i, 0, 0))),
        out_shape=(row_arr, jax.ShapeDtypeStruct((bsz, nc, LANES, t), F32), row_arr, row_arr,
                   jax.ShapeDtypeStruct((bsz, nc, sub8, LANES), F32)),
        compiler_params=_cparams(("parallel", "parallel")),
        name="ssd_decay_terms",
    )(dt_raw, dtb, alog)

    cps = next(n for n in (4, 2, 1) if nc % n == 0)
    rev = lambda c: nc // cps - 1 - c
    sel_f, sel_b = _head_expand_tables()
    by_group = lambda a: a.reshape(a.shape[0], SSD_GROUPS, SSD_GW).transpose(1, 0, 2)
    sel_spec = pl.BlockSpec((SSD_GROUPS, sel_f.shape[0], SSD_GW), lambda b, c: (0, 0, 0))
    gin = pl.pallas_call(
        _ssd_bwd_state_kernel,
        grid=(nc // cps,),
        in_specs=[
            pl.BlockSpec((bsz, cps * t, SSD_INNER), lambda c: (0, rev(c), 0)),
            pl.BlockSpec((bsz, cps * t, gn), lambda c: (0, rev(c), b_blk)),
            pl.BlockSpec((bsz, cps * t, LANES), lambda c: (0, rev(c), 0)),
            pl.BlockSpec((bsz, cps, sub8, LANES), lambda c: (0, rev(c), 0, 0)),
            pl.BlockSpec(sel_b.shape, lambda c: (0, 0)),
        ],
        out_specs=pl.BlockSpec((bsz, cps, SSD_STATE, SSD_INNER), lambda c: (0, rev(c), 0, 0)),
        out_shape=jax.ShapeDtypeStruct((bsz, nc, SSD_STATE, SSD_INNER), BF16),
        scratch_shapes=[pltpu.VMEM((bsz, SSD_STATE, SSD_INNER), F32)],
        compiler_params=_cparams(("arbitrary",)),
        name="ssd_bwd_states",
    )(conv_out, conv_out, wst, etot, sel_b)

    wide_spec = pl.BlockSpec((1, SSD_INNER), lambda b, c: (0, 0))
    tok_spec = pl.BlockSpec((None, cps * t, LANES), lambda b, c: (b, c, 0))
    return pl.pallas_call(
        _ssd_main_kernel,
        grid=(bsz, nc // cps),
        in_specs=[
            pl.BlockSpec((None, cps * t, SSD_INNER), lambda b, c: (b, c, 0)),
            pl.BlockSpec((None, cps * t, gn), lambda b, c: (b, c, b_blk)),
            pl.BlockSpec((None, cps * t, gn), lambda b, c: (b, c, c_blk)),
            tok_spec,
            pl.BlockSpec((None, cps, LANES, t), lambda b, c: (b, c, 0, 0)),
            tok_spec, tok_spec,
            pl.BlockSpec((None, cps, sub8, LANES), lambda b, c: (b, c, 0, 0)),
            pl.BlockSpec((None, cps, SSD_STATE, SSD_INNER), lambda b, c: (b, c, 0, 0)),
            wide_spec, sel_spec, sel_spec,
        ],
        out_specs=pl.BlockSpec((None, cps * t, SSD_INNER), lambda b, c: (b, c, 0)),
        out_shape=jax.ShapeDtypeStruct((bsz, seq, SSD_INNER), F32),
        scratch_shapes=[pltpu.VMEM((SSD_STATE, SSD_INNER), F32)],
        compiler_params=_cparams(("parallel", "arbitrary")),
        name="ssd_main",
    )(conv_out, conv_out, conv_out, ccol, rt, esc, wst, etot, gin, dskip, by_group(sel_f), by_group(sel_b))


def _dft_tables(seq):
    n1, n2 = seq // F_N2, F_N2
    c = np.arange(F_GD)
    ang = 2 * np.pi * np.outer(c, c) / F_GD
    chan = np.concatenate([np.cos(ang), -np.sin(ang)], axis=1) / math.sqrt(F_GD)
    k1 = np.arange(n1)
    a1 = 2 * np.pi * np.outer(k1, k1) / n1
    stage_a = np.block([[np.cos(a1), np.sin(a1)], [-np.sin(a1), np.cos(a1)]])
    stage_a = np.kron(stage_a, np.eye(2))
    s2 = np.arange(n2)
    at = 2 * np.pi * np.outer(s2, k1) / seq
    at = at.reshape(n2 // 2, 2, n1).transpose(0, 2, 1).reshape(n2 // 2, 2 * n1)
    tw_c, tw_s = np.cos(at)[..., None], np.sin(at)[..., None]
    a2 = 2 * np.pi * np.outer(s2, s2) / n2
    stage_b = np.concatenate([np.cos(a2), np.sin(a2)], axis=1)
    return (jnp.asarray(chan, BF16), jnp.asarray(stage_a, BF16), jnp.asarray(tw_c, F32),
            jnp.asarray(tw_s, F32), jnp.asarray(stage_b, BF16))


def _fft_a_kernel(z_ref, ma_ref, twc_ref, tws_ref, o_ref, zs, ys, *, n1, rb):
    ct = z_ref.shape[-1]
    half = rb // 2
    _put_rows(zs, pltpu.bitcast(z_ref[...].reshape(2 * n1 * rb, ct), jnp.uint32))
    for j in range(half):
        zz = pltpu.bitcast(_get_rows(zs, j, 2 * n1, half), BF16)
        y = jnp.dot(ma_ref[...], zz, preferred_element_type=F32)
        yr, yi = y[:2 * n1], y[2 * n1:]
        tc, ts = twc_ref[j], tws_ref[j]
        out = jnp.concatenate([yr * tc + yi * ts, yi * tc - yr * ts], axis=0).astype(BF16)
        _put_rows(ys, pltpu.bitcast(out, jnp.uint32), j, half)
    o_ref[...] = pltpu.bitcast(_get_rows(ys), BF16).reshape(2, n1, rb, ct)


def _fft_b_kernel(y_ref, mb_ref, o_ref, os_ref, *, n2, kb, scale):
    ct = y_ref.shape[-1]
    for kk in range(kb):
        yy = y_ref[:, kk].reshape(2 * n2, ct)
        _put_rows(os_ref, jnp.dot(mb_ref[...], yy, preferred_element_type=F32) * scale, kk, kb)
    o_ref[...] = _get_rows(os_ref).reshape(n2, kb, ct).astype(o_ref.dtype)


def _fourier_seq(z2, tabs, bsz, seq, *, rb=BF16_ROWS, kb=BF16_ROWS, ct_a=1536, ct_b=768):
    _, ma, twc, tws, mb = tabs
    n1, n2 = seq // F_N2, F_N2
    w = z2.shape[-1]
    ct, nct = ct_a, w // ct_a
    za = z2.reshape(2, bsz, n1, n2, w)
    ab_spec = pl.BlockSpec((2, None, n1, rb, ct), lambda b, a, j: (0, b, 0, a, j))
    ya = pl.pallas_call(
        functools.partial(_fft_a_kernel, n1=n1, rb=rb),
        grid=(bsz, n2 // rb, nct),
        in_specs=[
            ab_spec,
            pl.BlockSpec((4 * n1, 4 * n1), lambda b, a, j: (0, 0)),
            pl.BlockSpec((rb // 2, 2 * n1, 1), lambda b, a, j: (a, 0, 0)),
            pl.BlockSpec((rb // 2, 2 * n1, 1), lambda b, a, j: (a, 0, 0)),
        ],
        out_specs=ab_spec,
        out_shape=jax.ShapeDtypeStruct((2, bsz, n1, n2, w), BF16),
        scratch_shapes=[pltpu.VMEM((ct // LANES, n1 * rb, LANES), jnp.uint32)] * 2,
        compiler_params=_cparams(("parallel", "parallel", "parallel")),
        name="fourier_stage_a",
    )(za, ma, twc, tws)
    ct, nct = ct_b, w // ct_b
    out = pl.pallas_call(
        functools.partial(_fft_b_kernel, n2=n2, kb=kb, scale=1.0 / math.sqrt(seq)),
        grid=(bsz, n1 // kb, nct),
        in_specs=[
            pl.BlockSpec((2, None, kb, n2, ct), lambda b, k, j: (0, b, k, 0, j)),
            pl.BlockSpec((n2, 2 * n2), lambda b, k, j: (0, 0)),
        ],
        out_specs=pl.BlockSpec((None, n2, kb, ct), lambda b, k, j: (b, 0, k, j)),
        out_shape=jax.ShapeDtypeStruct((bsz, n2, n1, w), BF16),
        scratch_shapes=[pltpu.VMEM((ct // LANES, n2 * kb, LANES), F32)],
        compiler_params=_cparams(("parallel", "parallel", "parallel")),
        name="fourier_stage_b",
    )(ya, mb)
    return out.reshape(bsz, seq, w)


def _merge_kernel(*refs, dils):
    ng = len(dils)
    x_ref = refs[0]
    o_refs = refs[1:1 + ng]
    l_refs = refs[1 + ng:1 + 2 * ng]
    (y_ref, z_ref, yg_ref, f_ref, gt_ref, wa_ref, ws_ref, wf_ref, wo_ref,
     xg_ref, wq_ref, qg_ref, k_ref, v_ref, wxo_ref, out_ref, o_sc, l_sc) = refs[1 + 2 * ng:]
    tm, d = x_ref.shape

    yf = jnp.dot(f_ref[...], wf_ref[...], preferred_element_type=F32)

    ys = None
    for g in range(SSD_GROUPS):
        gs = slice(g * SSD_GW, (g + 1) * SSD_GW)
        yy = y_ref[:, gs] * _silu(z_ref[:, gs].astype(F32))
        part = jnp.dot(_rms(yy, yg_ref[:, gs]).astype(BF16), ws_ref[gs, :], preferred_element_type=F32)
        ys = part if ys is None else ys + part

    for gi, dil in enumerate(dils):
        rows = tm // dil
        idx = lambda r: slice(None) if dil == 1 else pl.ds(r, rows, stride=dil)
        for r in range(dil):
            l_sc[gi, idx(r), :] = l_refs[gi][:, r * LANES:(r + 1) * LANES]
            for hh in range(ATTN_HPG):
                c0 = r * ATTN_OUT + hh * ATTN_HD
                o_sc[gi * ATTN_HPG + hh, idx(r), :] = o_refs[gi][:, c0:c0 + ATTN_HD].astype(F32)

    ls = [l_sc[gi] for gi in range(ng)]
    mx = functools.reduce(jnp.maximum, ls)
    ws = [jnp.exp(v - mx) for v in ls]
    inv = 1.0 / functools.reduce(lambda a, b: a + b, ws)
    heads = []
    for hh in range(ATTN_HPG):
        acc = None
        for gi in range(ng):
            alpha = (ws[gi] * inv)[:, hh:hh + 1]
            term = alpha * o_sc[gi * ATTN_HPG + hh]
            acc = term if acc is None else acc + term
        heads.append(acc.astype(BF16))
    o_attn = jnp.concatenate(heads, axis=1)

    ya = jnp.dot(o_attn, wa_ref[...], preferred_element_type=F32)
    gates = gt_ref[...].astype(F32)
    merged = gates[:, :d] * ya + gates[:, d:2 * d] * ys + gates[:, 2 * d:] * yf
    x1 = x_ref[...] + jnp.dot(merged.astype(BF16), wo_ref[...], preferred_element_type=F32)
    rc = min(tm, XATTN_ROW_CHUNK)
    for c in range(tm // rc):
        rs = slice(c * rc, (c + 1) * rc)
        out_ref[rs, :] = _xattn_rows(x1[rs], xg_ref, wq_ref, qg_ref, k_ref, v_ref, wxo_ref)


def _xattn_rows(x, g_ref, wq_ref, qg_ref, k_ref, v_ref, wo_ref):
    d = x.shape[1]
    hd = d // MEM_HEADS
    scale = hd ** -0.5
    h = _rms(x, g_ref[...]).astype(BF16)
    q = jnp.dot(h, wq_ref[...], preferred_element_type=F32)
    heads = []
    for hh in range(MEM_HEADS):
        sl = slice(hh * hd, (hh + 1) * hd)
        qn = _rms(q[:, sl], qg_ref[...]).astype(BF16)
        s = lax.dot_general(qn, k_ref[:, sl], (((1,), (1,)), ((), ())), preferred_element_type=F32) * scale
        m = jnp.max(s, axis=-1, keepdims=True)
        p = jnp.exp(s - m)
        l = jnp.sum(p, axis=-1, keepdims=True)
        heads.append((jnp.dot(p.astype(BF16), v_ref[:, sl], preferred_element_type=F32) / l).astype(BF16))
    return x + jnp.dot(jnp.concatenate(heads, axis=1), wo_ref[...], preferred_element_type=F32)


def _merge(x, os_, ls_, dils, y, z, y_gain, f, gates, wa, ws, wf, wo, xg, wq, qg, km, vm, wxo, bsz, seq,
           *, tm=256):
    d = x.shape[-1]
    mt = km.shape[1]
    xg, qg = xg.reshape(1, d), qg.reshape(1, -1)
    mem = pl.BlockSpec((None, mt, d), lambda b, i: (b, 0, 0))
    assert all((tm // dil) % BF16_ROWS == 0 for dil in dils)
    tok = lambda width: pl.BlockSpec((None, tm, width), lambda b, i: (b, i, 0))
    full = lambda a: pl.BlockSpec(a.shape, lambda b, i: (0, 0), pipeline_mode=pl.Buffered(1))
    ng = len(dils)
    in_specs = [tok(d)]
    in_specs += [pl.BlockSpec((None, tm // dil, dil * ATTN_OUT), lambda b, i: (b, i, 0)) for dil in dils]
    in_specs += [pl.BlockSpec((None, tm // dil, dil * LANES), lambda b, i: (b, i, 0)) for dil in dils]
    in_specs += [tok(y.shape[-1]), tok(z.shape[-1]), full(y_gain), tok(f.shape[-1]), tok(gates.shape[-1]),
                 full(wa), full(ws), full(wf), full(wo), full(xg), full(wq), full(qg), mem, mem, full(wxo)]
    return pl.pallas_call(
        functools.partial(_merge_kernel, dils=tuple(dils)),
        grid=(bsz, seq // tm),
        in_specs=in_specs,
        out_specs=tok(d),
        out_shape=jax.ShapeDtypeStruct((bsz, seq, d), F32),
        scratch_shapes=[pltpu.VMEM((ng * ATTN_HPG, tm, ATTN_HD), F32), pltpu.VMEM((ng, tm, LANES), F32)],
        compiler_params=_cparams(("parallel", "parallel")),
        name="branch_merge_xattn",
    )(x, *os_, *ls_, y, z, y_gain, f, gates, wa, ws, wf, wo, xg, wq, qg, km, vm, wxo)


def _ffn_kernel(x_ref, g_ref, wg_ref, wu_ref, wd_ref, *rest, tf):
    x = x_ref[...]
    h = _rms(x, g_ref[...]).astype(BF16)
    acc = x
    for c in range(wg_ref.shape[1] // tf):
        sl = slice(c * tf, (c + 1) * tf)
        gt = jnp.dot(h, wg_ref[:, sl], preferred_element_type=F32)
        up = jnp.dot(h, wu_ref[:, sl], preferred_element_type=F32)
        a = (_silu(gt) * up).astype(BF16)
        acc = acc + jnp.dot(a, wd_ref[sl, :], preferred_element_type=F32)
    if len(rest) == 1:
        rest[0][...] = acc
    else:
        gn_ref, out_ref, hn_ref = rest
        out_ref[...] = acc
        hn_ref[...] = _rms(acc, gn_ref[...]).astype(hn_ref.dtype)


def _ffn(x, g, wg, wu, wd, next_gain=None, *, tm=512):
    m, d = x.shape
    dff = wg.shape[1]
    tf = dff // 2 if (dff // 2) % LANES == 0 else dff
    full = lambda a: pl.BlockSpec(a.shape, lambda i: (0, 0), pipeline_mode=pl.Buffered(1))
    tok = pl.BlockSpec((tm, d), lambda i: (i, 0))
    args = [x, g.reshape(1, d), wg, wu, wd]
    out_shape, out_specs = jax.ShapeDtypeStruct((m, d), F32), tok
    if next_gain is not None:
        args.append(next_gain.reshape(1, d))
        out_shape, out_specs = (out_shape, jax.ShapeDtypeStruct((m, d), BF16)), (tok, tok)
    return pl.pallas_call(
        functools.partial(_ffn_kernel, tf=tf),
        grid=(m // tm,),
        in_specs=[tok] + [full(a) for a in args[1:]],
        out_specs=out_specs,
        out_shape=out_shape,
        compiler_params=_cparams(("parallel",)),
        name="swiglu_ffn",
    )(*args)


def _pad_lanes(a):
    a = a.reshape(-1, 2 * SSD_HEADS)
    return jnp.pad(a, ((0, 0), (0, LANES - 2 * SSD_HEADS)))


def kernel(x, mem, rel_bias, mix_norm_g, w_in, gate_bias, attn_q_norm_g, attn_k_norm_g, conv_w, conv_b, dt_bias, a_log, d_skip, ssd_norm_g, w_branch_attn, w_branch_ssd, w_branch_fourier, w_mix_out, xattn_norm_g, mem_norm_g, w_xq, w_xk, w_xv, xattn_q_norm_g, xattn_k_norm_g, w_xo, ffn_norm_g, w_ffn_gate, w_ffn_up, w_ffn_down):
    bsz, seq, d = x.shape
    depth = w_in.shape[0]
    m = bsz * seq
    xf = x.reshape(m, d)
    memf = mem.reshape(bsz * mem.shape[1], d)
    tabs = _dft_tables(seq)
    dils = [dil for _, dil in ATTN_GROUPS]
    biases = [_attn_bias(rel_bias, gi, dil) for gi, dil in enumerate(dils)]
    offs = np.cumsum([0, ATTN_WIDTH, ATTN_WIDTH, ATTN_WIDTH, SSD_INNER, SSD_CONV_CH, 2 * SSD_HEADS, F_WIDTH, 3 * d])
    bf = lambda a: a.astype(BF16)

    for l in range(depth):
        wl = bf(w_in[l])
        seg = lambda i: wl[:, offs[i]:offs[i + 1]]
        if l == 0:
            h = _rmsnorm(xf, mix_norm_g[l])
        gains = jnp.stack([attn_q_norm_g[l], attn_k_norm_g[l], jnp.ones_like(attn_q_norm_g[l])]).reshape(3, 1, ATTN_HD)
        os_, ls_ = [], []
        for gi, dil in enumerate(dils):
            qkv = _qkv_group(h, wl, gains, gi, dil, bsz, seq)
            o_g, lse_g = _attn_group(qkv, biases[gi], dil, bsz, seq)
            os_.append(o_g)
            ls_.append(lse_g)

        z = _matmul(h, wl, out_dtype=BF16, tn=512, tm=2048, col0=int(offs[3]), n=SSD_INNER)
        dt_raw = _matmul(h, _pad_lanes(seg(5)), out_dtype=F32, tn=LANES, tm=2048)
        z2 = _matmul(h, seg(6), out_dtype=BF16, tn=768, tm=2048, epilogue="chandft", extra=tabs[0],
                     row_chunk=2 * MM_ROW_CHUNK)
        gates = _matmul(h, seg(7), out_dtype=BF16, tn=1024, tm=2048, epilogue="sigmoid_bias", extra=gate_bias[l])

        conv_out = _proj_conv_silu(h, wl, conv_w[l], conv_b[l], seq, col0=int(offs[4])).reshape(bsz, seq, SSD_CONV_CH)
        y_ssd = _ssd(conv_out, dt_raw.reshape(bsz, seq, -1), _pad_lanes(dt_bias[l]), _pad_lanes(a_log[l]),
                     jnp.repeat(d_skip[l], SSD_HD).reshape(1, SSD_INNER), bsz, seq)

        f_re = _fourier_seq(z2, tabs, bsz, seq)

        hd_m = d // MEM_HEADS
        hm = _rmsnorm(memf, mem_norm_g[l])
        km = _matmul(hm, bf(w_xk[l]), out_dtype=BF16, tn=512, epilogue="headnorm",
                     extra=xattn_k_norm_g[l], head_dim=hd_m)
        vm = _matmul(hm, bf(w_xv[l]), out_dtype=BF16, tn=512)

        x3 = _merge(xf.reshape(bsz, seq, d), os_, ls_, dils, y_ssd, z.reshape(bsz, seq, SSD_INNER),
                    ssd_norm_g[l].reshape(1, SSD_INNER), f_re, gates.reshape(bsz, seq, 3 * d),
                    bf(w_branch_attn[l]), bf(w_branch_ssd[l]), bf(w_branch_fourier[l]), bf(w_mix_out[l]),
                    xattn_norm_g[l], bf(w_xq[l]), xattn_q_norm_g[l], km.reshape(bsz, -1, d),
                    vm.reshape(bsz, -1, d), bf(w_xo[l]), bsz, seq)

        ffn_args = (x3.reshape(m, d), ffn_norm_g[l], bf(w_ffn_gate[l]), bf(w_ffn_up[l]), bf(w_ffn_down[l]))
        if l + 1 < depth:
            xf, h = _ffn(*ffn_args, mix_norm_g[l + 1])
        else:
            xf = _ffn(*ffn_args)

    return xf.reshape(bsz, seq, d)
```
